```python
import math
import jax
import jax.numpy as jnp
from jax import lax
import numpy as np

D_MODEL = 2048
BATCH = 8
SEQ = 2048
DEPTH = 4

RMS_EPS = 1e-6
NEG_INF = -1e30
N_BRANCH = 3
SSM_WIDTH = D_MODEL // 4
SSM_GROUP = 16
SSM_GROUPS = SSM_WIDTH // SSM_GROUP
SSM_STATE = 64
DT_MIN = 1e-3
DT_MAX = 1e-1
HEAD_DIM = 64
DSWA_PATTERNS = ((128, 1), (512, 4), (2048, 16))
ATTN_WIDTH = D_MODEL // 4
HEADS_PER_PATTERN = ATTN_WIDTH // HEAD_DIM
N_ATTN_HEADS = HEADS_PER_PATTERN * len(DSWA_PATTERNS)
QKV_WIDTH = N_ATTN_HEADS * HEAD_DIM
CONV_WIDTH = D_MODEL // 4
CONV_K = 3
D_FF = 256 * ((8 * D_MODEL // 3 + 255) // 256)
FFN_CONV_K = 3
OFF_Q = SSM_WIDTH
OFF_K = OFF_Q + QKV_WIDTH
OFF_V = OFF_K + QKV_WIDTH
OFF_CONV = OFF_V + QKV_WIDTH
OFF_GATE = OFF_CONV + 3 * CONV_WIDTH
N_IN = OFF_GATE + N_BRANCH * D_MODEL

kernel_name = 'hybrid_ssm_dilated_attn_conv_trunk'


def alibi_slopes(n_heads):
    return np.array([2.0 ** (-8.0 * (h + 1) / n_heads) for h in range(n_heads)], dtype=np.float32)


def rms_norm(x, g):
    x32 = x.astype(jnp.float32)
    y = x32 * lax.rsqrt(jnp.mean(x32 * x32, axis=-1, keepdims=True) + RMS_EPS)
    return y.astype(x.dtype) * g


def causal_dwconv(z, w):
    k_width = w.shape[0]
    seq = z.shape[1]
    zp = jnp.pad(z, ((0, 0), (k_width - 1, 0), (0, 0)))
    return sum(w[k] * zp[:, k_width - 1 - k:k_width - 1 - k + seq] for k in range(k_width))


def s5_mixer(u, log_dt, a_re, a_im, b_re, b_im, c_re, c_im, d_skip, w_glu, b_glu):
    f32 = jnp.float32
    bsz, seq, _ = u.shape
    u32 = u.astype(f32)
    ug = u32.reshape(bsz, seq, SSM_GROUPS, SSM_GROUP)
    dt = jnp.exp(log_dt.astype(f32))[:, None]
    ar, ai = a_re.astype(f32), a_im.astype(f32)
    mag = jnp.exp(ar * dt)
    lr, li = mag * jnp.cos(ai * dt), mag * jnp.sin(ai * dt)
    den = ar * ar + ai * ai
    fr = ((lr - 1.0) * ar + li * ai) / den
    fi = (li * ar - (lr - 1.0) * ai) / den
    br, bi = b_re.astype(f32), b_im.astype(f32)
    bbr = fr[..., None] * br - fi[..., None] * bi
    bbi = fr[..., None] * bi + fi[..., None] * br
    xr = jnp.einsum('blgc,gnc->blgn', ug, bbr)
    xi = jnp.einsum('blgc,gnc->blgn', ug, bbi)
    lam_r = jnp.broadcast_to(lr, xr.shape)
    lam_i = jnp.broadcast_to(li, xr.shape)

    def combine(e1, e2):
        a1r, a1i, b1r, b1i = e1
        a2r, a2i, b2r, b2i = e2
        return (a2r * a1r - a2i * a1i,
                a2r * a1i + a2i * a1r,
                a2r * b1r - a2i * b1i + b2r,
                a2r * b1i + a2i * b1r + b2i)

    _, _, hr, hi = lax.associative_scan(combine, (lam_r, lam_i, xr, xi), axis=1)
    y = (jnp.einsum('blgn,gcn->blgc', hr, c_re.astype(f32))
         - jnp.einsum('blgn,gcn->blgc', hi, c_im.astype(f32)))
    y = y.reshape(bsz, seq, SSM_WIDTH) + d_skip.astype(f32) * u32
    g = jax.nn.gelu(y).astype(u.dtype)
    return g * jax.nn.sigmoid(g @ w_glu + b_glu)


def dilated_window_attention(q, k, v, slopes, window, dilation):
    f32 = jnp.float32
    bsz, seq, n_h, e = q.shape
    w_steps = window // dilation
    qb_len = w_steps
    unit = dilation * qb_len
    seq_p = -(-seq // unit) * unit
    pad = seq_p - seq
    m_len = seq_p // dilation
    nb = m_len // qb_len

    def to_blocks(t):
        t = jnp.pad(t, ((0, 0), (0, pad), (0, 0), (0, 0)))
        t = t.reshape(bsz, m_len, dilation, n_h, e).transpose(0, 2, 1, 3, 4)
        return t.reshape(bsz, dilation, nb, qb_len, n_h, e)

    def with_prev(t):
        prev = jnp.pad(t[:, :, :-1], ((0, 0), (0, 0), (1, 0), (0, 0), (0, 0), (0, 0)))
        return jnp.concatenate([prev, t], axis=3)

    qb = to_blocks(q)
    kk = with_prev(to_blocks(k))
    vv = with_prev(to_blocks(v))
    s = jnp.einsum('brnqhe,brnkhe->brnhqk', qb, kk).astype(f32) * (e ** -0.5)
    qi = jnp.arange(qb_len)[:, None]
    kj = jnp.arange(2 * qb_len)[None, :]
    dist = qb_len + qi - kj
    blk = jnp.arange(nb)[:, None, None]
    valid = (dist >= 0) & (dist <= w_steps) & ((blk > 0) | (kj >= qb_len))
    bias = -slopes[:, None, None] * (dist * dilation).astype(f32)
    s = jnp.where(valid[:, None], s + bias, NEG_INF)
    m = jnp.max(s, axis=-1, keepdims=True)
    p = jnp.exp(s - m)
    l = jnp.sum(p, axis=-1, keepdims=True)
    o = jnp.einsum('brnhqk,brnkhe->brnqhe', p / l, vv.astype(f32))
    lse = (m + jnp.log(l))[..., 0]
    o = o.reshape(bsz, dilation, m_len, n_h, e).transpose(0, 2, 1, 3, 4).reshape(bsz, seq_p, n_h, e)
    lse = lse.transpose(0, 1, 2, 4, 3).reshape(bsz, dilation, m_len, n_h)
    lse = lse.transpose(0, 2, 1, 3).reshape(bsz, seq_p, n_h)
    return o[:, :seq], lse[:, :seq]


def dilated_attention_mixer(q, k, v):
    bsz, seq = q.shape[0], q.shape[1]
    slopes = jnp.asarray(alibi_slopes(N_ATTN_HEADS))
    outs, lses = [], []
    for g, (window, dilation) in enumerate(DSWA_PATTERNS):
        hs = slice(g * HEADS_PER_PATTERN, (g + 1) * HEADS_PER_PATTERN)
        o, lse = dilated_window_attention(q[:, :, hs], k[:, :, hs], v[:, :, hs], slopes[hs], window, dilation)
        outs.append(o)
        lses.append(lse)
    alpha = jax.nn.softmax(jnp.stack(lses, axis=0), axis=0)
    o = jnp.sum(alpha[..., None] * jnp.stack(outs, axis=0), axis=0)
    return o.reshape(bsz, seq, ATTN_WIDTH).astype(q.dtype)


def hybrid_mixer(h, w_in, ssm_log_dt, ssm_a_re, ssm_a_im, ssm_b_re, ssm_b_im, ssm_c_re, ssm_c_im,
                 ssm_d, w_glu, b_glu, conv_mix_w, w_ssm_out, w_attn_out, w_conv_out, b_gate, w_o):
    bsz, seq, _ = h.shape
    proj = h @ w_in
    u = proj[..., :OFF_Q]
    q = proj[..., OFF_Q:OFF_K].reshape(bsz, seq, N_ATTN_HEADS, HEAD_DIM)
    k = proj[..., OFF_K:OFF_V].reshape(bsz, seq, N_ATTN_HEADS, HEAD_DIM)
    v = proj[..., OFF_V:OFF_CONV].reshape(bsz, seq, N_ATTN_HEADS, HEAD_DIM)
    conv_b, conv_c, conv_h = jnp.split(proj[..., OFF_CONV:OFF_GATE], 3, axis=-1)
    gates = jax.nn.sigmoid(proj[..., OFF_GATE:] + b_gate).reshape(bsz, seq, N_BRANCH, D_MODEL)
    y_ssm = s5_mixer(u, ssm_log_dt, ssm_a_re, ssm_a_im, ssm_b_re, ssm_b_im, ssm_c_re, ssm_c_im,
                     ssm_d, w_glu, b_glu) @ w_ssm_out
    y_attn = dilated_attention_mixer(q, k, v) @ w_attn_out
    y_conv = (conv_b * causal_dwconv(conv_c * conv_h, conv_mix_w)) @ w_conv_out
    merged = gates[:, :, 0] * y_ssm + gates[:, :, 1] * y_attn + gates[:, :, 2] * y_conv
    return merged @ w_o


def conv_ffn(h, w_up, ffn_conv_w, w_down):
    up = causal_dwconv(h @ w_up, ffn_conv_w)
    a, b = jnp.split(up, 2, axis=-1)
    return (jax.nn.silu(a) * b) @ w_down


def _fwd_setup_inputs(seed: int = 0) -> dict:
    key = jax.random.key(seed)
    ks = jax.random.split(key, 32)
    f32 = jnp.float32

    def nrm(k, shape, scale):
        return scale * jax.random.normal(k, shape, f32)

    nl = DEPTH
    n_idx = jnp.arange(SSM_STATE, dtype=f32)
    return {
        'x': nrm(ks[0], (BATCH, SEQ, D_MODEL), 1.0),
        'c': nrm(ks[1], (BATCH, D_MODEL), 1.0),
        'w_mod': nrm(ks[2], (nl, D_MODEL, 6 * D_MODEL), 0.5 * D_MODEL ** -0.5),
        'b_mod': nrm(ks[3], (nl, 6 * D_MODEL), 0.01),
        'g_pre_mix': 1.0 + nrm(ks[4], (nl, D_MODEL), 0.02),
        'g_post_mix': 1.0 + nrm(ks[5], (nl, D_MODEL), 0.02),
        'g_pre_ffn': 1.0 + nrm(ks[6], (nl, D_MODEL), 0.02),
        'g_post_ffn': 1.0 + nrm(ks[7], (nl, D_MODEL), 0.02),
        'w_in': nrm(ks[8], (nl, D_MODEL, N_IN), D_MODEL ** -0.5),
        'ssm_log_dt': jax.random.uniform(ks[9], (nl, SSM_GROUPS), f32, math.log(DT_MIN), math.log(DT_MAX)),
        'ssm_a_re': -0.5 + nrm(ks[10], (nl, SSM_GROUPS, SSM_STATE), 0.01),
        'ssm_a_im': jnp.pi * n_idx + nrm(ks[11], (nl, SSM_GROUPS, SSM_STATE), 0.01),
        'ssm_b_re': nrm(ks[12], (nl, SSM_GROUPS, SSM_STATE, SSM_GROUP), (2 * SSM_GROUP) ** -0.5),
        'ssm_b_im': nrm(ks[13], (nl, SSM_GROUPS, SSM_STATE, SSM_GROUP), (2 * SSM_GROUP) ** -0.5),
        'ssm_c_re': nrm(ks[14], (nl, SSM_GROUPS, SSM_GROUP, SSM_STATE), 0.5),
        'ssm_c_im': nrm(ks[15], (nl, SSM_GROUPS, SSM_GROUP, SSM_STATE), 0.5),
        'ssm_d': nrm(ks[16], (nl, SSM_WIDTH), 1.0),
        'w_glu': nrm(ks[17], (nl, SSM_WIDTH, SSM_WIDTH), SSM_WIDTH ** -0.5),
        'b_glu': nrm(ks[18], (nl, SSM_WIDTH), 0.01),
        'conv_mix_w': nrm(ks[19], (nl, CONV_K, CONV_WIDTH), CONV_K ** -0.5),
        'w_ssm_out': nrm(ks[20], (nl, SSM_WIDTH, D_MODEL), SSM_WIDTH ** -0.5),
        'w_attn_out': nrm(ks[21], (nl, ATTN_WIDTH, D_MODEL), ATTN_WIDTH ** -0.5),
        'w_conv_out': nrm(ks[22], (nl, CONV_WIDTH, D_MODEL), CONV_WIDTH ** -0.5),
        'b_gate': nrm(ks[23], (nl, N_BRANCH * D_MODEL), 0.01),
        'w_o': nrm(ks[24], (nl, D_MODEL, D_MODEL), D_MODEL ** -0.5),
        'w_up': nrm(ks[25], (nl, D_MODEL, 2 * D_FF), D_MODEL ** -0.5),
        'ffn_conv_w': nrm(ks[26], (nl, FFN_CONV_K, 2 * D_FF), FFN_CONV_K ** -0.5),
        'w_down': nrm(ks[27], (nl, D_FF, D_MODEL), D_FF ** -0.5),
    }


def _fwd_reference(x, c, w_mod, b_mod, g_pre_mix, g_post_mix, g_pre_ffn, g_post_ffn, w_in,
              ssm_log_dt, ssm_a_re, ssm_a_im, ssm_b_re, ssm_b_im, ssm_c_re, ssm_c_im, ssm_d,
              w_glu, b_glu, conv_mix_w, w_ssm_out, w_attn_out, w_conv_out, b_gate, w_o,
              w_up, ffn_conv_w, w_down):
    cond = jax.nn.silu(c)
    for l in range(DEPTH):
        mod = cond @ w_mod[l] + b_mod[l]
        sh1, sc1, gt1, sh2, sc2, gt2 = jnp.split(mod, 6, axis=-1)
        h = rms_norm(x, g_pre_mix[l]) * (1.0 + sc1[:, None]) + sh1[:, None]
        y = hybrid_mixer(h, w_in[l], ssm_log_dt[l], ssm_a_re[l], ssm_a_im[l], ssm_b_re[l], ssm_b_im[l],
                         ssm_c_re[l], ssm_c_im[l], ssm_d[l], w_glu[l], b_glu[l], conv_mix_w[l],
                         w_ssm_out[l], w_attn_out[l], w_conv_out[l], b_gate[l], w_o[l])
        x = x + gt1[:, None] * rms_norm(y, g_post_mix[l])
        h = rms_norm(x, g_pre_ffn[l]) * (1.0 + sc2[:, None]) + sh2[:, None]
        y = conv_ffn(h, w_up[l], ffn_conv_w[l], w_down[l])
        x = x + gt2[:, None] * rms_norm(y, g_post_ffn[l])
    return x


import jax as _jax
import jax.numpy as _jnp

TWIN_FORMAT = 'train_step'
FWD_PARAMS = ['x', 'c', 'w_mod', 'b_mod', 'g_pre_mix', 'g_post_mix', 'g_pre_ffn', 'g_post_ffn', 'w_in', 'ssm_log_dt', 'ssm_a_re', 'ssm_a_im', 'ssm_b_re', 'ssm_b_im', 'ssm_c_re', 'ssm_c_im', 'ssm_d', 'w_glu', 'b_glu', 'conv_mix_w', 'w_ssm_out', 'w_attn_out', 'w_conv_out', 'b_gate', 'w_o', 'w_up', 'ffn_conv_w', 'w_down']
TWIN_WEIGHTS = ['w_mod', 'b_mod', 'g_pre_mix', 'g_post_mix', 'g_pre_ffn', 'g_post_ffn', 'w_in', 'ssm_log_dt', 'ssm_a_re', 'ssm_a_im', 'ssm_b_re', 'ssm_b_im', 'ssm_c_re', 'ssm_c_im', 'ssm_d', 'w_glu', 'b_glu', 'conv_mix_w', 'w_ssm_out', 'w_attn_out', 'w_conv_out', 'b_gate', 'w_o', 'w_up', 'ffn_conv_w', 'w_down']
TWIN_DIFF_INPUT = 'x'
TWIN_INPUTS = ['x', 'c', 'w_mod', 'b_mod', 'g_pre_mix', 'g_post_mix', 'g_pre_ffn', 'g_post_ffn', 'w_in', 'ssm_log_dt', 'ssm_a_re', 'ssm_a_im', 'ssm_b_re', 'ssm_b_im', 'ssm_c_re', 'ssm_c_im', 'ssm_d', 'w_glu', 'b_glu', 'conv_mix_w', 'w_ssm_out', 'w_attn_out', 'w_conv_out', 'b_gate', 'w_o', 'w_up', 'ffn_conv_w', 'w_down', 'loss_target', 'm_w_mod', 'm_b_mod', 'm_g_pre_mix', 'm_g_post_mix', 'm_g_pre_ffn', 'm_g_post_ffn', 'm_w_in', 'm_ssm_log_dt', 'm_ssm_a_re', 'm_ssm_a_im', 'm_ssm_b_re', 'm_ssm_b_im', 'm_ssm_c_re', 'm_ssm_c_im', 'm_ssm_d', 'm_w_glu', 'm_b_glu', 'm_conv_mix_w', 'm_w_ssm_out', 'm_w_attn_out', 'm_w_conv_out', 'm_b_gate', 'm_w_o', 'm_w_up', 'm_ffn_conv_w', 'm_w_down', 'v_w_mod', 'v_b_mod', 'v_g_pre_mix', 'v_g_post_mix', 'v_g_pre_ffn', 'v_g_post_ffn', 'v_w_in', 'v_ssm_log_dt', 'v_ssm_a_re', 'v_ssm_a_im', 'v_ssm_b_re', 'v_ssm_b_im', 'v_ssm_c_re', 'v_ssm_c_im', 'v_ssm_d', 'v_w_glu', 'v_b_glu', 'v_conv_mix_w', 'v_w_ssm_out', 'v_w_attn_out', 'v_w_conv_out', 'v_b_gate', 'v_w_o', 'v_w_up', 'v_ffn_conv_w', 'v_w_down']
TWIN_OUTPUTS = ['loss', 'grad_x', 'grad_w_mod', 'grad_b_mod', 'grad_g_pre_mix', 'grad_g_post_mix', 'grad_g_pre_ffn', 'grad_g_post_ffn', 'grad_w_in', 'grad_ssm_log_dt', 'grad_ssm_a_re', 'grad_ssm_a_im', 'grad_ssm_b_re', 'grad_ssm_b_im', 'grad_ssm_c_re', 'grad_ssm_c_im', 'grad_ssm_d', 'grad_w_glu', 'grad_b_glu', 'grad_conv_mix_w', 'grad_w_ssm_out', 'grad_w_attn_out', 'grad_w_conv_out', 'grad_b_gate', 'grad_w_o', 'grad_w_up', 'grad_ffn_conv_w', 'grad_w_down', 'delta_w_mod', 'delta_b_mod', 'delta_g_pre_mix', 'delta_g_post_mix', 'delta_g_pre_ffn', 'delta_g_post_ffn', 'delta_w_in', 'delta_ssm_log_dt', 'delta_ssm_a_re', 'delta_ssm_a_im', 'delta_ssm_b_re', 'delta_ssm_b_im', 'delta_ssm_c_re', 'delta_ssm_c_im', 'delta_ssm_d', 'delta_w_glu', 'delta_b_glu', 'delta_conv_mix_w', 'delta_w_ssm_out', 'delta_w_attn_out', 'delta_w_conv_out', 'delta_b_gate', 'delta_w_o', 'delta_w_up', 'delta_ffn_conv_w', 'delta_w_down', 'new_m_w_mod', 'new_m_b_mod', 'new_m_g_pre_mix', 'new_m_g_post_mix', 'new_m_g_pre_ffn', 'new_m_g_post_ffn', 'new_m_w_in', 'new_m_ssm_log_dt', 'new_m_ssm_a_re', 'new_m_ssm_a_im', 'new_m_ssm_b_re', 'new_m_ssm_b_im', 'new_m_ssm_c_re', 'new_m_ssm_c_im', 'new_m_ssm_d', 'new_m_w_glu', 'new_m_b_glu', 'new_m_conv_mix_w', 'new_m_w_ssm_out', 'new_m_w_attn_out', 'new_m_w_conv_out', 'new_m_b_gate', 'new_m_w_o', 'new_m_w_up', 'new_m_ffn_conv_w', 'new_m_w_down', 'new_v_w_mod', 'new_v_b_mod', 'new_v_g_pre_mix', 'new_v_g_post_mix', 'new_v_g_pre_ffn', 'new_v_g_post_ffn', 'new_v_w_in', 'new_v_ssm_log_dt', 'new_v_ssm_a_re', 'new_v_ssm_a_im', 'new_v_ssm_b_re', 'new_v_ssm_b_im', 'new_v_ssm_c_re', 'new_v_ssm_c_im', 'new_v_ssm_d', 'new_v_w_glu', 'new_v_b_glu', 'new_v_conv_mix_w', 'new_v_w_ssm_out', 'new_v_w_attn_out', 'new_v_w_conv_out', 'new_v_b_gate', 'new_v_w_o', 'new_v_w_up', 'new_v_ffn_conv_w', 'new_v_w_down']
TWIN_LEAF_KINDS = {'loss': 'loss', 'grad_x': 'grad_x', 'grad_w_mod': 'grad_w', 'grad_b_mod': 'grad_w', 'grad_g_pre_mix': 'grad_w', 'grad_g_post_mix': 'grad_w', 'grad_g_pre_ffn': 'grad_w', 'grad_g_post_ffn': 'grad_w', 'grad_w_in': 'grad_w', 'grad_ssm_log_dt': 'grad_w', 'grad_ssm_a_re': 'grad_w', 'grad_ssm_a_im': 'grad_w', 'grad_ssm_b_re': 'grad_w', 'grad_ssm_b_im': 'grad_w', 'grad_ssm_c_re': 'grad_w', 'grad_ssm_c_im': 'grad_w', 'grad_ssm_d': 'grad_w', 'grad_w_glu': 'grad_w', 'grad_b_glu': 'grad_w', 'grad_conv_mix_w': 'grad_w', 'grad_w_ssm_out': 'grad_w', 'grad_w_attn_out': 'grad_w', 'grad_w_conv_out': 'grad_w', 'grad_b_gate': 'grad_w', 'grad_w_o': 'grad_w', 'grad_w_up': 'grad_w', 'grad_ffn_conv_w': 'grad_w', 'grad_w_down': 'grad_w', 'delta_w_mod': 'delta_w', 'delta_b_mod': 'delta_w', 'delta_g_pre_mix': 'delta_w', 'delta_g_post_mix': 'delta_w', 'delta_g_pre_ffn': 'delta_w', 'delta_g_post_ffn': 'delta_w', 'delta_w_in': 'delta_w', 'delta_ssm_log_dt': 'delta_w', 'delta_ssm_a_re': 'delta_w', 'delta_ssm_a_im': 'delta_w', 'delta_ssm_b_re': 'delta_w', 'delta_ssm_b_im': 'delta_w', 'delta_ssm_c_re': 'delta_w', 'delta_ssm_c_im': 'delta_w', 'delta_ssm_d': 'delta_w', 'delta_w_glu': 'delta_w', 'delta_b_glu': 'delta_w', 'delta_conv_mix_w': 'delta_w', 'delta_w_ssm_out': 'delta_w', 'delta_w_attn_out': 'delta_w', 'delta_w_conv_out': 'delta_w', 'delta_b_gate': 'delta_w', 'delta_w_o': 'delta_w', 'delta_w_up': 'delta_w', 'delta_ffn_conv_w': 'delta_w', 'delta_w_down': 'delta_w', 'new_m_w_mod': 'new_m', 'new_m_b_mod': 'new_m', 'new_m_g_pre_mix': 'new_m', 'new_m_g_post_mix': 'new_m', 'new_m_g_pre_ffn': 'new_m', 'new_m_g_post_ffn': 'new_m', 'new_m_w_in': 'new_m', 'new_m_ssm_log_dt': 'new_m', 'new_m_ssm_a_re': 'new_m', 'new_m_ssm_a_im': 'new_m', 'new_m_ssm_b_re': 'new_m', 'new_m_ssm_b_im': 'new_m', 'new_m_ssm_c_re': 'new_m', 'new_m_ssm_c_im': 'new_m', 'new_m_ssm_d': 'new_m', 'new_m_w_glu': 'new_m', 'new_m_b_glu': 'new_m', 'new_m_conv_mix_w': 'new_m', 'new_m_w_ssm_out': 'new_m', 'new_m_w_attn_out': 'new_m', 'new_m_w_conv_out': 'new_m', 'new_m_b_gate': 'new_m', 'new_m_w_o': 'new_m', 'new_m_w_up': 'new_m', 'new_m_ffn_conv_w': 'new_m', 'new_m_w_down': 'new_m', 'new_v_w_mod': 'new_v', 'new_v_b_mod': 'new_v', 'new_v_g_pre_mix': 'new_v', 'new_v_g_post_mix': 'new_v', 'new_v_g_pre_ffn': 'new_v', 'new_v_g_post_ffn': 'new_v', 'new_v_w_in': 'new_v', 'new_v_ssm_log_dt': 'new_v', 'new_v_ssm_a_re': 'new_v', 'new_v_ssm_a_im': 'new_v', 'new_v_ssm_b_re': 'new_v', 'new_v_ssm_b_im': 'new_v', 'new_v_ssm_c_re': 'new_v', 'new_v_ssm_c_im': 'new_v', 'new_v_ssm_d': 'new_v', 'new_v_w_glu': 'new_v', 'new_v_b_glu': 'new_v', 'new_v_conv_mix_w': 'new_v', 'new_v_w_ssm_out': 'new_v', 'new_v_w_attn_out': 'new_v', 'new_v_w_conv_out': 'new_v', 'new_v_b_gate': 'new_v', 'new_v_w_o': 'new_v', 'new_v_w_up': 'new_v', 'new_v_ffn_conv_w': 'new_v', 'new_v_w_down': 'new_v'}


def _forward(args):
    return _fwd_reference(*[args[k] for k in FWD_PARAMS])


def _output_shape():
    out = _jax.eval_shape(lambda: _forward(_fwd_setup_inputs(0)))
    return out.shape, out.dtype

N_MICROBATCH = 1
ADAM_LR = 0.001
ADAM_B1 = 0.9
ADAM_B2 = 0.999
ADAM_EPS = 1e-08
ADAM_WD = 0.01
ADAM_STEP = 10
PER_EXAMPLE_BATCH_AXIS = {'x': 0, 'c': 0, 'loss_target': 0}
SHARED_INPUTS = []
_WEIGHT_DTYPES = {'w_mod': _jnp.float32, 'b_mod': _jnp.float32, 'g_pre_mix': _jnp.float32, 'g_post_mix': _jnp.float32, 'g_pre_ffn': _jnp.float32, 'g_post_ffn': _jnp.float32, 'w_in': _jnp.float32, 'ssm_log_dt': _jnp.float32, 'ssm_a_re': _jnp.float32, 'ssm_a_im': _jnp.float32, 'ssm_b_re': _jnp.float32, 'ssm_b_im': _jnp.float32, 'ssm_c_re': _jnp.float32, 'ssm_c_im': _jnp.float32, 'ssm_d': _jnp.float32, 'w_glu': _jnp.float32, 'b_glu': _jnp.float32, 'conv_mix_w': _jnp.float32, 'w_ssm_out': _jnp.float32, 'w_attn_out': _jnp.float32, 'w_conv_out': _jnp.float32, 'b_gate': _jnp.float32, 'w_o': _jnp.float32, 'w_up': _jnp.float32, 'ffn_conv_w': _jnp.float32, 'w_down': _jnp.float32}
MOMENT_SCALE = {'w_mod': 4.099062e-01, 'b_mod': 7.701001e-01, 'g_pre_mix': 7.177277e-02, 'g_post_mix': 8.899588e-01, 'g_pre_ffn': 5.561354e-02, 'g_post_ffn': 8.896191e-01, 'w_in': 3.072885e-02, 'ssm_log_dt': 4.588915e+00, 'ssm_a_re': 2.089046e-02, 'ssm_a_im': 1.952303e-02, 'ssm_b_re': 9.899313e-03, 'ssm_b_im': 1.055826e-02, 'ssm_c_re': 3.578746e-03, 'ssm_c_im': 3.659986e-03, 'ssm_d': 5.440365e-02, 'w_glu': 1.330258e-02, 'b_glu': 2.683394e-02, 'conv_mix_w': 8.346887e-02, 'w_ssm_out': 2.707810e-02, 'w_attn_out': 2.106223e-02, 'w_conv_out': 4.128435e-02, 'b_gate': 1.140838e-02, 'w_o': 5.310188e-02, 'w_up': 2.623386e-02, 'ffn_conv_w': 2.687679e-02, 'w_down': 4.528861e-02}


def _to_microbatches(a, axis):
    t = _jnp.moveaxis(a, axis, 0)
    t = t.reshape((N_MICROBATCH, t.shape[0] // N_MICROBATCH) + t.shape[1:])
    return _jnp.moveaxis(t, 1, axis + 1)


def setup_inputs(seed: int = 0) -> dict:
    inp = _fwd_setup_inputs(seed)
    key = _jax.random.fold_in(_jax.random.key(seed), 7919)
    shape, _ = _output_shape()
    out = dict(inp)
    out["loss_target"] = _jax.random.normal(_jax.random.fold_in(key, 0), shape, _jnp.float32)
    for i, name in enumerate(TWIN_WEIGHTS):
        w = inp[name].astype(_jnp.float32)
        if MOMENT_SCALE is None:
            s = _jnp.sqrt(_jnp.mean(_jnp.square(w)) + 1e-30)
        else:
            s = MOMENT_SCALE[name]
        km, kv = _jax.random.split(_jax.random.fold_in(key, i + 1))
        out[name] = w
        out["m_" + name] = s * _jax.random.normal(km, w.shape, _jnp.float32)
        out["v_" + name] = (s * s) * _jax.random.uniform(kv, w.shape, _jnp.float32, 0.5, 1.5)
    if N_MICROBATCH > 1:
        for name, axis in PER_EXAMPLE_BATCH_AXIS.items():
            out[name] = _to_microbatches(out[name], axis)
    return {'x': out['x'], 'c': out['c'], 'w_mod': out['w_mod'], 'b_mod': out['b_mod'], 'g_pre_mix': out['g_pre_mix'], 'g_post_mix': out['g_post_mix'], 'g_pre_ffn': out['g_pre_ffn'], 'g_post_ffn': out['g_post_ffn'], 'w_in': out['w_in'], 'ssm_log_dt': out['ssm_log_dt'], 'ssm_a_re': out['ssm_a_re'], 'ssm_a_im': out['ssm_a_im'], 'ssm_b_re': out['ssm_b_re'], 'ssm_b_im': out['ssm_b_im'], 'ssm_c_re': out['ssm_c_re'], 'ssm_c_im': out['ssm_c_im'], 'ssm_d': out['ssm_d'], 'w_glu': out['w_glu'], 'b_glu': out['b_glu'], 'conv_mix_w': out['conv_mix_w'], 'w_ssm_out': out['w_ssm_out'], 'w_attn_out': out['w_attn_out'], 'w_conv_out': out['w_conv_out'], 'b_gate': out['b_gate'], 'w_o': out['w_o'], 'w_up': out['w_up'], 'ffn_conv_w': out['ffn_conv_w'], 'w_down': out['w_down'], 'loss_target': out['loss_target'], 'm_w_mod': out['m_w_mod'], 'm_b_mod': out['m_b_mod'], 'm_g_pre_mix': out['m_g_pre_mix'], 'm_g_post_mix': out['m_g_post_mix'], 'm_g_pre_ffn': out['m_g_pre_ffn'], 'm_g_post_ffn': out['m_g_post_ffn'], 'm_w_in': out['m_w_in'], 'm_ssm_log_dt': out['m_ssm_log_dt'], 'm_ssm_a_re': out['m_ssm_a_re'], 'm_ssm_a_im': out['m_ssm_a_im'], 'm_ssm_b_re': out['m_ssm_b_re'], 'm_ssm_b_im': out['m_ssm_b_im'], 'm_ssm_c_re': out['m_ssm_c_re'], 'm_ssm_c_im': out['m_ssm_c_im'], 'm_ssm_d': out['m_ssm_d'], 'm_w_glu': out['m_w_glu'], 'm_b_glu': out['m_b_glu'], 'm_conv_mix_w': out['m_conv_mix_w'], 'm_w_ssm_out': out['m_w_ssm_out'], 'm_w_attn_out': out['m_w_attn_out'], 'm_w_conv_out': out['m_w_conv_out'], 'm_b_gate': out['m_b_gate'], 'm_w_o': out['m_w_o'], 'm_w_up': out['m_w_up'], 'm_ffn_conv_w': out['m_ffn_conv_w'], 'm_w_down': out['m_w_down'], 'v_w_mod': out['v_w_mod'], 'v_b_mod': out['v_b_mod'], 'v_g_pre_mix': out['v_g_pre_mix'], 'v_g_post_mix': out['v_g_post_mix'], 'v_g_pre_ffn': out['v_g_pre_ffn'], 'v_g_post_ffn': out['v_g_post_ffn'], 'v_w_in': out['v_w_in'], 'v_ssm_log_dt': out['v_ssm_log_dt'], 'v_ssm_a_re': out['v_ssm_a_re'], 'v_ssm_a_im': out['v_ssm_a_im'], 'v_ssm_b_re': out['v_ssm_b_re'], 'v_ssm_b_im': out['v_ssm_b_im'], 'v_ssm_c_re': out['v_ssm_c_re'], 'v_ssm_c_im': out['v_ssm_c_im'], 'v_ssm_d': out['v_ssm_d'], 'v_w_glu': out['v_w_glu'], 'v_b_glu': out['v_b_glu'], 'v_conv_mix_w': out['v_conv_mix_w'], 'v_w_ssm_out': out['v_w_ssm_out'], 'v_w_attn_out': out['v_w_attn_out'], 'v_w_conv_out': out['v_w_conv_out'], 'v_b_gate': out['v_b_gate'], 'v_w_o': out['v_w_o'], 'v_w_up': out['v_w_up'], 'v_ffn_conv_w': out['v_ffn_conv_w'], 'v_w_down': out['v_w_down']}


def _loss(weights, diff, rest, loss_target):
    with _jax.named_scope("forward"):
        args = {**rest, TWIN_DIFF_INPUT: diff, **{k: w.astype(_WEIGHT_DTYPES[k]) for k, w in weights.items()}}
        y = _forward(args)
    with _jax.named_scope("loss_head"):
        err = _jnp.square(y.astype(_jnp.float32) - loss_target)
        return 0.5 * _jnp.sum(_jnp.mean(err, axis=-1)) if err.ndim else 0.5 * err


def _adamw(w, g, m, v):
    m = ADAM_B1 * m + (1.0 - ADAM_B1) * g
    v = ADAM_B2 * v + (1.0 - ADAM_B2) * _jnp.square(g)
    m_hat = m / (1.0 - ADAM_B1 ** ADAM_STEP)
    v_hat = v / (1.0 - ADAM_B2 ** ADAM_STEP)
    delta = -ADAM_LR * (m_hat / (_jnp.sqrt(v_hat) + ADAM_EPS) + ADAM_WD * w)
    return delta, m, v


def reference(x, c, w_mod, b_mod, g_pre_mix, g_post_mix, g_pre_ffn, g_post_ffn, w_in, ssm_log_dt, ssm_a_re, ssm_a_im, ssm_b_re, ssm_b_im, ssm_c_re, ssm_c_im, ssm_d, w_glu, b_glu, conv_mix_w, w_ssm_out, w_attn_out, w_conv_out, b_gate, w_o, w_up, ffn_conv_w, w_down, loss_target, m_w_mod, m_b_mod, m_g_pre_mix, m_g_post_mix, m_g_pre_ffn, m_g_post_ffn, m_w_in, m_ssm_log_dt, m_ssm_a_re, m_ssm_a_im, m_ssm_b_re, m_ssm_b_im, m_ssm_c_re, m_ssm_c_im, m_ssm_d, m_w_glu, m_b_glu, m_conv_mix_w, m_w_ssm_out, m_w_attn_out, m_w_conv_out, m_b_gate, m_w_o, m_w_up, m_ffn_conv_w, m_w_down, v_w_mod, v_b_mod, v_g_pre_mix, v_g_post_mix, v_g_pre_ffn, v_g_post_ffn, v_w_in, v_ssm_log_dt, v_ssm_a_re, v_ssm_a_im, v_ssm_b_re, v_ssm_b_im, v_ssm_c_re, v_ssm_c_im, v_ssm_d, v_w_glu, v_b_glu, v_conv_mix_w, v_w_ssm_out, v_w_attn_out, v_w_conv_out, v_b_gate, v_w_o, v_w_up, v_ffn_conv_w, v_w_down):
    given = dict(x=x, c=c, w_mod=w_mod, b_mod=b_mod, g_pre_mix=g_pre_mix, g_post_mix=g_post_mix, g_pre_ffn=g_pre_ffn, g_post_ffn=g_post_ffn, w_in=w_in, ssm_log_dt=ssm_log_dt, ssm_a_re=ssm_a_re, ssm_a_im=ssm_a_im, ssm_b_re=ssm_b_re, ssm_b_im=ssm_b_im, ssm_c_re=ssm_c_re, ssm_c_im=ssm_c_im, ssm_d=ssm_d, w_glu=w_glu, b_glu=b_glu, conv_mix_w=conv_mix_w, w_ssm_out=w_ssm_out, w_attn_out=w_attn_out, w_conv_out=w_conv_out, b_gate=b_gate, w_o=w_o, w_up=w_up, ffn_conv_w=ffn_conv_w, w_down=w_down, loss_target=loss_target, m_w_mod=m_w_mod, m_b_mod=m_b_mod, m_g_pre_mix=m_g_pre_mix, m_g_post_mix=m_g_post_mix, m_g_pre_ffn=m_g_pre_ffn, m_g_post_ffn=m_g_post_ffn, m_w_in=m_w_in, m_ssm_log_dt=m_ssm_log_dt, m_ssm_a_re=m_ssm_a_re, m_ssm_a_im=m_ssm_a_im, m_ssm_b_re=m_ssm_b_re, m_ssm_b_im=m_ssm_b_im, m_ssm_c_re=m_ssm_c_re, m_ssm_c_im=m_ssm_c_im, m_ssm_d=m_ssm_d, m_w_glu=m_w_glu, m_b_glu=m_b_glu, m_conv_mix_w=m_conv_mix_w, m_w_ssm_out=m_w_ssm_out, m_w_attn_out=m_w_attn_out, m_w_conv_out=m_w_conv_out, m_b_gate=m_b_gate, m_w_o=m_w_o, m_w_up=m_w_up, m_ffn_conv_w=m_ffn_conv_w, m_w_down=m_w_down, v_w_mod=v_w_mod, v_b_mod=v_b_mod, v_g_pre_mix=v_g_pre_mix, v_g_post_mix=v_g_post_mix, v_g_pre_ffn=v_g_pre_ffn, v_g_post_ffn=v_g_post_ffn, v_w_in=v_w_in, v_ssm_log_dt=v_ssm_log_dt, v_ssm_a_re=v_ssm_a_re, v_ssm_a_im=v_ssm_a_im, v_ssm_b_re=v_ssm_b_re, v_ssm_b_im=v_ssm_b_im, v_ssm_c_re=v_ssm_c_re, v_ssm_c_im=v_ssm_c_im, v_ssm_d=v_ssm_d, v_w_glu=v_w_glu, v_b_glu=v_b_glu, v_conv_mix_w=v_conv_mix_w, v_w_ssm_out=v_w_ssm_out, v_w_attn_out=v_w_attn_out, v_w_conv_out=v_w_conv_out, v_b_gate=v_b_gate, v_w_o=v_w_o, v_w_up=v_w_up, v_ffn_conv_w=v_ffn_conv_w, v_w_down=v_w_down)
    weights = {n: given[n] for n in TWIN_WEIGHTS}
    shared = {n: given[n] for n in SHARED_INPUTS}
    per_example = {n: given[n] for n in ['x', 'c']}
    grad_fn = _jax.value_and_grad(_loss, argnums=(0, 1))

    def one_microbatch(ex, loss_target):
        ex = dict(ex)
        diff = ex.pop(TWIN_DIFF_INPUT)
        return grad_fn(weights, diff, {**shared, **ex}, loss_target)

    if N_MICROBATCH == 1:
        loss, (grad_w, grad_x) = one_microbatch(per_example, given["loss_target"])
    else:
        def body(carry, xs):
            loss_sum, grad_sum = carry
            l_k, (gw_k, gx_k) = one_microbatch(xs[0], xs[1])
            with _jax.named_scope("update"):
                return (loss_sum + l_k, _jax.tree.map(_jnp.add, grad_sum, gw_k)), gx_k

        init = (_jnp.zeros((), _jnp.float32), _jax.tree.map(_jnp.zeros_like, weights))
        (loss, grad_w), grad_x = _jax.lax.scan(body, init, (per_example, given["loss_target"]))
    with _jax.named_scope("update"):
        delta_w, new_m, new_v = {}, {}, {}
        for n in TWIN_WEIGHTS:
            delta_w[n], new_m[n], new_v[n] = _adamw(weights[n], grad_w[n], given["m_" + n], given["v_" + n])
    return (loss, grad_x, *[grad_w[n] for n in TWIN_WEIGHTS], *[delta_w[n] for n in TWIN_WEIGHTS],
            *[new_m[n] for n in TWIN_WEIGHTS], *[new_v[n] for n in TWIN_WEIGHTS])
```

```python
import functools
import math

import numpy as np
import jax
import jax.numpy as jnp
from jax import lax
from jax.experimental import pallas as pl
from jax.experimental.pallas import tpu as pltpu

F32 = jnp.float32
BF16 = jnp.bfloat16
MESH = pl.DeviceIdType.MESH
ANY = pl.BlockSpec(memory_space=pl.ANY)

VMEM_LIMIT_BYTES = 48 * 1024 * 1024
LANE = 128
SUBLANE = 8

RMS_EPS = 1e-6
NEG_INF = -1e30
SSM_GROUP = 16
SSM_STATE = 64
HEAD_DIM = 64
DSWA_PATTERNS = ((128, 1), (512, 4), (2048, 16))
ATTN_BLOCK = 128
N_DEV = 8
N_CHIP = 4

ADAM_LR = 0.001
ADAM_B1 = 0.9
ADAM_B2 = 0.999
ADAM_EPS = 1e-08
ADAM_WD = 0.01
ADAM_STEP = 10


def _cp(sem=None):
    return pltpu.CompilerParams(dimension_semantics=sem, vmem_limit_bytes=VMEM_LIMIT_BYTES)


def _tile(n, pref, align=LANE):
    if n <= pref:
        return n
    t = (pref // align) * align
    while t >= align:
        if n % t == 0:
            return t
        t -= align
    return n


def _sds(shape, dtype):
    return jax.ShapeDtypeStruct(tuple(shape), dtype)


def _mm_nn(a, w, layer, *, name, k_dim=None, a_col0=0, out_dtype=F32, a_fn=None):
    m = a.shape[0]
    _, nb, kw, n = w.shape
    k_dim = kw if k_dim is None else k_dim
    assert k_dim == kw
    tm = _tile(m, 512, SUBLANE)
    tk = _tile(k_dim, 512)
    tn = _tile(n, 1408)
    assert a_col0 % tk == 0
    npb = n // tn
    nk = k_dim // tk
    a0 = a_col0 // tk

    def body(a_ref, w_ref, o_ref, acc_ref):
        k = pl.program_id(2)

        @pl.when(k == 0)
        def _():
            acc_ref[...] = jnp.zeros_like(acc_ref)

        av = a_ref[...]
        if a_fn is not None:
            av = a_fn(av.astype(F32))
        acc_ref[...] += jnp.dot(av.astype(BF16), w_ref[...].astype(BF16), preferred_element_type=F32)

        @pl.when(k == nk - 1)
        def _():
            o_ref[...] = acc_ref[...].astype(out_dtype)

    return pl.pallas_call(
        body,
        out_shape=_sds((m, nb * n), out_dtype),
        grid=(m // tm, nb * npb, nk),
        in_specs=[pl.BlockSpec((tm, tk), lambda i, j, k: (i, a0 + k)),
                  pl.BlockSpec((None, None, tk, tn), lambda i, j, k: (layer, j // npb, k, j % npb))],
        out_specs=pl.BlockSpec((tm, tn), lambda i, j, k: (i, j)),
        scratch_shapes=[pltpu.VMEM((tm, tn), F32)],
        compiler_params=_cp(("parallel", "parallel", "arbitrary")),
        name=name,
    )(a, w)


def _mm_nt(g, w, layer, *, name, out_dtype=F32):
    m = g.shape[0]
    _, nb, k_dim, n = w.shape
    assert g.shape[1] == nb * n
    tm = _tile(m, 512, SUBLANE)
    tko = _tile(k_dim, 1024)
    tc = _tile(n, 512)
    npb = n // tc
    nr = nb * npb

    def body(g_ref, w_ref, o_ref, acc_ref):
        r = pl.program_id(2)

        @pl.when(r == 0)
        def _():
            acc_ref[...] = jnp.zeros_like(acc_ref)

        acc_ref[...] += lax.dot_general(g_ref[...].astype(BF16), w_ref[...].astype(BF16),
                                        (((1,), (1,)), ((), ())), preferred_element_type=F32)

        @pl.when(r == nr - 1)
        def _():
            o_ref[...] = acc_ref[...].astype(out_dtype)

    return pl.pallas_call(
        body,
        out_shape=_sds((m, k_dim), out_dtype),
        grid=(m // tm, k_dim // tko, nr),
        in_specs=[pl.BlockSpec((tm, tc), lambda i, kk, r: (i, r)),
                  pl.BlockSpec((None, None, tko, tc), lambda i, kk, r: (layer, r // npb, kk, r % npb))],
        out_specs=pl.BlockSpec((tm, tko), lambda i, kk, r: (i, kk)),
        scratch_shapes=[pltpu.VMEM((tm, tko), F32)],
        compiler_params=_cp(("parallel", "parallel", "arbitrary")),
        name=name,
    )(g, w)


def _mm_tn(a, g, out_buf, layer, *, name, a_col0=0):
    m = a.shape[0]
    _, nb, k_dim, n = out_buf.shape
    assert g.shape == (m, nb * n)
    tm = _tile(m, 512, SUBLANE)
    tk = _tile(k_dim, 512)
    tn = _tile(n, 1408)
    assert a_col0 % tk == 0
    a0 = a_col0 // tk
    npb = n // tn
    nr = m // tm

    def body(a_ref, g_ref, buf_ref, o_ref, acc_ref):
        del buf_ref
        r = pl.program_id(2)

        @pl.when(r == 0)
        def _():
            acc_ref[...] = jnp.zeros_like(acc_ref)

        acc_ref[...] += lax.dot_general(a_ref[...].astype(BF16), g_ref[...].astype(BF16),
                                        (((0,), (0,)), ((), ())), preferred_element_type=F32)

        @pl.when(r == nr - 1)
        def _():
            o_ref[...] = acc_ref[...]

    return pl.pallas_call(
        body,
        out_shape=_sds(out_buf.shape, F32),
        grid=(k_dim // tk, nb * npb, nr),
        in_specs=[pl.BlockSpec((tm, tk), lambda kk, j, r: (r, a0 + kk)),
                  pl.BlockSpec((tm, tn), lambda kk, j, r: (r, j)),
                  ANY],
        out_specs=pl.BlockSpec((None, None, tk, tn), lambda kk, j, r: (layer, j // npb, kk, j % npb)),
        scratch_shapes=[pltpu.VMEM((tk, tn), F32)],
        input_output_aliases={2: 0},
        compiler_params=_cp(("parallel", "parallel", "arbitrary")),
        name=name,
    )(a, g, out_buf)


def _ew_fwd(fn, xs, ps, out_dtypes, *, name, width, tw=None, tm=256):
    rows = xs[0][0].shape[0]
    tm = _tile(rows, tm, SUBLANE)
    tw = width if tw is None else tw
    nx, n_p = len(xs), len(ps)

    def body(*refs):
        xv = [r[...].astype(F32) for r in refs[:nx]]
        pv = [r[...].astype(F32) for r in refs[nx:nx + n_p]]
        outs = fn(*xv, *pv)
        if not isinstance(outs, (tuple, list)):
            outs = (outs,)
        for o_ref, o in zip(refs[nx + n_p:], outs):
            o_ref[...] = o.astype(o_ref.dtype)

    in_specs = []
    for arr, c0, w in xs:
        assert w == width and c0 % tw == 0
        in_specs.append(pl.BlockSpec((tm, tw), functools.partial(lambda i, j, b: (i, b + j), b=c0 // tw)))
    for p in ps:
        assert p.shape == (1, width)
        in_specs.append(pl.BlockSpec((1, tw), lambda i, j: (0, j)))
    outs = pl.pallas_call(
        body,
        out_shape=[_sds((rows, width), d) for d in out_dtypes],
        grid=(rows // tm, width // tw),
        in_specs=in_specs,
        out_specs=[pl.BlockSpec((tm, tw), lambda i, j: (i, j)) for _ in out_dtypes],
        compiler_params=_cp(("parallel", "parallel")),
        name=name,
    )(*[x[0] for x in xs], *ps)
    return outs


def _ew_bwd(fn, xs, ps, cts, dx_dtypes, *, name, width, tw=None, tm=256, dx_add=None):
    rows = xs[0][0].shape[0]
    tm = _tile(rows, tm, SUBLANE)
    tw = width if tw is None else tw
    nx, n_p = len(xs), len(ps)
    dx_add = dx_add or {}
    flat_cts = [c for group in cts for c in group]
    add_keys = sorted(dx_add)
    n_in = nx + n_p + len(flat_cts) + len(add_keys)
    dx_idx = [i for i, d in enumerate(dx_dtypes) if d is not None]

    def body(*refs):
        i = pl.program_id(1)
        xv = [r[...].astype(F32) for r in refs[:nx]]
        pv = [r[...].astype(F32) for r in refs[nx:nx + n_p]]
        pos = nx + n_p
        ct_vals = []
        for group in cts:
            acc = refs[pos][...].astype(F32)
            pos += 1
            for _ in group[1:]:
                acc = acc + refs[pos][...].astype(F32)
                pos += 1
            ct_vals.append(acc)
        add_vals = {}
        for key in add_keys:
            add_vals[key] = refs[pos][...].astype(F32)
            pos += 1
        out_refs = refs[n_in:]
        outs, vjp = jax.vjp(fn, *xv, *pv)
        grads = vjp(tuple(ct_vals) if isinstance(outs, (tuple, list)) else ct_vals[0])
        o = 0
        for idx in dx_idx:
            gval = grads[idx]
            if idx in add_vals:
                gval = gval + add_vals[idx]
            out_refs[o][...] = gval.astype(out_refs[o].dtype)
            o += 1
        for q in range(n_p):
            gp = grads[nx + q]
            ref = out_refs[o + q]

            @pl.when(i == 0)
            def _(ref=ref, gp=gp):
                ref[...] = gp

            @pl.when(i > 0)
            def _(ref=ref, gp=gp):
                ref[...] += gp

    tile_spec = pl.BlockSpec((tm, tw), lambda j, i: (i, j))
    in_specs = []
    for arr, c0, w in xs:
        assert w == width and c0 % tw == 0
        in_specs.append(pl.BlockSpec((tm, tw), functools.partial(lambda j, i, b: (i, b + j), b=c0 // tw)))
    for p in ps:
        in_specs.append(pl.BlockSpec((1, tw), lambda j, i: (0, j)))
    in_specs += [tile_spec] * (len(flat_cts) + len(add_keys))
    out_shape = [_sds((rows, width), dx_dtypes[idx]) for idx in dx_idx] + [_sds((1, width), F32)] * n_p
    out_specs = [tile_spec] * len(dx_idx) + [pl.BlockSpec((1, tw), lambda j, i: (0, j))] * n_p
    outs = pl.pallas_call(
        body,
        out_shape=out_shape,
        grid=(width // tw, rows // tm),
        in_specs=in_specs,
        out_specs=out_specs,
        compiler_params=_cp(("parallel", "arbitrary")),
        name=name,
    )(*[x[0] for x in xs], *ps, *flat_cts, *[dx_add[k] for k in add_keys])
    return outs[:len(dx_idx)], outs[len(dx_idx):]


def _rms(x):
    return x * lax.rsqrt(jnp.mean(x * x, axis=-1, keepdims=True) + RMS_EPS)


def _fn_norm_mod(x, g, sc, sh):
    return (_rms(x) * g) * (1.0 + sc) + sh


def _fn_residual(x, y, gt, g):
    return x + gt * (_rms(y) * g)


def _fn_gelu(y, u, d):
    return jax.nn.gelu(y + d * u)


def _fn_glu(g, z, b):
    return g * jax.nn.sigmoid(z + b)


def _fn_gates(p0, p1, p2, ys, ya, yc, b0, b1, b2):
    return (jax.nn.sigmoid(p0 + b0) * ys + jax.nn.sigmoid(p1 + b1) * ya + jax.nn.sigmoid(p2 + b2) * yc)


def _fn_disc(log_dt, ar, ai, br_t, bi_t):
    dt = jnp.exp(log_dt)
    mag = jnp.exp(ar * dt)
    lr, li = mag * jnp.cos(ai * dt), mag * jnp.sin(ai * dt)
    den = ar * ar + ai * ai
    fr = ((lr - 1.0) * ar + li * ai) / den
    fi = (li * ar - (lr - 1.0) * ai) / den
    bbr = fr[None] * br_t - fi[None] * bi_t
    bbi = fr[None] * bi_t + fi[None] * br_t
    return lr, li, bbr, bbi


def _ssm_disc_fwd(log_dt, ar, ai, br_t, bi_t):
    g, n = ar.shape

    def body(ld_ref, ar_ref, ai_ref, br_ref, bi_ref, lr_ref, li_ref, bbr_ref, bbi_ref):
        lr, li, bbr, bbi = _fn_disc(ld_ref[...], ar_ref[...], ai_ref[...], br_ref[...], bi_ref[...])
        lr_ref[...] = lr
        li_ref[...] = li
        bbr_ref[...] = bbr
        bbi_ref[...] = bbi

    return pl.pallas_call(
        body,
        out_shape=[_sds((g, n), F32), _sds((g, n), F32), _sds(br_t.shape, F32), _sds(br_t.shape, F32)],
        compiler_params=_cp(),
        name="ssm_disc_fwd",
    )(log_dt, ar, ai, br_t, bi_t)


def _ssm_disc_bwd(log_dt, ar, ai, br_t, bi_t, dlr, dli, dbbr, dbbi):
    g, n = ar.shape

    def body(ld_ref, ar_ref, ai_ref, br_ref, bi_ref, dlr_ref, dli_ref, dbbr_ref, dbbi_ref,
             gld_ref, gar_ref, gai_ref, gbr_ref, gbi_ref):
        _, vjp = jax.vjp(_fn_disc, ld_ref[...], ar_ref[...], ai_ref[...], br_ref[...], bi_ref[...])
        gld, gar, gai, gbr, gbi = vjp((dlr_ref[...], dli_ref[...], dbbr_ref[...], dbbi_ref[...]))
        gld_ref[...] = gld
        gar_ref[...] = gar
        gai_ref[...] = gai
        gbr_ref[...] = gbr
        gbi_ref[...] = gbi

    return pl.pallas_call(
        body,
        out_shape=[_sds((g, 1), F32), _sds((g, n), F32), _sds((g, n), F32), _sds(br_t.shape, F32),
                   _sds(br_t.shape, F32)],
        compiler_params=_cp(),
        name="ssm_disc_bwd",
    )(log_dt, ar, ai, br_t, bi_t, dlr, dli, dbbr, dbbi)


def _cmul(ar, ai, br, bi):
    return ar * br - ai * bi, ar * bi + ai * br


def _scan_tables(lr, li, reverse):
    c = lr.shape[-1]
    p1 = (jnp.broadcast_to(lr, (SUBLANE, c)), jnp.broadcast_to(li, (SUBLANE, c)))
    p2 = _cmul(*p1, *p1)
    p4 = _cmul(*p2, *p2)
    p8 = _cmul(*p4, *p4)
    row = lax.broadcasted_iota(jnp.int32, (SUBLANE, c), 0)
    dist = (SUBLANE - row) if reverse else (row + 1)
    pr, pi = jnp.ones((SUBLANE, c), F32), jnp.zeros((SUBLANE, c), F32)
    for bit, pw in ((1, p1), (2, p2), (4, p4), (8, p8)):
        qr, qi = _cmul(pr, pi, *pw)
        take = (dist & bit) != 0
        pr, pi = jnp.where(take, qr, pr), jnp.where(take, qi, pi)
    return row, (p1, p2, p4), (pr, pi)


def _shift_rows(x, s, row, reverse):
    if reverse:
        return jnp.where(row < SUBLANE - s, pltpu.roll(x, SUBLANE - s, 0), 0.0)
    return jnp.where(row >= s, pltpu.roll(x, s, 0), 0.0)


def _scan_tile(xr, xi, carry, row, pows, carry_pow, reverse):
    for s, pw in zip((1, 2, 4), pows):
        sr, si = _shift_rows(xr, s, row, reverse), _shift_rows(xi, s, row, reverse)
        tr, ti = _cmul(*pw, sr, si)
        xr, xi = xr + tr, xi + ti
    tr, ti = _cmul(*carry_pow, *carry)
    hr, hi = xr + tr, xi + ti
    edge = 0 if reverse else SUBLANE - 1
    c = hr.shape[-1]
    new_carry = (jnp.broadcast_to(hr[edge:edge + 1, :], (SUBLANE, c)),
                 jnp.broadcast_to(hi[edge:edge + 1, :], (SUBLANE, c)))
    return hr, hi, new_carry


def _scan_cols(gn):
    return _tile(gn, 256)


def _ssm_scan_fwd(xcat, lam):
    rows, gn2 = xcat.shape
    c = _scan_cols(gn2 // 2)
    n_tiles = rows // SUBLANE

    def body(lam_ref, x_ref, h_ref):
        lr, li = lam_ref[:, :c], lam_ref[:, c:]
        row, pows, carry_pow = _scan_tables(lr, li, False)

        def step(k, carry):
            t0 = pl.multiple_of(k * SUBLANE, SUBLANE)
            hr, hi, carry = _scan_tile(x_ref[pl.ds(t0, SUBLANE), :c], x_ref[pl.ds(t0, SUBLANE), c:], carry,
                                       row, pows, carry_pow, False)
            h_ref[pl.ds(t0, SUBLANE), :c] = hr
            h_ref[pl.ds(t0, SUBLANE), c:] = hi
            return carry

        zero = jnp.zeros((SUBLANE, c), F32)
        lax.fori_loop(0, n_tiles, step, (zero, zero))

    return pl.pallas_call(
        body,
        out_shape=_sds((rows, gn2), F32),
        grid=(gn2 // (2 * c),),
        in_specs=[pl.BlockSpec((1, 2 * c), lambda j: (0, j)), pl.BlockSpec((rows, 2 * c), lambda j: (0, j))],
        out_specs=pl.BlockSpec((rows, 2 * c), lambda j: (0, j)),
        compiler_params=_cp(("parallel",)),
        name="ssm_scan_fwd",
    )(lam, xcat)


def _ssm_scan_bwd(dhcat, hcat, lam):
    rows, gn2 = dhcat.shape
    c = _scan_cols(gn2 // 2)
    n_tiles = rows // SUBLANE

    def body(lam_ref, dh_ref, h_ref, g_ref, dlam_ref):
        lr, li = lam_ref[:, :c], -lam_ref[:, c:]
        row, pows, carry_pow = _scan_tables(lr, li, True)

        def step(k, state):
            carry, acc_r, acc_i = state
            kk = n_tiles - 1 - k
            t0 = pl.multiple_of(kk * SUBLANE, SUBLANE)
            gr, gi, carry = _scan_tile(dh_ref[pl.ds(t0, SUBLANE), :c], dh_ref[pl.ds(t0, SUBLANE), c:], carry,
                                       row, pows, carry_pow, True)
            g_ref[pl.ds(t0, SUBLANE), :c] = gr
            g_ref[pl.ds(t0, SUBLANE), c:] = gi
            tp = pl.multiple_of(jnp.maximum(kk - 1, 0) * SUBLANE, SUBLANE)
            has_prev = (kk > 0).astype(F32)
            prev_r = pltpu.roll(h_ref[pl.ds(tp, SUBLANE), :c], 1, 0) * has_prev
            prev_i = pltpu.roll(h_ref[pl.ds(tp, SUBLANE), c:], 1, 0) * has_prev
            hpr = jnp.where(row >= 1, pltpu.roll(h_ref[pl.ds(t0, SUBLANE), :c], 1, 0), prev_r)
            hpi = jnp.where(row >= 1, pltpu.roll(h_ref[pl.ds(t0, SUBLANE), c:], 1, 0), prev_i)
            acc_r = acc_r + gr * hpr + gi * hpi
            acc_i = acc_i + gi * hpr - gr * hpi
            return carry, acc_r, acc_i

        zero = jnp.zeros((SUBLANE, c), F32)
        _, acc_r, acc_i = lax.fori_loop(0, n_tiles, step, ((zero, zero), zero, zero))
        dlam_ref[:, :c] = jnp.sum(acc_r, axis=0, keepdims=True)
        dlam_ref[:, c:] = jnp.sum(acc_i, axis=0, keepdims=True)

    blk = pl.BlockSpec((rows, 2 * c), lambda j: (0, j))
    return pl.pallas_call(
        body,
        out_shape=[_sds((rows, gn2), F32), _sds((1, gn2), F32)],
        grid=(gn2 // (2 * c),),
        in_specs=[pl.BlockSpec((1, 2 * c), lambda j: (0, j)), blk, blk],
        out_specs=[blk, pl.BlockSpec((1, 2 * c), lambda j: (0, j))],
        compiler_params=_cp(("parallel",)),
        name="ssm_scan_bwd",
    )(lam, dhcat, hcat)


def _shift_down(x, k, row):
    return x if k == 0 else jnp.where(row >= k, pltpu.roll(x, k, 0), 0.0)


def _shift_up(x, k, row):
    n = x.shape[0]
    return x if k == 0 else jnp.where(row < n - k, pltpu.roll(x, n - k, 0), 0.0)


def _taps(w_ref):
    return [w_ref[k:k + 1, :] for k in range(3)]


def _conv3(x, w, row):
    return sum(w[k] * _shift_down(x, k, row) for k in range(3))


def _conv3_bwd(x, w, dy, row):
    dx = sum(w[k] * _shift_up(dy, k, row) for k in range(3))
    dw = [jnp.sum(dy * _shift_down(x, k, row), axis=0, keepdims=True) for k in range(3)]
    return dx, dw


def _gconv_fwd(proj, off, cw, w):
    rows = proj.shape[0]
    tc = _tile(cw, LANE)
    nb = cw // tc

    def body(b_ref, c_ref, h_ref, w_ref, o_ref):
        row = lax.broadcasted_iota(jnp.int32, (rows, tc), 0)
        o_ref[...] = (b_ref[...] * _conv3(c_ref[...] * h_ref[...], _taps(w_ref), row)).astype(o_ref.dtype)

    specs = [pl.BlockSpec((rows, tc), functools.partial(lambda j, b: (0, b + j), b=(off + q * cw) // tc))
             for q in range(3)]
    return pl.pallas_call(
        body,
        out_shape=_sds((rows, cw), BF16),
        grid=(nb,),
        in_specs=specs + [pl.BlockSpec((3, tc), lambda j: (0, j))],
        out_specs=pl.BlockSpec((rows, tc), lambda j: (0, j)),
        compiler_params=_cp(("parallel",)),
        name="gconv_fwd",
    )(proj, proj, proj, w)


def _gconv_bwd(proj, off, cw, w, dy):
    rows = proj.shape[0]
    tc = _tile(cw, LANE)
    nb = cw // tc

    def body(b_ref, c_ref, h_ref, w_ref, dy_ref, db_ref, dc_ref, dh_ref, dw_ref):
        row = lax.broadcasted_iota(jnp.int32, (rows, tc), 0)
        cv, hv, dyv = c_ref[...], h_ref[...], dy_ref[...].astype(F32)
        t = cv * hv
        db_ref[...] = (dyv * _conv3(t, _taps(w_ref), row)).astype(db_ref.dtype)
        dt, dw = _conv3_bwd(t, _taps(w_ref), dyv * b_ref[...], row)
        dc_ref[...] = (dt * hv).astype(dc_ref.dtype)
        dh_ref[...] = (dt * cv).astype(dh_ref.dtype)
        for k in range(3):
            dw_ref[k:k + 1, :] = dw[k]

    specs = [pl.BlockSpec((rows, tc), functools.partial(lambda j, b: (0, b + j), b=(off + q * cw) // tc))
             for q in range(3)]
    col = pl.BlockSpec((rows, tc), lambda j: (0, j))
    wspec = pl.BlockSpec((3, tc), lambda j: (0, j))
    return pl.pallas_call(
        body,
        out_shape=[_sds((rows, cw), BF16)] * 3 + [_sds((3, cw), F32)],
        grid=(nb,),
        in_specs=specs + [wspec, col],
        out_specs=[col, col, col, wspec],
        compiler_params=_cp(("parallel",)),
        name="gconv_bwd",
    )(proj, proj, proj, w, dy)


def _ffn_act_fwd(up, w):
    rows, f2 = up.shape
    f = f2 // 2
    tc = _tile(f, LANE)
    nb = f // tc

    def body(a_ref, b_ref, wa_ref, wb_ref, o_ref):
        row = lax.broadcasted_iota(jnp.int32, (rows, tc), 0)
        a = _conv3(a_ref[...], _taps(wa_ref), row)
        b = _conv3(b_ref[...], _taps(wb_ref), row)
        o_ref[...] = (jax.nn.silu(a) * b).astype(o_ref.dtype)

    return pl.pallas_call(
        body,
        out_shape=_sds((rows, f), BF16),
        grid=(nb,),
        in_specs=[pl.BlockSpec((rows, tc), lambda j: (0, j)), pl.BlockSpec((rows, tc), lambda j: (0, nb + j)),
                  pl.BlockSpec((3, tc), lambda j: (0, j)), pl.BlockSpec((3, tc), lambda j: (0, nb + j))],
        out_specs=pl.BlockSpec((rows, tc), lambda j: (0, j)),
        compiler_params=_cp(("parallel",)),
        name="ffn_act_fwd",
    )(up, up, w, w)


def _ffn_act_bwd(up, w, dact):
    rows, f2 = up.shape
    f = f2 // 2
    tc = _tile(f, LANE)
    nb = f // tc

    def body(a_ref, b_ref, wa_ref, wb_ref, d_ref, da_ref, db_ref, dwa_ref, dwb_ref):
        row = lax.broadcasted_iota(jnp.int32, (rows, tc), 0)
        av, bv, dv = a_ref[...], b_ref[...], d_ref[...].astype(F32)
        ac = _conv3(av, _taps(wa_ref), row)
        bc = _conv3(bv, _taps(wb_ref), row)
        _, vjp = jax.vjp(lambda p, q: jax.nn.silu(p) * q, ac, bc)
        dac, dbc = vjp(dv)
        dxa, dwa = _conv3_bwd(av, _taps(wa_ref), dac, row)
        dxb, dwb = _conv3_bwd(bv, _taps(wb_ref), dbc, row)
        da_ref[...] = dxa.astype(da_ref.dtype)
        db_ref[...] = dxb.astype(db_ref.dtype)
        for k in range(3):
            dwa_ref[k:k + 1, :] = dwa[k]
            dwb_ref[k:k + 1, :] = dwb[k]

    col = pl.BlockSpec((rows, tc), lambda j: (0, j))
    wspec = pl.BlockSpec((3, tc), lambda j: (0, j))
    return pl.pallas_call(
        body,
        out_shape=[_sds((rows, f), BF16)] * 2 + [_sds((3, f), F32)] * 2,
        grid=(nb,),
        in_specs=[col, pl.BlockSpec((rows, tc), lambda j: (0, nb + j)), wspec,
                  pl.BlockSpec((3, tc), lambda j: (0, nb + j)), col],
        out_specs=[col, col, wspec, wspec],
        compiler_params=_cp(("parallel",)),
        name="ffn_act_bwd",
    )(up, up, w, w, dact)


def _attn_scores(q, kc, kp, slope, dilation, has_prev):
    scale = HEAD_DIM ** -0.5
    nt = (((1,), (1,)), ((), ()))
    s_c = lax.dot_general(q, kc, nt, preferred_element_type=F32) * scale
    s_p = lax.dot_general(q, kp, nt, preferred_element_type=F32) * scale
    qi = lax.broadcasted_iota(jnp.int32, (ATTN_BLOCK, ATTN_BLOCK), 0)
    kj = lax.broadcasted_iota(jnp.int32, (ATTN_BLOCK, ATTN_BLOCK), 1)
    dist_c = qi - kj
    dist_p = dist_c + ATTN_BLOCK
    s_c = jnp.where(dist_c >= 0, s_c - slope * (dist_c * dilation).astype(F32), NEG_INF)
    s_p = jnp.where((dist_p <= ATTN_BLOCK) & has_prev, s_p - slope * (dist_p * dilation).astype(F32), NEG_INF)
    return s_c, s_p


def _attn_fwd(q, k, v, slopes, dilation):
    bp, m, e = q.shape
    nb = m // ATTN_BLOCK

    def body(q_ref, kc_ref, kp_ref, vc_ref, vp_ref, sl_ref, o_ref, lse_ref):
        i = pl.program_id(1)
        s_c, s_p = _attn_scores(q_ref[...], kc_ref[...], kp_ref[...], sl_ref[:, :1], dilation, i > 0)
        mx = jnp.maximum(jnp.max(s_c, axis=-1, keepdims=True), jnp.max(s_p, axis=-1, keepdims=True))
        p_c, p_p = jnp.exp(s_c - mx), jnp.exp(s_p - mx)
        den = jnp.sum(p_c, axis=-1, keepdims=True) + jnp.sum(p_p, axis=-1, keepdims=True)
        o = (jnp.dot(p_c.astype(BF16), vc_ref[...], preferred_element_type=F32)
             + jnp.dot(p_p.astype(BF16), vp_ref[...], preferred_element_type=F32))
        o_ref[...] = o / den
        lse_ref[...] = jnp.broadcast_to(mx + jnp.log(den), (ATTN_BLOCK, e))

    cur = pl.BlockSpec((None, ATTN_BLOCK, e), lambda b, i: (b, i, 0))
    prev = pl.BlockSpec((None, ATTN_BLOCK, e), lambda b, i: (b, jnp.maximum(i - 1, 0), 0))
    return pl.pallas_call(
        body,
        out_shape=[_sds((bp, m, e), F32), _sds((bp, m, e), F32)],
        grid=(bp, nb),
        in_specs=[cur, cur, prev, cur, prev, pl.BlockSpec((None, 1, LANE), lambda b, i: (b, 0, 0))],
        out_specs=[cur, cur],
        compiler_params=_cp(("parallel", "parallel")),
        name=f"attn_fwd_d{dilation}",
    )(q, k, k, v, v, slopes)


def _attn_bwd(q, k, v, do, lse, delta, slopes, dilation):
    bp, m, e = q.shape
    nb = m // ATTN_BLOCK

    def body(q_ref, kc_ref, kp_ref, vc_ref, vp_ref, do_ref, lse_ref, dl_ref, sl_ref,
             dq_ref, dk_ref, dv_ref, ck_ref, cv_ref):
        step = pl.program_id(1)
        i = nb - 1 - step
        scale = HEAD_DIM ** -0.5
        nt = (((1,), (1,)), ((), ()))
        qv, kc, kp, vc, vp, dov = q_ref[...], kc_ref[...], kp_ref[...], vc_ref[...], vp_ref[...], do_ref[...]
        s_c, s_p = _attn_scores(qv, kc, kp, sl_ref[:, :1], dilation, i > 0)
        lse_col, dl_col = lse_ref[:, :1], dl_ref[:, :1]
        p_c, p_p = jnp.exp(s_c - lse_col), jnp.exp(s_p - lse_col)
        ds_c = p_c * (lax.dot_general(dov, vc, nt, preferred_element_type=F32) - dl_col)
        ds_p = p_p * (lax.dot_general(dov, vp, nt, preferred_element_type=F32) - dl_col)
        dq = (jnp.dot(ds_c.astype(BF16), kc, preferred_element_type=F32)
              + jnp.dot(ds_p.astype(BF16), kp, preferred_element_type=F32)) * scale
        dq_ref[...] = dq.astype(dq_ref.dtype)

        @pl.when(step == 0)
        def _():
            ck_ref[...] = jnp.zeros_like(ck_ref)
            cv_ref[...] = jnp.zeros_like(cv_ref)

        dk_c = jnp.dot(ds_c.T.astype(BF16), qv, preferred_element_type=F32) * scale
        dv_c = jnp.dot(p_c.T.astype(BF16), dov, preferred_element_type=F32)
        dk_ref[...] = (dk_c + ck_ref[...]).astype(dk_ref.dtype)
        dv_ref[...] = (dv_c + cv_ref[...]).astype(dv_ref.dtype)
        ck_ref[...] = jnp.dot(ds_p.T.astype(BF16), qv, preferred_element_type=F32) * scale
        cv_ref[...] = jnp.dot(p_p.T.astype(BF16), dov, preferred_element_type=F32)

    cur = pl.BlockSpec((None, ATTN_BLOCK, e), lambda b, s: (b, nb - 1 - s, 0))
    prev = pl.BlockSpec((None, ATTN_BLOCK, e), lambda b, s: (b, jnp.maximum(nb - 2 - s, 0), 0))
    return pl.pallas_call(
        body,
        out_shape=[_sds((bp, m, e), BF16)] * 3,
        grid=(bp, nb),
        in_specs=[cur, cur, prev, cur, prev, cur, cur, cur, pl.BlockSpec((None, 1, LANE), lambda b, s: (b, 0, 0))],
        out_specs=[cur, cur, cur],
        scratch_shapes=[pltpu.VMEM((ATTN_BLOCK, e), F32), pltpu.VMEM((ATTN_BLOCK, e), F32)],
        compiler_params=_cp(("parallel", "arbitrary")),
        name=f"attn_bwd_d{dilation}",
    )(q, k, k, v, v, do, lse, delta, slopes)


def _attn_merge(outs, lses):
    rows, aw = outs[0].shape
    tm = _tile(rows, 256, SUBLANE)

    def body(o0, o1, o2, l0, l1, l2, o_ref, lse_ref):
        lv = [l0[...], l1[...], l2[...]]
        mx = jnp.maximum(jnp.maximum(lv[0], lv[1]), lv[2])
        w = [jnp.exp(t - mx) for t in lv]
        den = w[0] + w[1] + w[2]
        o_ref[...] = (w[0] * o0[...] + w[1] * o1[...] + w[2] * o2[...]) / den
        lse_ref[...] = mx + jnp.log(den)

    spec = pl.BlockSpec((tm, aw), lambda i: (i, 0))
    return pl.pallas_call(
        body,
        out_shape=[_sds((rows, aw), F32)] * 2,
        grid=(rows // tm,),
        in_specs=[spec] * 6,
        out_specs=[spec, spec],
        compiler_params=_cp(("parallel",)),
        name="attn_merge",
    )(*outs, *lses)


def _attn_delta(do, o, head_ones):
    rows, aw = do.shape
    tm = _tile(rows, 256, SUBLANE)

    def body(do_ref, o_ref, e_ref, d_ref):
        d_ref[...] = jnp.dot(do_ref[...] * o_ref[...], e_ref[...], preferred_element_type=F32,
                             precision=lax.Precision.HIGHEST)

    spec = pl.BlockSpec((tm, aw), lambda i: (i, 0))
    return pl.pallas_call(
        body,
        out_shape=_sds((rows, aw), F32),
        grid=(rows // tm,),
        in_specs=[spec, spec, pl.BlockSpec((aw, aw), lambda i: (0, 0))],
        out_specs=spec,
        compiler_params=_cp(("parallel",)),
        name="attn_delta",
    )(do, o, head_ones)


def _to_residues(t, dilation, hp):
    rows = t.shape[0]
    m = rows // dilation
    return t.reshape(m, dilation, hp, HEAD_DIM).transpose(1, 2, 0, 3).reshape(dilation * hp, m, HEAD_DIM)


def _from_residues(t, dilation, hp):
    m = t.shape[1]
    return t.reshape(dilation, hp, m, HEAD_DIM).transpose(2, 0, 1, 3).reshape(m * dilation, hp * HEAD_DIM)


def _alibi_slopes(pattern, hp, dilation):
    n_heads = hp * len(DSWA_PATTERNS)
    s = np.array([2.0 ** (-8.0 * (pattern * hp + h + 1) / n_heads) for h in range(hp)], dtype=np.float32)
    s = np.tile(s, dilation)
    return jnp.asarray(np.broadcast_to(s[:, None, None], (dilation * hp, 1, LANE)).copy())


def _loss_fwd_bwd(y, target):
    rows, d = y.shape
    tm = _tile(rows, 256, SUBLANE)

    def body(y_ref, t_ref, dy_ref, l_ref):
        i = pl.program_id(0)
        err = y_ref[...] - t_ref[...]
        dy_ref[...] = err * (1.0 / d)
        part = 0.5 * jnp.sum(jnp.mean(err * err, axis=-1, keepdims=True), axis=0, keepdims=True)

        @pl.when(i == 0)
        def _():
            l_ref[...] = jnp.zeros_like(l_ref)

        l_ref[...] += jnp.broadcast_to(part, l_ref.shape)

    spec = pl.BlockSpec((tm, d), lambda i: (i, 0))
    dy, loss = pl.pallas_call(
        body,
        out_shape=[_sds((rows, d), F32), _sds((SUBLANE, LANE), F32)],
        grid=(rows // tm,),
        in_specs=[spec, spec],
        out_specs=[spec, pl.BlockSpec((SUBLANE, LANE), lambda i: (0, 0))],
        compiler_params=_cp(("arbitrary",)),
        name="loss",
    )(y, target)
    return dy, loss[0, 0]


def _adam_math(w, g, m, v):
    m = ADAM_B1 * m + (1.0 - ADAM_B1) * g
    v = ADAM_B2 * v + (1.0 - ADAM_B2) * jnp.square(g)
    m_hat = m / (1.0 - ADAM_B1 ** ADAM_STEP)
    v_hat = v / (1.0 - ADAM_B2 ** ADAM_STEP)
    delta = -ADAM_LR * (m_hat / (jnp.sqrt(v_hat) + ADAM_EPS) + ADAM_WD * w)
    return delta, m, v


def _as2d(a):
    if a.ndim == 1:
        return a.reshape(1, -1)
    return a.reshape(-1, a.shape[-1])


def _adam(w, g, m, v, name):
    shape = w.shape
    w2, g2, m2, v2 = _as2d(w), _as2d(g), _as2d(m), _as2d(v)
    r, c = w2.shape
    tr = _tile(r, 256, SUBLANE)
    tc = _tile(c, 1024)

    def body(w_ref, g_ref, m_ref, v_ref, d_ref, mo_ref, vo_ref):
        delta, mn, vn = _adam_math(w_ref[...], g_ref[...], m_ref[...], v_ref[...])
        d_ref[...] = delta
        mo_ref[...] = mn
        vo_ref[...] = vn

    spec = pl.BlockSpec((tr, tc), lambda i, j: (i, j))
    outs = pl.pallas_call(
        body,
        out_shape=[_sds((r, c), F32)] * 3,
        grid=(r // tr, c // tc),
        in_specs=[spec] * 4,
        out_specs=[spec] * 3,
        compiler_params=_cp(("parallel", "parallel")),
        name=name,
    )(w2, g2, m2, v2)
    return [o.reshape(shape) for o in outs]


def _wmod_grad_adam(c_t, dmod, w, m, v):
    nl, d, cols = w.shape
    nex = c_t.shape[1]
    tr = _tile(d, 256, SUBLANE)
    tc = _tile(cols, 1024)

    def body(c_ref, dm_ref, w_ref, m_ref, v_ref, g_ref, d_ref, mo_ref, vo_ref):
        cond = jax.nn.silu(c_ref[...]).astype(BF16)
        g = jnp.dot(cond, dm_ref[...].astype(BF16), preferred_element_type=F32)
        delta, mn, vn = _adam_math(w_ref[...], g, m_ref[...], v_ref[...])
        g_ref[...] = g
        d_ref[...] = delta
        mo_ref[...] = mn
        vo_ref[...] = vn

    spec = pl.BlockSpec((None, tr, tc), lambda l, i, j: (l, i, j))
    return pl.pallas_call(
        body,
        out_shape=[_sds((nl, d, cols), F32)] * 4,
        grid=(nl, d // tr, cols // tc),
        in_specs=[pl.BlockSpec((tr, nex), lambda l, i, j: (i, 0)),
                  pl.BlockSpec((None, nex, tc), lambda l, i, j: (l, 0, j)), spec, spec, spec],
        out_specs=[spec] * 4,
        compiler_params=_cp(("parallel", "parallel", "parallel")),
        name="wmod_grad_adam",
    )(c_t, dmod, w, m, v)


def _my_pos():
    return lax.axis_index("x"), lax.axis_index("y"), lax.axis_index("c")


def _ag8(x4, select_half, name):
    a, s, r, c = x4.shape
    assert s == (2 if select_half else 1)

    def body(x_ref, out_ref, send_sems, recv_sems, local_sem):
        x, y, cc = _my_pos()
        me, sibling = (x, y, cc), (x, y, 1 - cc)
        chips = [(1 - x, y), (x, 1 - y), (1 - x, 1 - y)]
        src_mine = x_ref.at[:, pl.ds(cc if select_half else 0, 1)]

        def blk(px, py, pc):
            return out_ref.at[:, pl.ds(4 * px + 2 * py + pc, 1)]

        def copy(k, block, to, src=None):
            return pltpu.make_async_remote_copy(
                src_ref=blk(*block) if src is None else src, dst_ref=blk(*block),
                send_sem=send_sems.at[k], recv_sem=recv_sems.at[k], device_id=to, device_id_type=MESH)

        mine = pltpu.make_async_copy(src_mine, blk(*me), local_sem)
        mine.start()
        first = [copy(0, me, sibling, src=src_mine)]
        first += [copy(1 + j, me, (*chip, cc), src=src_mine) for j, chip in enumerate(chips)]
        for cp in first:
            cp.start()
        passed = [copy(4 + j, (*chip, cc), sibling) for j, chip in enumerate(chips)]
        for j, chip in enumerate(chips):
            copy(1 + j, (*chip, cc), me).wait_recv()
            passed[j].start()
        copy(0, sibling, me).wait_recv()
        for j, chip in enumerate(chips):
            copy(4 + j, (*chip, 1 - cc), me).wait_recv()
        for cp in first + passed:
            cp.wait_send()
        mine.wait()

    return pl.pallas_call(
        body,
        out_shape=_sds((a, N_DEV, r, c), x4.dtype),
        in_specs=[ANY],
        out_specs=ANY,
        scratch_shapes=[pltpu.SemaphoreType.DMA((7,)), pltpu.SemaphoreType.DMA((7,)), pltpu.SemaphoreType.DMA],
        name=name,
    )(x4)


def _rs_sibling(g8, name):
    a, _, r, c = g8.shape
    g5 = g8.reshape(a, N_CHIP, 2, r, c)

    def body(g_ref, out_ref, send_sem, recv_sem):
        x, y, cc = _my_pos()
        cp = pltpu.make_async_remote_copy(
            src_ref=g_ref.at[:, :, pl.ds(1 - cc, 1)], dst_ref=out_ref, send_sem=send_sem, recv_sem=recv_sem,
            device_id=(x, y, 1 - cc), device_id_type=MESH)
        cp.start()
        cp.wait()

    return pl.pallas_call(
        body,
        out_shape=_sds((a, N_CHIP, 1, r, c), g8.dtype),
        in_specs=[ANY],
        out_specs=ANY,
        scratch_shapes=[pltpu.SemaphoreType.DMA, pltpu.SemaphoreType.DMA],
        name=name,
    )(g5)


def _chip_of(x, y, k):
    return (1 - x if k & 2 else x), (1 - y if k & 1 else y)


def _rs_chips(s_rem, name):
    def body(s_ref, out_ref, send_sems, recv_sems):
        x, y, cc = _my_pos()
        copies = []
        for k in (1, 2, 3):
            px, py = _chip_of(x, y, k)
            copies.append(pltpu.make_async_remote_copy(
                src_ref=s_ref.at[:, pl.ds(k - 1, 1)], dst_ref=out_ref.at[:, pl.ds(k - 1, 1)],
                send_sem=send_sems.at[k - 1], recv_sem=recv_sems.at[k - 1],
                device_id=(px, py, cc), device_id_type=MESH))
        for cp in copies:
            cp.start()
        for cp in copies:
            cp.wait()

    return pl.pallas_call(
        body,
        out_shape=_sds(s_rem.shape, s_rem.dtype),
        in_specs=[ANY],
        out_specs=ANY,
        scratch_shapes=[pltpu.SemaphoreType.DMA((3,)), pltpu.SemaphoreType.DMA((3,))],
        name=name,
    )(s_rem)


def _share_halves(r3, name):
    a, r, c = r3.shape
    r4 = r3.reshape(a, 1, r, c)

    def body(r_ref, out_ref, send_sem, recv_sem, local_sem):
        x, y, cc = _my_pos()
        mine = pltpu.make_async_copy(r_ref, out_ref.at[:, pl.ds(cc, 1)], local_sem)
        mine.start()
        cp = pltpu.make_async_remote_copy(
            src_ref=r_ref, dst_ref=out_ref.at[:, pl.ds(cc, 1)], send_sem=send_sem, recv_sem=recv_sem,
            device_id=(x, y, 1 - cc), device_id_type=MESH)
        cp.start()
        cp.wait()
        mine.wait()

    return pl.pallas_call(
        body,
        out_shape=_sds((a, 2, r, c), r3.dtype),
        in_specs=[ANY],
        out_specs=ANY,
        scratch_shapes=[pltpu.SemaphoreType.DMA, pltpu.SemaphoreType.DMA, pltpu.SemaphoreType.DMA],
        name=name,
    )(r4)


def _rs_add_remote(g8, recv_a, pos, name):
    a, _, r, c = g8.shape
    ra = recv_a.reshape(a, N_CHIP, r, c)
    tr = _tile(r, 256, SUBLANE)
    tc = _tile(c, 1024)

    def body(pos_ref, g_ref, r_ref, o_ref):
        del pos_ref
        o_ref[...] = (g_ref[...] + r_ref[...]).astype(o_ref.dtype)

    grid_spec = pltpu.PrefetchScalarGridSpec(
        num_scalar_prefetch=1,
        grid=(a, 3, r // tr, c // tc),
        in_specs=[pl.BlockSpec((None, None, tr, tc), lambda l, k, i, j, p: (l, 2 * (p[0] ^ (k + 1)) + p[1], i, j)),
                  pl.BlockSpec((None, None, tr, tc), lambda l, k, i, j, p: (l, p[0] ^ (k + 1), i, j))],
        out_specs=pl.BlockSpec((None, None, tr, tc), lambda l, k, i, j, p: (l, k, i, j)),
    )
    return pl.pallas_call(
        body,
        out_shape=_sds((a, 3, r, c), BF16),
        grid_spec=grid_spec,
        compiler_params=_cp(("parallel",) * 4),
        name=name,
    )(pos, g8, ra)


def _rs_add_final(g8, recv_a, recv_b, pos, name):
    a, _, r, c = g8.shape
    ra = recv_a.reshape(a, N_CHIP, r, c)
    tr = _tile(r, 256, SUBLANE)
    tc = _tile(c, 1024)

    def body(pos_ref, g_ref, r_ref, b0_ref, b1_ref, b2_ref, o_ref):
        del pos_ref
        o_ref[...] = (((g_ref[...] + r_ref[...]) + b0_ref[...].astype(F32)) + b1_ref[...].astype(F32)
                      ) + b2_ref[...].astype(F32)

    def bspec(k):
        return pl.BlockSpec((None, None, tr, tc), functools.partial(lambda l, i, j, p, k: (l, k, i, j), k=k))

    grid_spec = pltpu.PrefetchScalarGridSpec(
        num_scalar_prefetch=1,
        grid=(a, r // tr, c // tc),
        in_specs=[pl.BlockSpec((None, None, tr, tc), lambda l, i, j, p: (l, 2 * p[0] + p[1], i, j)),
                  pl.BlockSpec((None, None, tr, tc), lambda l, i, j, p: (l, p[0], i, j)),
                  bspec(0), bspec(1), bspec(2)],
        out_specs=pl.BlockSpec((None, tr, tc), lambda l, i, j, p: (l, i, j)),
    )
    return pl.pallas_call(
        body,
        out_shape=_sds((a, r, c), F32),
        grid_spec=grid_spec,
        compiler_params=_cp(("parallel",) * 3),
        name=name,
    )(pos, g8, ra, recv_b, recv_b, recv_b)


def _reduce_scatter(g8, pos, tag):
    recv_a = _rs_sibling(g8, f"rs_sibling_{tag}")
    s_rem = _rs_add_remote(g8, recv_a, pos, f"rs_add_remote_{tag}")
    recv_b = _rs_chips(s_rem, f"rs_chips_{tag}")
    mine = _rs_add_final(g8, recv_a, recv_b, pos, f"rs_add_final_{tag}")
    return _share_halves(mine, f"rs_share_{tag}")


def _sum8(x8, name):
    _, r, c = x8.shape
    tr = _tile(r, 256, SUBLANE)

    def body(x_ref, o_ref):
        acc = x_ref[0]
        for b in range(1, N_DEV):
            acc = acc + x_ref[b]
        o_ref[...] = acc

    return pl.pallas_call(
        body,
        out_shape=_sds((r, c), F32),
        grid=(r // tr,),
        in_specs=[pl.BlockSpec((N_DEV, tr, c), lambda i: (0, i, 0))],
        out_specs=pl.BlockSpec((tr, c), lambda i: (i, 0)),
        compiler_params=_cp(("parallel",)),
        name=name,
    )(x8)


def _block_diag(t):
    g, p, q = t.shape
    eye = jnp.eye(g, dtype=t.dtype)
    return (t[:, :, None, :] * eye[:, None, :, None]).reshape(g * p, g * q)


def _diag_blocks(mat, g):
    p, q = mat.shape[0] // g, mat.shape[1] // g
    eye = jnp.eye(g, dtype=mat.dtype)
    return jnp.sum(mat.reshape(g, p, g, q) * eye[:, None, :, None], axis=2)


def _interleave(re, im, c):
    lead = re.shape[:-1]
    gn = re.shape[-1]
    return jnp.stack([re.reshape(*lead, gn // c, c), im.reshape(*lead, gn // c, c)], axis=-2).reshape(*lead, 2 * gn)


def _deinterleave(cat, c):
    lead = cat.shape[:-1]
    gn = cat.shape[-1] // 2
    t = cat.reshape(*lead, gn // c, 2, c)
    return t[..., 0, :].reshape(*lead, gn), t[..., 1, :].reshape(*lead, gn)


def kernel(x, c, w_mod, b_mod, g_pre_mix, g_post_mix, g_pre_ffn, g_post_ffn, w_in, ssm_log_dt, ssm_a_re, ssm_a_im, ssm_b_re, ssm_b_im, ssm_c_re, ssm_c_im, ssm_d, w_glu, b_glu, conv_mix_w, w_ssm_out, w_attn_out, w_conv_out, b_gate, w_o, w_up, ffn_conv_w, w_down, loss_target, m_w_mod, m_b_mod, m_g_pre_mix, m_g_post_mix, m_g_pre_ffn, m_g_post_ffn, m_w_in, m_ssm_log_dt, m_ssm_a_re, m_ssm_a_im, m_ssm_b_re, m_ssm_b_im, m_ssm_c_re, m_ssm_c_im, m_ssm_d, m_w_glu, m_b_glu, m_conv_mix_w, m_w_ssm_out, m_w_attn_out, m_w_conv_out, m_b_gate, m_w_o, m_w_up, m_ffn_conv_w, m_w_down, v_w_mod, v_b_mod, v_g_pre_mix, v_g_post_mix, v_g_pre_ffn, v_g_post_ffn, v_w_in, v_ssm_log_dt, v_ssm_a_re, v_ssm_a_im, v_ssm_b_re, v_ssm_b_im, v_ssm_c_re, v_ssm_c_im, v_ssm_d, v_w_glu, v_b_glu, v_conv_mix_w, v_w_ssm_out, v_w_attn_out, v_w_conv_out, v_b_gate, v_w_o, v_w_up, v_ffn_conv_w, v_w_down):
    weights = dict(w_mod=w_mod, b_mod=b_mod, g_pre_mix=g_pre_mix, g_post_mix=g_post_mix, g_pre_ffn=g_pre_ffn, g_post_ffn=g_post_ffn, w_in=w_in, ssm_log_dt=ssm_log_dt, ssm_a_re=ssm_a_re, ssm_a_im=ssm_a_im, ssm_b_re=ssm_b_re, ssm_b_im=ssm_b_im, ssm_c_re=ssm_c_re, ssm_c_im=ssm_c_im, ssm_d=ssm_d, w_glu=w_glu, b_glu=b_glu, conv_mix_w=conv_mix_w, w_ssm_out=w_ssm_out, w_attn_out=w_attn_out, w_conv_out=w_conv_out, b_gate=b_gate, w_o=w_o, w_up=w_up, ffn_conv_w=ffn_conv_w, w_down=w_down)
    mom_m = dict(w_mod=m_w_mod, b_mod=m_b_mod, g_pre_mix=m_g_pre_mix, g_post_mix=m_g_post_mix, g_pre_ffn=m_g_pre_ffn, g_post_ffn=m_g_post_ffn, w_in=m_w_in, ssm_log_dt=m_ssm_log_dt, ssm_a_re=m_ssm_a_re, ssm_a_im=m_ssm_a_im, ssm_b_re=m_ssm_b_re, ssm_b_im=m_ssm_b_im, ssm_c_re=m_ssm_c_re, ssm_c_im=m_ssm_c_im, ssm_d=m_ssm_d, w_glu=m_w_glu, b_glu=m_b_glu, conv_mix_w=m_conv_mix_w, w_ssm_out=m_w_ssm_out, w_attn_out=m_w_attn_out, w_conv_out=m_w_conv_out, b_gate=m_b_gate, w_o=m_w_o, w_up=m_w_up, ffn_conv_w=m_ffn_conv_w, w_down=m_w_down)
    mom_v = dict(w_mod=v_w_mod, b_mod=v_b_mod, g_pre_mix=v_g_pre_mix, g_post_mix=v_g_post_mix, g_pre_ffn=v_g_pre_ffn, g_post_ffn=v_g_post_ffn, w_in=v_w_in, ssm_log_dt=v_ssm_log_dt, ssm_a_re=v_ssm_a_re, ssm_a_im=v_ssm_a_im, ssm_b_re=v_ssm_b_re, ssm_b_im=v_ssm_b_im, ssm_c_re=v_ssm_c_re, ssm_c_im=v_ssm_c_im, ssm_d=v_ssm_d, w_glu=v_w_glu, b_glu=v_b_glu, conv_mix_w=v_conv_mix_w, w_ssm_out=v_w_ssm_out, w_attn_out=v_w_attn_out, w_conv_out=v_w_conv_out, b_gate=v_b_gate, w_o=v_w_o, w_up=v_w_up, ffn_conv_w=v_ffn_conv_w, w_down=v_w_down)
    names = list(weights)

    nl = w_in.shape[0]
    seq, d = x.shape[1], x.shape[2]
    sw = d // 4
    groups = sw // SSM_GROUP
    gn = groups * SSM_STATE
    hp = sw // HEAD_DIM
    qw = 3 * sw
    off_q, off_k, off_v = sw, sw + qw, sw + 2 * qw
    off_conv = sw + 3 * qw
    off_gate = off_conv + 3 * sw
    n_in = off_gate + 3 * d
    f = w_down.shape[1] * N_CHIP
    scan_c = _scan_cols(gn)
    assert seq % (ATTN_BLOCK * DSWA_PATTERNS[-1][1]) == 0 and all(w // dl == ATTN_BLOCK for w, dl in DSWA_PATTERNS)

    px, py, pc = _my_pos()
    chip = 2 * px + py
    dev = 2 * chip + pc
    pos = jnp.stack([chip, pc]).astype(jnp.int32)

    x2 = x.reshape(seq, d)
    target2 = loss_target.reshape(seq, d)

    c_all = _ag8(c.reshape(1, 1, 1, d), False, "ag_cond").reshape(N_DEV, d)
    c_pad = jnp.concatenate([c_all, jnp.zeros((SUBLANE, d), F32)], axis=0)
    mcols = w_mod.shape[2]
    w_mod4 = w_mod.reshape(nl, 1, d, mcols)
    mod_loc = jnp.stack([_mm_nn(c_pad, w_mod4, l, name="mod_fwd", a_fn=jax.nn.silu) for l in range(nl)])
    mod_all = _ag8(mod_loc.reshape(nl, 1, 2 * SUBLANE, mcols), False, "ag_mod")
    mod_rows = lax.dynamic_slice_in_dim(mod_all[:, 0::2], dev, 1, axis=2)
    mod = mod_rows.reshape(nl, N_CHIP * mcols) + b_mod
    mods = mod.reshape(nl, 6, 1, d)

    def gather_halves(w):
        _, r, cols = w.shape
        return _ag8(w.astype(BF16).reshape(nl, 2, r // 2, cols), True, "ag_weight")

    def col_sharded(w):
        _, r, cols = w.shape
        return gather_halves(w).reshape(nl, N_CHIP, r, cols)

    def row_sharded(w):
        _, r, cols = w.shape
        return gather_halves(w).reshape(nl, 1, N_CHIP * r, cols)

    def gather_whole(w):
        _, r, cols = w.shape
        got = _ag8(w.reshape(nl, 1, r, cols), False, "ag_small_weight")[:, 0::2]
        return got.transpose(0, 2, 1, 3).reshape(nl, r, N_CHIP * cols)

    wf = dict(w_in=col_sharded(w_in), w_glu=row_sharded(w_glu), w_ssm_out=col_sharded(w_ssm_out),
              w_attn_out=col_sharded(w_attn_out), w_conv_out=col_sharded(w_conv_out), w_o=row_sharded(w_o),
              w_up=col_sharded(w_up), w_down=row_sharded(w_down))
    conv_w_full = gather_whole(conv_mix_w)
    ffn_w_full = gather_whole(ffn_conv_w)

    head_ones = jnp.asarray(np.kron(np.eye(hp, dtype=np.float32), np.ones((HEAD_DIM, HEAD_DIM), np.float32)))
    slopes = [_alibi_slopes(p, hp, dl) for p, (_, dl) in enumerate(DSWA_PATTERNS)]

    def row(v):
        return v.reshape(1, -1)

    saved = []
    xl = x2
    for l in range(nl):
        sh1, sc1, gt1, sh2, sc2, gt2 = [mods[l, q] for q in range(6)]
        s = dict(x_in=xl)
        (h1,) = _ew_fwd(_fn_norm_mod, [(xl, 0, d)], [row(g_pre_mix[l]), sc1, sh1], [BF16], name="norm_mod_fwd", width=d)
        proj = _mm_nn(h1, wf["w_in"], l, name="w_in_fwd")
        br_t = jnp.transpose(ssm_b_re[l], (2, 0, 1))
        bi_t = jnp.transpose(ssm_b_im[l], (2, 0, 1))
        disc_in = (ssm_log_dt[l].reshape(groups, 1), ssm_a_re[l], ssm_a_im[l], br_t, bi_t)
        lr, li, bbr_t, bbi_t = _ssm_disc_fwd(*disc_in)
        lam = _interleave(lr.reshape(1, gn), li.reshape(1, gn), scan_c)
        bcat = _interleave(_block_diag(jnp.transpose(bbr_t, (1, 0, 2))), _block_diag(jnp.transpose(bbi_t, (1, 0, 2))),
                           scan_c).astype(BF16).reshape(1, 1, sw, 2 * gn)
        cre = _block_diag(jnp.transpose(ssm_c_re[l], (0, 2, 1)))
        cim = _block_diag(jnp.transpose(ssm_c_im[l], (0, 2, 1)))
        ccat = jnp.transpose(_interleave(cre.T, -cim.T, scan_c)).astype(BF16).reshape(1, 1, 2 * gn, sw)
        xcat = _mm_nn(proj, bcat, 0, name="ssm_b_fwd", k_dim=sw)
        hcat = _ssm_scan_fwd(xcat, lam)
        y_ssm_pre = _mm_nn(hcat, ccat, 0, name="ssm_c_fwd")
        (gact,) = _ew_fwd(_fn_gelu, [(y_ssm_pre, 0, sw), (proj, 0, sw)], [row(ssm_d[l])], [F32], name="gelu_fwd", width=sw,
                          tw=_tile(sw, 512))
        z = _mm_nn(gact, wf["w_glu"], l, name="w_glu_fwd")
        (s_ssm,) = _ew_fwd(_fn_glu, [(gact, 0, sw), (z, 0, sw)], [row(b_glu[l])], [BF16], name="glu_fwd", width=sw,
                           tw=_tile(sw, 512))
        y_ssm = _mm_nn(s_ssm, wf["w_ssm_out"], l, name="w_branch_out_fwd")
        qkv, outs, lses = [], [], []
        for p, (_, dl) in enumerate(DSWA_PATTERNS):
            q_p = _to_residues(proj[:, off_q + p * sw: off_q + (p + 1) * sw].astype(BF16), dl, hp)
            k_p = _to_residues(proj[:, off_k + p * sw: off_k + (p + 1) * sw].astype(BF16), dl, hp)
            v_p = _to_residues(proj[:, off_v + p * sw: off_v + (p + 1) * sw].astype(BF16), dl, hp)
            o_p, lse_p = _attn_fwd(q_p, k_p, v_p, slopes[p], dl)
            qkv.append((q_p, k_p, v_p))
            outs.append(_from_residues(o_p, dl, hp))
            lses.append(_from_residues(lse_p, dl, hp))
        o_attn, lse_attn = _attn_merge(outs, lses)
        y_attn = _mm_nn(o_attn, wf["w_attn_out"], l, name="w_branch_out_fwd")
        cv = _gconv_fwd(proj, off_conv, sw, conv_w_full[l])
        y_conv = _mm_nn(cv, wf["w_conv_out"], l, name="w_branch_out_fwd")
        bg = b_gate[l].reshape(3, 1, d)
        gate_xs = [(proj, off_gate + q * d, d) for q in range(3)] + [(y_ssm, 0, d), (y_attn, 0, d), (y_conv, 0, d)]
        (merged,) = _ew_fwd(_fn_gates, gate_xs, [bg[0], bg[1], bg[2]], [BF16], name="gates_fwd", width=d,
                            tw=_tile(sw, 512))
        y_mix = _mm_nn(merged, wf["w_o"], l, name="w_o_fwd")
        (x_mid,) = _ew_fwd(_fn_residual, [(xl, 0, d), (y_mix, 0, d)], [gt1, row(g_post_mix[l])], [F32], name="residual_fwd",
                           width=d)
        (h2,) = _ew_fwd(_fn_norm_mod, [(x_mid, 0, d)], [row(g_pre_ffn[l]), sc2, sh2], [BF16], name="norm_mod_fwd", width=d)
        up = _mm_nn(h2, wf["w_up"], l, name="w_up_fwd")
        act = _ffn_act_fwd(up, ffn_w_full[l])
        y_ffn = _mm_nn(act, wf["w_down"], l, name="w_down_fwd")
        (x_out,) = _ew_fwd(_fn_residual, [(x_mid, 0, d), (y_ffn, 0, d)], [gt2, row(g_post_ffn[l])], [F32],
                           name="residual_fwd", width=d)
        s.update(h1=h1, proj=proj, disc_in=disc_in, lam=lam, bcat=bcat, ccat=ccat, hcat=hcat, y_ssm_pre=y_ssm_pre,
                 gact=gact, z=z, s_ssm=s_ssm, y_ssm=y_ssm, qkv=qkv, o_attn=o_attn, lse_attn=lse_attn, y_attn=y_attn,
                 cv=cv, y_conv=y_conv, merged=merged, y_mix=y_mix, x_mid=x_mid, h2=h2, up=up, act=act, y_ffn=y_ffn)
        saved.append(s)
        xl = x_out

    dxl, loss_local = _loss_fwd_bwd(xl, target2)
    loss = lax.psum(loss_local, ("x", "y", "c"))

    gbuf = {k: lax.empty(v.shape, F32) for k, v in wf.items()}
    small = {k: [None] * nl for k in ("g_pre_mix", "g_post_mix", "g_pre_ffn", "g_post_ffn", "ssm_log_dt", "ssm_a_re",
                                      "ssm_a_im", "ssm_b_re", "ssm_b_im", "ssm_c_re", "ssm_c_im", "ssm_d", "b_glu",
                                      "conv_mix_w", "b_gate", "ffn_conv_w", "dmod")}
    for l in reversed(range(nl)):
        s = saved[l]
        sh1, sc1, gt1, sh2, sc2, gt2 = [mods[l, q] for q in range(6)]
        proj = s["proj"]
        (dy_ffn,), (dgt2, dg_post_ffn) = _ew_bwd(
            _fn_residual, [(s["x_mid"], 0, d), (s["y_ffn"], 0, d)], [gt2, row(g_post_ffn[l])], [[dxl]], [None, BF16],
            name="residual_bwd", width=d)
        dact = _mm_nt(dy_ffn, wf["w_down"], l, name="w_down_bwd_x", out_dtype=BF16)
        gbuf["w_down"] = _mm_tn(s["act"], dy_ffn, gbuf["w_down"], l, name="w_down_bwd_w")
        dup_a, dup_b, dwa, dwb = _ffn_act_bwd(s["up"], ffn_w_full[l], dact)
        dup = jnp.concatenate([dup_a, dup_b], axis=1)
        small["ffn_conv_w"][l] = jnp.concatenate([dwa, dwb], axis=1)
        dh2 = _mm_nt(dup, wf["w_up"], l, name="w_up_bwd_x")
        gbuf["w_up"] = _mm_tn(s["h2"], dup, gbuf["w_up"], l, name="w_up_bwd_w")
        (dx_mid,), (dg_pre_ffn, dsc2, dsh2) = _ew_bwd(
            _fn_norm_mod, [(s["x_mid"], 0, d)], [row(g_pre_ffn[l]), sc2, sh2], [[dh2]], [F32], name="norm_mod_bwd", width=d,
            dx_add={0: dxl})
        (dy_mix,), (dgt1, dg_post_mix) = _ew_bwd(
            _fn_residual, [(s["x_in"], 0, d), (s["y_mix"], 0, d)], [gt1, row(g_post_mix[l])], [[dx_mid]], [None, BF16],
            name="residual_bwd", width=d)
        dmerged = _mm_nt(dy_mix, wf["w_o"], l, name="w_o_bwd_x")
        gbuf["w_o"] = _mm_tn(s["merged"], dy_mix, gbuf["w_o"], l, name="w_o_bwd_w")
        bg = b_gate[l].reshape(3, 1, d)
        gate_xs = [(proj, off_gate + q * d, d) for q in range(3)] + [(s["y_ssm"], 0, d), (s["y_attn"], 0, d),
                                                                     (s["y_conv"], 0, d)]
        (dp0, dp1, dp2, dy_ssm, dy_attn, dy_conv), dbg = _ew_bwd(
            _fn_gates, gate_xs, [bg[0], bg[1], bg[2]], [[dmerged]], [BF16] * 6, name="gates_bwd", width=d,
            tw=_tile(sw, 512))
        small["b_gate"][l] = jnp.concatenate(dbg, axis=1)[0]
        ds_ssm = _mm_nt(dy_ssm, wf["w_ssm_out"], l, name="w_branch_out_bwd_x")
        gbuf["w_ssm_out"] = _mm_tn(s["s_ssm"], dy_ssm, gbuf["w_ssm_out"], l, name="w_branch_out_bwd_w")
        (dg1, dz), (db_glu,) = _ew_bwd(_fn_glu, [(s["gact"], 0, sw), (s["z"], 0, sw)], [row(b_glu[l])], [[ds_ssm]],
                                       [F32, BF16], name="glu_bwd", width=sw, tw=_tile(sw, 512))
        dg2 = _mm_nt(dz, wf["w_glu"], l, name="w_glu_bwd_x")
        gbuf["w_glu"] = _mm_tn(s["gact"], dz, gbuf["w_glu"], l, name="w_glu_bwd_w")
        (dy_pre, du_skip), (dd_skip,) = _ew_bwd(_fn_gelu, [(s["y_ssm_pre"], 0, sw), (proj, 0, sw)], [row(ssm_d[l])],
                                               [[dg1, dg2]], [BF16, F32], name="gelu_bwd", width=sw, tw=_tile(sw, 512))
        dhcat = _mm_nt(dy_pre, s["ccat"], 0, name="ssm_c_bwd_x")
        dccat = _mm_tn(s["hcat"], dy_pre, lax.empty((1, 1, 2 * gn, sw), F32), 0, name="ssm_c_bwd_w")[0, 0]
        gcat, dlam = _ssm_scan_bwd(dhcat, s["hcat"], s["lam"])
        du_b = _mm_nt(gcat, s["bcat"], 0, name="ssm_b_bwd_x")
        dbcat = _mm_tn(proj, gcat, lax.empty((1, 1, sw, 2 * gn), F32), 0, name="ssm_b_bwd_w")[0, 0]
        dlr, dli = _deinterleave(dlam, scan_c)
        dbre, dbim = _deinterleave(dbcat, scan_c)
        dbbr_t = jnp.transpose(_diag_blocks(dbre, groups), (1, 0, 2))
        dbbi_t = jnp.transpose(_diag_blocks(dbim, groups), (1, 0, 2))
        gld, gar, gai, gbr_t, gbi_t = _ssm_disc_bwd(*s["disc_in"], dlr.reshape(groups, SSM_STATE),
                                                    dli.reshape(groups, SSM_STATE), dbbr_t, dbbi_t)
        dcre_t, dcim_t = _deinterleave(dccat.T, scan_c)
        small["ssm_c_re"][l] = _diag_blocks(dcre_t, groups)
        small["ssm_c_im"][l] = -_diag_blocks(dcim_t, groups)
        small["ssm_log_dt"][l] = gld.reshape(groups)
        small["ssm_a_re"][l], small["ssm_a_im"][l] = gar, gai
        small["ssm_b_re"][l] = jnp.transpose(gbr_t, (1, 2, 0))
        small["ssm_b_im"][l] = jnp.transpose(gbi_t, (1, 2, 0))
        small["ssm_d"][l], small["b_glu"][l] = dd_skip[0], db_glu[0]
        du = (du_skip + du_b).astype(BF16)
        do_attn = _mm_nt(dy_attn, wf["w_attn_out"], l, name="w_branch_out_bwd_x")
        gbuf["w_attn_out"] = _mm_tn(s["o_attn"], dy_attn, gbuf["w_attn_out"], l, name="w_branch_out_bwd_w")
        delta = _attn_delta(do_attn, s["o_attn"], head_ones)
        dqs, dks, dvs = [], [], []
        for p, (_, dl) in enumerate(DSWA_PATTERNS):
            q_p, k_p, v_p = s["qkv"][p]
            dq_p, dk_p, dv_p = _attn_bwd(q_p, k_p, v_p, _to_residues(do_attn.astype(BF16), dl, hp),
                                         _to_residues(s["lse_attn"], dl, hp), _to_residues(delta, dl, hp), slopes[p], dl)
            dqs.append(_from_residues(dq_p, dl, hp))
            dks.append(_from_residues(dk_p, dl, hp))
            dvs.append(_from_residues(dv_p, dl, hp))
        dcv = _mm_nt(dy_conv, wf["w_conv_out"], l, name="w_branch_out_bwd_x", out_dtype=BF16)
        gbuf["w_conv_out"] = _mm_tn(s["cv"], dy_conv, gbuf["w_conv_out"], l, name="w_branch_out_bwd_w")
        dcb, dcc, dch, dconv_w = _gconv_bwd(proj, off_conv, sw, conv_w_full[l], dcv)
        small["conv_mix_w"][l] = dconv_w
        dproj = jnp.concatenate([du] + dqs + dks + dvs + [dcb, dcc, dch, dp0, dp1, dp2], axis=1)
        dh1 = _mm_nt(dproj, wf["w_in"], l, name="w_in_bwd_x")
        gbuf["w_in"] = _mm_tn(s["h1"], dproj, gbuf["w_in"], l, name="w_in_bwd_w")
        (dx_in,), (dg_pre_mix, dsc1, dsh1) = _ew_bwd(
            _fn_norm_mod, [(s["x_in"], 0, d)], [row(g_pre_mix[l]), sc1, sh1], [[dh1]], [F32], name="norm_mod_bwd", width=d,
            dx_add={0: dx_mid})
        small["g_pre_mix"][l], small["g_post_mix"][l] = dg_pre_mix[0], dg_post_mix[0]
        small["g_pre_ffn"][l], small["g_post_ffn"][l] = dg_pre_ffn[0], dg_post_ffn[0]
        small["dmod"][l] = jnp.concatenate([dsh1, dsc1, dgt1, dsh2, dsc2, dgt2], axis=1)[0]
        dxl = dx_in

    grad_x = dxl.reshape(x.shape)

    small = {k: jnp.stack(v) for k, v in small.items()}
    order = sorted(small)
    flat = jnp.concatenate([small[k].reshape(-1) for k in order])
    n_small = flat.shape[0]
    pack_w = 8 * LANE
    pack_r = -(-n_small // (pack_w * SUBLANE)) * SUBLANE
    flat = jnp.concatenate([flat, jnp.zeros((pack_r * pack_w - n_small,), F32)])
    gathered = _ag8(flat.reshape(1, 1, pack_r, pack_w), False, "ag_small_grads")[0]
    summed = _sum8(gathered, "sum_small_grads").reshape(-1)
    sgrad, at = {}, 0
    for k in order:
        size = small[k].size
        sgrad[k] = summed[at:at + size].reshape(small[k].shape)
        at += size
    dmod_off = sum(small[k].size for k in order[:order.index("dmod")])
    dmod_all = gathered.reshape(N_DEV, -1)[:, dmod_off:dmod_off + nl * 6 * d].reshape(N_DEV, nl, 6 * d)
    dmod_loc = lax.dynamic_slice_in_dim(jnp.transpose(dmod_all, (1, 0, 2)), chip * mcols, mcols, axis=2)

    grads = dict(sgrad)
    grads["b_mod"] = grads.pop("dmod")
    grads["conv_mix_w"] = lax.dynamic_slice_in_dim(sgrad["conv_mix_w"], chip * conv_mix_w.shape[2], conv_mix_w.shape[2], axis=2)
    grads["ffn_conv_w"] = lax.dynamic_slice_in_dim(sgrad["ffn_conv_w"], chip * ffn_conv_w.shape[2], ffn_conv_w.shape[2], axis=2)

    for k, buf in gbuf.items():
        _, nb, kk, n = buf.shape
        local = weights[k].shape
        g8 = buf.reshape(nl, N_DEV, local[1] // 2, local[2])
        grads[k] = _reduce_scatter(g8, pos, k).reshape(local)

    delta_w, new_m, new_v = {}, {}, {}
    c_t = jnp.pad(jnp.transpose(c_all), ((0, 0), (0, LANE - N_DEV)))
    dmod_pad = jnp.pad(dmod_loc, ((0, 0), (0, LANE - N_DEV), (0, 0)))
    grads["w_mod"], delta_w["w_mod"], new_m["w_mod"], new_v["w_mod"] = _wmod_grad_adam(
        c_t, dmod_pad, w_mod, m_w_mod, v_w_mod)
    for k in names:
        if k == "w_mod":
            continue
        delta_w[k], new_m[k], new_v[k] = _adam(weights[k], grads[k], mom_m[k], mom_v[k], "adamw")

    return (loss, grad_x, *[grads[k] for k in names], *[delta_w[k] for k in names], *[new_m[k] for k in names],
            *[new_v[k] for k in names])
```

```python
import functools
import math

import numpy as np
import jax
import jax.numpy as jnp
from jax import lax
from jax.experimental import pallas as pl
from jax.experimental.pallas import tpu as pltpu

F32 = jnp.float32
BF16 = jnp.bfloat16
MESH = pl.DeviceIdType.MESH
ANY = pl.BlockSpec(memory_space=pl.ANY)

VMEM_LIMIT_BYTES = 48 * 1024 * 1024
LANE = 128
SUBLANE = 8

RMS_EPS = 1e-6
NEG_INF = -1e30
SSM_GROUP = 16
SSM_STATE = 64
HEAD_DIM = 64
DSWA_PATTERNS = ((128, 1), (512, 4), (2048, 16))
ATTN_BLOCK = 128
N_DEV = 8
N_CHIP = 4

ADAM_LR = 0.001
ADAM_B1 = 0.9
ADAM_B2 = 0.999
ADAM_EPS = 1e-08
ADAM_WD = 0.01
ADAM_STEP = 10


def _cp(sem=None):
    return pltpu.CompilerParams(dimension_semantics=sem, vmem_limit_bytes=VMEM_LIMIT_BYTES)


def _tile(n, pref, align=LANE):
    if n <= pref:
        return n
    t = (pref // align) * align
    while t >= align:
        if n % t == 0:
            return t
        t -= align
    return n


def _sds(shape, dtype):
    return jax.ShapeDtypeStruct(tuple(shape), dtype)


MM_VMEM_BUDGET = 34 * 1024 * 1024
MM_MAX_CONTRACT = 2048


def _divisors(n, align):
    if n % align:
        return [n]
    return [t for t in range(n, 0, -align) if n % t == 0]


def _halvings(n, align, floor=256):
    out = [n]
    while out[-1] % (2 * align) == 0 and out[-1] // 2 >= floor:
        out.append(out[-1] // 2)
    return out


def _pick_tiles(rows, cols, fixed_bytes, row_bytes, col_bytes, cell_bytes):
    best = None
    for tr in rows:
        for tc in cols:
            if fixed_bytes + row_bytes * tr + col_bytes * tc + cell_bytes * tr * tc <= MM_VMEM_BUDGET:
                if best is None or tr * tc > best[0] * best[1]:
                    best = (tr, tc)
                break
    assert best is not None
    return best


def _accumulate(step, n_steps, part, o_ref, acc_ref):
    if n_steps == 1:
        o_ref[...] = part.astype(o_ref.dtype)
        return
    acc = o_ref if acc_ref is None else acc_ref

    @pl.when(step == 0)
    def _():
        acc[...] = part

    @pl.when(step > 0)
    def _():
        acc[...] += part

    if acc_ref is not None:
        @pl.when(step == n_steps - 1)
        def _():
            o_ref[...] = acc_ref[...].astype(o_ref.dtype)


def _mm_nn(a, w, layer, *, name, k_dim=None, a_col0=0, out_dtype=F32, a_fn=None):
    m = a.shape[0]
    _, nb, kw, n = w.shape
    k_dim = kw if k_dim is None else k_dim
    assert k_dim == kw
    tk = _tile(k_dim, MM_MAX_CONTRACT)
    nk = k_dim // tk
    sa, so = a.dtype.itemsize, jnp.dtype(out_dtype).itemsize
    use_acc = nk > 1 and so != 4
    row_bytes = tk * (2 * sa + (2 if sa == 4 else 0) + (4 if a_fn is not None else 0))
    tm, tn = _pick_tiles(_halvings(m, SUBLANE), _divisors(n, LANE), 0, row_bytes, 2 * tk * w.dtype.itemsize,
                         2 * so + 4 + (4 if use_acc else 0))
    assert a_col0 % tk == 0
    npb = n // tn
    a0 = a_col0 // tk

    def body(a_ref, w_ref, o_ref, *scratch):
        av = a_ref[...]
        if a_fn is not None:
            av = a_fn(av.astype(F32))
        part = jnp.dot(av.astype(BF16), w_ref[...].astype(BF16), preferred_element_type=F32)
        _accumulate(pl.program_id(2), nk, part, o_ref, scratch[0] if use_acc else None)

    return pl.pallas_call(
        body,
        out_shape=_sds((m, nb * n), out_dtype),
        grid=(m // tm, nb * npb, nk),
        in_specs=[pl.BlockSpec((tm, tk), lambda i, j, k: (i, a0 + k)),
                  pl.BlockSpec((None, None, tk, tn), lambda i, j, k: (layer, j // npb, k, j % npb))],
        out_specs=pl.BlockSpec((tm, tn), lambda i, j, k: (i, j)),
        scratch_shapes=[pltpu.VMEM((tm, tn), F32)] if use_acc else [],
        compiler_params=_cp(("parallel", "parallel", "arbitrary")),
        name=name,
    )(a, w)


def _mm_nt(g, w, layer, *, name, out_dtype=F32):
    m = g.shape[0]
    _, nb, k_dim, n = w.shape
    assert g.shape[1] == nb * n
    tko = _tile(k_dim, MM_MAX_CONTRACT)
    sg, so = g.dtype.itemsize, jnp.dtype(out_dtype).itemsize
    use_acc = so != 4
    res_row = tko * (2 * so + 4 + (4 if use_acc else 0))
    tm, tc = _pick_tiles(_halvings(m, SUBLANE), _divisors(n, LANE), 0, res_row, 2 * tko * w.dtype.itemsize,
                         2 * sg + (2 if sg == 4 else 0))
    npb = n // tc
    nr = nb * npb
    use_acc = use_acc and nr > 1

    def body(g_ref, w_ref, o_ref, *scratch):
        part = lax.dot_general(g_ref[...].astype(BF16), w_ref[...].astype(BF16),
                               (((1,), (1,)), ((), ())), preferred_element_type=F32)
        _accumulate(pl.program_id(2), nr, part, o_ref, scratch[0] if use_acc else None)

    return pl.pallas_call(
        body,
        out_shape=_sds((m, k_dim), out_dtype),
        grid=(m // tm, k_dim // tko, nr),
        in_specs=[pl.BlockSpec((tm, tc), lambda i, kk, r: (i, r)),
                  pl.BlockSpec((None, None, tko, tc), lambda i, kk, r: (layer, r // npb, kk, r % npb))],
        out_specs=pl.BlockSpec((tm, tko), lambda i, kk, r: (i, kk)),
        scratch_shapes=[pltpu.VMEM((tm, tko), F32)] if use_acc else [],
        compiler_params=_cp(("parallel", "parallel", "arbitrary")),
        name=name,
    )(g, w)


def _mm_tn(a, g, out_buf, layer, *, name, a_col0=0):
    m = a.shape[0]
    _, nb, k_dim, n = out_buf.shape
    assert g.shape == (m, nb * n)
    tm = _tile(m, MM_MAX_CONTRACT, SUBLANE)
    nr = m // tm
    sa, sg = a.dtype.itemsize, g.dtype.itemsize
    tk, tn = _pick_tiles(_halvings(k_dim, LANE), _divisors(n, LANE), 0, tm * (2 * sa + (2 if sa == 4 else 0) + 2),
                         tm * (2 * sg + (2 if sg == 4 else 0)), 2 * 4 + 4)
    assert a_col0 % tk == 0
    a0 = a_col0 // tk
    npb = n // tn

    def body(a_ref, g_ref, buf_ref, o_ref):
        del buf_ref
        part = lax.dot_general(a_ref[...].astype(BF16), g_ref[...].astype(BF16),
                               (((0,), (0,)), ((), ())), preferred_element_type=F32)
        _accumulate(pl.program_id(2), nr, part, o_ref, None)

    return pl.pallas_call(
        body,
        out_shape=_sds(out_buf.shape, F32),
        grid=(k_dim // tk, nb * npb, nr),
        in_specs=[pl.BlockSpec((tm, tk), lambda kk, j, r: (r, a0 + kk)),
                  pl.BlockSpec((tm, tn), lambda kk, j, r: (r, j)),
                  ANY],
        out_specs=pl.BlockSpec((None, None, tk, tn), lambda kk, j, r: (layer, j // npb, kk, j % npb)),
        input_output_aliases={2: 0},
        compiler_params=_cp(("parallel", "parallel", "arbitrary")),
        name=name,
    )(a, g, out_buf)


def _ew_fwd(fn, xs, ps, out_dtypes, *, name, width, tw=None, tm=256):
    rows = xs[0][0].shape[0]
    tm = _tile(rows, tm, SUBLANE)
    tw = width if tw is None else tw
    nx, n_p = len(xs), len(ps)

    def body(*refs):
        xv = [r[...].astype(F32) for r in refs[:nx]]
        pv = [r[...].astype(F32) for r in refs[nx:nx + n_p]]
        outs = fn(*xv, *pv)
        if not isinstance(outs, (tuple, list)):
            outs = (outs,)
        for o_ref, o in zip(refs[nx + n_p:], outs):
            o_ref[...] = o.astype(o_ref.dtype)

    in_specs = []
    for arr, c0, w in xs:
        assert w == width and c0 % tw == 0
        in_specs.append(pl.BlockSpec((tm, tw), functools.partial(lambda i, j, b: (i, b + j), b=c0 // tw)))
    for p in ps:
        assert p.shape == (1, width)
        in_specs.append(pl.BlockSpec((1, tw), lambda i, j: (0, j)))
    outs = pl.pallas_call(
        body,
        out_shape=[_sds((rows, width), d) for d in out_dtypes],
        grid=(rows // tm, width // tw),
        in_specs=in_specs,
        out_specs=[pl.BlockSpec((tm, tw), lambda i, j: (i, j)) for _ in out_dtypes],
        compiler_params=_cp(("parallel", "parallel")),
        name=name,
    )(*[x[0] for x in xs], *ps)
    return outs


def _ew_bwd(fn, xs, ps, cts, dx_dtypes, *, name, width, tw=None, tm=256, dx_add=None):
    rows = xs[0][0].shape[0]
    tm = _tile(rows, tm, SUBLANE)
    tw = width if tw is None else tw
    nx, n_p = len(xs), len(ps)
    dx_add = dx_add or {}
    flat_cts = [c for group in cts for c in group]
    add_keys = sorted(dx_add)
    n_in = nx + n_p + len(flat_cts) + len(add_keys)
    dx_idx = [i for i, d in enumerate(dx_dtypes) if d is not None]

    def body(*refs):
        i = pl.program_id(1)
        xv = [r[...].astype(F32) for r in refs[:nx]]
        pv = [r[...].astype(F32) for r in refs[nx:nx + n_p]]
        pos = nx + n_p
        ct_vals = []
        for group in cts:
            acc = refs[pos][...].astype(F32)
            pos += 1
            for _ in group[1:]:
                acc = acc + refs[pos][...].astype(F32)
                pos += 1
            ct_vals.append(acc)
        add_vals = {}
        for key in add_keys:
            add_vals[key] = refs[pos][...].astype(F32)
            pos += 1
        out_refs = refs[n_in:]
        outs, vjp = jax.vjp(fn, *xv, *pv)
        grads = vjp(tuple(ct_vals) if isinstance(outs, (tuple, list)) else ct_vals[0])
        o = 0
        for idx in dx_idx:
            gval = grads[idx]
            if idx in add_vals:
                gval = gval + add_vals[idx]
            out_refs[o][...] = gval.astype(out_refs[o].dtype)
            o += 1
        for q in range(n_p):
            gp = grads[nx + q]
            ref = out_refs[o + q]

            @pl.when(i == 0)
            def _(ref=ref, gp=gp):
                ref[...] = gp

            @pl.when(i > 0)
            def _(ref=ref, gp=gp):
                ref[...] += gp

    tile_spec = pl.BlockSpec((tm, tw), lambda j, i: (i, j))
    in_specs = []
    for arr, c0, w in xs:
        assert w == width and c0 % tw == 0
        in_specs.append(pl.BlockSpec((tm, tw), functools.partial(lambda j, i, b: (i, b + j), b=c0 // tw)))
    for p in ps:
        in_specs.append(pl.BlockSpec((1, tw), lambda j, i: (0, j)))
    in_specs += [tile_spec] * (len(flat_cts) + len(add_keys))
    out_shape = [_sds((rows, width), dx_dtypes[idx]) for idx in dx_idx] + [_sds((1, width), F32)] * n_p
    out_specs = [tile_spec] * len(dx_idx) + [pl.BlockSpec((1, tw), lambda j, i: (0, j))] * n_p
    outs = pl.pallas_call(
        body,
        out_shape=out_shape,
        grid=(width // tw, rows // tm),
        in_specs=in_specs,
        out_specs=out_specs,
        compiler_params=_cp(("parallel", "arbitrary")),
        name=name,
    )(*[x[0] for x in xs], *ps, *flat_cts, *[dx_add[k] for k in add_keys])
    return outs[:len(dx_idx)], outs[len(dx_idx):]


def _rms(x):
    return x * lax.rsqrt(jnp.mean(x * x, axis=-1, keepdims=True) + RMS_EPS)


def _fn_norm_mod(x, g, sc, sh):
    return (_rms(x) * g) * (1.0 + sc) + sh


def _fn_residual(x, y, gt, g):
    return x + gt * (_rms(y) * g)


def _fn_gelu(y, u, d):
    return jax.nn.gelu(y + d * u)


def _fn_glu(g, z, b):
    return g * jax.nn.sigmoid(z + b)


def _fn_gates(p0, p1, p2, ys, ya, yc, b0, b1, b2):
    return (jax.nn.sigmoid(p0 + b0) * ys + jax.nn.sigmoid(p1 + b1) * ya + jax.nn.sigmoid(p2 + b2) * yc)


def _fn_disc(log_dt, ar, ai, br_t, bi_t):
    dt = jnp.exp(log_dt)
    mag = jnp.exp(ar * dt)
    lr, li = mag * jnp.cos(ai * dt), mag * jnp.sin(ai * dt)
    den = ar * ar + ai * ai
    fr = ((lr - 1.0) * ar + li * ai) / den
    fi = (li * ar - (lr - 1.0) * ai) / den
    bbr = fr[None] * br_t - fi[None] * bi_t
    bbi = fr[None] * bi_t + fi[None] * br_t
    return lr, li, bbr, bbi


def _ssm_disc_fwd(log_dt, ar, ai, br_t, bi_t):
    g, n = ar.shape

    def body(ld_ref, ar_ref, ai_ref, br_ref, bi_ref, lr_ref, li_ref, bbr_ref, bbi_ref):
        lr, li, bbr, bbi = _fn_disc(ld_ref[...], ar_ref[...], ai_ref[...], br_ref[...], bi_ref[...])
        lr_ref[...] = lr
        li_ref[...] = li
        bbr_ref[...] = bbr
        bbi_ref[...] = bbi

    return pl.pallas_call(
        body,
        out_shape=[_sds((g, n), F32), _sds((g, n), F32), _sds(br_t.shape, F32), _sds(br_t.shape, F32)],
        compiler_params=_cp(),
        name="ssm_disc_fwd",
    )(log_dt, ar, ai, br_t, bi_t)


def _ssm_disc_bwd(log_dt, ar, ai, br_t, bi_t, dlr, dli, dbbr, dbbi):
    g, n = ar.shape

    def body(ld_ref, ar_ref, ai_ref, br_ref, bi_ref, dlr_ref, dli_ref, dbbr_ref, dbbi_ref,
             gld_ref, gar_ref, gai_ref, gbr_ref, gbi_ref):
        _, vjp = jax.vjp(_fn_disc, ld_ref[...], ar_ref[...], ai_ref[...], br_ref[...], bi_ref[...])
        gld, gar, gai, gbr, gbi = vjp((dlr_ref[...], dli_ref[...], dbbr_ref[...], dbbi_ref[...]))
        gld_ref[...] = gld
        gar_ref[...] = gar
        gai_ref[...] = gai
        gbr_ref[...] = gbr
        gbi_ref[...] = gbi

    return pl.pallas_call(
        body,
        out_shape=[_sds((g, 1), F32), _sds((g, n), F32), _sds((g, n), F32), _sds(br_t.shape, F32),
                   _sds(br_t.shape, F32)],
        compiler_params=_cp(),
        name="ssm_disc_bwd",
    )(log_dt, ar, ai, br_t, bi_t, dlr, dli, dbbr, dbbi)


def _cmul(ar, ai, br, bi):
    return ar * br - ai * bi, ar * bi + ai * br


def _scan_tables(lr, li, reverse):
    c = lr.shape[-1]
    p1 = (jnp.broadcast_to(lr, (SUBLANE, c)), jnp.broadcast_to(li, (SUBLANE, c)))
    p2 = _cmul(*p1, *p1)
    p4 = _cmul(*p2, *p2)
    p8 = _cmul(*p4, *p4)
    row = lax.broadcasted_iota(jnp.int32, (SUBLANE, c), 0)
    dist = (SUBLANE - row) if reverse else (row + 1)
    pr, pi = jnp.ones((SUBLANE, c), F32), jnp.zeros((SUBLANE, c), F32)
    for bit, pw in ((1, p1), (2, p2), (4, p4), (8, p8)):
        qr, qi = _cmul(pr, pi, *pw)
        take = (dist & bit) != 0
        pr, pi = jnp.where(take, qr, pr), jnp.where(take, qi, pi)
    return row, (p1, p2, p4), (pr, pi)


def _shift_rows(x, s, row, reverse):
    if reverse:
        return jnp.where(row < SUBLANE - s, pltpu.roll(x, SUBLANE - s, 0), 0.0)
    return jnp.where(row >= s, pltpu.roll(x, s, 0), 0.0)


def _scan_tile(xr, xi, carry, row, pows, carry_pow, reverse):
    for s, pw in zip((1, 2, 4), pows):
        sr, si = _shift_rows(xr, s, row, reverse), _shift_rows(xi, s, row, reverse)
        tr, ti = _cmul(*pw, sr, si)
        xr, xi = xr + tr, xi + ti
    tr, ti = _cmul(*carry_pow, *carry)
    hr, hi = xr + tr, xi + ti
    edge = 0 if reverse else SUBLANE - 1
    c = hr.shape[-1]
    new_carry = (jnp.broadcast_to(hr[edge:edge + 1, :], (SUBLANE, c)),
                 jnp.broadcast_to(hi[edge:edge + 1, :], (SUBLANE, c)))
    return hr, hi, new_carry


def _scan_cols(gn):
    return _tile(gn, 256)


def _ssm_scan_fwd(xcat, lam):
    rows, gn2 = xcat.shape
    c = _scan_cols(gn2 // 2)
    n_tiles = rows // SUBLANE

    def body(lam_ref, x_ref, h_ref):
        lr, li = lam_ref[:, :c], lam_ref[:, c:]
        row, pows, carry_pow = _scan_tables(lr, li, False)

        def step(k, carry):
            t0 = pl.multiple_of(k * SUBLANE, SUBLANE)
            hr, hi, carry = _scan_tile(x_ref[pl.ds(t0, SUBLANE), :c], x_ref[pl.ds(t0, SUBLANE), c:], carry,
                                       row, pows, carry_pow, False)
            h_ref[pl.ds(t0, SUBLANE), :c] = hr
            h_ref[pl.ds(t0, SUBLANE), c:] = hi
            return carry

        zero = jnp.zeros((SUBLANE, c), F32)
        lax.fori_loop(0, n_tiles, step, (zero, zero))

    return pl.pallas_call(
        body,
        out_shape=_sds((rows, gn2), F32),
        grid=(gn2 // (2 * c),),
        in_specs=[pl.BlockSpec((1, 2 * c), lambda j: (0, j)), pl.BlockSpec((rows, 2 * c), lambda j: (0, j))],
        out_specs=pl.BlockSpec((rows, 2 * c), lambda j: (0, j)),
        compiler_params=_cp(("parallel",)),
        name="ssm_scan_fwd",
    )(lam, xcat)


def _ssm_scan_bwd(dhcat, hcat, lam):
    rows, gn2 = dhcat.shape
    c = _scan_cols(gn2 // 2)
    n_tiles = rows // SUBLANE

    def body(lam_ref, dh_ref, h_ref, g_ref, dlam_ref):
        lr, li = lam_ref[:, :c], -lam_ref[:, c:]
        row, pows, carry_pow = _scan_tables(lr, li, True)

        def step(k, state):
            carry, acc_r, acc_i = state
            kk = n_tiles - 1 - k
            t0 = pl.multiple_of(kk * SUBLANE, SUBLANE)
            gr, gi, carry = _scan_tile(dh_ref[pl.ds(t0, SUBLANE), :c], dh_ref[pl.ds(t0, SUBLANE), c:], carry,
                                       row, pows, carry_pow, True)
            g_ref[pl.ds(t0, SUBLANE), :c] = gr
            g_ref[pl.ds(t0, SUBLANE), c:] = gi
            tp = pl.multiple_of(jnp.maximum(kk - 1, 0) * SUBLANE, SUBLANE)
            has_prev = (kk > 0).astype(F32)
            prev_r = pltpu.roll(h_ref[pl.ds(tp, SUBLANE), :c], 1, 0) * has_prev
            prev_i = pltpu.roll(h_ref[pl.ds(tp, SUBLANE), c:], 1, 0) * has_prev
            hpr = jnp.where(row >= 1, pltpu.roll(h_ref[pl.ds(t0, SUBLANE), :c], 1, 0), prev_r)
            hpi = jnp.where(row >= 1, pltpu.roll(h_ref[pl.ds(t0, SUBLANE), c:], 1, 0), prev_i)
            acc_r = acc_r + gr * hpr + gi * hpi
            acc_i = acc_i + gi * hpr - gr * hpi
            return carry, acc_r, acc_i

        zero = jnp.zeros((SUBLANE, c), F32)
        _, acc_r, acc_i = lax.fori_loop(0, n_tiles, step, ((zero, zero), zero, zero))
        dlam_ref[:, :c] = jnp.sum(acc_r, axis=0, keepdims=True)
        dlam_ref[:, c:] = jnp.sum(acc_i, axis=0, keepdims=True)

    blk = pl.BlockSpec((rows, 2 * c), lambda j: (0, j))
    return pl.pallas_call(
        body,
        out_shape=[_sds((rows, gn2), F32), _sds((1, gn2), F32)],
        grid=(gn2 // (2 * c),),
        in_specs=[pl.BlockSpec((1, 2 * c), lambda j: (0, j)), blk, blk],
        out_specs=[blk, pl.BlockSpec((1, 2 * c), lambda j: (0, j))],
        compiler_params=_cp(("parallel",)),
        name="ssm_scan_bwd",
    )(lam, dhcat, hcat)


def _shift_down(x, k, row):
    return x if k == 0 else jnp.where(row >= k, pltpu.roll(x, k, 0), 0.0)


def _shift_up(x, k, row):
    n = x.shape[0]
    return x if k == 0 else jnp.where(row < n - k, pltpu.roll(x, n - k, 0), 0.0)


def _taps(w_ref):
    return [w_ref[k:k + 1, :] for k in range(3)]


def _conv3(x, w, row):
    return sum(w[k] * _shift_down(x, k, row) for k in range(3))


def _conv3_bwd(x, w, dy, row):
    dx = sum(w[k] * _shift_up(dy, k, row) for k in range(3))
    dw = [jnp.sum(dy * _shift_down(x, k, row), axis=0, keepdims=True) for k in range(3)]
    return dx, dw


def _gconv_fwd(proj, off, cw, w):
    rows = proj.shape[0]
    tc = _tile(cw, LANE)
    nb = cw // tc

    def body(b_ref, c_ref, h_ref, w_ref, o_ref):
        row = lax.broadcasted_iota(jnp.int32, (rows, tc), 0)
        o_ref[...] = (b_ref[...] * _conv3(c_ref[...] * h_ref[...], _taps(w_ref), row)).astype(o_ref.dtype)

    specs = [pl.BlockSpec((rows, tc), functools.partial(lambda j, b: (0, b + j), b=(off + q * cw) // tc))
             for q in range(3)]
    return pl.pallas_call(
        body,
        out_shape=_sds((rows, cw), BF16),
        grid=(nb,),
        in_specs=specs + [pl.BlockSpec((3, tc), lambda j: (0, j))],
        out_specs=pl.BlockSpec((rows, tc), lambda j: (0, j)),
        compiler_params=_cp(("parallel",)),
        name="gconv_fwd",
    )(proj, proj, proj, w)


def _gconv_bwd(proj, off, cw, w, dy):
    rows = proj.shape[0]
    tc = _tile(cw, LANE)
    nb = cw // tc

    def body(b_ref, c_ref, h_ref, w_ref, dy_ref, db_ref, dc_ref, dh_ref, dw_ref):
        row = lax.broadcasted_iota(jnp.int32, (rows, tc), 0)
        cv, hv, dyv = c_ref[...], h_ref[...], dy_ref[...].astype(F32)
        t = cv * hv
        db_ref[...] = (dyv * _conv3(t, _taps(w_ref), row)).astype(db_ref.dtype)
        dt, dw = _conv3_bwd(t, _taps(w_ref), dyv * b_ref[...], row)
        dc_ref[...] = (dt * hv).astype(dc_ref.dtype)
        dh_ref[...] = (dt * cv).astype(dh_ref.dtype)
        for k in range(3):
            dw_ref[k:k + 1, :] = dw[k]

    specs = [pl.BlockSpec((rows, tc), functools.partial(lambda j, b: (0, b + j), b=(off + q * cw) // tc))
             for q in range(3)]
    col = pl.BlockSpec((rows, tc), lambda j: (0, j))
    wspec = pl.BlockSpec((3, tc), lambda j: (0, j))
    return pl.pallas_call(
        body,
        out_shape=[_sds((rows, cw), BF16)] * 3 + [_sds((3, cw), F32)],
        grid=(nb,),
        in_specs=specs + [wspec, col],
        out_specs=[col, col, col, wspec],
        compiler_params=_cp(("parallel",)),
        name="gconv_bwd",
    )(proj, proj, proj, w, dy)


def _ffn_act_fwd(up, w):
    rows, f2 = up.shape
    f = f2 // 2
    tc = _tile(f, LANE)
    nb = f // tc

    def body(a_ref, b_ref, wa_ref, wb_ref, o_ref):
        row = lax.broadcasted_iota(jnp.int32, (rows, tc), 0)
        a = _conv3(a_ref[...], _taps(wa_ref), row)
        b = _conv3(b_ref[...], _taps(wb_ref), row)
        o_ref[...] = (jax.nn.silu(a) * b).astype(o_ref.dtype)

    return pl.pallas_call(
        body,
        out_shape=_sds((rows, f), BF16),
        grid=(nb,),
        in_specs=[pl.BlockSpec((rows, tc), lambda j: (0, j)), pl.BlockSpec((rows, tc), lambda j: (0, nb + j)),
                  pl.BlockSpec((3, tc), lambda j: (0, j)), pl.BlockSpec((3, tc), lambda j: (0, nb + j))],
        out_specs=pl.BlockSpec((rows, tc), lambda j: (0, j)),
        compiler_params=_cp(("parallel",)),
        name="ffn_act_fwd",
    )(up, up, w, w)


def _ffn_act_bwd(up, w, dact):
    rows, f2 = up.shape
    f = f2 // 2
    tc = _tile(f, LANE)
    nb = f // tc

    def body(a_ref, b_ref, wa_ref, wb_ref, d_ref, da_ref, db_ref, dwa_ref, dwb_ref):
        row = lax.broadcasted_iota(jnp.int32, (rows, tc), 0)
        av, bv, dv = a_ref[...], b_ref[...], d_ref[...].astype(F32)
        ac = _conv3(av, _taps(wa_ref), row)
        bc = _conv3(bv, _taps(wb_ref), row)
        _, vjp = jax.vjp(lambda p, q: jax.nn.silu(p) * q, ac, bc)
        dac, dbc = vjp(dv)
        dxa, dwa = _conv3_bwd(av, _taps(wa_ref), dac, row)
        dxb, dwb = _conv3_bwd(bv, _taps(wb_ref), dbc, row)
        da_ref[...] = dxa.astype(da_ref.dtype)
        db_ref[...] = dxb.astype(db_ref.dtype)
        for k in range(3):
            dwa_ref[k:k + 1, :] = dwa[k]
            dwb_ref[k:k + 1, :] = dwb[k]

    col = pl.BlockSpec((rows, tc), lambda j: (0, j))
    wspec = pl.BlockSpec((3, tc), lambda j: (0, j))
    return pl.pallas_call(
        body,
        out_shape=[_sds((rows, f), BF16)] * 2 + [_sds((3, f), F32)] * 2,
        grid=(nb,),
        in_specs=[col, pl.BlockSpec((rows, tc), lambda j: (0, nb + j)), wspec,
                  pl.BlockSpec((3, tc), lambda j: (0, nb + j)), col],
        out_specs=[col, col, wspec, wspec],
        compiler_params=_cp(("parallel",)),
        name="ffn_act_bwd",
    )(up, up, w, w, dact)


def _attn_scores(q, kc, kp, slope, dilation, has_prev):
    scale = HEAD_DIM ** -0.5
    nt = (((1,), (1,)), ((), ()))
    s_c = lax.dot_general(q, kc, nt, preferred_element_type=F32) * scale
    s_p = lax.dot_general(q, kp, nt, preferred_element_type=F32) * scale
    qi = lax.broadcasted_iota(jnp.int32, (ATTN_BLOCK, ATTN_BLOCK), 0)
    kj = lax.broadcasted_iota(jnp.int32, (ATTN_BLOCK, ATTN_BLOCK), 1)
    dist_c = qi - kj
    dist_p = dist_c + ATTN_BLOCK
    s_c = jnp.where(dist_c >= 0, s_c - slope * (dist_c * dilation).astype(F32), NEG_INF)
    s_p = jnp.where((dist_p <= ATTN_BLOCK) & has_prev, s_p - slope * (dist_p * dilation).astype(F32), NEG_INF)
    return s_c, s_p


def _attn_fwd(q, k, v, slopes, dilation):
    bp, m, e = q.shape
    nb = m // ATTN_BLOCK

    def body(q_ref, kc_ref, kp_ref, vc_ref, vp_ref, sl_ref, o_ref, lse_ref):
        i = pl.program_id(1)
        s_c, s_p = _attn_scores(q_ref[...], kc_ref[...], kp_ref[...], sl_ref[:, :1], dilation, i > 0)
        mx = jnp.maximum(jnp.max(s_c, axis=-1, keepdims=True), jnp.max(s_p, axis=-1, keepdims=True))
        p_c, p_p = jnp.exp(s_c - mx), jnp.exp(s_p - mx)
        den = jnp.sum(p_c, axis=-1, keepdims=True) + jnp.sum(p_p, axis=-1, keepdims=True)
        o = (jnp.dot(p_c.astype(BF16), vc_ref[...], preferred_element_type=F32)
             + jnp.dot(p_p.astype(BF16), vp_ref[...], preferred_element_type=F32))
        o_ref[...] = o / den
        lse_ref[...] = jnp.broadcast_to(mx + jnp.log(den), (ATTN_BLOCK, e))

    cur = pl.BlockSpec((None, ATTN_BLOCK, e), lambda b, i: (b, i, 0))
    prev = pl.BlockSpec((None, ATTN_BLOCK, e), lambda b, i: (b, jnp.maximum(i - 1, 0), 0))
    return pl.pallas_call(
        body,
        out_shape=[_sds((bp, m, e), F32), _sds((bp, m, e), F32)],
        grid=(bp, nb),
        in_specs=[cur, cur, prev, cur, prev, pl.BlockSpec((None, 1, LANE), lambda b, i: (b, 0, 0))],
        out_specs=[cur, cur],
        compiler_params=_cp(("parallel", "parallel")),
        name=f"attn_fwd_d{dilation}",
    )(q, k, k, v, v, slopes)


def _attn_bwd(q, k, v, do, lse, delta, slopes, dilation):
    bp, m, e = q.shape
    nb = m // ATTN_BLOCK

    def body(q_ref, kc_ref, kp_ref, vc_ref, vp_ref, do_ref, lse_ref, dl_ref, sl_ref,
             dq_ref, dk_ref, dv_ref, ck_ref, cv_ref):
        step = pl.program_id(1)
        i = nb - 1 - step
        scale = HEAD_DIM ** -0.5
        nt = (((1,), (1,)), ((), ()))
        qv, kc, kp, vc, vp, dov = q_ref[...], kc_ref[...], kp_ref[...], vc_ref[...], vp_ref[...], do_ref[...]
        s_c, s_p = _attn_scores(qv, kc, kp, sl_ref[:, :1], dilation, i > 0)
        lse_col, dl_col = lse_ref[:, :1], dl_ref[:, :1]
        p_c, p_p = jnp.exp(s_c - lse_col), jnp.exp(s_p - lse_col)
        ds_c = p_c * (lax.dot_general(dov, vc, nt, preferred_element_type=F32) - dl_col)
        ds_p = p_p * (lax.dot_general(dov, vp, nt, preferred_element_type=F32) - dl_col)
        dq = (jnp.dot(ds_c.astype(BF16), kc, preferred_element_type=F32)
              + jnp.dot(ds_p.astype(BF16), kp, preferred_element_type=F32)) * scale
        dq_ref[...] = dq.astype(dq_ref.dtype)

        @pl.when(step == 0)
        def _():
            ck_ref[...] = jnp.zeros_like(ck_ref)
            cv_ref[...] = jnp.zeros_like(cv_ref)

        dk_c = jnp.dot(ds_c.T.astype(BF16), qv, preferred_element_type=F32) * scale
        dv_c = jnp.dot(p_c.T.astype(BF16), dov, preferred_element_type=F32)
        dk_ref[...] = (dk_c + ck_ref[...]).astype(dk_ref.dtype)
        dv_ref[...] = (dv_c + cv_ref[...]).astype(dv_ref.dtype)
        ck_ref[...] = jnp.dot(ds_p.T.astype(BF16), qv, preferred_element_type=F32) * scale
        cv_ref[...] = jnp.dot(p_p.T.astype(BF16), dov, preferred_element_type=F32)

    cur = pl.BlockSpec((None, ATTN_BLOCK, e), lambda b, s: (b, nb - 1 - s, 0))
    prev = pl.BlockSpec((None, ATTN_BLOCK, e), lambda b, s: (b, jnp.maximum(nb - 2 - s, 0), 0))
    return pl.pallas_call(
        body,
        out_shape=[_sds((bp, m, e), BF16)] * 3,
        grid=(bp, nb),
        in_specs=[cur, cur, prev, cur, prev, cur, cur, cur, pl.BlockSpec((None, 1, LANE), lambda b, s: (b, 0, 0))],
        out_specs=[cur, cur, cur],
        scratch_shapes=[pltpu.VMEM((ATTN_BLOCK, e), F32), pltpu.VMEM((ATTN_BLOCK, e), F32)],
        compiler_params=_cp(("parallel", "arbitrary")),
        name=f"attn_bwd_d{dilation}",
    )(q, k, k, v, v, do, lse, delta, slopes)


def _attn_merge(outs, lses):
    rows, aw = outs[0].shape
    tm = _tile(rows, 256, SUBLANE)

    def body(o0, o1, o2, l0, l1, l2, o_ref, lse_ref):
        lv = [l0[...], l1[...], l2[...]]
        mx = jnp.maximum(jnp.maximum(lv[0], lv[1]), lv[2])
        w = [jnp.exp(t - mx) for t in lv]
        den = w[0] + w[1] + w[2]
        o_ref[...] = (w[0] * o0[...] + w[1] * o1[...] + w[2] * o2[...]) / den
        lse_ref[...] = mx + jnp.log(den)

    spec = pl.BlockSpec((tm, aw), lambda i: (i, 0))
    return pl.pallas_call(
        body,
        out_shape=[_sds((rows, aw), F32)] * 2,
        grid=(rows // tm,),
        in_specs=[spec] * 6,
        out_specs=[spec, spec],
        compiler_params=_cp(("parallel",)),
        name="attn_merge",
    )(*outs, *lses)


def _attn_delta(do, o, head_ones):
    rows, aw = do.shape
    tm = _tile(rows, 256, SUBLANE)

    def body(do_ref, o_ref, e_ref, d_ref):
        d_ref[...] = jnp.dot(do_ref[...] * o_ref[...], e_ref[...], preferred_element_type=F32,
                             precision=lax.Precision.HIGHEST)

    spec = pl.BlockSpec((tm, aw), lambda i: (i, 0))
    return pl.pallas_call(
        body,
        out_shape=_sds((rows, aw), F32),
        grid=(rows // tm,),
        in_specs=[spec, spec, pl.BlockSpec((aw, aw), lambda i: (0, 0))],
        out_specs=spec,
        compiler_params=_cp(("parallel",)),
        name="attn_delta",
    )(do, o, head_ones)


def _to_residues(t, dilation, hp):
    rows = t.shape[0]
    m = rows // dilation
    return t.reshape(m, dilation, hp, HEAD_DIM).transpose(1, 2, 0, 3).reshape(dilation * hp, m, HEAD_DIM)


def _from_residues(t, dilation, hp):
    m = t.shape[1]
    return t.reshape(dilation, hp, m, HEAD_DIM).transpose(2, 0, 1, 3).reshape(m * dilation, hp * HEAD_DIM)


def _alibi_slopes(pattern, hp, dilation):
    n_heads = hp * len(DSWA_PATTERNS)
    s = np.array([2.0 ** (-8.0 * (pattern * hp + h + 1) / n_heads) for h in range(hp)], dtype=np.float32)
    s = np.tile(s, dilation)
    return jnp.asarray(np.broadcast_to(s[:, None, None], (dilation * hp, 1, LANE)).copy())


def _loss_fwd_bwd(y, target):
    rows, d = y.shape
    tm = _tile(rows, 256, SUBLANE)

    def body(y_ref, t_ref, dy_ref, l_ref):
        i = pl.program_id(0)
        err = y_ref[...] - t_ref[...]
        dy_ref[...] = err * (1.0 / d)
        part = 0.5 * jnp.sum(jnp.mean(err * err, axis=-1, keepdims=True), axis=0, keepdims=True)

        @pl.when(i == 0)
        def _():
            l_ref[...] = jnp.zeros_like(l_ref)

        l_ref[...] += jnp.broadcast_to(part, l_ref.shape)

    spec = pl.BlockSpec((tm, d), lambda i: (i, 0))
    dy, loss = pl.pallas_call(
        body,
        out_shape=[_sds((rows, d), F32), _sds((SUBLANE, LANE), F32)],
        grid=(rows // tm,),
        in_specs=[spec, spec],
        out_specs=[spec, pl.BlockSpec((SUBLANE, LANE), lambda i: (0, 0))],
        compiler_params=_cp(("arbitrary",)),
        name="loss",
    )(y, target)
    return dy, loss[0, 0]


def _adam_math(w, g, m, v):
    m = ADAM_B1 * m + (1.0 - ADAM_B1) * g
    v = ADAM_B2 * v + (1.0 - ADAM_B2) * jnp.square(g)
    m_hat = m / (1.0 - ADAM_B1 ** ADAM_STEP)
    v_hat = v / (1.0 - ADAM_B2 ** ADAM_STEP)
    delta = -ADAM_LR * (m_hat / (jnp.sqrt(v_hat) + ADAM_EPS) + ADAM_WD * w)
    return delta, m, v


def _as2d(a):
    if a.ndim == 1:
        return a.reshape(1, -1)
    return a.reshape(-1, a.shape[-1])


def _adam(w, g, m, v, name):
    shape = w.shape
    w2, g2, m2, v2 = _as2d(w), _as2d(g), _as2d(m), _as2d(v)
    r, c = w2.shape
    tr = _tile(r, 512, SUBLANE)
    tc = _tile(c, 1024)

    def body(w_ref, g_ref, m_ref, v_ref, d_ref, mo_ref, vo_ref):
        delta, mn, vn = _adam_math(w_ref[...], g_ref[...], m_ref[...], v_ref[...])
        d_ref[...] = delta
        mo_ref[...] = mn
        vo_ref[...] = vn

    spec = pl.BlockSpec((tr, tc), lambda i, j: (i, j))
    outs = pl.pallas_call(
        body,
        out_shape=[_sds((r, c), F32)] * 3,
        grid=(r // tr, c // tc),
        in_specs=[spec] * 4,
        out_specs=[spec] * 3,
        compiler_params=_cp(("parallel", "parallel")),
        name=name,
    )(w2, g2, m2, v2)
    return [o.reshape(shape) for o in outs]


def _wmod_grad_adam(c_t, dmod, w, m, v):
    nl, d, cols = w.shape
    nex = c_t.shape[1]
    tr = _tile(d, 256, SUBLANE)
    tc = _tile(cols, 1024)

    def body(c_ref, dm_ref, w_ref, m_ref, v_ref, g_ref, d_ref, mo_ref, vo_ref):
        cond = jax.nn.silu(c_ref[...]).astype(BF16)
        g = jnp.dot(cond, dm_ref[...].astype(BF16), preferred_element_type=F32)
        delta, mn, vn = _adam_math(w_ref[...], g, m_ref[...], v_ref[...])
        g_ref[...] = g
        d_ref[...] = delta
        mo_ref[...] = mn
        vo_ref[...] = vn

    spec = pl.BlockSpec((None, tr, tc), lambda l, i, j: (l, i, j))
    return pl.pallas_call(
        body,
        out_shape=[_sds((nl, d, cols), F32)] * 4,
        grid=(nl, d // tr, cols // tc),
        in_specs=[pl.BlockSpec((tr, nex), lambda l, i, j: (i, 0)),
                  pl.BlockSpec((None, nex, tc), lambda l, i, j: (l, 0, j)), spec, spec, spec],
        out_specs=[spec] * 4,
        compiler_params=_cp(("parallel", "parallel", "parallel")),
        name="wmod_grad_adam",
    )(c_t, dmod, w, m, v)


def _my_pos():
    return lax.axis_index("x"), lax.axis_index("y"), lax.axis_index("c")


def _ag8(x4, select_half, name):
    a, s, r, c = x4.shape
    assert s == (2 if select_half else 1)

    def body(x_ref, out_ref, send_sems, recv_sems, local_sem):
        x, y, cc = _my_pos()
        me, sibling = (x, y, cc), (x, y, 1 - cc)
        chips = [(1 - x, y), (x, 1 - y), (1 - x, 1 - y)]
        src_mine = x_ref.at[:, pl.ds(cc if select_half else 0, 1)]

        def blk(px, py, pc):
            return out_ref.at[:, pl.ds(4 * px + 2 * py + pc, 1)]

        def copy(k, block, to, src=None):
            return pltpu.make_async_remote_copy(
                src_ref=blk(*block) if src is None else src, dst_ref=blk(*block),
                send_sem=send_sems.at[k], recv_sem=recv_sems.at[k], device_id=to, device_id_type=MESH)

        mine = pltpu.make_async_copy(src_mine, blk(*me), local_sem)
        mine.start()
        first = [copy(0, me, sibling, src=src_mine)]
        first += [copy(1 + j, me, (*chip, cc), src=src_mine) for j, chip in enumerate(chips)]
        for cp in first:
            cp.start()
        passed = [copy(4 + j, (*chip, cc), sibling) for j, chip in enumerate(chips)]
        for j, chip in enumerate(chips):
            copy(1 + j, (*chip, cc), me).wait_recv()
            passed[j].start()
        copy(0, sibling, me).wait_recv()
        for j, chip in enumerate(chips):
            copy(4 + j, (*chip, 1 - cc), me).wait_recv()
        for cp in first + passed:
            cp.wait_send()
        mine.wait()

    return pl.pallas_call(
        body,
        out_shape=_sds((a, N_DEV, r, c), x4.dtype),
        in_specs=[ANY],
        out_specs=ANY,
        scratch_shapes=[pltpu.SemaphoreType.DMA((7,)), pltpu.SemaphoreType.DMA((7,)), pltpu.SemaphoreType.DMA],
        name=name,
    )(x4)


def _rs_sibling(g8, name):
    a, _, r, c = g8.shape
    g5 = g8.reshape(a, N_CHIP, 2, r, c)

    def body(g_ref, out_ref, send_sem, recv_sem):
        x, y, cc = _my_pos()
        cp = pltpu.make_async_remote_copy(
            src_ref=g_ref.at[:, :, pl.ds(1 - cc, 1)], dst_ref=out_ref, send_sem=send_sem, recv_sem=recv_sem,
            device_id=(x, y, 1 - cc), device_id_type=MESH)
        cp.start()
        cp.wait()

    return pl.pallas_call(
        body,
        out_shape=_sds((a, N_CHIP, 1, r, c), g8.dtype),
        in_specs=[ANY],
        out_specs=ANY,
        scratch_shapes=[pltpu.SemaphoreType.DMA, pltpu.SemaphoreType.DMA],
        name=name,
    )(g5)


def _chip_of(x, y, k):
    return (1 - x if k & 2 else x), (1 - y if k & 1 else y)


def _rs_chips(s_rem, name):
    def body(s_ref, out_ref, send_sems, recv_sems):
        x, y, cc = _my_pos()
        copies = []
        for k in (1, 2, 3):
            px, py = _chip_of(x, y, k)
            copies.append(pltpu.make_async_remote_copy(
                src_ref=s_ref.at[:, pl.ds(k - 1, 1)], dst_ref=out_ref.at[:, pl.ds(k - 1, 1)],
                send_sem=send_sems.at[k - 1], recv_sem=recv_sems.at[k - 1],
                device_id=(px, py, cc), device_id_type=MESH))
        for cp in copies:
            cp.start()
        for cp in copies:
            cp.wait()

    return pl.pallas_call(
        body,
        out_shape=_sds(s_rem.shape, s_rem.dtype),
        in_specs=[ANY],
        out_specs=ANY,
        scratch_shapes=[pltpu.SemaphoreType.DMA((3,)), pltpu.SemaphoreType.DMA((3,))],
        name=name,
    )(s_rem)


def _share_halves(halves, name):
    def body(in_ref, out_ref, send_sem, recv_sem):
        x, y, cc = _my_pos()
        cp = pltpu.make_async_remote_copy(
            src_ref=in_ref.at[:, pl.ds(cc, 1)], dst_ref=out_ref.at[:, pl.ds(cc, 1)], send_sem=send_sem,
            recv_sem=recv_sem, device_id=(x, y, 1 - cc), device_id_type=MESH)
        cp.start()
        cp.wait()

    return pl.pallas_call(
        body,
        out_shape=_sds(halves.shape, halves.dtype),
        in_specs=[ANY],
        out_specs=ANY,
        scratch_shapes=[pltpu.SemaphoreType.DMA, pltpu.SemaphoreType.DMA],
        input_output_aliases={0: 0},
        name=name,
    )(halves)


def _rs_add_remote(g8, recv_a, pos, name):
    a, _, r, c = g8.shape
    ra = recv_a.reshape(a, N_CHIP, r, c)
    tr = _tile(r, 256, SUBLANE)
    tc = _tile(c, 1024)

    def body(pos_ref, g_ref, r_ref, o_ref):
        del pos_ref
        o_ref[...] = (g_ref[...] + r_ref[...]).astype(o_ref.dtype)

    grid_spec = pltpu.PrefetchScalarGridSpec(
        num_scalar_prefetch=1,
        grid=(a, 3, r // tr, c // tc),
        in_specs=[pl.BlockSpec((None, None, tr, tc), lambda l, k, i, j, p: (l, 2 * (p[0] ^ (k + 1)) + p[1], i, j)),
                  pl.BlockSpec((None, None, tr, tc), lambda l, k, i, j, p: (l, p[0] ^ (k + 1), i, j))],
        out_specs=pl.BlockSpec((None, None, tr, tc), lambda l, k, i, j, p: (l, k, i, j)),
    )
    return pl.pallas_call(
        body,
        out_shape=_sds((a, 3, r, c), BF16),
        grid_spec=grid_spec,
        compiler_params=_cp(("parallel",) * 4),
        name=name,
    )(pos, g8, ra)


def _rs_add_final(g8, recv_a, recv_b, pos, name):
    a, _, r, c = g8.shape
    ra = recv_a.reshape(a, N_CHIP, r, c)
    tr = _tile(r, 256, SUBLANE)
    tc = _tile(c, 1024)

    def body(pos_ref, g_ref, r_ref, b0_ref, b1_ref, b2_ref, o_ref):
        del pos_ref
        o_ref[...] = (((g_ref[...] + r_ref[...]) + b0_ref[...].astype(F32)) + b1_ref[...].astype(F32)
                      ) + b2_ref[...].astype(F32)

    def bspec(k):
        return pl.BlockSpec((None, None, tr, tc), functools.partial(lambda l, i, j, p, k: (l, k, i, j), k=k))

    grid_spec = pltpu.PrefetchScalarGridSpec(
        num_scalar_prefetch=1,
        grid=(a, r // tr, c // tc),
        in_specs=[pl.BlockSpec((None, None, tr, tc), lambda l, i, j, p: (l, 2 * p[0] + p[1], i, j)),
                  pl.BlockSpec((None, None, tr, tc), lambda l, i, j, p: (l, p[0], i, j)),
                  bspec(0), bspec(1), bspec(2)],
        out_specs=pl.BlockSpec((None, None, tr, tc), lambda l, i, j, p: (l, p[1], i, j)),
    )
    return pl.pallas_call(
        body,
        out_shape=_sds((a, 2, r, c), F32),
        grid_spec=grid_spec,
        compiler_params=_cp(("parallel",) * 3),
        name=name,
    )(pos, g8, ra, recv_b, recv_b, recv_b)


def _reduce_scatter(g8, pos, tag):
    recv_a = _rs_sibling(g8, f"rs_sibling_{tag}")
    s_rem = _rs_add_remote(g8, recv_a, pos, f"rs_add_remote_{tag}")
    recv_b = _rs_chips(s_rem, f"rs_chips_{tag}")
    mine = _rs_add_final(g8, recv_a, recv_b, pos, f"rs_add_final_{tag}")
    return _share_halves(mine, f"rs_share_{tag}")


def _sum8(x8, name):
    _, r, c = x8.shape
    tr = _tile(r, 256, SUBLANE)

    def body(x_ref, o_ref):
        acc = x_ref[0]
        for b in range(1, N_DEV):
            acc = acc + x_ref[b]
        o_ref[...] = acc

    return pl.pallas_call(
        body,
        out_shape=_sds((r, c), F32),
        grid=(r // tr,),
        in_specs=[pl.BlockSpec((N_DEV, tr, c), lambda i: (0, i, 0))],
        out_specs=pl.BlockSpec((tr, c), lambda i: (i, 0)),
        compiler_params=_cp(("parallel",)),
        name=name,
    )(x8)


def _block_diag(t):
    g, p, q = t.shape
    eye = jnp.eye(g, dtype=t.dtype)
    return (t[:, :, None, :] * eye[:, None, :, None]).reshape(g * p, g * q)


def _diag_blocks(mat, g):
    p, q = mat.shape[0] // g, mat.shape[1] // g
    eye = jnp.eye(g, dtype=mat.dtype)
    return jnp.sum(mat.reshape(g, p, g, q) * eye[:, None, :, None], axis=2)


def _interleave(re, im, c):
    lead = re.shape[:-1]
    gn = re.shape[-1]
    return jnp.stack([re.reshape(*lead, gn // c, c), im.reshape(*lead, gn // c, c)], axis=-2).reshape(*lead, 2 * gn)


def _deinterleave(cat, c):
    lead = cat.shape[:-1]
    gn = cat.shape[-1] // 2
    t = cat.reshape(*lead, gn // c, 2, c)
    return t[..., 0, :].reshape(*lead, gn), t[..., 1, :].reshape(*lead, gn)


def kernel(x, c, w_mod, b_mod, g_pre_mix, g_post_mix, g_pre_ffn, g_post_ffn, w_in, ssm_log_dt, ssm_a_re, ssm_a_im, ssm_b_re, ssm_b_im, ssm_c_re, ssm_c_im, ssm_d, w_glu, b_glu, conv_mix_w, w_ssm_out, w_attn_out, w_conv_out, b_gate, w_o, w_up, ffn_conv_w, w_down, loss_target, m_w_mod, m_b_mod, m_g_pre_mix, m_g_post_mix, m_g_pre_ffn, m_g_post_ffn, m_w_in, m_ssm_log_dt, m_ssm_a_re, m_ssm_a_im, m_ssm_b_re, m_ssm_b_im, m_ssm_c_re, m_ssm_c_im, m_ssm_d, m_w_glu, m_b_glu, m_conv_mix_w, m_w_ssm_out, m_w_attn_out, m_w_conv_out, m_b_gate, m_w_o, m_w_up, m_ffn_conv_w, m_w_down, v_w_mod, v_b_mod, v_g_pre_mix, v_g_post_mix, v_g_pre_ffn, v_g_post_ffn, v_w_in, v_ssm_log_dt, v_ssm_a_re, v_ssm_a_im, v_ssm_b_re, v_ssm_b_im, v_ssm_c_re, v_ssm_c_im, v_ssm_d, v_w_glu, v_b_glu, v_conv_mix_w, v_w_ssm_out, v_w_attn_out, v_w_conv_out, v_b_gate, v_w_o, v_w_up, v_ffn_conv_w, v_w_down):
    weights = dict(w_mod=w_mod, b_mod=b_mod, g_pre_mix=g_pre_mix, g_post_mix=g_post_mix, g_pre_ffn=g_pre_ffn, g_post_ffn=g_post_ffn, w_in=w_in, ssm_log_dt=ssm_log_dt, ssm_a_re=ssm_a_re, ssm_a_im=ssm_a_im, ssm_b_re=ssm_b_re, ssm_b_im=ssm_b_im, ssm_c_re=ssm_c_re, ssm_c_im=ssm_c_im, ssm_d=ssm_d, w_glu=w_glu, b_glu=b_glu, conv_mix_w=conv_mix_w, w_ssm_out=w_ssm_out, w_attn_out=w_attn_out, w_conv_out=w_conv_out, b_gate=b_gate, w_o=w_o, w_up=w_up, ffn_conv_w=ffn_conv_w, w_down=w_down)
    mom_m = dict(w_mod=m_w_mod, b_mod=m_b_mod, g_pre_mix=m_g_pre_mix, g_post_mix=m_g_post_mix, g_pre_ffn=m_g_pre_ffn, g_post_ffn=m_g_post_ffn, w_in=m_w_in, ssm_log_dt=m_ssm_log_dt, ssm_a_re=m_ssm_a_re, ssm_a_im=m_ssm_a_im, ssm_b_re=m_ssm_b_re, ssm_b_im=m_ssm_b_im, ssm_c_re=m_ssm_c_re, ssm_c_im=m_ssm_c_im, ssm_d=m_ssm_d, w_glu=m_w_glu, b_glu=m_b_glu, conv_mix_w=m_conv_mix_w, w_ssm_out=m_w_ssm_out, w_attn_out=m_w_attn_out, w_conv_out=m_w_conv_out, b_gate=m_b_gate, w_o=m_w_o, w_up=m_w_up, ffn_conv_w=m_ffn_conv_w, w_down=m_w_down)
    mom_v = dict(w_mod=v_w_mod, b_mod=v_b_mod, g_pre_mix=v_g_pre_mix, g_post_mix=v_g_post_mix, g_pre_ffn=v_g_pre_ffn, g_post_ffn=v_g_post_ffn, w_in=v_w_in, ssm_log_dt=v_ssm_log_dt, ssm_a_re=v_ssm_a_re, ssm_a_im=v_ssm_a_im, ssm_b_re=v_ssm_b_re, ssm_b_im=v_ssm_b_im, ssm_c_re=v_ssm_c_re, ssm_c_im=v_ssm_c_im, ssm_d=v_ssm_d, w_glu=v_w_glu, b_glu=v_b_glu, conv_mix_w=v_conv_mix_w, w_ssm_out=v_w_ssm_out, w_attn_out=v_w_attn_out, w_conv_out=v_w_conv_out, b_gate=v_b_gate, w_o=v_w_o, w_up=v_w_up, ffn_conv_w=v_ffn_conv_w, w_down=v_w_down)
    names = list(weights)

    nl = w_in.shape[0]
    seq, d = x.shape[1], x.shape[2]
    sw = d // 4
    groups = sw // SSM_GROUP
    gn = groups * SSM_STATE
    hp = sw // HEAD_DIM
    qw = 3 * sw
    off_q, off_k, off_v = sw, sw + qw, sw + 2 * qw
    off_conv = sw + 3 * qw
    off_gate = off_conv + 3 * sw
    n_in = off_gate + 3 * d
    f = w_down.shape[1] * N_CHIP
    scan_c = _scan_cols(gn)
    assert seq % (ATTN_BLOCK * DSWA_PATTERNS[-1][1]) == 0 and all(w // dl == ATTN_BLOCK for w, dl in DSWA_PATTERNS)

    px, py, pc = _my_pos()
    chip = 2 * px + py
    dev = 2 * chip + pc
    pos = jnp.stack([chip, pc]).astype(jnp.int32)

    x2 = x.reshape(seq, d)
    target2 = loss_target.reshape(seq, d)

    c_all = _ag8(c.reshape(1, 1, 1, d), False, "ag_cond").reshape(N_DEV, d)
    c_pad = jnp.concatenate([c_all, jnp.zeros((SUBLANE, d), F32)], axis=0)
    mcols = w_mod.shape[2]
    w_mod4 = w_mod.reshape(nl, 1, d, mcols)
    mod_loc = jnp.stack([_mm_nn(c_pad, w_mod4, l, name="mod_fwd", a_fn=jax.nn.silu) for l in range(nl)])
    mod_all = _ag8(mod_loc.reshape(nl, 1, 2 * SUBLANE, mcols), False, "ag_mod")
    mod_rows = lax.dynamic_slice_in_dim(mod_all[:, 0::2], dev, 1, axis=2)
    mod = mod_rows.reshape(nl, N_CHIP * mcols) + b_mod
    mods = mod.reshape(nl, 6, 1, d)

    def gather_halves(w):
        _, r, cols = w.shape
        return _ag8(w.astype(BF16).reshape(nl, 2, r // 2, cols), True, "ag_weight")

    def col_sharded(w):
        _, r, cols = w.shape
        return gather_halves(w).reshape(nl, N_CHIP, r, cols)

    def row_sharded(w):
        _, r, cols = w.shape
        return gather_halves(w).reshape(nl, 1, N_CHIP * r, cols)

    def gather_whole(w):
        _, r, cols = w.shape
        got = _ag8(w.reshape(nl, 1, r, cols), False, "ag_small_weight")[:, 0::2]
        return got.transpose(0, 2, 1, 3).reshape(nl, r, N_CHIP * cols)

    wf = dict(w_in=col_sharded(w_in), w_glu=row_sharded(w_glu), w_ssm_out=col_sharded(w_ssm_out),
              w_attn_out=col_sharded(w_attn_out), w_conv_out=col_sharded(w_conv_out), w_o=row_sharded(w_o),
              w_up=col_sharded(w_up), w_down=row_sharded(w_down))
    conv_w_full = gather_whole(conv_mix_w)
    ffn_w_full = gather_whole(ffn_conv_w)

    head_ones = jnp.asarray(np.kron(np.eye(hp, dtype=np.float32), np.ones((HEAD_DIM, HEAD_DIM), np.float32)))
    slopes = [_alibi_slopes(p, hp, dl) for p, (_, dl) in enumerate(DSWA_PATTERNS)]

    def row(v):
        return v.reshape(1, -1)

    saved = []
    xl = x2
    for l in range(nl):
        sh1, sc1, gt1, sh2, sc2, gt2 = [mods[l, q] for q in range(6)]
        s = dict(x_in=xl)
        (h1,) = _ew_fwd(_fn_norm_mod, [(xl, 0, d)], [row(g_pre_mix[l]), sc1, sh1], [BF16], name="norm_mod_fwd", width=d)
        proj = _mm_nn(h1, wf["w_in"], l, name="w_in_fwd")
        br_t = jnp.transpose(ssm_b_re[l], (2, 0, 1))
        bi_t = jnp.transpose(ssm_b_im[l], (2, 0, 1))
        disc_in = (ssm_log_dt[l].reshape(groups, 1), ssm_a_re[l], ssm_a_im[l], br_t, bi_t)
        lr, li, bbr_t, bbi_t = _ssm_disc_fwd(*disc_in)
        lam = _interleave(lr.reshape(1, gn), li.reshape(1, gn), scan_c)
        bcat = _interleave(_block_diag(jnp.transpose(bbr_t, (1, 0, 2))), _block_diag(jnp.transpose(bbi_t, (1, 0, 2))),
                           scan_c).astype(BF16).reshape(1, 1, sw, 2 * gn)
        cre = _block_diag(jnp.transpose(ssm_c_re[l], (0, 2, 1)))
        cim = _block_diag(jnp.transpose(ssm_c_im[l], (0, 2, 1)))
        ccat = jnp.transpose(_interleave(cre.T, -cim.T, scan_c)).astype(BF16).reshape(1, 1, 2 * gn, sw)
        xcat = _mm_nn(proj, bcat, 0, name="ssm_b_fwd", k_dim=sw)
        hcat = _ssm_scan_fwd(xcat, lam)
        y_ssm_pre = _mm_nn(hcat, ccat, 0, name="ssm_c_fwd")
        (gact,) = _ew_fwd(_fn_gelu, [(y_ssm_pre, 0, sw), (proj, 0, sw)], [row(ssm_d[l])], [F32], name="gelu_fwd", width=sw,
                          tw=_tile(sw, 512))
        z = _mm_nn(gact, wf["w_glu"], l, name="w_glu_fwd")
        (s_ssm,) = _ew_fwd(_fn_glu, [(gact, 0, sw), (z, 0, sw)], [row(b_glu[l])], [BF16], name="glu_fwd", width=sw,
                           tw=_tile(sw, 512))
        y_ssm = _mm_nn(s_ssm, wf["w_ssm_out"], l, name="w_branch_out_fwd")
        qkv, outs, lses = [], [], []
        for p, (_, dl) in enumerate(DSWA_PATTERNS):
            q_p = _to_residues(proj[:, off_q + p * sw: off_q + (p + 1) * sw].astype(BF16), dl, hp)
            k_p = _to_residues(proj[:, off_k + p * sw: off_k + (p + 1) * sw].astype(BF16), dl, hp)
            v_p = _to_residues(proj[:, off_v + p * sw: off_v + (p + 1) * sw].astype(BF16), dl, hp)
            o_p, lse_p = _attn_fwd(q_p, k_p, v_p, slopes[p], dl)
            qkv.append((q_p, k_p, v_p))
            outs.append(_from_residues(o_p, dl, hp))
            lses.append(_from_residues(lse_p, dl, hp))
        o_attn, lse_attn = _attn_merge(outs, lses)
        y_attn = _mm_nn(o_attn, wf["w_attn_out"], l, name="w_branch_out_fwd")
        cv = _gconv_fwd(proj, off_conv, sw, conv_w_full[l])
        y_conv = _mm_nn(cv, wf["w_conv_out"], l, name="w_branch_out_fwd")
        bg = b_gate[l].reshape(3, 1, d)
        gate_xs = [(proj, off_gate + q * d, d) for q in range(3)] + [(y_ssm, 0, d), (y_attn, 0, d), (y_conv, 0, d)]
        (merged,) = _ew_fwd(_fn_gates, gate_xs, [bg[0], bg[1], bg[2]], [BF16], name="gates_fwd", width=d,
                            tw=_tile(sw, 512))
        y_mix = _mm_nn(merged, wf["w_o"], l, name="w_o_fwd")
        (x_mid,) = _ew_fwd(_fn_residual, [(xl, 0, d), (y_mix, 0, d)], [gt1, row(g_post_mix[l])], [F32], name="residual_fwd",
                           width=d)
        (h2,) = _ew_fwd(_fn_norm_mod, [(x_mid, 0, d)], [row(g_pre_ffn[l]), sc2, sh2], [BF16], name="norm_mod_fwd", width=d)
        up = _mm_nn(h2, wf["w_up"], l, name="w_up_fwd")
        act = _ffn_act_fwd(up, ffn_w_full[l])
        y_ffn = _mm_nn(act, wf["w_down"], l, name="w_down_fwd")
        (x_out,) = _ew_fwd(_fn_residual, [(x_mid, 0, d), (y_ffn, 0, d)], [gt2, row(g_post_ffn[l])], [F32],
                           name="residual_fwd", width=d)
        s.update(h1=h1, proj=proj, disc_in=disc_in, lam=lam, bcat=bcat, ccat=ccat, hcat=hcat, y_ssm_pre=y_ssm_pre,
                 gact=gact, z=z, s_ssm=s_ssm, y_ssm=y_ssm, qkv=qkv, o_attn=o_attn, lse_attn=lse_attn, y_attn=y_attn,
                 cv=cv, y_conv=y_conv, merged=merged, y_mix=y_mix, x_mid=x_mid, h2=h2, up=up, act=act, y_ffn=y_ffn)
        saved.append(s)
        xl = x_out

    dxl, loss_local = _loss_fwd_bwd(xl, target2)
    loss = lax.psum(loss_local, ("x", "y", "c"))

    gbuf = {k: lax.empty(v.shape, F32) for k, v in wf.items()}
    small = {k: [None] * nl for k in ("g_pre_mix", "g_post_mix", "g_pre_ffn", "g_post_ffn", "ssm_log_dt", "ssm_a_re",
                                      "ssm_a_im", "ssm_b_re", "ssm_b_im", "ssm_c_re", "ssm_c_im", "ssm_d", "b_glu",
                                      "conv_mix_w", "b_gate", "ffn_conv_w", "dmod")}
    for l in reversed(range(nl)):
        s = saved[l]
        sh1, sc1, gt1, sh2, sc2, gt2 = [mods[l, q] for q in range(6)]
        proj = s["proj"]
        (dy_ffn,), (dgt2, dg_post_ffn) = _ew_bwd(
            _fn_residual, [(s["x_mid"], 0, d), (s["y_ffn"], 0, d)], [gt2, row(g_post_ffn[l])], [[dxl]], [None, BF16],
            name="residual_bwd", width=d)
        dact = _mm_nt(dy_ffn, wf["w_down"], l, name="w_down_bwd_x", out_dtype=BF16)
        gbuf["w_down"] = _mm_tn(s["act"], dy_ffn, gbuf["w_down"], l, name="w_down_bwd_w")
        dup_a, dup_b, dwa, dwb = _ffn_act_bwd(s["up"], ffn_w_full[l], dact)
        dup = jnp.concatenate([dup_a, dup_b], axis=1)
        small["ffn_conv_w"][l] = jnp.concatenate([dwa, dwb], axis=1)
        dh2 = _mm_nt(dup, wf["w_up"], l, name="w_up_bwd_x")
        gbuf["w_up"] = _mm_tn(s["h2"], dup, gbuf["w_up"], l, name="w_up_bwd_w")
        (dx_mid,), (dg_pre_ffn, dsc2, dsh2) = _ew_bwd(
            _fn_norm_mod, [(s["x_mid"], 0, d)], [row(g_pre_ffn[l]), sc2, sh2], [[dh2]], [F32], name="norm_mod_bwd", width=d,
            dx_add={0: dxl})
        (dy_mix,), (dgt1, dg_post_mix) = _ew_bwd(
            _fn_residual, [(s["x_in"], 0, d), (s["y_mix"], 0, d)], [gt1, row(g_post_mix[l])], [[dx_mid]], [None, BF16],
            name="residual_bwd", width=d)
        dmerged = _mm_nt(dy_mix, wf["w_o"], l, name="w_o_bwd_x")
        gbuf["w_o"] = _mm_tn(s["merged"], dy_mix, gbuf["w_o"], l, name="w_o_bwd_w")
        bg = b_gate[l].reshape(3, 1, d)
        gate_xs = [(proj, off_gate + q * d, d) for q in range(3)] + [(s["y_ssm"], 0, d), (s["y_attn"], 0, d),
                                                                     (s["y_conv"], 0, d)]
        (dp0, dp1, dp2, dy_ssm, dy_attn, dy_conv), dbg = _ew_bwd(
            _fn_gates, gate_xs, [bg[0], bg[1], bg[2]], [[dmerged]], [BF16] * 6, name="gates_bwd", width=d,
            tw=_tile(sw, 512))
        small["b_gate"][l] = jnp.concatenate(dbg, axis=1)[0]
        ds_ssm = _mm_nt(dy_ssm, wf["w_ssm_out"], l, name="w_branch_out_bwd_x")
        gbuf["w_ssm_out"] = _mm_tn(s["s_ssm"], dy_ssm, gbuf["w_ssm_out"], l, name="w_branch_out_bwd_w")
        (dg1, dz), (db_glu,) = _ew_bwd(_fn_glu, [(s["gact"], 0, sw), (s["z"], 0, sw)], [row(b_glu[l])], [[ds_ssm]],
                                       [F32, BF16], name="glu_bwd", width=sw, tw=_tile(sw, 512))
        dg2 = _mm_nt(dz, wf["w_glu"], l, name="w_glu_bwd_x")
        gbuf["w_glu"] = _mm_tn(s["gact"], dz, gbuf["w_glu"], l, name="w_glu_bwd_w")
        (dy_pre, du_skip), (dd_skip,) = _ew_bwd(_fn_gelu, [(s["y_ssm_pre"], 0, sw), (proj, 0, sw)], [row(ssm_d[l])],
                                               [[dg1, dg2]], [BF16, F32], name="gelu_bwd", width=sw, tw=_tile(sw, 512))
        dhcat = _mm_nt(dy_pre, s["ccat"], 0, name="ssm_c_bwd_x")
        dccat = _mm_tn(s["hcat"], dy_pre, lax.empty((1, 1, 2 * gn, sw), F32), 0, name="ssm_c_bwd_w")[0, 0]
        gcat, dlam = _ssm_scan_bwd(dhcat, s["hcat"], s["lam"])
        du_b = _mm_nt(gcat, s["bcat"], 0, name="ssm_b_bwd_x")
        dbcat = _mm_tn(proj, gcat, lax.empty((1, 1, sw, 2 * gn), F32), 0, name="ssm_b_bwd_w")[0, 0]
        dlr, dli = _deinterleave(dlam, scan_c)
        dbre, dbim = _deinterleave(dbcat, scan_c)
        dbbr_t = jnp.transpose(_diag_blocks(dbre, groups), (1, 0, 2))
        dbbi_t = jnp.transpose(_diag_blocks(dbim, groups), (1, 0, 2))
        gld, gar, gai, gbr_t, gbi_t = _ssm_disc_bwd(*s["disc_in"], dlr.reshape(groups, SSM_STATE),
                                                    dli.reshape(groups, SSM_STATE), dbbr_t, dbbi_t)
        dcre_t, dcim_t = _deinterleave(dccat.T, scan_c)
        small["ssm_c_re"][l] = _diag_blocks(dcre_t, groups)
        small["ssm_c_im"][l] = -_diag_blocks(dcim_t, groups)
        small["ssm_log_dt"][l] = gld.reshape(groups)
        small["ssm_a_re"][l], small["ssm_a_im"][l] = gar, gai
        small["ssm_b_re"][l] = jnp.transpose(gbr_t, (1, 2, 0))
        small["ssm_b_im"][l] = jnp.transpose(gbi_t, (1, 2, 0))
        small["ssm_d"][l], small["b_glu"][l] = dd_skip[0], db_glu[0]
        du = (du_skip + du_b).astype(BF16)
        do_attn = _mm_nt(dy_attn, wf["w_attn_out"], l, name="w_branch_out_bwd_x")
        gbuf["w_attn_out"] = _mm_tn(s["o_attn"], dy_attn, gbuf["w_attn_out"], l, name="w_branch_out_bwd_w")
        delta = _attn_delta(do_attn, s["o_attn"], head_ones)
        dqs, dks, dvs = [], [], []
        for p, (_, dl) in enumerate(DSWA_PATTERNS):
            q_p, k_p, v_p = s["qkv"][p]
            dq_p, dk_p, dv_p = _attn_bwd(q_p, k_p, v_p, _to_residues(do_attn.astype(BF16), dl, hp),
                                         _to_residues(s["lse_attn"], dl, hp), _to_residues(delta, dl, hp), slopes[p], dl)
            dqs.append(_from_residues(dq_p, dl, hp))
            dks.append(_from_residues(dk_p, dl, hp))
            dvs.append(_from_residues(dv_p, dl, hp))
        dcv = _mm_nt(dy_conv, wf["w_conv_out"], l, name="w_branch_out_bwd_x", out_dtype=BF16)
        gbuf["w_conv_out"] = _mm_tn(s["cv"], dy_conv, gbuf["w_conv_out"], l, name="w_branch_out_bwd_w")
        dcb, dcc, dch, dconv_w = _gconv_bwd(proj, off_conv, sw, conv_w_full[l], dcv)
        small["conv_mix_w"][l] = dconv_w
        dproj = jnp.concatenate([du] + dqs + dks + dvs + [dcb, dcc, dch, dp0, dp1, dp2], axis=1)
        dh1 = _mm_nt(dproj, wf["w_in"], l, name="w_in_bwd_x")
        gbuf["w_in"] = _mm_tn(s["h1"], dproj, gbuf["w_in"], l, name="w_in_bwd_w")
        (dx_in,), (dg_pre_mix, dsc1, dsh1) = _ew_bwd(
            _fn_norm_mod, [(s["x_in"], 0, d)], [row(g_pre_mix[l]), sc1, sh1], [[dh1]], [F32], name="norm_mod_bwd", width=d,
            dx_add={0: dx_mid})
        small["g_pre_mix"][l], small["g_post_mix"][l] = dg_pre_mix[0], dg_post_mix[0]
        small["g_pre_ffn"][l], small["g_post_ffn"][l] = dg_pre_ffn[0], dg_post_ffn[0]
        small["dmod"][l] = jnp.concatenate([dsh1, dsc1, dgt1, dsh2, dsc2, dgt2], axis=1)[0]
        dxl = dx_in

    grad_x = dxl.reshape(x.shape)

    small = {k: jnp.stack(v) for k, v in small.items()}
    order = sorted(small)
    flat = jnp.concatenate([small[k].reshape(-1) for k in order])
    n_small = flat.shape[0]
    pack_w = 8 * LANE
    pack_r = -(-n_small // (pack_w * SUBLANE)) * SUBLANE
    flat = jnp.concatenate([flat, jnp.zeros((pack_r * pack_w - n_small,), F32)])
    gathered = _ag8(flat.reshape(1, 1, pack_r, pack_w), False, "ag_small_grads")[0]
    summed = _sum8(gathered, "sum_small_grads").reshape(-1)
    sgrad, at = {}, 0
    for k in order:
        size = small[k].size
        sgrad[k] = summed[at:at + size].reshape(small[k].shape)
        at += size
    dmod_off = sum(small[k].size for k in order[:order.index("dmod")])
    dmod_all = gathered.reshape(N_DEV, -1)[:, dmod_off:dmod_off + nl * 6 * d].reshape(N_DEV, nl, 6 * d)
    dmod_loc = lax.dynamic_slice_in_dim(jnp.transpose(dmod_all, (1, 0, 2)), chip * mcols, mcols, axis=2)

    grads = dict(sgrad)
    grads["b_mod"] = grads.pop("dmod")
    grads["conv_mix_w"] = lax.dynamic_slice_in_dim(sgrad["conv_mix_w"], chip * conv_mix_w.shape[2], conv_mix_w.shape[2], axis=2)
    grads["ffn_conv_w"] = lax.dynamic_slice_in_dim(sgrad["ffn_conv_w"], chip * ffn_conv_w.shape[2], ffn_conv_w.shape[2], axis=2)

    for k, buf in gbuf.items():
        _, nb, kk, n = buf.shape
        local = weights[k].shape
        g8 = buf.reshape(nl, N_DEV, local[1] // 2, local[2])
        grads[k] = _reduce_scatter(g8, pos, k).reshape(local)

    delta_w, new_m, new_v = {}, {}, {}
    c_t = jnp.pad(jnp.transpose(c_all), ((0, 0), (0, LANE - N_DEV)))
    dmod_pad = jnp.pad(dmod_loc, ((0, 0), (0, LANE - N_DEV), (0, 0)))
    grads["w_mod"], delta_w["w_mod"], new_m["w_mod"], new_v["w_mod"] = _wmod_grad_adam(
        c_t, dmod_pad, w_mod, m_w_mod, v_w_mod)
    for k in names:
        if k == "w_mod":
            continue
        delta_w[k], new_m[k], new_v[k] = _adam(weights[k], grads[k], mom_m[k], mom_v[k], "adamw")

    return (loss, grad_x, *[grads[k] for k in names], *[delta_w[k] for k in names], *[new_m[k] for k in names],
            *[new_v[k] for k in names])
```

```python
import functools
import math

import numpy as np
import jax
import jax.numpy as jnp
from jax import lax
from jax.experimental import pallas as pl
from jax.experimental.pallas import tpu as pltpu

F32 = jnp.float32
BF16 = jnp.bfloat16
MESH = pl.DeviceIdType.MESH
ANY = pl.BlockSpec(memory_space=pl.ANY)

VMEM_LIMIT_BYTES = 48 * 1024 * 1024
LANE = 128
SUBLANE = 8

RMS_EPS = 1e-6
NEG_INF = -1e30
SSM_GROUP = 16
SSM_STATE = 64
HEAD_DIM = 64
DSWA_PATTERNS = ((128, 1), (512, 4), (2048, 16))
ATTN_BLOCK = 128
N_DEV = 8
N_CHIP = 4

ADAM_LR = 0.001
ADAM_B1 = 0.9
ADAM_B2 = 0.999
ADAM_EPS = 1e-08
ADAM_WD = 0.01
ADAM_STEP = 10


def _cp(sem=None):
    return pltpu.CompilerParams(dimension_semantics=sem, vmem_limit_bytes=VMEM_LIMIT_BYTES)


def _tile(n, pref, align=LANE):
    if n <= pref:
        return n
    t = (pref // align) * align
    while t >= align:
        if n % t == 0:
            return t
        t -= align
    return n


def _sds(shape, dtype):
    return jax.ShapeDtypeStruct(tuple(shape), dtype)


MM_VMEM_BUDGET = 34 * 1024 * 1024
MM_MAX_CONTRACT = 2048


def _divisors(n, align):
    if n % align:
        return [n]
    return [t for t in range(n, 0, -align) if n % t == 0]


def _halvings(n, align, floor=256):
    out = [n]
    while out[-1] % (2 * align) == 0 and out[-1] // 2 >= floor:
        out.append(out[-1] // 2)
    return out


def _pick_tiles(rows, cols, fixed_bytes, row_bytes, col_bytes, cell_bytes):
    best = None
    for tr in rows:
        for tc in cols:
            if fixed_bytes + row_bytes * tr + col_bytes * tc + cell_bytes * tr * tc <= MM_VMEM_BUDGET:
                if best is None or tr * tc > best[0] * best[1]:
                    best = (tr, tc)
                break
    assert best is not None
    return best


def _accumulate(step, n_steps, part, o_ref, acc_ref):
    if n_steps == 1:
        o_ref[...] = part.astype(o_ref.dtype)
        return
    acc = o_ref if acc_ref is None else acc_ref

    @pl.when(step == 0)
    def _():
        acc[...] = part

    @pl.when(step > 0)
    def _():
        acc[...] += part

    if acc_ref is not None:
        @pl.when(step == n_steps - 1)
        def _():
            o_ref[...] = acc_ref[...].astype(o_ref.dtype)


def _mm_nn(a, w, layer, *, name, k_dim=None, a_col0=0, out_dtype=F32, a_fn=None):
    m = a.shape[0]
    _, nb, kw, n = w.shape
    k_dim = kw if k_dim is None else k_dim
    assert k_dim == kw
    tk = _tile(k_dim, MM_MAX_CONTRACT)
    nk = k_dim // tk
    sa, so = a.dtype.itemsize, jnp.dtype(out_dtype).itemsize
    use_acc = nk > 1 and so != 4
    row_bytes = tk * (2 * sa + (2 if sa == 4 else 0) + (4 if a_fn is not None else 0))
    tm, tn = _pick_tiles(_halvings(m, SUBLANE), _divisors(n, LANE), 0, row_bytes, 2 * tk * w.dtype.itemsize,
                         2 * so + 4 + (4 if use_acc else 0))
    assert a_col0 % tk == 0
    npb = n // tn
    a0 = a_col0 // tk

    def body(a_ref, w_ref, o_ref, *scratch):
        av = a_ref[...]
        if a_fn is not None:
            av = a_fn(av.astype(F32))
        part = jnp.dot(av.astype(BF16), w_ref[...].astype(BF16), preferred_element_type=F32)
        _accumulate(pl.program_id(2), nk, part, o_ref, scratch[0] if use_acc else None)

    return pl.pallas_call(
        body,
        out_shape=_sds((m, nb * n), out_dtype),
        grid=(m // tm, nb * npb, nk),
        in_specs=[pl.BlockSpec((tm, tk), lambda i, j, k: (i, a0 + k)),
                  pl.BlockSpec((None, None, tk, tn), lambda i, j, k: (layer, j // npb, k, j % npb))],
        out_specs=pl.BlockSpec((tm, tn), lambda i, j, k: (i, j)),
        scratch_shapes=[pltpu.VMEM((tm, tn), F32)] if use_acc else [],
        compiler_params=_cp(("parallel", "parallel", "arbitrary")),
        name=name,
    )(a, w)


def _mm_nt(g, w, layer, *, name, out_dtype=F32):
    m = g.shape[0]
    _, nb, k_dim, n = w.shape
    assert g.shape[1] == nb * n
    tko = _tile(k_dim, MM_MAX_CONTRACT)
    sg, so = g.dtype.itemsize, jnp.dtype(out_dtype).itemsize
    use_acc = so != 4
    res_row = tko * (2 * so + 4 + (4 if use_acc else 0))
    tm, tc = _pick_tiles(_halvings(m, SUBLANE), _divisors(n, LANE), 0, res_row, 2 * tko * w.dtype.itemsize,
                         2 * sg + (2 if sg == 4 else 0))
    npb = n // tc
    nr = nb * npb
    use_acc = use_acc and nr > 1

    def body(g_ref, w_ref, o_ref, *scratch):
        part = lax.dot_general(g_ref[...].astype(BF16), w_ref[...].astype(BF16),
                               (((1,), (1,)), ((), ())), preferred_element_type=F32)
        _accumulate(pl.program_id(2), nr, part, o_ref, scratch[0] if use_acc else None)

    return pl.pallas_call(
        body,
        out_shape=_sds((m, k_dim), out_dtype),
        grid=(m // tm, k_dim // tko, nr),
        in_specs=[pl.BlockSpec((tm, tc), lambda i, kk, r: (i, r)),
                  pl.BlockSpec((None, None, tko, tc), lambda i, kk, r: (layer, r // npb, kk, r % npb))],
        out_specs=pl.BlockSpec((tm, tko), lambda i, kk, r: (i, kk)),
        scratch_shapes=[pltpu.VMEM((tm, tko), F32)] if use_acc else [],
        compiler_params=_cp(("parallel", "parallel", "arbitrary")),
        name=name,
    )(g, w)


def _mm_tn(a, g, out_buf, layer, *, name, a_col0=0):
    m = a.shape[0]
    _, nb, k_dim, n = out_buf.shape
    assert g.shape == (m, nb * n)
    tm = _tile(m, MM_MAX_CONTRACT, SUBLANE)
    nr = m // tm
    sa, sg = a.dtype.itemsize, g.dtype.itemsize
    tk, tn = _pick_tiles(_halvings(k_dim, LANE), _divisors(n, LANE), 0, tm * (2 * sa + (2 if sa == 4 else 0) + 2),
                         tm * (2 * sg + (2 if sg == 4 else 0)), 2 * 4 + 4)
    assert a_col0 % tk == 0
    a0 = a_col0 // tk
    npb = n // tn

    def body(a_ref, g_ref, buf_ref, o_ref):
        del buf_ref
        part = lax.dot_general(a_ref[...].astype(BF16), g_ref[...].astype(BF16),
                               (((0,), (0,)), ((), ())), preferred_element_type=F32)
        _accumulate(pl.program_id(2), nr, part, o_ref, None)

    return pl.pallas_call(
        body,
        out_shape=_sds(out_buf.shape, F32),
        grid=(k_dim // tk, nb * npb, nr),
        in_specs=[pl.BlockSpec((tm, tk), lambda kk, j, r: (r, a0 + kk)),
                  pl.BlockSpec((tm, tn), lambda kk, j, r: (r, j)),
                  ANY],
        out_specs=pl.BlockSpec((None, None, tk, tn), lambda kk, j, r: (layer, j // npb, kk, j % npb)),
        input_output_aliases={2: 0},
        compiler_params=_cp(("parallel", "parallel", "arbitrary")),
        name=name,
    )(a, g, out_buf)


def _ew_fwd(fn, xs, ps, out_dtypes, *, name, width, tw=None, tm=256):
    rows = xs[0][0].shape[0]
    tm = _tile(rows, tm, SUBLANE)
    tw = width if tw is None else tw
    nx, n_p = len(xs), len(ps)

    def body(*refs):
        xv = [r[...].astype(F32) for r in refs[:nx]]
        pv = [r[...].astype(F32) for r in refs[nx:nx + n_p]]
        outs = fn(*xv, *pv)
        if not isinstance(outs, (tuple, list)):
            outs = (outs,)
        for o_ref, o in zip(refs[nx + n_p:], outs):
            o_ref[...] = o.astype(o_ref.dtype)

    in_specs = []
    for arr, c0, w in xs:
        assert w == width and c0 % tw == 0
        in_specs.append(pl.BlockSpec((tm, tw), functools.partial(lambda i, j, b: (i, b + j), b=c0 // tw)))
    for p in ps:
        assert p.shape == (1, width)
        in_specs.append(pl.BlockSpec((1, tw), lambda i, j: (0, j)))
    outs = pl.pallas_call(
        body,
        out_shape=[_sds((rows, width), d) for d in out_dtypes],
        grid=(rows // tm, width // tw),
        in_specs=in_specs,
        out_specs=[pl.BlockSpec((tm, tw), lambda i, j: (i, j)) for _ in out_dtypes],
        compiler_params=_cp(("parallel", "parallel")),
        name=name,
    )(*[x[0] for x in xs], *ps)
    return outs


def _ew_bwd(fn, xs, ps, cts, dx_dtypes, *, name, width, tw=None, tm=256, dx_add=None):
    rows = xs[0][0].shape[0]
    tm = _tile(rows, tm, SUBLANE)
    tw = width if tw is None else tw
    nx, n_p = len(xs), len(ps)
    dx_add = dx_add or {}
    flat_cts = [c for group in cts for c in group]
    add_keys = sorted(dx_add)
    n_in = nx + n_p + len(flat_cts) + len(add_keys)
    dx_idx = [i for i, d in enumerate(dx_dtypes) if d is not None]

    def body(*refs):
        i = pl.program_id(1)
        xv = [r[...].astype(F32) for r in refs[:nx]]
        pv = [r[...].astype(F32) for r in refs[nx:nx + n_p]]
        pos = nx + n_p
        ct_vals = []
        for group in cts:
            acc = refs[pos][...].astype(F32)
            pos += 1
            for _ in group[1:]:
                acc = acc + refs[pos][...].astype(F32)
                pos += 1
            ct_vals.append(acc)
        add_vals = {}
        for key in add_keys:
            add_vals[key] = refs[pos][...].astype(F32)
            pos += 1
        out_refs = refs[n_in:]
        outs, vjp = jax.vjp(fn, *xv, *pv)
        grads = vjp(tuple(ct_vals) if isinstance(outs, (tuple, list)) else ct_vals[0])
        o = 0
        for idx in dx_idx:
            gval = grads[idx]
            if idx in add_vals:
                gval = gval + add_vals[idx]
            out_refs[o][...] = gval.astype(out_refs[o].dtype)
            o += 1
        for q in range(n_p):
            gp = grads[nx + q]
            ref = out_refs[o + q]

            @pl.when(i == 0)
            def _(ref=ref, gp=gp):
                ref[...] = gp

            @pl.when(i > 0)
            def _(ref=ref, gp=gp):
                ref[...] += gp

    tile_spec = pl.BlockSpec((tm, tw), lambda j, i: (i, j))
    in_specs = []
    for arr, c0, w in xs:
        assert w == width and c0 % tw == 0
        in_specs.append(pl.BlockSpec((tm, tw), functools.partial(lambda j, i, b: (i, b + j), b=c0 // tw)))
    for p in ps:
        in_specs.append(pl.BlockSpec((1, tw), lambda j, i: (0, j)))
    in_specs += [tile_spec] * (len(flat_cts) + len(add_keys))
    out_shape = [_sds((rows, width), dx_dtypes[idx]) for idx in dx_idx] + [_sds((1, width), F32)] * n_p
    out_specs = [tile_spec] * len(dx_idx) + [pl.BlockSpec((1, tw), lambda j, i: (0, j))] * n_p
    outs = pl.pallas_call(
        body,
        out_shape=out_shape,
        grid=(width // tw, rows // tm),
        in_specs=in_specs,
        out_specs=out_specs,
        compiler_params=_cp(("parallel", "arbitrary")),
        name=name,
    )(*[x[0] for x in xs], *ps, *flat_cts, *[dx_add[k] for k in add_keys])
    return outs[:len(dx_idx)], outs[len(dx_idx):]


def _rms(x):
    return x * lax.rsqrt(jnp.mean(x * x, axis=-1, keepdims=True) + RMS_EPS)


def _fn_norm_mod(x, g, sc, sh):
    return (_rms(x) * g) * (1.0 + sc) + sh


def _fn_residual(x, y, gt, g):
    return x + gt * (_rms(y) * g)


def _fn_gelu(y, u, d):
    return jax.nn.gelu(y + d * u)


def _fn_glu(g, z, b):
    return g * jax.nn.sigmoid(z + b)


def _fn_gates(p0, p1, p2, ys, ya, yc, b0, b1, b2):
    return (jax.nn.sigmoid(p0 + b0) * ys + jax.nn.sigmoid(p1 + b1) * ya + jax.nn.sigmoid(p2 + b2) * yc)


def _fn_disc(log_dt, ar, ai, br_t, bi_t):
    dt = jnp.exp(log_dt)
    mag = jnp.exp(ar * dt)
    lr, li = mag * jnp.cos(ai * dt), mag * jnp.sin(ai * dt)
    den = ar * ar + ai * ai
    fr = ((lr - 1.0) * ar + li * ai) / den
    fi = (li * ar - (lr - 1.0) * ai) / den
    bbr = fr[None] * br_t - fi[None] * bi_t
    bbi = fr[None] * bi_t + fi[None] * br_t
    return lr, li, bbr, bbi


def _ssm_disc_fwd(log_dt, ar, ai, br_t, bi_t):
    g, n = ar.shape

    def body(ld_ref, ar_ref, ai_ref, br_ref, bi_ref, lr_ref, li_ref, bbr_ref, bbi_ref):
        lr, li, bbr, bbi = _fn_disc(ld_ref[...], ar_ref[...], ai_ref[...], br_ref[...], bi_ref[...])
        lr_ref[...] = lr
        li_ref[...] = li
        bbr_ref[...] = bbr
        bbi_ref[...] = bbi

    return pl.pallas_call(
        body,
        out_shape=[_sds((g, n), F32), _sds((g, n), F32), _sds(br_t.shape, F32), _sds(br_t.shape, F32)],
        compiler_params=_cp(),
        name="ssm_disc_fwd",
    )(log_dt, ar, ai, br_t, bi_t)


def _ssm_disc_bwd(log_dt, ar, ai, br_t, bi_t, dlr, dli, dbbr, dbbi):
    g, n = ar.shape

    def body(ld_ref, ar_ref, ai_ref, br_ref, bi_ref, dlr_ref, dli_ref, dbbr_ref, dbbi_ref,
             gld_ref, gar_ref, gai_ref, gbr_ref, gbi_ref):
        _, vjp = jax.vjp(_fn_disc, ld_ref[...], ar_ref[...], ai_ref[...], br_ref[...], bi_ref[...])
        gld, gar, gai, gbr, gbi = vjp((dlr_ref[...], dli_ref[...], dbbr_ref[...], dbbi_ref[...]))
        gld_ref[...] = gld
        gar_ref[...] = gar
        gai_ref[...] = gai
        gbr_ref[...] = gbr
        gbi_ref[...] = gbi

    return pl.pallas_call(
        body,
        out_shape=[_sds((g, 1), F32), _sds((g, n), F32), _sds((g, n), F32), _sds(br_t.shape, F32),
                   _sds(br_t.shape, F32)],
        compiler_params=_cp(),
        name="ssm_disc_bwd",
    )(log_dt, ar, ai, br_t, bi_t, dlr, dli, dbbr, dbbi)


def _cmul(ar, ai, br, bi):
    return ar * br - ai * bi, ar * bi + ai * br


def _scan_tables(lr, li, reverse):
    c = lr.shape[-1]
    p1 = (jnp.broadcast_to(lr, (SUBLANE, c)), jnp.broadcast_to(li, (SUBLANE, c)))
    p2 = _cmul(*p1, *p1)
    p4 = _cmul(*p2, *p2)
    p8 = _cmul(*p4, *p4)
    row = lax.broadcasted_iota(jnp.int32, (SUBLANE, c), 0)
    dist = (SUBLANE - row) if reverse else (row + 1)
    pr, pi = jnp.ones((SUBLANE, c), F32), jnp.zeros((SUBLANE, c), F32)
    for bit, pw in ((1, p1), (2, p2), (4, p4), (8, p8)):
        qr, qi = _cmul(pr, pi, *pw)
        take = (dist & bit) != 0
        pr, pi = jnp.where(take, qr, pr), jnp.where(take, qi, pi)
    return row, (p1, p2, p4), (pr, pi)


def _shift_rows(x, s, row, reverse):
    if reverse:
        return jnp.where(row < SUBLANE - s, pltpu.roll(x, SUBLANE - s, 0), 0.0)
    return jnp.where(row >= s, pltpu.roll(x, s, 0), 0.0)


def _scan_tile(xr, xi, carry, row, pows, carry_pow, reverse):
    for s, pw in zip((1, 2, 4), pows):
        sr, si = _shift_rows(xr, s, row, reverse), _shift_rows(xi, s, row, reverse)
        tr, ti = _cmul(*pw, sr, si)
        xr, xi = xr + tr, xi + ti
    tr, ti = _cmul(*carry_pow, *carry)
    hr, hi = xr + tr, xi + ti
    edge = 0 if reverse else SUBLANE - 1
    c = hr.shape[-1]
    new_carry = (jnp.broadcast_to(hr[edge:edge + 1, :], (SUBLANE, c)),
                 jnp.broadcast_to(hi[edge:edge + 1, :], (SUBLANE, c)))
    return hr, hi, new_carry


def _scan_cols(gn):
    return _tile(gn, 256)


def _ssm_scan_fwd(xcat, lam):
    rows, gn2 = xcat.shape
    c = _scan_cols(gn2 // 2)
    n_tiles = rows // SUBLANE

    def body(lam_ref, x_ref, h_ref):
        lr, li = lam_ref[:, :c], lam_ref[:, c:]
        row, pows, carry_pow = _scan_tables(lr, li, False)

        def step(k, carry):
            t0 = pl.multiple_of(k * SUBLANE, SUBLANE)
            hr, hi, carry = _scan_tile(x_ref[pl.ds(t0, SUBLANE), :c], x_ref[pl.ds(t0, SUBLANE), c:], carry,
                                       row, pows, carry_pow, False)
            h_ref[pl.ds(t0, SUBLANE), :c] = hr
            h_ref[pl.ds(t0, SUBLANE), c:] = hi
            return carry

        zero = jnp.zeros((SUBLANE, c), F32)
        lax.fori_loop(0, n_tiles, step, (zero, zero))

    return pl.pallas_call(
        body,
        out_shape=_sds((rows, gn2), F32),
        grid=(gn2 // (2 * c),),
        in_specs=[pl.BlockSpec((1, 2 * c), lambda j: (0, j)), pl.BlockSpec((rows, 2 * c), lambda j: (0, j))],
        out_specs=pl.BlockSpec((rows, 2 * c), lambda j: (0, j)),
        compiler_params=_cp(("parallel",)),
        name="ssm_scan_fwd",
    )(lam, xcat)


def _ssm_scan_bwd(dhcat, hcat, lam):
    rows, gn2 = dhcat.shape
    c = _scan_cols(gn2 // 2)
    n_tiles = rows // SUBLANE

    def body(lam_ref, dh_ref, h_ref, g_ref, dlam_ref):
        lr, li = lam_ref[:, :c], -lam_ref[:, c:]
        row, pows, carry_pow = _scan_tables(lr, li, True)

        def step(k, state):
            carry, acc_r, acc_i = state
            kk = n_tiles - 1 - k
            t0 = pl.multiple_of(kk * SUBLANE, SUBLANE)
            gr, gi, carry = _scan_tile(dh_ref[pl.ds(t0, SUBLANE), :c], dh_ref[pl.ds(t0, SUBLANE), c:], carry,
                                       row, pows, carry_pow, True)
            g_ref[pl.ds(t0, SUBLANE), :c] = gr
            g_ref[pl.ds(t0, SUBLANE), c:] = gi
            tp = pl.multiple_of(jnp.maximum(kk - 1, 0) * SUBLANE, SUBLANE)
            has_prev = (kk > 0).astype(F32)
            prev_r = pltpu.roll(h_ref[pl.ds(tp, SUBLANE), :c], 1, 0) * has_prev
            prev_i = pltpu.roll(h_ref[pl.ds(tp, SUBLANE), c:], 1, 0) * has_prev
            hpr = jnp.where(row >= 1, pltpu.roll(h_ref[pl.ds(t0, SUBLANE), :c], 1, 0), prev_r)
            hpi = jnp.where(row >= 1, pltpu.roll(h_ref[pl.ds(t0, SUBLANE), c:], 1, 0), prev_i)
            acc_r = acc_r + gr * hpr + gi * hpi
            acc_i = acc_i + gi * hpr - gr * hpi
            return carry, acc_r, acc_i

        zero = jnp.zeros((SUBLANE, c), F32)
        _, acc_r, acc_i = lax.fori_loop(0, n_tiles, step, ((zero, zero), zero, zero))
        dlam_ref[:, :c] = jnp.sum(acc_r, axis=0, keepdims=True)
        dlam_ref[:, c:] = jnp.sum(acc_i, axis=0, keepdims=True)

    blk = pl.BlockSpec((rows, 2 * c), lambda j: (0, j))
    return pl.pallas_call(
        body,
        out_shape=[_sds((rows, gn2), F32), _sds((1, gn2), F32)],
        grid=(gn2 // (2 * c),),
        in_specs=[pl.BlockSpec((1, 2 * c), lambda j: (0, j)), blk, blk],
        out_specs=[blk, pl.BlockSpec((1, 2 * c), lambda j: (0, j))],
        compiler_params=_cp(("parallel",)),
        name="ssm_scan_bwd",
    )(lam, dhcat, hcat)


def _shift_down(x, k, row):
    return x if k == 0 else jnp.where(row >= k, pltpu.roll(x, k, 0), 0.0)


def _shift_up(x, k, row):
    n = x.shape[0]
    return x if k == 0 else jnp.where(row < n - k, pltpu.roll(x, n - k, 0), 0.0)


def _taps(w_ref):
    return [w_ref[k:k + 1, :] for k in range(3)]


def _conv3(x, w, row):
    return sum(w[k] * _shift_down(x, k, row) for k in range(3))


def _conv3_bwd(x, w, dy, row):
    dx = sum(w[k] * _shift_up(dy, k, row) for k in range(3))
    dw = [jnp.sum(dy * _shift_down(x, k, row), axis=0, keepdims=True) for k in range(3)]
    return dx, dw


def _gconv_fwd(proj, off, cw, w):
    rows = proj.shape[0]
    tc = _tile(cw, LANE)
    nb = cw // tc

    def body(b_ref, c_ref, h_ref, w_ref, o_ref):
        row = lax.broadcasted_iota(jnp.int32, (rows, tc), 0)
        o_ref[...] = (b_ref[...] * _conv3(c_ref[...] * h_ref[...], _taps(w_ref), row)).astype(o_ref.dtype)

    specs = [pl.BlockSpec((rows, tc), functools.partial(lambda j, b: (0, b + j), b=(off + q * cw) // tc))
             for q in range(3)]
    return pl.pallas_call(
        body,
        out_shape=_sds((rows, cw), BF16),
        grid=(nb,),
        in_specs=specs + [pl.BlockSpec((3, tc), lambda j: (0, j))],
        out_specs=pl.BlockSpec((rows, tc), lambda j: (0, j)),
        compiler_params=_cp(("parallel",)),
        name="gconv_fwd",
    )(proj, proj, proj, w)


def _gconv_bwd(proj, off, cw, w, dy):
    rows = proj.shape[0]
    tc = _tile(cw, LANE)
    nb = cw // tc

    def body(b_ref, c_ref, h_ref, w_ref, dy_ref, db_ref, dc_ref, dh_ref, dw_ref):
        row = lax.broadcasted_iota(jnp.int32, (rows, tc), 0)
        cv, hv, dyv = c_ref[...], h_ref[...], dy_ref[...].astype(F32)
        t = cv * hv
        db_ref[...] = (dyv * _conv3(t, _taps(w_ref), row)).astype(db_ref.dtype)
        dt, dw = _conv3_bwd(t, _taps(w_ref), dyv * b_ref[...], row)
        dc_ref[...] = (dt * hv).astype(dc_ref.dtype)
        dh_ref[...] = (dt * cv).astype(dh_ref.dtype)
        for k in range(3):
            dw_ref[k:k + 1, :] = dw[k]

    specs = [pl.BlockSpec((rows, tc), functools.partial(lambda j, b: (0, b + j), b=(off + q * cw) // tc))
             for q in range(3)]
    col = pl.BlockSpec((rows, tc), lambda j: (0, j))
    wspec = pl.BlockSpec((3, tc), lambda j: (0, j))
    return pl.pallas_call(
        body,
        out_shape=[_sds((rows, cw), BF16)] * 3 + [_sds((3, cw), F32)],
        grid=(nb,),
        in_specs=specs + [wspec, col],
        out_specs=[col, col, col, wspec],
        compiler_params=_cp(("parallel",)),
        name="gconv_bwd",
    )(proj, proj, proj, w, dy)


def _ffn_act_fwd(up, w):
    rows, f2 = up.shape
    f = f2 // 2
    tc = _tile(f, LANE)
    nb = f // tc

    def body(a_ref, b_ref, wa_ref, wb_ref, o_ref):
        row = lax.broadcasted_iota(jnp.int32, (rows, tc), 0)
        a = _conv3(a_ref[...], _taps(wa_ref), row)
        b = _conv3(b_ref[...], _taps(wb_ref), row)
        o_ref[...] = (jax.nn.silu(a) * b).astype(o_ref.dtype)

    return pl.pallas_call(
        body,
        out_shape=_sds((rows, f), BF16),
        grid=(nb,),
        in_specs=[pl.BlockSpec((rows, tc), lambda j: (0, j)), pl.BlockSpec((rows, tc), lambda j: (0, nb + j)),
                  pl.BlockSpec((3, tc), lambda j: (0, j)), pl.BlockSpec((3, tc), lambda j: (0, nb + j))],
        out_specs=pl.BlockSpec((rows, tc), lambda j: (0, j)),
        compiler_params=_cp(("parallel",)),
        name="ffn_act_fwd",
    )(up, up, w, w)


def _ffn_act_bwd(up, w, dact):
    rows, f2 = up.shape
    f = f2 // 2
    tc = _tile(f, LANE)
    nb = f // tc

    def body(a_ref, b_ref, wa_ref, wb_ref, d_ref, da_ref, db_ref, dwa_ref, dwb_ref):
        row = lax.broadcasted_iota(jnp.int32, (rows, tc), 0)
        av, bv, dv = a_ref[...], b_ref[...], d_ref[...].astype(F32)
        ac = _conv3(av, _taps(wa_ref), row)
        bc = _conv3(bv, _taps(wb_ref), row)
        _, vjp = jax.vjp(lambda p, q: jax.nn.silu(p) * q, ac, bc)
        dac, dbc = vjp(dv)
        dxa, dwa = _conv3_bwd(av, _taps(wa_ref), dac, row)
        dxb, dwb = _conv3_bwd(bv, _taps(wb_ref), dbc, row)
        da_ref[...] = dxa.astype(da_ref.dtype)
        db_ref[...] = dxb.astype(db_ref.dtype)
        for k in range(3):
            dwa_ref[k:k + 1, :] = dwa[k]
            dwb_ref[k:k + 1, :] = dwb[k]

    col = pl.BlockSpec((rows, tc), lambda j: (0, j))
    wspec = pl.BlockSpec((3, tc), lambda j: (0, j))
    return pl.pallas_call(
        body,
        out_shape=[_sds((rows, f), BF16)] * 2 + [_sds((3, f), F32)] * 2,
        grid=(nb,),
        in_specs=[col, pl.BlockSpec((rows, tc), lambda j: (0, nb + j)), wspec,
                  pl.BlockSpec((3, tc), lambda j: (0, nb + j)), col],
        out_specs=[col, col, wspec, wspec],
        compiler_params=_cp(("parallel",)),
        name="ffn_act_bwd",
    )(up, up, w, w, dact)


def _attn_scores(q, kc, kp, slope, dilation, has_prev):
    scale = HEAD_DIM ** -0.5
    nt = (((1,), (1,)), ((), ()))
    s_c = lax.dot_general(q, kc, nt, preferred_element_type=F32) * scale
    s_p = lax.dot_general(q, kp, nt, preferred_element_type=F32) * scale
    qi = lax.broadcasted_iota(jnp.int32, (ATTN_BLOCK, ATTN_BLOCK), 0)
    kj = lax.broadcasted_iota(jnp.int32, (ATTN_BLOCK, ATTN_BLOCK), 1)
    dist_c = qi - kj
    dist_p = dist_c + ATTN_BLOCK
    s_c = jnp.where(dist_c >= 0, s_c - slope * (dist_c * dilation).astype(F32), NEG_INF)
    s_p = jnp.where((dist_p <= ATTN_BLOCK) & has_prev, s_p - slope * (dist_p * dilation).astype(F32), NEG_INF)
    return s_c, s_p


def _attn_fwd(q, k, v, slopes, dilation):
    bp, m, e = q.shape
    nb = m // ATTN_BLOCK

    def body(q_ref, kc_ref, kp_ref, vc_ref, vp_ref, sl_ref, o_ref, lse_ref):
        i = pl.program_id(1)
        s_c, s_p = _attn_scores(q_ref[...], kc_ref[...], kp_ref[...], sl_ref[:, :1], dilation, i > 0)
        mx = jnp.maximum(jnp.max(s_c, axis=-1, keepdims=True), jnp.max(s_p, axis=-1, keepdims=True))
        p_c, p_p = jnp.exp(s_c - mx), jnp.exp(s_p - mx)
        den = jnp.sum(p_c, axis=-1, keepdims=True) + jnp.sum(p_p, axis=-1, keepdims=True)
        o = (jnp.dot(p_c.astype(BF16), vc_ref[...], preferred_element_type=F32)
             + jnp.dot(p_p.astype(BF16), vp_ref[...], preferred_element_type=F32))
        o_ref[...] = o / den
        lse_ref[...] = jnp.broadcast_to(mx + jnp.log(den), (ATTN_BLOCK, e))

    cur = pl.BlockSpec((None, ATTN_BLOCK, e), lambda b, i: (b, i, 0))
    prev = pl.BlockSpec((None, ATTN_BLOCK, e), lambda b, i: (b, jnp.maximum(i - 1, 0), 0))
    return pl.pallas_call(
        body,
        out_shape=[_sds((bp, m, e), F32), _sds((bp, m, e), F32)],
        grid=(bp, nb),
        in_specs=[cur, cur, prev, cur, prev, pl.BlockSpec((None, 1, LANE), lambda b, i: (b, 0, 0))],
        out_specs=[cur, cur],
        compiler_params=_cp(("parallel", "parallel")),
        name=f"attn_fwd_d{dilation}",
    )(q, k, k, v, v, slopes)


def _attn_bwd(q, k, v, do, lse, delta, slopes, dilation):
    bp, m, e = q.shape
    nb = m // ATTN_BLOCK

    def body(q_ref, kc_ref, kp_ref, vc_ref, vp_ref, do_ref, lse_ref, dl_ref, sl_ref,
             dq_ref, dk_ref, dv_ref, ck_ref, cv_ref):
        step = pl.program_id(1)
        i = nb - 1 - step
        scale = HEAD_DIM ** -0.5
        nt = (((1,), (1,)), ((), ()))
        qv, kc, kp, vc, vp, dov = q_ref[...], kc_ref[...], kp_ref[...], vc_ref[...], vp_ref[...], do_ref[...]
        s_c, s_p = _attn_scores(qv, kc, kp, sl_ref[:, :1], dilation, i > 0)
        lse_col, dl_col = lse_ref[:, :1], dl_ref[:, :1]
        p_c, p_p = jnp.exp(s_c - lse_col), jnp.exp(s_p - lse_col)
        ds_c = p_c * (lax.dot_general(dov, vc, nt, preferred_element_type=F32) - dl_col)
        ds_p = p_p * (lax.dot_general(dov, vp, nt, preferred_element_type=F32) - dl_col)
        dq = (jnp.dot(ds_c.astype(BF16), kc, preferred_element_type=F32)
              + jnp.dot(ds_p.astype(BF16), kp, preferred_element_type=F32)) * scale
        dq_ref[...] = dq.astype(dq_ref.dtype)

        @pl.when(step == 0)
        def _():
            ck_ref[...] = jnp.zeros_like(ck_ref)
            cv_ref[...] = jnp.zeros_like(cv_ref)

        dk_c = jnp.dot(ds_c.T.astype(BF16), qv, preferred_element_type=F32) * scale
        dv_c = jnp.dot(p_c.T.astype(BF16), dov, preferred_element_type=F32)
        dk_ref[...] = (dk_c + ck_ref[...]).astype(dk_ref.dtype)
        dv_ref[...] = (dv_c + cv_ref[...]).astype(dv_ref.dtype)
        ck_ref[...] = jnp.dot(ds_p.T.astype(BF16), qv, preferred_element_type=F32) * scale
        cv_ref[...] = jnp.dot(p_p.T.astype(BF16), dov, preferred_element_type=F32)

    cur = pl.BlockSpec((None, ATTN_BLOCK, e), lambda b, s: (b, nb - 1 - s, 0))
    prev = pl.BlockSpec((None, ATTN_BLOCK, e), lambda b, s: (b, jnp.maximum(nb - 2 - s, 0), 0))
    return pl.pallas_call(
        body,
        out_shape=[_sds((bp, m, e), BF16)] * 3,
        grid=(bp, nb),
        in_specs=[cur, cur, prev, cur, prev, cur, cur, cur, pl.BlockSpec((None, 1, LANE), lambda b, s: (b, 0, 0))],
        out_specs=[cur, cur, cur],
        scratch_shapes=[pltpu.VMEM((ATTN_BLOCK, e), F32), pltpu.VMEM((ATTN_BLOCK, e), F32)],
        compiler_params=_cp(("parallel", "arbitrary")),
        name=f"attn_bwd_d{dilation}",
    )(q, k, k, v, v, do, lse, delta, slopes)


def _attn_merge(outs, lses):
    rows, aw = outs[0].shape
    tm = _tile(rows, 256, SUBLANE)

    def body(o0, o1, o2, l0, l1, l2, o_ref, lse_ref):
        lv = [l0[...], l1[...], l2[...]]
        mx = jnp.maximum(jnp.maximum(lv[0], lv[1]), lv[2])
        w = [jnp.exp(t - mx) for t in lv]
        den = w[0] + w[1] + w[2]
        o_ref[...] = (w[0] * o0[...] + w[1] * o1[...] + w[2] * o2[...]) / den
        lse_ref[...] = mx + jnp.log(den)

    spec = pl.BlockSpec((tm, aw), lambda i: (i, 0))
    return pl.pallas_call(
        body,
        out_shape=[_sds((rows, aw), F32)] * 2,
        grid=(rows // tm,),
        in_specs=[spec] * 6,
        out_specs=[spec, spec],
        compiler_params=_cp(("parallel",)),
        name="attn_merge",
    )(*outs, *lses)


def _attn_delta(do, o, head_ones):
    rows, aw = do.shape
    tm = _tile(rows, 256, SUBLANE)

    def body(do_ref, o_ref, e_ref, d_ref):
        d_ref[...] = jnp.dot(do_ref[...] * o_ref[...], e_ref[...], preferred_element_type=F32,
                             precision=lax.Precision.HIGHEST)

    spec = pl.BlockSpec((tm, aw), lambda i: (i, 0))
    return pl.pallas_call(
        body,
        out_shape=_sds((rows, aw), F32),
        grid=(rows // tm,),
        in_specs=[spec, spec, pl.BlockSpec((aw, aw), lambda i: (0, 0))],
        out_specs=spec,
        compiler_params=_cp(("parallel",)),
        name="attn_delta",
    )(do, o, head_ones)


def _to_residues(t, dilation, hp):
    rows = t.shape[0]
    m = rows // dilation
    return t.reshape(m, dilation, hp, HEAD_DIM).transpose(1, 2, 0, 3).reshape(dilation * hp, m, HEAD_DIM)


def _from_residues(t, dilation, hp):
    m = t.shape[1]
    return t.reshape(dilation, hp, m, HEAD_DIM).transpose(2, 0, 1, 3).reshape(m * dilation, hp * HEAD_DIM)


def _alibi_slopes(pattern, hp, dilation):
    n_heads = hp * len(DSWA_PATTERNS)
    s = np.array([2.0 ** (-8.0 * (pattern * hp + h + 1) / n_heads) for h in range(hp)], dtype=np.float32)
    s = np.tile(s, dilation)
    return jnp.asarray(np.broadcast_to(s[:, None, None], (dilation * hp, 1, LANE)).copy())


def _loss_fwd_bwd(y, target):
    rows, d = y.shape
    tm = _tile(rows, 256, SUBLANE)

    def body(y_ref, t_ref, dy_ref, l_ref):
        i = pl.program_id(0)
        err = y_ref[...] - t_ref[...]
        dy_ref[...] = err * (1.0 / d)
        part = 0.5 * jnp.sum(jnp.mean(err * err, axis=-1, keepdims=True), axis=0, keepdims=True)

        @pl.when(i == 0)
        def _():
            l_ref[...] = jnp.zeros_like(l_ref)

        l_ref[...] += jnp.broadcast_to(part, l_ref.shape)

    spec = pl.BlockSpec((tm, d), lambda i: (i, 0))
    dy, loss = pl.pallas_call(
        body,
        out_shape=[_sds((rows, d), F32), _sds((SUBLANE, LANE), F32)],
        grid=(rows // tm,),
        in_specs=[spec, spec],
        out_specs=[spec, pl.BlockSpec((SUBLANE, LANE), lambda i: (0, 0))],
        compiler_params=_cp(("arbitrary",)),
        name="loss",
    )(y, target)
    return dy, loss[0, 0]


def _adam_math(w, g, m, v):
    m = ADAM_B1 * m + (1.0 - ADAM_B1) * g
    v = ADAM_B2 * v + (1.0 - ADAM_B2) * jnp.square(g)
    m_hat = m / (1.0 - ADAM_B1 ** ADAM_STEP)
    v_hat = v / (1.0 - ADAM_B2 ** ADAM_STEP)
    delta = -ADAM_LR * (m_hat / (jnp.sqrt(v_hat) + ADAM_EPS) + ADAM_WD * w)
    return delta, m, v


def _as2d(a):
    if a.ndim == 1:
        return a.reshape(1, -1)
    return a.reshape(-1, a.shape[-1])


def _adam(w, g, m, v, name):
    shape = w.shape
    w2, g2, m2, v2 = _as2d(w), _as2d(g), _as2d(m), _as2d(v)
    r, c = w2.shape
    tr = _tile(r, 512, SUBLANE)
    tc = _tile(c, 1024)

    def body(w_ref, g_ref, m_ref, v_ref, d_ref, mo_ref, vo_ref):
        delta, mn, vn = _adam_math(w_ref[...], g_ref[...], m_ref[...], v_ref[...])
        d_ref[...] = delta
        mo_ref[...] = mn
        vo_ref[...] = vn

    spec = pl.BlockSpec((tr, tc), lambda i, j: (i, j))
    outs = pl.pallas_call(
        body,
        out_shape=[_sds((r, c), F32)] * 3,
        grid=(r // tr, c // tc),
        in_specs=[spec] * 4,
        out_specs=[spec] * 3,
        compiler_params=_cp(("parallel", "parallel")),
        name=name,
    )(w2, g2, m2, v2)
    return [o.reshape(shape) for o in outs]


def _wmod_grad_adam(c_t, dmod, w, m, v):
    nl, d, cols = w.shape
    nex = c_t.shape[1]
    tr = _tile(d, 256, SUBLANE)
    tc = _tile(cols, 1024)

    def body(c_ref, dm_ref, w_ref, m_ref, v_ref, g_ref, d_ref, mo_ref, vo_ref):
        cond = jax.nn.silu(c_ref[...]).astype(BF16)
        g = jnp.dot(cond, dm_ref[...].astype(BF16), preferred_element_type=F32)
        delta, mn, vn = _adam_math(w_ref[...], g, m_ref[...], v_ref[...])
        g_ref[...] = g
        d_ref[...] = delta
        mo_ref[...] = mn
        vo_ref[...] = vn

    spec = pl.BlockSpec((None, tr, tc), lambda l, i, j: (l, i, j))
    return pl.pallas_call(
        body,
        out_shape=[_sds((nl, d, cols), F32)] * 4,
        grid=(nl, d // tr, cols // tc),
        in_specs=[pl.BlockSpec((tr, nex), lambda l, i, j: (i, 0)),
                  pl.BlockSpec((None, nex, tc), lambda l, i, j: (l, 0, j)), spec, spec, spec],
        out_specs=[spec] * 4,
        compiler_params=_cp(("parallel", "parallel", "parallel")),
        name="wmod_grad_adam",
    )(c_t, dmod, w, m, v)


def _my_pos():
    return lax.axis_index("x"), lax.axis_index("y"), lax.axis_index("c")


def _ag8(x4, select_half, name):
    a, s, r, c = x4.shape
    assert s == (2 if select_half else 1)

    def body(x_ref, out_ref, send_sems, recv_sems, local_sem):
        x, y, cc = _my_pos()
        me, sibling = (x, y, cc), (x, y, 1 - cc)
        chips = [(1 - x, y), (x, 1 - y), (1 - x, 1 - y)]
        src_mine = x_ref.at[:, pl.ds(cc if select_half else 0, 1)]

        def blk(px, py, pc):
            return out_ref.at[:, pl.ds(4 * px + 2 * py + pc, 1)]

        def copy(k, block, to, src=None):
            return pltpu.make_async_remote_copy(
                src_ref=blk(*block) if src is None else src, dst_ref=blk(*block),
                send_sem=send_sems.at[k], recv_sem=recv_sems.at[k], device_id=to, device_id_type=MESH)

        mine = pltpu.make_async_copy(src_mine, blk(*me), local_sem)
        mine.start()
        first = [copy(0, me, sibling, src=src_mine)]
        first += [copy(1 + j, me, (*chip, cc), src=src_mine) for j, chip in enumerate(chips)]
        for cp in first:
            cp.start()
        passed = [copy(4 + j, (*chip, cc), sibling) for j, chip in enumerate(chips)]
        for j, chip in enumerate(chips):
            copy(1 + j, (*chip, cc), me).wait_recv()
            passed[j].start()
        copy(0, sibling, me).wait_recv()
        for j, chip in enumerate(chips):
            copy(4 + j, (*chip, 1 - cc), me).wait_recv()
        for cp in first + passed:
            cp.wait_send()
        mine.wait()

    return pl.pallas_call(
        body,
        out_shape=_sds((a, N_DEV, r, c), x4.dtype),
        in_specs=[ANY],
        out_specs=ANY,
        scratch_shapes=[pltpu.SemaphoreType.DMA((7,)), pltpu.SemaphoreType.DMA((7,)), pltpu.SemaphoreType.DMA],
        name=name,
    )(x4)


def _chip_of(x, y, k):
    return (1 - x if k & 2 else x), (1 - y if k & 1 else y)


HBM = pl.BlockSpec(memory_space=pltpu.HBM)
SEM = pl.BlockSpec(memory_space=pltpu.SEMAPHORE)
DATAFLOW = pltpu.SideEffectType.DATAFLOW_SIDE_EFFECTING


def _own_block(k, chip, cc):
    del k
    return 2 * chip + cc


def _distance_slot(k, chip, cc):
    del chip, cc
    return k - 1


def _ici_copies(srcs, dsts, slot, send_sems, recv_sems):
    x, y, cc = _my_pos()
    chip = 2 * x + y
    copies = []
    for n, (s_ref, d_ref) in enumerate(zip(srcs, dsts)):
        for k in (1, 2, 3):
            px, py = _chip_of(x, y, k)
            at = slot(k, chip, cc)
            copies.append(pltpu.make_async_remote_copy(
                src_ref=s_ref.at[pl.ds(at, 1)], dst_ref=d_ref.at[pl.ds(at, 1)],
                send_sem=send_sems.at[3 * n + k - 1], recv_sem=recv_sems.at[3 * n + k - 1],
                device_id=(px, py, cc), device_id_type=MESH))
    return copies


def _ici_start(srcs, lands, slot, name):
    n = len(srcs)
    arrays = list(srcs) + ([] if lands is None else list(lands))
    na = len(arrays)

    def body(*refs):
        s_refs = refs[:n]
        d_refs = s_refs if lands is None else refs[n:na]
        send_sems, recv_sems, token = refs[na], refs[na + 1], refs[-1]
        for cp in _ici_copies(s_refs, d_refs, slot, send_sems, recv_sems):
            cp.start()
        token[...] = jnp.zeros_like(token)

    outs = pl.pallas_call(
        body,
        name=name,
        out_shape=(pltpu.SemaphoreType.DMA((3 * n,)), pltpu.SemaphoreType.DMA((3 * n,)),
                   *[pltpu.HBM(a.shape, a.dtype) for a in arrays], _sds((SUBLANE, LANE), F32)),
        in_specs=[HBM] * na,
        out_specs=(SEM, SEM, *([HBM] * na), pl.BlockSpec(memory_space=pltpu.VMEM)),
        input_output_aliases={i: 2 + i for i in range(na)},
        compiler_params=pltpu.CompilerParams(has_side_effects=DATAFLOW),
    )(*[pltpu.with_memory_space_constraint(a, pltpu.HBM) for a in arrays])
    return outs[0], outs[1], list(outs[2:2 + na]), outs[-1]


def _ici_wait(send_sems, recv_sems, arrays, n, shared, slot, after, name):
    na = len(arrays)

    def body(*refs):
        s_refs = refs[:n]
        d_refs = s_refs if shared else refs[n:na]
        for cp in _ici_copies(s_refs, d_refs, slot, refs[na], refs[na + 1]):
            cp.wait_send()
            cp.wait_recv()

    outs = pl.pallas_call(
        body,
        name=name,
        out_shape=tuple(pltpu.HBM(a.shape, a.dtype) for a in arrays),
        in_specs=[HBM] * na + [SEM, SEM, ANY],
        out_specs=tuple([HBM] * na),
        input_output_aliases={i: i for i in range(na)},
        compiler_params=pltpu.CompilerParams(has_side_effects=DATAFLOW),
    )(*arrays, send_sems, recv_sems, after)
    return list(outs)


def _my_chip():
    return 2 * lax.axis_index("x") + lax.axis_index("y")


def _cast_own(w, layer, name):
    _, r, cols = w.shape
    tr = _tile(r, 512, 2 * SUBLANE)
    tc = _tile(cols, 1024)

    def body(w_ref, o_ref):
        o_ref[...] = w_ref[...].astype(o_ref.dtype)

    return pl.pallas_call(
        body,
        out_shape=_sds((N_CHIP, r, cols), BF16),
        grid=(r // tr, cols // tc),
        in_specs=[pl.BlockSpec((None, tr, tc), lambda i, j: (layer, i, j))],
        out_specs=pl.BlockSpec((None, tr, tc), lambda i, j: (_my_chip(), i, j)),
        compiler_params=_cp(("parallel", "parallel")),
        name=name,
    )(w)


def _forward_halves(bufs, name):
    n = len(bufs)

    def body(*refs):
        ins, outs = refs[:n], refs[n:2 * n]
        send_sems, recv_sems = refs[2 * n], refs[2 * n + 1]
        x, y, cc = _my_pos()
        chip = 2 * x + y
        copies = []
        for i in range(n):
            for k in (1, 2, 3):
                at = 2 * (chip ^ k) + cc
                copies.append(pltpu.make_async_remote_copy(
                    src_ref=ins[i].at[pl.ds(at, 1)], dst_ref=outs[i].at[pl.ds(at, 1)],
                    send_sem=send_sems.at[3 * i + k - 1], recv_sem=recv_sems.at[3 * i + k - 1],
                    device_id=(x, y, 1 - cc), device_id_type=MESH))
        for cp in copies:
            cp.start()
        for cp in copies:
            cp.wait()

    return pl.pallas_call(
        body,
        out_shape=[_sds(b.shape, b.dtype) for b in bufs],
        in_specs=[ANY] * n,
        out_specs=[ANY] * n,
        scratch_shapes=[pltpu.SemaphoreType.DMA((3 * n,)), pltpu.SemaphoreType.DMA((3 * n,))],
        input_output_aliases={i: i for i in range(n)},
        name=name,
    )(*bufs)


def _rs_sibling(g8s, name):
    n = len(g8s)
    g4s = [g.reshape(N_CHIP, 2, g.shape[1], g.shape[2]) for g in g8s]

    def body(*refs):
        ins, outs = refs[:n], refs[n:2 * n]
        send_sems, recv_sems = refs[2 * n], refs[2 * n + 1]
        x, y, cc = _my_pos()
        copies = [pltpu.make_async_remote_copy(
            src_ref=ins[i].at[:, pl.ds(1 - cc, 1)], dst_ref=outs[i], send_sem=send_sems.at[i],
            recv_sem=recv_sems.at[i], device_id=(x, y, 1 - cc), device_id_type=MESH) for i in range(n)]
        for cp in copies:
            cp.start()
        for cp in copies:
            cp.wait()

    return pl.pallas_call(
        body,
        out_shape=[_sds((N_CHIP, 1, g.shape[2], g.shape[3]), g.dtype) for g in g4s],
        in_specs=[ANY] * n,
        out_specs=[ANY] * n,
        scratch_shapes=[pltpu.SemaphoreType.DMA((n,)), pltpu.SemaphoreType.DMA((n,))],
        name=name,
    )(*g4s)


def _share_halves(halves, name):
    def body(in_ref, out_ref, send_sem, recv_sem):
        x, y, cc = _my_pos()
        cp = pltpu.make_async_remote_copy(
            src_ref=in_ref.at[:, pl.ds(cc, 1)], dst_ref=out_ref.at[:, pl.ds(cc, 1)], send_sem=send_sem,
            recv_sem=recv_sem, device_id=(x, y, 1 - cc), device_id_type=MESH)
        cp.start()
        cp.wait()

    return pl.pallas_call(
        body,
        out_shape=_sds(halves.shape, halves.dtype),
        in_specs=[ANY],
        out_specs=ANY,
        scratch_shapes=[pltpu.SemaphoreType.DMA, pltpu.SemaphoreType.DMA],
        input_output_aliases={0: 0},
        name=name,
    )(halves)


def _rs_add_remote(g8, recv_a, name):
    _, r, c = g8.shape
    ra = recv_a.reshape(N_CHIP, r, c)
    tr = _tile(r, 512, SUBLANE)
    tc = _tile(c, 1024)

    def body(g_ref, r_ref, o_ref):
        o_ref[...] = (g_ref[...] + r_ref[...]).astype(o_ref.dtype)

    return pl.pallas_call(
        body,
        out_shape=_sds((3, r, c), BF16),
        grid=(3, r // tr, c // tc),
        in_specs=[pl.BlockSpec((None, tr, tc),
                               lambda k, i, j: (2 * (_my_chip() ^ (k + 1)) + lax.axis_index("c"), i, j)),
                  pl.BlockSpec((None, tr, tc), lambda k, i, j: (_my_chip() ^ (k + 1), i, j))],
        out_specs=pl.BlockSpec((None, tr, tc), lambda k, i, j: (k, i, j)),
        compiler_params=_cp(("parallel",) * 3),
        name=name,
    )(g8, ra)


def _rs_add_final(g8, recv_a, recv_b, out_buf, layer, name):
    _, r, c = g8.shape
    ra = recv_a.reshape(N_CHIP, r, c)
    tr = _tile(r, 512, SUBLANE)
    tc = _tile(c, 1024)

    def body(g_ref, r_ref, b0_ref, b1_ref, b2_ref, buf_ref, o_ref):
        del buf_ref
        o_ref[...] = (((g_ref[...] + r_ref[...]) + b0_ref[...].astype(F32)) + b1_ref[...].astype(F32)
                      ) + b2_ref[...].astype(F32)

    def bspec(k):
        return pl.BlockSpec((None, tr, tc), functools.partial(lambda i, j, k: (k, i, j), k=k))

    return pl.pallas_call(
        body,
        out_shape=_sds(out_buf.shape, F32),
        grid=(r // tr, c // tc),
        in_specs=[pl.BlockSpec((None, tr, tc), lambda i, j: (2 * _my_chip() + lax.axis_index("c"), i, j)),
                  pl.BlockSpec((None, tr, tc), lambda i, j: (_my_chip(), i, j)),
                  bspec(0), bspec(1), bspec(2), ANY],
        out_specs=pl.BlockSpec((None, None, tr, tc), lambda i, j: (layer, lax.axis_index("c"), i, j)),
        input_output_aliases={5: 0},
        compiler_params=_cp(("parallel",) * 2),
        name=name,
    )(g8, ra, recv_b, recv_b, recv_b, out_buf)


def _sum8(x8, name):
    _, r, c = x8.shape
    tr = _tile(r, 256, SUBLANE)

    def body(x_ref, o_ref):
        acc = x_ref[0]
        for b in range(1, N_DEV):
            acc = acc + x_ref[b]
        o_ref[...] = acc

    return pl.pallas_call(
        body,
        out_shape=_sds((r, c), F32),
        grid=(r // tr,),
        in_specs=[pl.BlockSpec((N_DEV, tr, c), lambda i: (0, i, 0))],
        out_specs=pl.BlockSpec((tr, c), lambda i: (i, 0)),
        compiler_params=_cp(("parallel",)),
        name=name,
    )(x8)


def _block_diag(t):
    g, p, q = t.shape
    eye = jnp.eye(g, dtype=t.dtype)
    return (t[:, :, None, :] * eye[:, None, :, None]).reshape(g * p, g * q)


def _diag_blocks(mat, g):
    p, q = mat.shape[0] // g, mat.shape[1] // g
    eye = jnp.eye(g, dtype=mat.dtype)
    return jnp.sum(mat.reshape(g, p, g, q) * eye[:, None, :, None], axis=2)


def _interleave(re, im, c):
    lead = re.shape[:-1]
    gn = re.shape[-1]
    return jnp.stack([re.reshape(*lead, gn // c, c), im.reshape(*lead, gn // c, c)], axis=-2).reshape(*lead, 2 * gn)


def _deinterleave(cat, c):
    lead = cat.shape[:-1]
    gn = cat.shape[-1] // 2
    t = cat.reshape(*lead, gn // c, 2, c)
    return t[..., 0, :].reshape(*lead, gn), t[..., 1, :].reshape(*lead, gn)


def kernel(x, c, w_mod, b_mod, g_pre_mix, g_post_mix, g_pre_ffn, g_post_ffn, w_in, ssm_log_dt, ssm_a_re, ssm_a_im, ssm_b_re, ssm_b_im, ssm_c_re, ssm_c_im, ssm_d, w_glu, b_glu, conv_mix_w, w_ssm_out, w_attn_out, w_conv_out, b_gate, w_o, w_up, ffn_conv_w, w_down, loss_target, m_w_mod, m_b_mod, m_g_pre_mix, m_g_post_mix, m_g_pre_ffn, m_g_post_ffn, m_w_in, m_ssm_log_dt, m_ssm_a_re, m_ssm_a_im, m_ssm_b_re, m_ssm_b_im, m_ssm_c_re, m_ssm_c_im, m_ssm_d, m_w_glu, m_b_glu, m_conv_mix_w, m_w_ssm_out, m_w_attn_out, m_w_conv_out, m_b_gate, m_w_o, m_w_up, m_ffn_conv_w, m_w_down, v_w_mod, v_b_mod, v_g_pre_mix, v_g_post_mix, v_g_pre_ffn, v_g_post_ffn, v_w_in, v_ssm_log_dt, v_ssm_a_re, v_ssm_a_im, v_ssm_b_re, v_ssm_b_im, v_ssm_c_re, v_ssm_c_im, v_ssm_d, v_w_glu, v_b_glu, v_conv_mix_w, v_w_ssm_out, v_w_attn_out, v_w_conv_out, v_b_gate, v_w_o, v_w_up, v_ffn_conv_w, v_w_down):
    weights = dict(w_mod=w_mod, b_mod=b_mod, g_pre_mix=g_pre_mix, g_post_mix=g_post_mix, g_pre_ffn=g_pre_ffn, g_post_ffn=g_post_ffn, w_in=w_in, ssm_log_dt=ssm_log_dt, ssm_a_re=ssm_a_re, ssm_a_im=ssm_a_im, ssm_b_re=ssm_b_re, ssm_b_im=ssm_b_im, ssm_c_re=ssm_c_re, ssm_c_im=ssm_c_im, ssm_d=ssm_d, w_glu=w_glu, b_glu=b_glu, conv_mix_w=conv_mix_w, w_ssm_out=w_ssm_out, w_attn_out=w_attn_out, w_conv_out=w_conv_out, b_gate=b_gate, w_o=w_o, w_up=w_up, ffn_conv_w=ffn_conv_w, w_down=w_down)
    mom_m = dict(w_mod=m_w_mod, b_mod=m_b_mod, g_pre_mix=m_g_pre_mix, g_post_mix=m_g_post_mix, g_pre_ffn=m_g_pre_ffn, g_post_ffn=m_g_post_ffn, w_in=m_w_in, ssm_log_dt=m_ssm_log_dt, ssm_a_re=m_ssm_a_re, ssm_a_im=m_ssm_a_im, ssm_b_re=m_ssm_b_re, ssm_b_im=m_ssm_b_im, ssm_c_re=m_ssm_c_re, ssm_c_im=m_ssm_c_im, ssm_d=m_ssm_d, w_glu=m_w_glu, b_glu=m_b_glu, conv_mix_w=m_conv_mix_w, w_ssm_out=m_w_ssm_out, w_attn_out=m_w_attn_out, w_conv_out=m_w_conv_out, b_gate=m_b_gate, w_o=m_w_o, w_up=m_w_up, ffn_conv_w=m_ffn_conv_w, w_down=m_w_down)
    mom_v = dict(w_mod=v_w_mod, b_mod=v_b_mod, g_pre_mix=v_g_pre_mix, g_post_mix=v_g_post_mix, g_pre_ffn=v_g_pre_ffn, g_post_ffn=v_g_post_ffn, w_in=v_w_in, ssm_log_dt=v_ssm_log_dt, ssm_a_re=v_ssm_a_re, ssm_a_im=v_ssm_a_im, ssm_b_re=v_ssm_b_re, ssm_b_im=v_ssm_b_im, ssm_c_re=v_ssm_c_re, ssm_c_im=v_ssm_c_im, ssm_d=v_ssm_d, w_glu=v_w_glu, b_glu=v_b_glu, conv_mix_w=v_conv_mix_w, w_ssm_out=v_w_ssm_out, w_attn_out=v_w_attn_out, w_conv_out=v_w_conv_out, b_gate=v_b_gate, w_o=v_w_o, w_up=v_w_up, ffn_conv_w=v_ffn_conv_w, w_down=v_w_down)
    names = list(weights)

    nl = w_in.shape[0]
    seq, d = x.shape[1], x.shape[2]
    sw = d // 4
    groups = sw // SSM_GROUP
    gn = groups * SSM_STATE
    hp = sw // HEAD_DIM
    qw = 3 * sw
    off_q, off_k, off_v = sw, sw + qw, sw + 2 * qw
    off_conv = sw + 3 * qw
    off_gate = off_conv + 3 * sw
    n_in = off_gate + 3 * d
    f = w_down.shape[1] * N_CHIP
    scan_c = _scan_cols(gn)
    assert seq % (ATTN_BLOCK * DSWA_PATTERNS[-1][1]) == 0 and all(w // dl == ATTN_BLOCK for w, dl in DSWA_PATTERNS)

    px, py, pc = _my_pos()
    chip = 2 * px + py
    dev = 2 * chip + pc

    x2 = x.reshape(seq, d)
    target2 = loss_target.reshape(seq, d)

    mix_keys = ("w_in", "w_glu", "w_ssm_out", "w_attn_out", "w_conv_out", "w_o")
    ffn_keys = ("w_up", "w_down")
    col_sharded = ("w_in", "w_ssm_out", "w_attn_out", "w_conv_out", "w_up")
    n_stages = 2 * nl

    def stage_keys(stage):
        return ffn_keys if stage % 2 else mix_keys

    def blocked(k, buf8):
        r, cols = weights[k].shape[1:]
        return buf8.reshape(1, N_CHIP, r, cols) if k in col_sharded else buf8.reshape(1, 1, N_CHIP * r, cols)

    def begin_gather(stage):
        keys = stage_keys(stage)
        bufs = []
        for k in keys:
            r, cols = weights[k].shape[1:]
            bufs.append(_cast_own(weights[k], stage // 2, "cast_own").reshape(N_DEV, r // 2, cols))
        return _ici_start(bufs, None, _own_block, f"gather_start_{stage}")

    def end_gather(stage, pending, after):
        send_sems, recv_sems, bufs, _ = pending
        bufs = _ici_wait(send_sems, recv_sems, bufs, len(bufs), True, _own_block, after, f"gather_wait_{stage}")
        bufs = _forward_halves(bufs, "gather_forward")
        return {k: blocked(k, b) for k, b in zip(stage_keys(stage), bufs)}

    pending = begin_gather(0)

    c_all = _ag8(c.reshape(1, 1, 1, d), False, "ag_cond").reshape(N_DEV, d)
    c_pad = jnp.concatenate([c_all, jnp.zeros((SUBLANE, d), F32)], axis=0)
    mcols = w_mod.shape[2]
    w_mod4 = w_mod.reshape(nl, 1, d, mcols)
    mod_loc = jnp.stack([_mm_nn(c_pad, w_mod4, l, name="mod_fwd", a_fn=jax.nn.silu) for l in range(nl)])
    mod_all = _ag8(mod_loc.reshape(nl, 1, 2 * SUBLANE, mcols), False, "ag_mod")
    mod_rows = lax.dynamic_slice_in_dim(mod_all[:, 0::2], dev, 1, axis=2)
    mod = mod_rows.reshape(nl, N_CHIP * mcols) + b_mod
    mods = mod.reshape(nl, 6, 1, d)

    def gather_whole(w):
        _, r, cols = w.shape
        got = _ag8(w.reshape(nl, 1, r, cols), False, "ag_small_weight")[:, 0::2]
        return got.transpose(0, 2, 1, 3).reshape(nl, r, N_CHIP * cols)

    conv_w_full = gather_whole(conv_mix_w)
    ffn_w_full = gather_whole(ffn_conv_w)
    wfs = [dict() for _ in range(nl)]
    wfs[0].update(end_gather(0, pending, mods))
    pending = begin_gather(1)

    head_ones = jnp.asarray(np.kron(np.eye(hp, dtype=np.float32), np.ones((HEAD_DIM, HEAD_DIM), np.float32)))
    slopes = [_alibi_slopes(p, hp, dl) for p, (_, dl) in enumerate(DSWA_PATTERNS)]

    def row(v):
        return v.reshape(1, -1)

    saved = []
    xl = x2
    for l in range(nl):
        sh1, sc1, gt1, sh2, sc2, gt2 = [mods[l, q] for q in range(6)]
        s = dict(x_in=xl)
        wf = wfs[l]
        (h1,) = _ew_fwd(_fn_norm_mod, [(xl, 0, d)], [row(g_pre_mix[l]), sc1, sh1 + pending[3][0, 0]], [BF16],
                        name="norm_mod_fwd", width=d)
        proj = _mm_nn(h1, wf["w_in"], 0, name="w_in_fwd")
        br_t = jnp.transpose(ssm_b_re[l], (2, 0, 1))
        bi_t = jnp.transpose(ssm_b_im[l], (2, 0, 1))
        disc_in = (ssm_log_dt[l].reshape(groups, 1), ssm_a_re[l], ssm_a_im[l], br_t, bi_t)
        lr, li, bbr_t, bbi_t = _ssm_disc_fwd(*disc_in)
        lam = _interleave(lr.reshape(1, gn), li.reshape(1, gn), scan_c)
        bcat = _interleave(_block_diag(jnp.transpose(bbr_t, (1, 0, 2))), _block_diag(jnp.transpose(bbi_t, (1, 0, 2))),
                           scan_c).astype(BF16).reshape(1, 1, sw, 2 * gn)
        cre = _block_diag(jnp.transpose(ssm_c_re[l], (0, 2, 1)))
        cim = _block_diag(jnp.transpose(ssm_c_im[l], (0, 2, 1)))
        ccat = jnp.transpose(_interleave(cre.T, -cim.T, scan_c)).astype(BF16).reshape(1, 1, 2 * gn, sw)
        xcat = _mm_nn(proj, bcat, 0, name="ssm_b_fwd", k_dim=sw)
        hcat = _ssm_scan_fwd(xcat, lam)
        y_ssm_pre = _mm_nn(hcat, ccat, 0, name="ssm_c_fwd")
        (gact,) = _ew_fwd(_fn_gelu, [(y_ssm_pre, 0, sw), (proj, 0, sw)], [row(ssm_d[l])], [F32], name="gelu_fwd", width=sw,
                          tw=_tile(sw, 512))
        z = _mm_nn(gact, wf["w_glu"], 0, name="w_glu_fwd")
        (s_ssm,) = _ew_fwd(_fn_glu, [(gact, 0, sw), (z, 0, sw)], [row(b_glu[l])], [BF16], name="glu_fwd", width=sw,
                           tw=_tile(sw, 512))
        y_ssm = _mm_nn(s_ssm, wf["w_ssm_out"], 0, name="w_branch_out_fwd")
        qkv, outs, lses = [], [], []
        for p, (_, dl) in enumerate(DSWA_PATTERNS):
            q_p = _to_residues(proj[:, off_q + p * sw: off_q + (p + 1) * sw].astype(BF16), dl, hp)
            k_p = _to_residues(proj[:, off_k + p * sw: off_k + (p + 1) * sw].astype(BF16), dl, hp)
            v_p = _to_residues(proj[:, off_v + p * sw: off_v + (p + 1) * sw].astype(BF16), dl, hp)
            o_p, lse_p = _attn_fwd(q_p, k_p, v_p, slopes[p], dl)
            qkv.append((q_p, k_p, v_p))
            outs.append(_from_residues(o_p, dl, hp))
            lses.append(_from_residues(lse_p, dl, hp))
        o_attn, lse_attn = _attn_merge(outs, lses)
        y_attn = _mm_nn(o_attn, wf["w_attn_out"], 0, name="w_branch_out_fwd")
        cv = _gconv_fwd(proj, off_conv, sw, conv_w_full[l])
        y_conv = _mm_nn(cv, wf["w_conv_out"], 0, name="w_branch_out_fwd")
        bg = b_gate[l].reshape(3, 1, d)
        gate_xs = [(proj, off_gate + q * d, d) for q in range(3)] + [(y_ssm, 0, d), (y_attn, 0, d), (y_conv, 0, d)]
        (merged,) = _ew_fwd(_fn_gates, gate_xs, [bg[0], bg[1], bg[2]], [BF16], name="gates_fwd", width=d,
                            tw=_tile(sw, 512))
        y_mix = _mm_nn(merged, wf["w_o"], 0, name="w_o_fwd")
        (x_mid,) = _ew_fwd(_fn_residual, [(xl, 0, d), (y_mix, 0, d)], [gt1, row(g_post_mix[l])], [F32], name="residual_fwd",
                           width=d)
        wf.update(end_gather(2 * l + 1, pending, x_mid))
        pending = begin_gather(2 * l + 2) if l + 1 < nl else None
        sh2_t = sh2 if pending is None else sh2 + pending[3][0, 0]
        (h2,) = _ew_fwd(_fn_norm_mod, [(x_mid, 0, d)], [row(g_pre_ffn[l]), sc2, sh2_t], [BF16], name="norm_mod_fwd",
                        width=d)
        up = _mm_nn(h2, wf["w_up"], 0, name="w_up_fwd")
        act = _ffn_act_fwd(up, ffn_w_full[l])
        y_ffn = _mm_nn(act, wf["w_down"], 0, name="w_down_fwd")
        (x_out,) = _ew_fwd(_fn_residual, [(x_mid, 0, d), (y_ffn, 0, d)], [gt2, row(g_post_ffn[l])], [F32],
                           name="residual_fwd", width=d)
        if pending is not None:
            wfs[l + 1].update(end_gather(2 * l + 2, pending, x_out))
            pending = begin_gather(2 * l + 3)
        s.update(h1=h1, proj=proj, disc_in=disc_in, lam=lam, bcat=bcat, ccat=ccat, hcat=hcat, y_ssm_pre=y_ssm_pre,
                 gact=gact, z=z, s_ssm=s_ssm, y_ssm=y_ssm, qkv=qkv, o_attn=o_attn, lse_attn=lse_attn, y_attn=y_attn,
                 cv=cv, y_conv=y_conv, merged=merged, y_mix=y_mix, x_mid=x_mid, h2=h2, up=up, act=act, y_ffn=y_ffn)
        saved.append(s)
        xl = x_out

    dxl, loss_local = _loss_fwd_bwd(xl, target2)
    loss = lax.psum(loss_local, ("x", "y", "c"))

    gfin = {k: lax.empty((nl, 2, weights[k].shape[1] // 2, weights[k].shape[2]), F32) for k in mix_keys + ffn_keys}

    def begin_rs(stage, grads_4d):
        keys = stage_keys(stage)
        g8s = [grads_4d[k].reshape(N_DEV, weights[k].shape[1] // 2, weights[k].shape[2]) for k in keys]
        recv_a = _rs_sibling(g8s, "rs_sibling")
        s_rem = [_rs_add_remote(g, ra, "rs_add_remote") for g, ra in zip(g8s, recv_a)]
        lands = [lax.empty(t.shape, BF16) for t in s_rem]
        return stage, g8s, recv_a, _ici_start(s_rem, lands, _distance_slot, f"rs_start_{stage}")

    def end_rs(pending_rs, after):
        stage, g8s, recv_a, (send_sems, recv_sems, arrays, _) = pending_rs
        n = len(g8s)
        arrays = _ici_wait(send_sems, recv_sems, arrays, n, False, _distance_slot, after, f"rs_wait_{stage}")
        for k, g8, ra, rb in zip(stage_keys(stage), g8s, recv_a, arrays[n:]):
            gfin[k] = _rs_add_final(g8, ra, rb, gfin[k], stage // 2, "rs_add_final")

    def after_rs_start(v, pending_rs):
        return v if pending_rs is None else v + pending_rs[3][3][0, 0]

    def grad_buf(k):
        return lax.empty(wf[k].shape, F32)

    pending_rs = None
    small = {k: [None] * nl for k in ("g_pre_mix", "g_post_mix", "g_pre_ffn", "g_post_ffn", "ssm_log_dt", "ssm_a_re",
                                      "ssm_a_im", "ssm_b_re", "ssm_b_im", "ssm_c_re", "ssm_c_im", "ssm_d", "b_glu",
                                      "conv_mix_w", "b_gate", "ffn_conv_w", "dmod")}
    for l in reversed(range(nl)):
        s = saved[l]
        sh1, sc1, gt1, sh2, sc2, gt2 = [mods[l, q] for q in range(6)]
        proj = s["proj"]
        wf = wfs[l]
        gw = {}
        (dy_ffn,), (dgt2, dg_post_ffn) = _ew_bwd(
            _fn_residual, [(s["x_mid"], 0, d), (s["y_ffn"], 0, d)], [after_rs_start(gt2, pending_rs), row(g_post_ffn[l])],
            [[dxl]], [None, BF16], name="residual_bwd", width=d)
        dact = _mm_nt(dy_ffn, wf["w_down"], 0, name="w_down_bwd_x", out_dtype=BF16)
        gw["w_down"] = _mm_tn(s["act"], dy_ffn, grad_buf("w_down"), 0, name="w_down_bwd_w")
        dup_a, dup_b, dwa, dwb = _ffn_act_bwd(s["up"], ffn_w_full[l], dact)
        dup = jnp.concatenate([dup_a, dup_b], axis=1)
        small["ffn_conv_w"][l] = jnp.concatenate([dwa, dwb], axis=1)
        dh2 = _mm_nt(dup, wf["w_up"], 0, name="w_up_bwd_x")
        gw["w_up"] = _mm_tn(s["h2"], dup, grad_buf("w_up"), 0, name="w_up_bwd_w")
        (dx_mid,), (dg_pre_ffn, dsc2, dsh2) = _ew_bwd(
            _fn_norm_mod, [(s["x_mid"], 0, d)], [row(g_pre_ffn[l]), sc2, sh2], [[dh2]], [F32], name="norm_mod_bwd", width=d,
            dx_add={0: dxl})
        if pending_rs is not None:
            end_rs(pending_rs, dx_mid)
        pending_rs = begin_rs(2 * l + 1, gw)
        (dy_mix,), (dgt1, dg_post_mix) = _ew_bwd(
            _fn_residual, [(s["x_in"], 0, d), (s["y_mix"], 0, d)], [after_rs_start(gt1, pending_rs), row(g_post_mix[l])],
            [[dx_mid]], [None, BF16], name="residual_bwd", width=d)
        dmerged = _mm_nt(dy_mix, wf["w_o"], 0, name="w_o_bwd_x")
        gw["w_o"] = _mm_tn(s["merged"], dy_mix, grad_buf("w_o"), 0, name="w_o_bwd_w")
        bg = b_gate[l].reshape(3, 1, d)
        gate_xs = [(proj, off_gate + q * d, d) for q in range(3)] + [(s["y_ssm"], 0, d), (s["y_attn"], 0, d),
                                                                     (s["y_conv"], 0, d)]
        (dp0, dp1, dp2, dy_ssm, dy_attn, dy_conv), dbg = _ew_bwd(
            _fn_gates, gate_xs, [bg[0], bg[1], bg[2]], [[dmerged]], [BF16] * 6, name="gates_bwd", width=d,
            tw=_tile(sw, 512))
        small["b_gate"][l] = jnp.concatenate(dbg, axis=1)[0]
        ds_ssm = _mm_nt(dy_ssm, wf["w_ssm_out"], 0, name="w_branch_out_bwd_x")
        gw["w_ssm_out"] = _mm_tn(s["s_ssm"], dy_ssm, grad_buf("w_ssm_out"), 0, name="w_branch_out_bwd_w")
        (dg1, dz), (db_glu,) = _ew_bwd(_fn_glu, [(s["gact"], 0, sw), (s["z"], 0, sw)], [row(b_glu[l])], [[ds_ssm]],
                                       [F32, BF16], name="glu_bwd", width=sw, tw=_tile(sw, 512))
        dg2 = _mm_nt(dz, wf["w_glu"], 0, name="w_glu_bwd_x")
        gw["w_glu"] = _mm_tn(s["gact"], dz, grad_buf("w_glu"), 0, name="w_glu_bwd_w")
        (dy_pre, du_skip), (dd_skip,) = _ew_bwd(_fn_gelu, [(s["y_ssm_pre"], 0, sw), (proj, 0, sw)], [row(ssm_d[l])],
                                               [[dg1, dg2]], [BF16, F32], name="gelu_bwd", width=sw, tw=_tile(sw, 512))
        dhcat = _mm_nt(dy_pre, s["ccat"], 0, name="ssm_c_bwd_x")
        dccat = _mm_tn(s["hcat"], dy_pre, lax.empty((1, 1, 2 * gn, sw), F32), 0, name="ssm_c_bwd_w")[0, 0]
        gcat, dlam = _ssm_scan_bwd(dhcat, s["hcat"], s["lam"])
        du_b = _mm_nt(gcat, s["bcat"], 0, name="ssm_b_bwd_x")
        dbcat = _mm_tn(proj, gcat, lax.empty((1, 1, sw, 2 * gn), F32), 0, name="ssm_b_bwd_w")[0, 0]
        dlr, dli = _deinterleave(dlam, scan_c)
        dbre, dbim = _deinterleave(dbcat, scan_c)
        dbbr_t = jnp.transpose(_diag_blocks(dbre, groups), (1, 0, 2))
        dbbi_t = jnp.transpose(_diag_blocks(dbim, groups), (1, 0, 2))
        gld, gar, gai, gbr_t, gbi_t = _ssm_disc_bwd(*s["disc_in"], dlr.reshape(groups, SSM_STATE),
                                                    dli.reshape(groups, SSM_STATE), dbbr_t, dbbi_t)
        dcre_t, dcim_t = _deinterleave(dccat.T, scan_c)
        small["ssm_c_re"][l] = _diag_blocks(dcre_t, groups)
        small["ssm_c_im"][l] = -_diag_blocks(dcim_t, groups)
        small["ssm_log_dt"][l] = gld.reshape(groups)
        small["ssm_a_re"][l], small["ssm_a_im"][l] = gar, gai
        small["ssm_b_re"][l] = jnp.transpose(gbr_t, (1, 2, 0))
        small["ssm_b_im"][l] = jnp.transpose(gbi_t, (1, 2, 0))
        small["ssm_d"][l], small["b_glu"][l] = dd_skip[0], db_glu[0]
        du = (du_skip + du_b).astype(BF16)
        do_attn = _mm_nt(dy_attn, wf["w_attn_out"], 0, name="w_branch_out_bwd_x")
        gw["w_attn_out"] = _mm_tn(s["o_attn"], dy_attn, grad_buf("w_attn_out"), 0, name="w_branch_out_bwd_w")
        delta = _attn_delta(do_attn, s["o_attn"], head_ones)
        dqs, dks, dvs = [], [], []
        for p, (_, dl) in enumerate(DSWA_PATTERNS):
            q_p, k_p, v_p = s["qkv"][p]
            dq_p, dk_p, dv_p = _attn_bwd(q_p, k_p, v_p, _to_residues(do_attn.astype(BF16), dl, hp),
                                         _to_residues(s["lse_attn"], dl, hp), _to_residues(delta, dl, hp), slopes[p], dl)
            dqs.append(_from_residues(dq_p, dl, hp))
            dks.append(_from_residues(dk_p, dl, hp))
            dvs.append(_from_residues(dv_p, dl, hp))
        dcv = _mm_nt(dy_conv, wf["w_conv_out"], 0, name="w_branch_out_bwd_x", out_dtype=BF16)
        gw["w_conv_out"] = _mm_tn(s["cv"], dy_conv, grad_buf("w_conv_out"), 0, name="w_branch_out_bwd_w")
        dcb, dcc, dch, dconv_w = _gconv_bwd(proj, off_conv, sw, conv_w_full[l], dcv)
        small["conv_mix_w"][l] = dconv_w
        dproj = jnp.concatenate([du] + dqs + dks + dvs + [dcb, dcc, dch, dp0, dp1, dp2], axis=1)
        dh1 = _mm_nt(dproj, wf["w_in"], 0, name="w_in_bwd_x")
        gw["w_in"] = _mm_tn(s["h1"], dproj, grad_buf("w_in"), 0, name="w_in_bwd_w")
        (dx_in,), (dg_pre_mix, dsc1, dsh1) = _ew_bwd(
            _fn_norm_mod, [(s["x_in"], 0, d)], [row(g_pre_mix[l]), sc1, sh1], [[dh1]], [F32], name="norm_mod_bwd", width=d,
            dx_add={0: dx_mid})
        end_rs(pending_rs, dx_in)
        pending_rs = begin_rs(2 * l, gw)
        small["g_pre_mix"][l], small["g_post_mix"][l] = dg_pre_mix[0], dg_post_mix[0]
        small["g_pre_ffn"][l], small["g_post_ffn"][l] = dg_pre_ffn[0], dg_post_ffn[0]
        small["dmod"][l] = jnp.concatenate([dsh1, dsc1, dgt1, dsh2, dsc2, dgt2], axis=1)[0]
        dxl = dx_in

    grad_x = dxl.reshape(x.shape)

    small = {k: jnp.stack(v) for k, v in small.items()}
    order = sorted(small)
    flat = jnp.concatenate([small[k].reshape(-1) for k in order])
    n_small = flat.shape[0]
    pack_w = 8 * LANE
    pack_r = -(-n_small // (pack_w * SUBLANE)) * SUBLANE
    flat = jnp.concatenate([flat, jnp.zeros((pack_r * pack_w - n_small,), F32)])
    gathered = _ag8(flat.reshape(1, 1, pack_r, pack_w), False, "ag_small_grads")[0]
    summed = _sum8(gathered, "sum_small_grads").reshape(-1)
    sgrad, at = {}, 0
    for k in order:
        size = small[k].size
        sgrad[k] = summed[at:at + size].reshape(small[k].shape)
        at += size
    dmod_off = sum(small[k].size for k in order[:order.index("dmod")])
    dmod_all = gathered.reshape(N_DEV, -1)[:, dmod_off:dmod_off + nl * 6 * d].reshape(N_DEV, nl, 6 * d)
    dmod_loc = lax.dynamic_slice_in_dim(jnp.transpose(dmod_all, (1, 0, 2)), chip * mcols, mcols, axis=2)

    grads = dict(sgrad)
    grads["b_mod"] = grads.pop("dmod")
    grads["conv_mix_w"] = lax.dynamic_slice_in_dim(sgrad["conv_mix_w"], chip * conv_mix_w.shape[2], conv_mix_w.shape[2], axis=2)
    grads["ffn_conv_w"] = lax.dynamic_slice_in_dim(sgrad["ffn_conv_w"], chip * ffn_conv_w.shape[2], ffn_conv_w.shape[2], axis=2)

    end_rs(pending_rs, summed)
    for k in mix_keys + ffn_keys:
        grads[k] = _share_halves(gfin[k], "rs_share").reshape(weights[k].shape)

    delta_w, new_m, new_v = {}, {}, {}
    c_t = jnp.pad(jnp.transpose(c_all), ((0, 0), (0, LANE - N_DEV)))
    dmod_pad = jnp.pad(dmod_loc, ((0, 0), (0, LANE - N_DEV), (0, 0)))
    grads["w_mod"], delta_w["w_mod"], new_m["w_mod"], new_v["w_mod"] = _wmod_grad_adam(
        c_t, dmod_pad, w_mod, m_w_mod, v_w_mod)
    for k in names:
        if k == "w_mod":
            continue
        delta_w[k], new_m[k], new_v[k] = _adam(weights[k], grads[k], mom_m[k], mom_v[k], "adamw")

    return (loss, grad_x, *[grads[k] for k in names], *[delta_w[k] for k in names], *[new_m[k] for k in names],
            *[new_v[k] for k in names])
```

```python
import functools
import math

import numpy as np
import jax
import jax.numpy as jnp
from jax import lax
from jax.experimental import pallas as pl
from jax.experimental.pallas import tpu as pltpu

F32 = jnp.float32
BF16 = jnp.bfloat16
MESH = pl.DeviceIdType.MESH
ANY = pl.BlockSpec(memory_space=pl.ANY)

VMEM_LIMIT_BYTES = 48 * 1024 * 1024
LANE = 128
SUBLANE = 8

RMS_EPS = 1e-6
NEG_INF = -1e30
SSM_GROUP = 16
SSM_STATE = 64
HEAD_DIM = 64
DSWA_PATTERNS = ((128, 1), (512, 4), (2048, 16))
ATTN_BLOCK = 128
N_DEV = 8
N_CHIP = 4

ADAM_LR = 0.001
ADAM_B1 = 0.9
ADAM_B2 = 0.999
ADAM_EPS = 1e-08
ADAM_WD = 0.01
ADAM_STEP = 10


def _cp(sem=None):
    return pltpu.CompilerParams(dimension_semantics=sem, vmem_limit_bytes=VMEM_LIMIT_BYTES)


def _tile(n, pref, align=LANE):
    if n <= pref:
        return n
    t = (pref // align) * align
    while t >= align:
        if n % t == 0:
            return t
        t -= align
    return n


def _sds(shape, dtype):
    return jax.ShapeDtypeStruct(tuple(shape), dtype)


MM_VMEM_BUDGET = 34 * 1024 * 1024
MM_MAX_CONTRACT = 2048


def _divisors(n, align):
    if n % align:
        return [n]
    return [t for t in range(n, 0, -align) if n % t == 0]


def _halvings(n, align, floor=256):
    out = [n]
    while out[-1] % (2 * align) == 0 and out[-1] // 2 >= floor:
        out.append(out[-1] // 2)
    return out


def _pick_tiles(rows, cols, fixed_bytes, row_bytes, col_bytes, cell_bytes):
    best = None
    for tr in rows:
        for tc in cols:
            if fixed_bytes + row_bytes * tr + col_bytes * tc + cell_bytes * tr * tc <= MM_VMEM_BUDGET:
                if best is None or tr * tc > best[0] * best[1]:
                    best = (tr, tc)
                break
    assert best is not None
    return best


def _accumulate(step, n_steps, part, o_ref, acc_ref):
    if n_steps == 1:
        o_ref[...] = part.astype(o_ref.dtype)
        return
    acc = o_ref if acc_ref is None else acc_ref

    @pl.when(step == 0)
    def _():
        acc[...] = part

    @pl.when(step > 0)
    def _():
        acc[...] += part

    if acc_ref is not None:
        @pl.when(step == n_steps - 1)
        def _():
            o_ref[...] = acc_ref[...].astype(o_ref.dtype)


def _mm_nn(a, w, layer, *, name, k_dim=None, a_col0=0, out_dtype=F32, a_fn=None):
    m = a.shape[0]
    _, nb, kw, n = w.shape
    k_dim = kw if k_dim is None else k_dim
    assert k_dim == kw
    tk = _tile(k_dim, MM_MAX_CONTRACT)
    nk = k_dim // tk
    sa, so = a.dtype.itemsize, jnp.dtype(out_dtype).itemsize
    use_acc = nk > 1 and so != 4
    row_bytes = tk * (2 * sa + (2 if sa == 4 else 0) + (4 if a_fn is not None else 0))
    tm, tn = _pick_tiles(_halvings(m, SUBLANE), _divisors(n, LANE), 0, row_bytes, 2 * tk * w.dtype.itemsize,
                         2 * so + 4 + (4 if use_acc else 0))
    assert a_col0 % tk == 0
    npb = n // tn
    a0 = a_col0 // tk

    def body(a_ref, w_ref, o_ref, *scratch):
        av = a_ref[...]
        if a_fn is not None:
            av = a_fn(av.astype(F32))
        part = jnp.dot(av.astype(BF16), w_ref[...].astype(BF16), preferred_element_type=F32)
        _accumulate(pl.program_id(2), nk, part, o_ref, scratch[0] if use_acc else None)

    return pl.pallas_call(
        body,
        out_shape=_sds((m, nb * n), out_dtype),
        grid=(m // tm, nb * npb, nk),
        in_specs=[pl.BlockSpec((tm, tk), lambda i, j, k: (i, a0 + k)),
                  pl.BlockSpec((None, None, tk, tn), lambda i, j, k: (layer, j // npb, k, j % npb))],
        out_specs=pl.BlockSpec((tm, tn), lambda i, j, k: (i, j)),
        scratch_shapes=[pltpu.VMEM((tm, tn), F32)] if use_acc else [],
        compiler_params=_cp(("parallel", "parallel", "arbitrary")),
        name=name,
    )(a, w)


def _mm_nt(g, w, layer, *, name, out_dtype=F32):
    m = g.shape[0]
    _, nb, k_dim, n = w.shape
    assert g.shape[1] == nb * n
    tko = _tile(k_dim, MM_MAX_CONTRACT)
    sg, so = g.dtype.itemsize, jnp.dtype(out_dtype).itemsize
    use_acc = so != 4
    res_row = tko * (2 * so + 4 + (4 if use_acc else 0))
    tm, tc = _pick_tiles(_halvings(m, SUBLANE), _divisors(n, LANE), 0, res_row, 2 * tko * w.dtype.itemsize,
                         2 * sg + (2 if sg == 4 else 0))
    npb = n // tc
    nr = nb * npb
    use_acc = use_acc and nr > 1

    def body(g_ref, w_ref, o_ref, *scratch):
        part = lax.dot_general(g_ref[...].astype(BF16), w_ref[...].astype(BF16),
                               (((1,), (1,)), ((), ())), preferred_element_type=F32)
        _accumulate(pl.program_id(2), nr, part, o_ref, scratch[0] if use_acc else None)

    return pl.pallas_call(
        body,
        out_shape=_sds((m, k_dim), out_dtype),
        grid=(m // tm, k_dim // tko, nr),
        in_specs=[pl.BlockSpec((tm, tc), lambda i, kk, r: (i, r)),
                  pl.BlockSpec((None, None, tko, tc), lambda i, kk, r: (layer, r // npb, kk, r % npb))],
        out_specs=pl.BlockSpec((tm, tko), lambda i, kk, r: (i, kk)),
        scratch_shapes=[pltpu.VMEM((tm, tko), F32)] if use_acc else [],
        compiler_params=_cp(("parallel", "parallel", "arbitrary")),
        name=name,
    )(g, w)


def _mm_tn(a, g, out_buf, layer, *, name, a_col0=0):
    m = a.shape[0]
    _, nb, k_dim, n = out_buf.shape
    assert g.shape == (m, nb * n)
    tm = _tile(m, MM_MAX_CONTRACT, SUBLANE)
    nr = m // tm
    sa, sg = a.dtype.itemsize, g.dtype.itemsize
    tk, tn = _pick_tiles(_halvings(k_dim, LANE), _divisors(n, LANE), 0, tm * (2 * sa + (2 if sa == 4 else 0) + 2),
                         tm * (2 * sg + (2 if sg == 4 else 0)), 2 * 4 + 4)
    assert a_col0 % tk == 0
    a0 = a_col0 // tk
    npb = n // tn

    def body(a_ref, g_ref, buf_ref, o_ref):
        del buf_ref
        part = lax.dot_general(a_ref[...].astype(BF16), g_ref[...].astype(BF16),
                               (((0,), (0,)), ((), ())), preferred_element_type=F32)
        _accumulate(pl.program_id(2), nr, part, o_ref, None)

    return pl.pallas_call(
        body,
        out_shape=_sds(out_buf.shape, F32),
        grid=(k_dim // tk, nb * npb, nr),
        in_specs=[pl.BlockSpec((tm, tk), lambda kk, j, r: (r, a0 + kk)),
                  pl.BlockSpec((tm, tn), lambda kk, j, r: (r, j)),
                  ANY],
        out_specs=pl.BlockSpec((None, None, tk, tn), lambda kk, j, r: (layer, j // npb, kk, j % npb)),
        input_output_aliases={2: 0},
        compiler_params=_cp(("parallel", "parallel", "arbitrary")),
        name=name,
    )(a, g, out_buf)


def _ew_fwd(fn, xs, ps, out_dtypes, *, name, width, tw=None, tm=256):
    rows = xs[0][0].shape[0]
    tm = _tile(rows, tm, SUBLANE)
    tw = width if tw is None else tw
    nx, n_p = len(xs), len(ps)

    def body(*refs):
        xv = [r[...].astype(F32) for r in refs[:nx]]
        pv = [r[...].astype(F32) for r in refs[nx:nx + n_p]]
        outs = fn(*xv, *pv)
        if not isinstance(outs, (tuple, list)):
            outs = (outs,)
        for o_ref, o in zip(refs[nx + n_p:], outs):
            o_ref[...] = o.astype(o_ref.dtype)

    in_specs = []
    for arr, c0, w in xs:
        assert w == width and c0 % tw == 0
        in_specs.append(pl.BlockSpec((tm, tw), functools.partial(lambda i, j, b: (i, b + j), b=c0 // tw)))
    for p in ps:
        assert p.shape == (1, width)
        in_specs.append(pl.BlockSpec((1, tw), lambda i, j: (0, j)))
    outs = pl.pallas_call(
        body,
        out_shape=[_sds((rows, width), d) for d in out_dtypes],
        grid=(rows // tm, width // tw),
        in_specs=in_specs,
        out_specs=[pl.BlockSpec((tm, tw), lambda i, j: (i, j)) for _ in out_dtypes],
        compiler_params=_cp(("parallel", "parallel")),
        name=name,
    )(*[x[0] for x in xs], *ps)
    return outs


def _ew_bwd(fn, xs, ps, cts, dx_dtypes, *, name, width, tw=None, tm=256, dx_add=None):
    rows = xs[0][0].shape[0]
    tm = _tile(rows, tm, SUBLANE)
    tw = width if tw is None else tw
    nx, n_p = len(xs), len(ps)
    dx_add = dx_add or {}
    flat_cts = [c for group in cts for c in group]
    add_keys = sorted(dx_add)
    n_in = nx + n_p + len(flat_cts) + len(add_keys)
    dx_idx = [i for i, d in enumerate(dx_dtypes) if d is not None]

    def body(*refs):
        i = pl.program_id(1)
        xv = [r[...].astype(F32) for r in refs[:nx]]
        pv = [r[...].astype(F32) for r in refs[nx:nx + n_p]]
        pos = nx + n_p
        ct_vals = []
        for group in cts:
            acc = refs[pos][...].astype(F32)
            pos += 1
            for _ in group[1:]:
                acc = acc + refs[pos][...].astype(F32)
                pos += 1
            ct_vals.append(acc)
        add_vals = {}
        for key in add_keys:
            add_vals[key] = refs[pos][...].astype(F32)
            pos += 1
        out_refs = refs[n_in:]
        outs, vjp = jax.vjp(fn, *xv, *pv)
        grads = vjp(tuple(ct_vals) if isinstance(outs, (tuple, list)) else ct_vals[0])
        o = 0
        for idx in dx_idx:
            gval = grads[idx]
            if idx in add_vals:
                gval = gval + add_vals[idx]
            out_refs[o][...] = gval.astype(out_refs[o].dtype)
            o += 1
        for q in range(n_p):
            gp = grads[nx + q]
            ref = out_refs[o + q]

            @pl.when(i == 0)
            def _(ref=ref, gp=gp):
                ref[...] = gp

            @pl.when(i > 0)
            def _(ref=ref, gp=gp):
                ref[...] += gp

    tile_spec = pl.BlockSpec((tm, tw), lambda j, i: (i, j))
    in_specs = []
    for arr, c0, w in xs:
        assert w == width and c0 % tw == 0
        in_specs.append(pl.BlockSpec((tm, tw), functools.partial(lambda j, i, b: (i, b + j), b=c0 // tw)))
    for p in ps:
        in_specs.append(pl.BlockSpec((1, tw), lambda j, i: (0, j)))
    in_specs += [tile_spec] * (len(flat_cts) + len(add_keys))
    out_shape = [_sds((rows, width), dx_dtypes[idx]) for idx in dx_idx] + [_sds((1, width), F32)] * n_p
    out_specs = [tile_spec] * len(dx_idx) + [pl.BlockSpec((1, tw), lambda j, i: (0, j))] * n_p
    outs = pl.pallas_call(
        body,
        out_shape=out_shape,
        grid=(width // tw, rows // tm),
        in_specs=in_specs,
        out_specs=out_specs,
        compiler_params=_cp(("parallel", "arbitrary")),
        name=name,
    )(*[x[0] for x in xs], *ps, *flat_cts, *[dx_add[k] for k in add_keys])
    return outs[:len(dx_idx)], outs[len(dx_idx):]


def _rms(x):
    return x * lax.rsqrt(jnp.mean(x * x, axis=-1, keepdims=True) + RMS_EPS)


def _fn_norm_mod(x, g, sc, sh):
    return (_rms(x) * g) * (1.0 + sc) + sh


def _fn_residual(x, y, gt, g):
    return x + gt * (_rms(y) * g)


def _fn_gelu(y, u, d):
    return jax.nn.gelu(y + d * u)


def _fn_glu(g, z, b):
    return g * jax.nn.sigmoid(z + b)


def _fn_gates(p0, p1, p2, ys, ya, yc, b0, b1, b2):
    return (jax.nn.sigmoid(p0 + b0) * ys + jax.nn.sigmoid(p1 + b1) * ya + jax.nn.sigmoid(p2 + b2) * yc)


def _fn_disc(log_dt, ar, ai, br_t, bi_t):
    dt = jnp.exp(log_dt)
    mag = jnp.exp(ar * dt)
    lr, li = mag * jnp.cos(ai * dt), mag * jnp.sin(ai * dt)
    den = ar * ar + ai * ai
    fr = ((lr - 1.0) * ar + li * ai) / den
    fi = (li * ar - (lr - 1.0) * ai) / den
    bbr = fr[None] * br_t - fi[None] * bi_t
    bbi = fr[None] * bi_t + fi[None] * br_t
    return lr, li, bbr, bbi


def _ssm_disc_fwd(log_dt, ar, ai, br_t, bi_t):
    g, n = ar.shape

    def body(ld_ref, ar_ref, ai_ref, br_ref, bi_ref, lr_ref, li_ref, bbr_ref, bbi_ref):
        lr, li, bbr, bbi = _fn_disc(ld_ref[...], ar_ref[...], ai_ref[...], br_ref[...], bi_ref[...])
        lr_ref[...] = lr
        li_ref[...] = li
        bbr_ref[...] = bbr
        bbi_ref[...] = bbi

    return pl.pallas_call(
        body,
        out_shape=[_sds((g, n), F32), _sds((g, n), F32), _sds(br_t.shape, F32), _sds(br_t.shape, F32)],
        compiler_params=_cp(),
        name="ssm_disc_fwd",
    )(log_dt, ar, ai, br_t, bi_t)


def _ssm_disc_bwd(log_dt, ar, ai, br_t, bi_t, dlr, dli, dbbr, dbbi):
    g, n = ar.shape

    def body(ld_ref, ar_ref, ai_ref, br_ref, bi_ref, dlr_ref, dli_ref, dbbr_ref, dbbi_ref,
             gld_ref, gar_ref, gai_ref, gbr_ref, gbi_ref):
        _, vjp = jax.vjp(_fn_disc, ld_ref[...], ar_ref[...], ai_ref[...], br_ref[...], bi_ref[...])
        gld, gar, gai, gbr, gbi = vjp((dlr_ref[...], dli_ref[...], dbbr_ref[...], dbbi_ref[...]))
        gld_ref[...] = gld
        gar_ref[...] = gar
        gai_ref[...] = gai
        gbr_ref[...] = gbr
        gbi_ref[...] = gbi

    return pl.pallas_call(
        body,
        out_shape=[_sds((g, 1), F32), _sds((g, n), F32), _sds((g, n), F32), _sds(br_t.shape, F32),
                   _sds(br_t.shape, F32)],
        compiler_params=_cp(),
        name="ssm_disc_bwd",
    )(log_dt, ar, ai, br_t, bi_t, dlr, dli, dbbr, dbbi)


def _cmul(ar, ai, br, bi):
    return ar * br - ai * bi, ar * bi + ai * br


def _scan_tables(lr, li, reverse):
    c = lr.shape[-1]
    p1 = (jnp.broadcast_to(lr, (SUBLANE, c)), jnp.broadcast_to(li, (SUBLANE, c)))
    p2 = _cmul(*p1, *p1)
    p4 = _cmul(*p2, *p2)
    p8 = _cmul(*p4, *p4)
    row = lax.broadcasted_iota(jnp.int32, (SUBLANE, c), 0)
    dist = (SUBLANE - row) if reverse else (row + 1)
    pr, pi = jnp.ones((SUBLANE, c), F32), jnp.zeros((SUBLANE, c), F32)
    for bit, pw in ((1, p1), (2, p2), (4, p4), (8, p8)):
        qr, qi = _cmul(pr, pi, *pw)
        take = (dist & bit) != 0
        pr, pi = jnp.where(take, qr, pr), jnp.where(take, qi, pi)
    return row, (p1, p2, p4), (pr, pi)


def _shift_rows(x, s, row, reverse):
    if reverse:
        return jnp.where(row < SUBLANE - s, pltpu.roll(x, SUBLANE - s, 0), 0.0)
    return jnp.where(row >= s, pltpu.roll(x, s, 0), 0.0)


def _scan_tile(xr, xi, carry, row, pows, carry_pow, reverse):
    for s, pw in zip((1, 2, 4), pows):
        sr, si = _shift_rows(xr, s, row, reverse), _shift_rows(xi, s, row, reverse)
        tr, ti = _cmul(*pw, sr, si)
        xr, xi = xr + tr, xi + ti
    tr, ti = _cmul(*carry_pow, *carry)
    hr, hi = xr + tr, xi + ti
    edge = 0 if reverse else SUBLANE - 1
    c = hr.shape[-1]
    new_carry = (jnp.broadcast_to(hr[edge:edge + 1, :], (SUBLANE, c)),
                 jnp.broadcast_to(hi[edge:edge + 1, :], (SUBLANE, c)))
    return hr, hi, new_carry


def _scan_cols(gn):
    return _tile(gn, 256)


def _ssm_scan_fwd(xcat, lam):
    rows, gn2 = xcat.shape
    c = _scan_cols(gn2 // 2)
    n_tiles = rows // SUBLANE

    def body(lam_ref, x_ref, h_ref):
        lr, li = lam_ref[:, :c], lam_ref[:, c:]
        row, pows, carry_pow = _scan_tables(lr, li, False)

        def step(k, carry):
            t0 = pl.multiple_of(k * SUBLANE, SUBLANE)
            hr, hi, carry = _scan_tile(x_ref[pl.ds(t0, SUBLANE), :c], x_ref[pl.ds(t0, SUBLANE), c:], carry,
                                       row, pows, carry_pow, False)
            h_ref[pl.ds(t0, SUBLANE), :c] = hr
            h_ref[pl.ds(t0, SUBLANE), c:] = hi
            return carry

        zero = jnp.zeros((SUBLANE, c), F32)
        lax.fori_loop(0, n_tiles, step, (zero, zero))

    return pl.pallas_call(
        body,
        out_shape=_sds((rows, gn2), F32),
        grid=(gn2 // (2 * c),),
        in_specs=[pl.BlockSpec((1, 2 * c), lambda j: (0, j)), pl.BlockSpec((rows, 2 * c), lambda j: (0, j))],
        out_specs=pl.BlockSpec((rows, 2 * c), lambda j: (0, j)),
        compiler_params=_cp(("parallel",)),
        name="ssm_scan_fwd",
    )(lam, xcat)


def _ssm_scan_bwd(dhcat, hcat, lam):
    rows, gn2 = dhcat.shape
    c = _scan_cols(gn2 // 2)
    n_tiles = rows // SUBLANE

    def body(lam_ref, dh_ref, h_ref, g_ref, dlam_ref):
        lr, li = lam_ref[:, :c], -lam_ref[:, c:]
        row, pows, carry_pow = _scan_tables(lr, li, True)

        def step(k, state):
            carry, acc_r, acc_i = state
            kk = n_tiles - 1 - k
            t0 = pl.multiple_of(kk * SUBLANE, SUBLANE)
            gr, gi, carry = _scan_tile(dh_ref[pl.ds(t0, SUBLANE), :c], dh_ref[pl.ds(t0, SUBLANE), c:], carry,
                                       row, pows, carry_pow, True)
            g_ref[pl.ds(t0, SUBLANE), :c] = gr
            g_ref[pl.ds(t0, SUBLANE), c:] = gi
            tp = pl.multiple_of(jnp.maximum(kk - 1, 0) * SUBLANE, SUBLANE)
            has_prev = (kk > 0).astype(F32)
            prev_r = pltpu.roll(h_ref[pl.ds(tp, SUBLANE), :c], 1, 0) * has_prev
            prev_i = pltpu.roll(h_ref[pl.ds(tp, SUBLANE), c:], 1, 0) * has_prev
            hpr = jnp.where(row >= 1, pltpu.roll(h_ref[pl.ds(t0, SUBLANE), :c], 1, 0), prev_r)
            hpi = jnp.where(row >= 1, pltpu.roll(h_ref[pl.ds(t0, SUBLANE), c:], 1, 0), prev_i)
            acc_r = acc_r + gr * hpr + gi * hpi
            acc_i = acc_i + gi * hpr - gr * hpi
            return carry, acc_r, acc_i

        zero = jnp.zeros((SUBLANE, c), F32)
        _, acc_r, acc_i = lax.fori_loop(0, n_tiles, step, ((zero, zero), zero, zero))
        dlam_ref[:, :c] = jnp.sum(acc_r, axis=0, keepdims=True)
        dlam_ref[:, c:] = jnp.sum(acc_i, axis=0, keepdims=True)

    blk = pl.BlockSpec((rows, 2 * c), lambda j: (0, j))
    return pl.pallas_call(
        body,
        out_shape=[_sds((rows, gn2), F32), _sds((1, gn2), F32)],
        grid=(gn2 // (2 * c),),
        in_specs=[pl.BlockSpec((1, 2 * c), lambda j: (0, j)), blk, blk],
        out_specs=[blk, pl.BlockSpec((1, 2 * c), lambda j: (0, j))],
        compiler_params=_cp(("parallel",)),
        name="ssm_scan_bwd",
    )(lam, dhcat, hcat)


def _shift_down(x, k, row):
    return x if k == 0 else jnp.where(row >= k, pltpu.roll(x, k, 0), 0.0)


def _shift_up(x, k, row):
    n = x.shape[0]
    return x if k == 0 else jnp.where(row < n - k, pltpu.roll(x, n - k, 0), 0.0)


def _taps(w_ref):
    return [w_ref[k:k + 1, :] for k in range(3)]


def _conv3(x, w, row):
    return sum(w[k] * _shift_down(x, k, row) for k in range(3))


def _conv3_bwd(x, w, dy, row):
    dx = sum(w[k] * _shift_up(dy, k, row) for k in range(3))
    dw = [jnp.sum(dy * _shift_down(x, k, row), axis=0, keepdims=True) for k in range(3)]
    return dx, dw


def _gconv_fwd(proj, off, cw, w):
    rows = proj.shape[0]
    tc = _tile(cw, LANE)
    nb = cw // tc

    def body(b_ref, c_ref, h_ref, w_ref, o_ref):
        row = lax.broadcasted_iota(jnp.int32, (rows, tc), 0)
        o_ref[...] = (b_ref[...] * _conv3(c_ref[...] * h_ref[...], _taps(w_ref), row)).astype(o_ref.dtype)

    specs = [pl.BlockSpec((rows, tc), functools.partial(lambda j, b: (0, b + j), b=(off + q * cw) // tc))
             for q in range(3)]
    return pl.pallas_call(
        body,
        out_shape=_sds((rows, cw), BF16),
        grid=(nb,),
        in_specs=specs + [pl.BlockSpec((3, tc), lambda j: (0, j))],
        out_specs=pl.BlockSpec((rows, tc), lambda j: (0, j)),
        compiler_params=_cp(("parallel",)),
        name="gconv_fwd",
    )(proj, proj, proj, w)


def _gconv_bwd(proj, off, cw, w, dy):
    rows = proj.shape[0]
    tc = _tile(cw, LANE)
    nb = cw // tc

    def body(b_ref, c_ref, h_ref, w_ref, dy_ref, db_ref, dc_ref, dh_ref, dw_ref):
        row = lax.broadcasted_iota(jnp.int32, (rows, tc), 0)
        cv, hv, dyv = c_ref[...], h_ref[...], dy_ref[...].astype(F32)
        t = cv * hv
        db_ref[...] = (dyv * _conv3(t, _taps(w_ref), row)).astype(db_ref.dtype)
        dt, dw = _conv3_bwd(t, _taps(w_ref), dyv * b_ref[...], row)
        dc_ref[...] = (dt * hv).astype(dc_ref.dtype)
        dh_ref[...] = (dt * cv).astype(dh_ref.dtype)
        for k in range(3):
            dw_ref[k:k + 1, :] = dw[k]

    specs = [pl.BlockSpec((rows, tc), functools.partial(lambda j, b: (0, b + j), b=(off + q * cw) // tc))
             for q in range(3)]
    col = pl.BlockSpec((rows, tc), lambda j: (0, j))
    wspec = pl.BlockSpec((3, tc), lambda j: (0, j))
    return pl.pallas_call(
        body,
        out_shape=[_sds((rows, cw), BF16)] * 3 + [_sds((3, cw), F32)],
        grid=(nb,),
        in_specs=specs + [wspec, col],
        out_specs=[col, col, col, wspec],
        compiler_params=_cp(("parallel",)),
        name="gconv_bwd",
    )(proj, proj, proj, w, dy)


def _ffn_act_fwd(up, w):
    rows, f2 = up.shape
    f = f2 // 2
    tc = _tile(f, LANE)
    nb = f // tc

    def body(a_ref, b_ref, wa_ref, wb_ref, o_ref):
        row = lax.broadcasted_iota(jnp.int32, (rows, tc), 0)
        a = _conv3(a_ref[...], _taps(wa_ref), row)
        b = _conv3(b_ref[...], _taps(wb_ref), row)
        o_ref[...] = (jax.nn.silu(a) * b).astype(o_ref.dtype)

    return pl.pallas_call(
        body,
        out_shape=_sds((rows, f), BF16),
        grid=(nb,),
        in_specs=[pl.BlockSpec((rows, tc), lambda j: (0, j)), pl.BlockSpec((rows, tc), lambda j: (0, nb + j)),
                  pl.BlockSpec((3, tc), lambda j: (0, j)), pl.BlockSpec((3, tc), lambda j: (0, nb + j))],
        out_specs=pl.BlockSpec((rows, tc), lambda j: (0, j)),
        compiler_params=_cp(("parallel",)),
        name="ffn_act_fwd",
    )(up, up, w, w)


def _ffn_act_bwd(up, w, dact):
    rows, f2 = up.shape
    f = f2 // 2
    tc = _tile(f, LANE)
    nb = f // tc

    def body(a_ref, b_ref, wa_ref, wb_ref, d_ref, da_ref, db_ref, dwa_ref, dwb_ref):
        row = lax.broadcasted_iota(jnp.int32, (rows, tc), 0)
        av, bv, dv = a_ref[...], b_ref[...], d_ref[...].astype(F32)
        ac = _conv3(av, _taps(wa_ref), row)
        bc = _conv3(bv, _taps(wb_ref), row)
        _, vjp = jax.vjp(lambda p, q: jax.nn.silu(p) * q, ac, bc)
        dac, dbc = vjp(dv)
        dxa, dwa = _conv3_bwd(av, _taps(wa_ref), dac, row)
        dxb, dwb = _conv3_bwd(bv, _taps(wb_ref), dbc, row)
        da_ref[...] = dxa.astype(da_ref.dtype)
        db_ref[...] = dxb.astype(db_ref.dtype)
        for k in range(3):
            dwa_ref[k:k + 1, :] = dwa[k]
            dwb_ref[k:k + 1, :] = dwb[k]

    col = pl.BlockSpec((rows, tc), lambda j: (0, j))
    wspec = pl.BlockSpec((3, tc), lambda j: (0, j))
    return pl.pallas_call(
        body,
        out_shape=[_sds((rows, f), BF16)] * 2 + [_sds((3, f), F32)] * 2,
        grid=(nb,),
        in_specs=[col, pl.BlockSpec((rows, tc), lambda j: (0, nb + j)), wspec,
                  pl.BlockSpec((3, tc), lambda j: (0, nb + j)), col],
        out_specs=[col, col, wspec, wspec],
        compiler_params=_cp(("parallel",)),
        name="ffn_act_bwd",
    )(up, up, w, w, dact)


def _attn_scores(q, kc, kp, slope, dilation, has_prev):
    scale = HEAD_DIM ** -0.5
    nt = (((1,), (1,)), ((), ()))
    s_c = lax.dot_general(q, kc, nt, preferred_element_type=F32) * scale
    s_p = lax.dot_general(q, kp, nt, preferred_element_type=F32) * scale
    qi = lax.broadcasted_iota(jnp.int32, (ATTN_BLOCK, ATTN_BLOCK), 0)
    kj = lax.broadcasted_iota(jnp.int32, (ATTN_BLOCK, ATTN_BLOCK), 1)
    dist_c = qi - kj
    dist_p = dist_c + ATTN_BLOCK
    s_c = jnp.where(dist_c >= 0, s_c - slope * (dist_c * dilation).astype(F32), NEG_INF)
    s_p = jnp.where((dist_p <= ATTN_BLOCK) & has_prev, s_p - slope * (dist_p * dilation).astype(F32), NEG_INF)
    return s_c, s_p


def _attn_specs(n_cols, col0, dilation, block_of_step):
    stride, base = n_cols // LANE, col0 // LANE
    assert n_cols % LANE == 0 and col0 % LANE == 0

    def cur(r, hh, s):
        return block_of_step(s), r * stride + base + hh

    def prev(r, hh, s):
        return jnp.maximum(block_of_step(s) - 1, 0), r * stride + base + hh

    return pl.BlockSpec((ATTN_BLOCK, LANE), cur), pl.BlockSpec((ATTN_BLOCK, LANE), prev)


def _head_col(x, mask):
    return jnp.max(jnp.where(mask, x, -jnp.inf), axis=-1, keepdims=True)


def _attn_fwd(proj, offs, pattern, dilation, sw, slopes):
    seq, n_in = proj.shape
    m = seq // dilation
    nb = m // ATTN_BLOCK
    pairs = sw // LANE
    view = proj.reshape(m, dilation * n_in)

    def body(q_ref, kc_ref, kp_ref, vc_ref, vp_ref, s0_ref, s1_ref, o_ref, lse_ref):
        i = pl.program_id(2)
        first = lax.broadcasted_iota(jnp.int32, (ATTN_BLOCK, LANE), 1) < HEAD_DIM
        q2 = q_ref[...]
        kc, kp = kc_ref[...].astype(BF16), kp_ref[...].astype(BF16)
        vc, vp = vc_ref[...].astype(BF16), vp_ref[...].astype(BF16)
        res = []
        for mask, sl_ref in ((first, s0_ref), (~first, s1_ref)):
            qh = jnp.where(mask, q2, 0.0).astype(BF16)
            s_c, s_p = _attn_scores(qh, kc, kp, sl_ref[:, :1], dilation, i > 0)
            mx = jnp.maximum(jnp.max(s_c, axis=-1, keepdims=True), jnp.max(s_p, axis=-1, keepdims=True))
            p_c, p_p = jnp.exp(s_c - mx), jnp.exp(s_p - mx)
            den = jnp.sum(p_c, axis=-1, keepdims=True) + jnp.sum(p_p, axis=-1, keepdims=True)
            o = (jnp.dot(p_c.astype(BF16), vc, preferred_element_type=F32)
                 + jnp.dot(p_p.astype(BF16), vp, preferred_element_type=F32))
            res.append((o / den, mx + jnp.log(den)))
        o_ref[...] = jnp.where(first, res[0][0], res[1][0])
        lse_ref[...] = jnp.where(first, res[0][1], res[1][1])

    def block(s):
        return s

    q_cur, _ = _attn_specs(n_in, offs[0] + pattern * sw, dilation, block)
    k_cur, k_prev = _attn_specs(n_in, offs[1] + pattern * sw, dilation, block)
    v_cur, v_prev = _attn_specs(n_in, offs[2] + pattern * sw, dilation, block)
    out, _ = _attn_specs(sw, 0, dilation, block)
    slope = pl.BlockSpec((None, 1, LANE), lambda r, hh, s: (hh, 0, 0))
    o, lse = pl.pallas_call(
        body,
        out_shape=[_sds((m, dilation * sw), F32)] * 2,
        grid=(dilation, pairs, nb),
        in_specs=[q_cur, k_cur, k_prev, v_cur, v_prev, slope, slope],
        out_specs=[out, out],
        compiler_params=_cp(("parallel", "parallel", "parallel")),
        name=f"attn_fwd_d{dilation}",
    )(view, view, view, view, view, *slopes)
    return o.reshape(seq, sw), lse.reshape(seq, sw)


def _attn_bwd(proj, do, lse, delta, offs, pattern, dilation, sw, slopes):
    seq, n_in = proj.shape
    m = seq // dilation
    nb = m // ATTN_BLOCK
    pairs = sw // LANE
    view = proj.reshape(m, dilation * n_in)
    do_v, lse_v, dl_v = [t.reshape(m, dilation * sw) for t in (do, lse, delta)]

    def body(q_ref, kc_ref, kp_ref, vc_ref, vp_ref, do_ref, lse_ref, dl_ref, s0_ref, s1_ref,
             dq_ref, dk_ref, dv_ref, ck_ref, cv_ref):
        step = pl.program_id(2)
        i = nb - 1 - step
        scale = HEAD_DIM ** -0.5
        nt = (((1,), (1,)), ((), ()))
        first = lax.broadcasted_iota(jnp.int32, (ATTN_BLOCK, LANE), 1) < HEAD_DIM
        q2, do2, lse2, dl2 = q_ref[...], do_ref[...], lse_ref[...], dl_ref[...]
        kc, kp = kc_ref[...].astype(BF16), kp_ref[...].astype(BF16)
        vc, vp = vc_ref[...].astype(BF16), vp_ref[...].astype(BF16)

        @pl.when(step == 0)
        def _():
            ck_ref[...] = jnp.zeros_like(ck_ref)
            cv_ref[...] = jnp.zeros_like(cv_ref)

        dq, dk_c, dv_c, dk_p, dv_p = [], 0.0, 0.0, 0.0, 0.0
        for mask, sl_ref in ((first, s0_ref), (~first, s1_ref)):
            qh = jnp.where(mask, q2, 0.0).astype(BF16)
            doh = jnp.where(mask, do2, 0.0).astype(BF16)
            lse_col, dl_col = _head_col(lse2, mask), _head_col(dl2, mask)
            s_c, s_p = _attn_scores(qh, kc, kp, sl_ref[:, :1], dilation, i > 0)
            p_c, p_p = jnp.exp(s_c - lse_col), jnp.exp(s_p - lse_col)
            ds_c = p_c * (lax.dot_general(doh, vc, nt, preferred_element_type=F32) - dl_col)
            ds_p = p_p * (lax.dot_general(doh, vp, nt, preferred_element_type=F32) - dl_col)
            dq.append(jnp.dot(ds_c.astype(BF16), kc, preferred_element_type=F32)
                      + jnp.dot(ds_p.astype(BF16), kp, preferred_element_type=F32))
            dk_c = dk_c + jnp.dot(ds_c.T.astype(BF16), qh, preferred_element_type=F32)
            dv_c = dv_c + jnp.dot(p_c.T.astype(BF16), doh, preferred_element_type=F32)
            dk_p = dk_p + jnp.dot(ds_p.T.astype(BF16), qh, preferred_element_type=F32)
            dv_p = dv_p + jnp.dot(p_p.T.astype(BF16), doh, preferred_element_type=F32)
        dq_ref[...] = (jnp.where(first, dq[0], dq[1]) * scale).astype(dq_ref.dtype)
        dk_ref[...] = (dk_c * scale + ck_ref[...]).astype(dk_ref.dtype)
        dv_ref[...] = (dv_c + cv_ref[...]).astype(dv_ref.dtype)
        ck_ref[...] = dk_p * scale
        cv_ref[...] = dv_p

    def block(s):
        return nb - 1 - s

    q_cur, _ = _attn_specs(n_in, offs[0] + pattern * sw, dilation, block)
    k_cur, k_prev = _attn_specs(n_in, offs[1] + pattern * sw, dilation, block)
    v_cur, v_prev = _attn_specs(n_in, offs[2] + pattern * sw, dilation, block)
    tok, _ = _attn_specs(sw, 0, dilation, block)
    slope = pl.BlockSpec((None, 1, LANE), lambda r, hh, s: (hh, 0, 0))
    outs = pl.pallas_call(
        body,
        out_shape=[_sds((m, dilation * sw), BF16)] * 3,
        grid=(dilation, pairs, nb),
        in_specs=[q_cur, k_cur, k_prev, v_cur, v_prev, tok, tok, tok, slope, slope],
        out_specs=[tok, tok, tok],
        scratch_shapes=[pltpu.VMEM((ATTN_BLOCK, LANE), F32), pltpu.VMEM((ATTN_BLOCK, LANE), F32)],
        compiler_params=_cp(("parallel", "parallel", "arbitrary")),
        name=f"attn_bwd_d{dilation}",
    )(view, view, view, view, view, do_v, lse_v, dl_v, *slopes)
    return [t.reshape(seq, sw) for t in outs]


def _attn_merge(outs, lses):
    rows, aw = outs[0].shape
    tm = _tile(rows, 256, SUBLANE)

    def body(o0, o1, o2, l0, l1, l2, o_ref, lse_ref):
        lv = [l0[...], l1[...], l2[...]]
        mx = jnp.maximum(jnp.maximum(lv[0], lv[1]), lv[2])
        w = [jnp.exp(t - mx) for t in lv]
        den = w[0] + w[1] + w[2]
        o_ref[...] = (w[0] * o0[...] + w[1] * o1[...] + w[2] * o2[...]) / den
        lse_ref[...] = mx + jnp.log(den)

    spec = pl.BlockSpec((tm, aw), lambda i: (i, 0))
    return pl.pallas_call(
        body,
        out_shape=[_sds((rows, aw), F32)] * 2,
        grid=(rows // tm,),
        in_specs=[spec] * 6,
        out_specs=[spec, spec],
        compiler_params=_cp(("parallel",)),
        name="attn_merge",
    )(*outs, *lses)


def _attn_delta(do, o, head_ones):
    rows, aw = do.shape
    tm = _tile(rows, 256, SUBLANE)

    def body(do_ref, o_ref, e_ref, d_ref):
        d_ref[...] = jnp.dot(do_ref[...] * o_ref[...], e_ref[...], preferred_element_type=F32,
                             precision=lax.Precision.HIGHEST)

    spec = pl.BlockSpec((tm, aw), lambda i: (i, 0))
    return pl.pallas_call(
        body,
        out_shape=_sds((rows, aw), F32),
        grid=(rows // tm,),
        in_specs=[spec, spec, pl.BlockSpec((aw, aw), lambda i: (0, 0))],
        out_specs=spec,
        compiler_params=_cp(("parallel",)),
        name="attn_delta",
    )(do, o, head_ones)


def _alibi_slopes(pattern, hp):
    n_heads = hp * len(DSWA_PATTERNS)
    s = np.array([2.0 ** (-8.0 * (pattern * hp + h + 1) / n_heads) for h in range(hp)], dtype=np.float32)
    return [jnp.asarray(np.broadcast_to(s[par::2, None, None], (hp // 2, 1, LANE)).copy()) for par in (0, 1)]


def _loss_fwd_bwd(y, target):
    rows, d = y.shape
    tm = _tile(rows, 256, SUBLANE)

    def body(y_ref, t_ref, dy_ref, l_ref):
        i = pl.program_id(0)
        err = y_ref[...] - t_ref[...]
        dy_ref[...] = err * (1.0 / d)
        part = 0.5 * jnp.sum(jnp.mean(err * err, axis=-1, keepdims=True), axis=0, keepdims=True)

        @pl.when(i == 0)
        def _():
            l_ref[...] = jnp.zeros_like(l_ref)

        l_ref[...] += jnp.broadcast_to(part, l_ref.shape)

    spec = pl.BlockSpec((tm, d), lambda i: (i, 0))
    dy, loss = pl.pallas_call(
        body,
        out_shape=[_sds((rows, d), F32), _sds((SUBLANE, LANE), F32)],
        grid=(rows // tm,),
        in_specs=[spec, spec],
        out_specs=[spec, pl.BlockSpec((SUBLANE, LANE), lambda i: (0, 0))],
        compiler_params=_cp(("arbitrary",)),
        name="loss",
    )(y, target)
    return dy, loss[0, 0]


def _adam_math(w, g, m, v):
    m = ADAM_B1 * m + (1.0 - ADAM_B1) * g
    v = ADAM_B2 * v + (1.0 - ADAM_B2) * jnp.square(g)
    m_hat = m / (1.0 - ADAM_B1 ** ADAM_STEP)
    v_hat = v / (1.0 - ADAM_B2 ** ADAM_STEP)
    delta = -ADAM_LR * (m_hat / (jnp.sqrt(v_hat) + ADAM_EPS) + ADAM_WD * w)
    return delta, m, v


def _as2d(a):
    if a.ndim == 1:
        return a.reshape(1, -1)
    return a.reshape(-1, a.shape[-1])


def _adam(w, g, m, v, name):
    shape = w.shape
    w2, g2, m2, v2 = _as2d(w), _as2d(g), _as2d(m), _as2d(v)
    r, c = w2.shape
    tr = _tile(r, 512, SUBLANE)
    tc = _tile(c, 1024)

    def body(w_ref, g_ref, m_ref, v_ref, d_ref, mo_ref, vo_ref):
        delta, mn, vn = _adam_math(w_ref[...], g_ref[...], m_ref[...], v_ref[...])
        d_ref[...] = delta
        mo_ref[...] = mn
        vo_ref[...] = vn

    spec = pl.BlockSpec((tr, tc), lambda i, j: (i, j))
    outs = pl.pallas_call(
        body,
        out_shape=[_sds((r, c), F32)] * 3,
        grid=(r // tr, c // tc),
        in_specs=[spec] * 4,
        out_specs=[spec] * 3,
        compiler_params=_cp(("parallel", "parallel")),
        name=name,
    )(w2, g2, m2, v2)
    return [o.reshape(shape) for o in outs]


def _wmod_grad_adam(c_t, dmod, w, m, v):
    nl, d, cols = w.shape
    nex = c_t.shape[1]
    tr = _tile(d, 256, SUBLANE)
    tc = _tile(cols, 1024)

    def body(c_ref, dm_ref, w_ref, m_ref, v_ref, g_ref, d_ref, mo_ref, vo_ref):
        cond = jax.nn.silu(c_ref[...]).astype(BF16)
        g = jnp.dot(cond, dm_ref[...].astype(BF16), preferred_element_type=F32)
        delta, mn, vn = _adam_math(w_ref[...], g, m_ref[...], v_ref[...])
        g_ref[...] = g
        d_ref[...] = delta
        mo_ref[...] = mn
        vo_ref[...] = vn

    spec = pl.BlockSpec((None, tr, tc), lambda l, i, j: (l, i, j))
    return pl.pallas_call(
        body,
        out_shape=[_sds((nl, d, cols), F32)] * 4,
        grid=(nl, d // tr, cols // tc),
        in_specs=[pl.BlockSpec((tr, nex), lambda l, i, j: (i, 0)),
                  pl.BlockSpec((None, nex, tc), lambda l, i, j: (l, 0, j)), spec, spec, spec],
        out_specs=[spec] * 4,
        compiler_params=_cp(("parallel", "parallel", "parallel")),
        name="wmod_grad_adam",
    )(c_t, dmod, w, m, v)


def _my_pos():
    return lax.axis_index("x"), lax.axis_index("y"), lax.axis_index("c")


def _ag8(x4, select_half, name):
    a, s, r, c = x4.shape
    assert s == (2 if select_half else 1)

    def body(x_ref, out_ref, send_sems, recv_sems, local_sem):
        x, y, cc = _my_pos()
        me, sibling = (x, y, cc), (x, y, 1 - cc)
        chips = [(1 - x, y), (x, 1 - y), (1 - x, 1 - y)]
        src_mine = x_ref.at[:, pl.ds(cc if select_half else 0, 1)]

        def blk(px, py, pc):
            return out_ref.at[:, pl.ds(4 * px + 2 * py + pc, 1)]

        def copy(k, block, to, src=None):
            return pltpu.make_async_remote_copy(
                src_ref=blk(*block) if src is None else src, dst_ref=blk(*block),
                send_sem=send_sems.at[k], recv_sem=recv_sems.at[k], device_id=to, device_id_type=MESH)

        mine = pltpu.make_async_copy(src_mine, blk(*me), local_sem)
        mine.start()
        first = [copy(0, me, sibling, src=src_mine)]
        first += [copy(1 + j, me, (*chip, cc), src=src_mine) for j, chip in enumerate(chips)]
        for cp in first:
            cp.start()
        passed = [copy(4 + j, (*chip, cc), sibling) for j, chip in enumerate(chips)]
        for j, chip in enumerate(chips):
            copy(1 + j, (*chip, cc), me).wait_recv()
            passed[j].start()
        copy(0, sibling, me).wait_recv()
        for j, chip in enumerate(chips):
            copy(4 + j, (*chip, 1 - cc), me).wait_recv()
        for cp in first + passed:
            cp.wait_send()
        mine.wait()

    return pl.pallas_call(
        body,
        out_shape=_sds((a, N_DEV, r, c), x4.dtype),
        in_specs=[ANY],
        out_specs=ANY,
        scratch_shapes=[pltpu.SemaphoreType.DMA((7,)), pltpu.SemaphoreType.DMA((7,)), pltpu.SemaphoreType.DMA],
        name=name,
    )(x4)


def _chip_of(x, y, k):
    return (1 - x if k & 2 else x), (1 - y if k & 1 else y)


HBM = pl.BlockSpec(memory_space=pltpu.HBM)
SEM = pl.BlockSpec(memory_space=pltpu.SEMAPHORE)
DATAFLOW = pltpu.SideEffectType.DATAFLOW_SIDE_EFFECTING


def _own_block(k, chip, cc):
    del k
    return 2 * chip + cc


def _distance_slot(k, chip, cc):
    del chip, cc
    return k - 1


def _ici_copies(srcs, dsts, slot, send_sems, recv_sems):
    x, y, cc = _my_pos()
    chip = 2 * x + y
    copies = []
    for n, (s_ref, d_ref) in enumerate(zip(srcs, dsts)):
        for k in (1, 2, 3):
            px, py = _chip_of(x, y, k)
            at = slot(k, chip, cc)
            copies.append(pltpu.make_async_remote_copy(
                src_ref=s_ref.at[pl.ds(at, 1)], dst_ref=d_ref.at[pl.ds(at, 1)],
                send_sem=send_sems.at[3 * n + k - 1], recv_sem=recv_sems.at[3 * n + k - 1],
                device_id=(px, py, cc), device_id_type=MESH))
    return copies


def _ici_start(srcs, lands, slot, name):
    n = len(srcs)
    arrays = list(srcs) + ([] if lands is None else list(lands))
    na = len(arrays)

    def body(*refs):
        s_refs = refs[:n]
        d_refs = s_refs if lands is None else refs[n:na]
        send_sems, recv_sems, token = refs[na], refs[na + 1], refs[-1]
        for cp in _ici_copies(s_refs, d_refs, slot, send_sems, recv_sems):
            cp.start()
        token[...] = jnp.zeros_like(token)

    outs = pl.pallas_call(
        body,
        name=name,
        out_shape=(pltpu.SemaphoreType.DMA((3 * n,)), pltpu.SemaphoreType.DMA((3 * n,)),
                   *[pltpu.HBM(a.shape, a.dtype) for a in arrays], _sds((SUBLANE, LANE), F32)),
        in_specs=[HBM] * na,
        out_specs=(SEM, SEM, *([HBM] * na), pl.BlockSpec(memory_space=pltpu.VMEM)),
        input_output_aliases={i: 2 + i for i in range(na)},
        compiler_params=pltpu.CompilerParams(has_side_effects=DATAFLOW),
    )(*[pltpu.with_memory_space_constraint(a, pltpu.HBM) for a in arrays])
    return outs[0], outs[1], list(outs[2:2 + na]), outs[-1]


def _ici_wait(send_sems, recv_sems, arrays, n, shared, slot, after, name):
    na = len(arrays)

    def body(*refs):
        s_refs = refs[:n]
        d_refs = s_refs if shared else refs[n:na]
        for cp in _ici_copies(s_refs, d_refs, slot, refs[na], refs[na + 1]):
            cp.wait_send()
            cp.wait_recv()

    outs = pl.pallas_call(
        body,
        name=name,
        out_shape=tuple(pltpu.HBM(a.shape, a.dtype) for a in arrays),
        in_specs=[HBM] * na + [SEM, SEM, ANY],
        out_specs=tuple([HBM] * na),
        input_output_aliases={i: i for i in range(na)},
        compiler_params=pltpu.CompilerParams(has_side_effects=DATAFLOW),
    )(*arrays, send_sems, recv_sems, after)
    return list(outs)


def _my_chip():
    return 2 * lax.axis_index("x") + lax.axis_index("y")


def _cast_own(w, layer, name):
    _, r, cols = w.shape
    tr = _tile(r, 512, 2 * SUBLANE)
    tc = _tile(cols, 1024)

    def body(w_ref, o_ref):
        o_ref[...] = w_ref[...].astype(o_ref.dtype)

    return pl.pallas_call(
        body,
        out_shape=_sds((N_CHIP, r, cols), BF16),
        grid=(r // tr, cols // tc),
        in_specs=[pl.BlockSpec((None, tr, tc), lambda i, j: (layer, i, j))],
        out_specs=pl.BlockSpec((None, tr, tc), lambda i, j: (_my_chip(), i, j)),
        compiler_params=_cp(("parallel", "parallel")),
        name=name,
    )(w)


def _forward_halves(bufs, name):
    n = len(bufs)

    def body(*refs):
        ins, outs = refs[:n], refs[n:2 * n]
        send_sems, recv_sems = refs[2 * n], refs[2 * n + 1]
        x, y, cc = _my_pos()
        chip = 2 * x + y
        copies = []
        for i in range(n):
            for k in (1, 2, 3):
                at = 2 * (chip ^ k) + cc
                copies.append(pltpu.make_async_remote_copy(
                    src_ref=ins[i].at[pl.ds(at, 1)], dst_ref=outs[i].at[pl.ds(at, 1)],
                    send_sem=send_sems.at[3 * i + k - 1], recv_sem=recv_sems.at[3 * i + k - 1],
                    device_id=(x, y, 1 - cc), device_id_type=MESH))
        for cp in copies:
            cp.start()
        for cp in copies:
            cp.wait()

    return pl.pallas_call(
        body,
        out_shape=[_sds(b.shape, b.dtype) for b in bufs],
        in_specs=[ANY] * n,
        out_specs=[ANY] * n,
        scratch_shapes=[pltpu.SemaphoreType.DMA((3 * n,)), pltpu.SemaphoreType.DMA((3 * n,))],
        input_output_aliases={i: i for i in range(n)},
        name=name,
    )(*bufs)


def _rs_sibling(g8s, name):
    n = len(g8s)
    g4s = [g.reshape(N_CHIP, 2, g.shape[1], g.shape[2]) for g in g8s]

    def body(*refs):
        ins, outs = refs[:n], refs[n:2 * n]
        send_sems, recv_sems = refs[2 * n], refs[2 * n + 1]
        x, y, cc = _my_pos()
        copies = [pltpu.make_async_remote_copy(
            src_ref=ins[i].at[:, pl.ds(1 - cc, 1)], dst_ref=outs[i], send_sem=send_sems.at[i],
            recv_sem=recv_sems.at[i], device_id=(x, y, 1 - cc), device_id_type=MESH) for i in range(n)]
        for cp in copies:
            cp.start()
        for cp in copies:
            cp.wait()

    return pl.pallas_call(
        body,
        out_shape=[_sds((N_CHIP, 1, g.shape[2], g.shape[3]), g.dtype) for g in g4s],
        in_specs=[ANY] * n,
        out_specs=[ANY] * n,
        scratch_shapes=[pltpu.SemaphoreType.DMA((n,)), pltpu.SemaphoreType.DMA((n,))],
        name=name,
    )(*g4s)


def _share_halves(halves, name):
    def body(in_ref, out_ref, send_sem, recv_sem):
        x, y, cc = _my_pos()
        cp = pltpu.make_async_remote_copy(
            src_ref=in_ref.at[:, pl.ds(cc, 1)], dst_ref=out_ref.at[:, pl.ds(cc, 1)], send_sem=send_sem,
            recv_sem=recv_sem, device_id=(x, y, 1 - cc), device_id_type=MESH)
        cp.start()
        cp.wait()

    return pl.pallas_call(
        body,
        out_shape=_sds(halves.shape, halves.dtype),
        in_specs=[ANY],
        out_specs=ANY,
        scratch_shapes=[pltpu.SemaphoreType.DMA, pltpu.SemaphoreType.DMA],
        input_output_aliases={0: 0},
        name=name,
    )(halves)


def _rs_add_remote(g8, recv_a, name):
    _, r, c = g8.shape
    ra = recv_a.reshape(N_CHIP, r, c)
    tr = _tile(r, 512, SUBLANE)
    tc = _tile(c, 1024)

    def body(g_ref, r_ref, o_ref):
        o_ref[...] = (g_ref[...] + r_ref[...]).astype(o_ref.dtype)

    return pl.pallas_call(
        body,
        out_shape=_sds((3, r, c), BF16),
        grid=(3, r // tr, c // tc),
        in_specs=[pl.BlockSpec((None, tr, tc),
                               lambda k, i, j: (2 * (_my_chip() ^ (k + 1)) + lax.axis_index("c"), i, j)),
                  pl.BlockSpec((None, tr, tc), lambda k, i, j: (_my_chip() ^ (k + 1), i, j))],
        out_specs=pl.BlockSpec((None, tr, tc), lambda k, i, j: (k, i, j)),
        compiler_params=_cp(("parallel",) * 3),
        name=name,
    )(g8, ra)


def _rs_add_final(g8, recv_a, recv_b, out_buf, layer, name):
    _, r, c = g8.shape
    ra = recv_a.reshape(N_CHIP, r, c)
    tr = _tile(r, 512, SUBLANE)
    tc = _tile(c, 1024)

    def body(g_ref, r_ref, b0_ref, b1_ref, b2_ref, buf_ref, o_ref):
        del buf_ref
        o_ref[...] = (((g_ref[...] + r_ref[...]) + b0_ref[...].astype(F32)) + b1_ref[...].astype(F32)
                      ) + b2_ref[...].astype(F32)

    def bspec(k):
        return pl.BlockSpec((None, tr, tc), functools.partial(lambda i, j, k: (k, i, j), k=k))

    return pl.pallas_call(
        body,
        out_shape=_sds(out_buf.shape, F32),
        grid=(r // tr, c // tc),
        in_specs=[pl.BlockSpec((None, tr, tc), lambda i, j: (2 * _my_chip() + lax.axis_index("c"), i, j)),
                  pl.BlockSpec((None, tr, tc), lambda i, j: (_my_chip(), i, j)),
                  bspec(0), bspec(1), bspec(2), ANY],
        out_specs=pl.BlockSpec((None, None, tr, tc), lambda i, j: (layer, lax.axis_index("c"), i, j)),
        input_output_aliases={5: 0},
        compiler_params=_cp(("parallel",) * 2),
        name=name,
    )(g8, ra, recv_b, recv_b, recv_b, out_buf)


def _sum8(x8, name):
    _, r, c = x8.shape
    tr = _tile(r, 256, SUBLANE)

    def body(x_ref, o_ref):
        acc = x_ref[0]
        for b in range(1, N_DEV):
            acc = acc + x_ref[b]
        o_ref[...] = acc

    return pl.pallas_call(
        body,
        out_shape=_sds((r, c), F32),
        grid=(r // tr,),
        in_specs=[pl.BlockSpec((N_DEV, tr, c), lambda i: (0, i, 0))],
        out_specs=pl.BlockSpec((tr, c), lambda i: (i, 0)),
        compiler_params=_cp(("parallel",)),
        name=name,
    )(x8)


def _block_diag(t):
    g, p, q = t.shape
    eye = jnp.eye(g, dtype=t.dtype)
    return (t[:, :, None, :] * eye[:, None, :, None]).reshape(g * p, g * q)


def _diag_blocks(mat, g):
    p, q = mat.shape[0] // g, mat.shape[1] // g
    eye = jnp.eye(g, dtype=mat.dtype)
    return jnp.sum(mat.reshape(g, p, g, q) * eye[:, None, :, None], axis=2)


def _interleave(re, im, c):
    lead = re.shape[:-1]
    gn = re.shape[-1]
    return jnp.stack([re.reshape(*lead, gn // c, c), im.reshape(*lead, gn // c, c)], axis=-2).reshape(*lead, 2 * gn)


def _deinterleave(cat, c):
    lead = cat.shape[:-1]
    gn = cat.shape[-1] // 2
    t = cat.reshape(*lead, gn // c, 2, c)
    return t[..., 0, :].reshape(*lead, gn), t[..., 1, :].reshape(*lead, gn)


def kernel(x, c, w_mod, b_mod, g_pre_mix, g_post_mix, g_pre_ffn, g_post_ffn, w_in, ssm_log_dt, ssm_a_re, ssm_a_im, ssm_b_re, ssm_b_im, ssm_c_re, ssm_c_im, ssm_d, w_glu, b_glu, conv_mix_w, w_ssm_out, w_attn_out, w_conv_out, b_gate, w_o, w_up, ffn_conv_w, w_down, loss_target, m_w_mod, m_b_mod, m_g_pre_mix, m_g_post_mix, m_g_pre_ffn, m_g_post_ffn, m_w_in, m_ssm_log_dt, m_ssm_a_re, m_ssm_a_im, m_ssm_b_re, m_ssm_b_im, m_ssm_c_re, m_ssm_c_im, m_ssm_d, m_w_glu, m_b_glu, m_conv_mix_w, m_w_ssm_out, m_w_attn_out, m_w_conv_out, m_b_gate, m_w_o, m_w_up, m_ffn_conv_w, m_w_down, v_w_mod, v_b_mod, v_g_pre_mix, v_g_post_mix, v_g_pre_ffn, v_g_post_ffn, v_w_in, v_ssm_log_dt, v_ssm_a_re, v_ssm_a_im, v_ssm_b_re, v_ssm_b_im, v_ssm_c_re, v_ssm_c_im, v_ssm_d, v_w_glu, v_b_glu, v_conv_mix_w, v_w_ssm_out, v_w_attn_out, v_w_conv_out, v_b_gate, v_w_o, v_w_up, v_ffn_conv_w, v_w_down):
    weights = dict(w_mod=w_mod, b_mod=b_mod, g_pre_mix=g_pre_mix, g_post_mix=g_post_mix, g_pre_ffn=g_pre_ffn, g_post_ffn=g_post_ffn, w_in=w_in, ssm_log_dt=ssm_log_dt, ssm_a_re=ssm_a_re, ssm_a_im=ssm_a_im, ssm_b_re=ssm_b_re, ssm_b_im=ssm_b_im, ssm_c_re=ssm_c_re, ssm_c_im=ssm_c_im, ssm_d=ssm_d, w_glu=w_glu, b_glu=b_glu, conv_mix_w=conv_mix_w, w_ssm_out=w_ssm_out, w_attn_out=w_attn_out, w_conv_out=w_conv_out, b_gate=b_gate, w_o=w_o, w_up=w_up, ffn_conv_w=ffn_conv_w, w_down=w_down)
    mom_m = dict(w_mod=m_w_mod, b_mod=m_b_mod, g_pre_mix=m_g_pre_mix, g_post_mix=m_g_post_mix, g_pre_ffn=m_g_pre_ffn, g_post_ffn=m_g_post_ffn, w_in=m_w_in, ssm_log_dt=m_ssm_log_dt, ssm_a_re=m_ssm_a_re, ssm_a_im=m_ssm_a_im, ssm_b_re=m_ssm_b_re, ssm_b_im=m_ssm_b_im, ssm_c_re=m_ssm_c_re, ssm_c_im=m_ssm_c_im, ssm_d=m_ssm_d, w_glu=m_w_glu, b_glu=m_b_glu, conv_mix_w=m_conv_mix_w, w_ssm_out=m_w_ssm_out, w_attn_out=m_w_attn_out, w_conv_out=m_w_conv_out, b_gate=m_b_gate, w_o=m_w_o, w_up=m_w_up, ffn_conv_w=m_ffn_conv_w, w_down=m_w_down)
    mom_v = dict(w_mod=v_w_mod, b_mod=v_b_mod, g_pre_mix=v_g_pre_mix, g_post_mix=v_g_post_mix, g_pre_ffn=v_g_pre_ffn, g_post_ffn=v_g_post_ffn, w_in=v_w_in, ssm_log_dt=v_ssm_log_dt, ssm_a_re=v_ssm_a_re, ssm_a_im=v_ssm_a_im, ssm_b_re=v_ssm_b_re, ssm_b_im=v_ssm_b_im, ssm_c_re=v_ssm_c_re, ssm_c_im=v_ssm_c_im, ssm_d=v_ssm_d, w_glu=v_w_glu, b_glu=v_b_glu, conv_mix_w=v_conv_mix_w, w_ssm_out=v_w_ssm_out, w_attn_out=v_w_attn_out, w_conv_out=v_w_conv_out, b_gate=v_b_gate, w_o=v_w_o, w_up=v_w_up, ffn_conv_w=v_ffn_conv_w, w_down=v_w_down)
    names = list(weights)

    nl = w_in.shape[0]
    seq, d = x.shape[1], x.shape[2]
    sw = d // 4
    groups = sw // SSM_GROUP
    gn = groups * SSM_STATE
    hp = sw // HEAD_DIM
    qw = 3 * sw
    off_q, off_k, off_v = sw, sw + qw, sw + 2 * qw
    off_conv = sw + 3 * qw
    off_gate = off_conv + 3 * sw
    n_in = off_gate + 3 * d
    f = w_down.shape[1] * N_CHIP
    scan_c = _scan_cols(gn)
    assert seq % (ATTN_BLOCK * DSWA_PATTERNS[-1][1]) == 0 and all(w // dl == ATTN_BLOCK for w, dl in DSWA_PATTERNS)

    px, py, pc = _my_pos()
    chip = 2 * px + py
    dev = 2 * chip + pc

    x2 = x.reshape(seq, d)
    target2 = loss_target.reshape(seq, d)

    mix_keys = ("w_in", "w_glu", "w_ssm_out", "w_attn_out", "w_conv_out", "w_o")
    ffn_keys = ("w_up", "w_down")
    col_sharded = ("w_in", "w_ssm_out", "w_attn_out", "w_conv_out", "w_up")
    n_stages = 2 * nl

    def stage_keys(stage):
        return ffn_keys if stage % 2 else mix_keys

    def blocked(k, buf8):
        r, cols = weights[k].shape[1:]
        return buf8.reshape(1, N_CHIP, r, cols) if k in col_sharded else buf8.reshape(1, 1, N_CHIP * r, cols)

    def begin_gather(stage):
        keys = stage_keys(stage)
        bufs = []
        for k in keys:
            r, cols = weights[k].shape[1:]
            bufs.append(_cast_own(weights[k], stage // 2, "cast_own").reshape(N_DEV, r // 2, cols))
        return _ici_start(bufs, None, _own_block, f"gather_start_{stage}")

    def end_gather(stage, pending, after):
        send_sems, recv_sems, bufs, _ = pending
        bufs = _ici_wait(send_sems, recv_sems, bufs, len(bufs), True, _own_block, after, f"gather_wait_{stage}")
        bufs = _forward_halves(bufs, "gather_forward")
        return {k: blocked(k, b) for k, b in zip(stage_keys(stage), bufs)}

    c_all = _ag8(c.reshape(1, 1, 1, d), False, "ag_cond").reshape(N_DEV, d)
    c_pad = jnp.concatenate([c_all, jnp.zeros((SUBLANE, d), F32)], axis=0)
    mcols = w_mod.shape[2]
    w_mod4 = w_mod.reshape(nl, 1, d, mcols)
    mod_loc = jnp.stack([_mm_nn(c_pad, w_mod4, l, name="mod_fwd", a_fn=jax.nn.silu) for l in range(nl)])
    mod_all = _ag8(mod_loc.reshape(nl, 1, 2 * SUBLANE, mcols), False, "ag_mod")
    mod_rows = lax.dynamic_slice_in_dim(mod_all[:, 0::2], dev, 1, axis=2)
    mod = mod_rows.reshape(nl, N_CHIP * mcols) + b_mod
    mods = mod.reshape(nl, 6, 1, d)

    def gather_whole(w):
        _, r, cols = w.shape
        got = _ag8(w.reshape(nl, 1, r, cols), False, "ag_small_weight")[:, 0::2]
        return got.transpose(0, 2, 1, 3).reshape(nl, r, N_CHIP * cols)

    conv_w_full = gather_whole(conv_mix_w)
    ffn_w_full = gather_whole(ffn_conv_w)
    wfs = [dict() for _ in range(nl)]
    pending = begin_gather(0)
    wfs[0].update(end_gather(0, pending, mods))
    pending = begin_gather(1)

    head_ones = jnp.asarray(np.kron(np.eye(hp, dtype=np.float32), np.ones((HEAD_DIM, HEAD_DIM), np.float32)))
    slopes = [_alibi_slopes(p, hp) for p in range(len(DSWA_PATTERNS))]
    qkv_offs = (off_q, off_k, off_v)

    def row(v):
        return v.reshape(1, -1)

    saved = []
    xl = x2
    for l in range(nl):
        sh1, sc1, gt1, sh2, sc2, gt2 = [mods[l, q] for q in range(6)]
        s = dict(x_in=xl)
        wf = wfs[l]
        (h1,) = _ew_fwd(_fn_norm_mod, [(xl, 0, d)], [row(g_pre_mix[l]), sc1, sh1 + pending[3][0, 0]], [BF16],
                        name="norm_mod_fwd", width=d)
        proj = _mm_nn(h1, wf["w_in"], 0, name="w_in_fwd")
        br_t = jnp.transpose(ssm_b_re[l], (2, 0, 1))
        bi_t = jnp.transpose(ssm_b_im[l], (2, 0, 1))
        disc_in = (ssm_log_dt[l].reshape(groups, 1), ssm_a_re[l], ssm_a_im[l], br_t, bi_t)
        lr, li, bbr_t, bbi_t = _ssm_disc_fwd(*disc_in)
        lam = _interleave(lr.reshape(1, gn), li.reshape(1, gn), scan_c)
        bcat = _interleave(_block_diag(jnp.transpose(bbr_t, (1, 0, 2))), _block_diag(jnp.transpose(bbi_t, (1, 0, 2))),
                           scan_c).astype(BF16).reshape(1, 1, sw, 2 * gn)
        cre = _block_diag(jnp.transpose(ssm_c_re[l], (0, 2, 1)))
        cim = _block_diag(jnp.transpose(ssm_c_im[l], (0, 2, 1)))
        ccat = jnp.transpose(_interleave(cre.T, -cim.T, scan_c)).astype(BF16).reshape(1, 1, 2 * gn, sw)
        xcat = _mm_nn(proj, bcat, 0, name="ssm_b_fwd", k_dim=sw)
        hcat = _ssm_scan_fwd(xcat, lam)
        y_ssm_pre = _mm_nn(hcat, ccat, 0, name="ssm_c_fwd")
        (gact,) = _ew_fwd(_fn_gelu, [(y_ssm_pre, 0, sw), (proj, 0, sw)], [row(ssm_d[l])], [F32], name="gelu_fwd", width=sw,
                          tw=_tile(sw, 512))
        z = _mm_nn(gact, wf["w_glu"], 0, name="w_glu_fwd")
        (s_ssm,) = _ew_fwd(_fn_glu, [(gact, 0, sw), (z, 0, sw)], [row(b_glu[l])], [BF16], name="glu_fwd", width=sw,
                           tw=_tile(sw, 512))
        y_ssm = _mm_nn(s_ssm, wf["w_ssm_out"], 0, name="w_branch_out_fwd")
        outs, lses = [], []
        for p, (_, dl) in enumerate(DSWA_PATTERNS):
            o_p, lse_p = _attn_fwd(proj, qkv_offs, p, dl, sw, slopes[p])
            outs.append(o_p)
            lses.append(lse_p)
        o_attn, lse_attn = _attn_merge(outs, lses)
        y_attn = _mm_nn(o_attn, wf["w_attn_out"], 0, name="w_branch_out_fwd")
        cv = _gconv_fwd(proj, off_conv, sw, conv_w_full[l])
        y_conv = _mm_nn(cv, wf["w_conv_out"], 0, name="w_branch_out_fwd")
        bg = b_gate[l].reshape(3, 1, d)
        gate_xs = [(proj, off_gate + q * d, d) for q in range(3)] + [(y_ssm, 0, d), (y_attn, 0, d), (y_conv, 0, d)]
        (merged,) = _ew_fwd(_fn_gates, gate_xs, [bg[0], bg[1], bg[2]], [BF16], name="gates_fwd", width=d,
                            tw=_tile(sw, 512))
        y_mix = _mm_nn(merged, wf["w_o"], 0, name="w_o_fwd")
        (x_mid,) = _ew_fwd(_fn_residual, [(xl, 0, d), (y_mix, 0, d)], [gt1, row(g_post_mix[l])], [F32], name="residual_fwd",
                           width=d)
        wf.update(end_gather(2 * l + 1, pending, x_mid))
        pending = begin_gather(2 * l + 2) if l + 1 < nl else None
        sh2_t = sh2 if pending is None else sh2 + pending[3][0, 0]
        (h2,) = _ew_fwd(_fn_norm_mod, [(x_mid, 0, d)], [row(g_pre_ffn[l]), sc2, sh2_t], [BF16], name="norm_mod_fwd",
                        width=d)
        up = _mm_nn(h2, wf["w_up"], 0, name="w_up_fwd")
        act = _ffn_act_fwd(up, ffn_w_full[l])
        y_ffn = _mm_nn(act, wf["w_down"], 0, name="w_down_fwd")
        (x_out,) = _ew_fwd(_fn_residual, [(x_mid, 0, d), (y_ffn, 0, d)], [gt2, row(g_post_ffn[l])], [F32],
                           name="residual_fwd", width=d)
        if pending is not None:
            wfs[l + 1].update(end_gather(2 * l + 2, pending, x_out))
            pending = begin_gather(2 * l + 3)
        s.update(h1=h1, proj=proj, disc_in=disc_in, lam=lam, bcat=bcat, ccat=ccat, hcat=hcat, y_ssm_pre=y_ssm_pre,
                 gact=gact, z=z, s_ssm=s_ssm, y_ssm=y_ssm, o_attn=o_attn, lse_attn=lse_attn, y_attn=y_attn,
                 cv=cv, y_conv=y_conv, merged=merged, y_mix=y_mix, x_mid=x_mid, h2=h2, up=up, act=act, y_ffn=y_ffn)
        saved.append(s)
        xl = x_out

    dxl, loss_local = _loss_fwd_bwd(xl, target2)
    loss = lax.psum(loss_local, ("x", "y", "c"))

    gfin = {k: lax.empty((nl, 2, weights[k].shape[1] // 2, weights[k].shape[2]), F32) for k in mix_keys + ffn_keys}

    def begin_rs(stage, grads_4d):
        keys = stage_keys(stage)
        g8s = [grads_4d[k].reshape(N_DEV, weights[k].shape[1] // 2, weights[k].shape[2]) for k in keys]
        recv_a = _rs_sibling(g8s, "rs_sibling")
        s_rem = [_rs_add_remote(g, ra, "rs_add_remote") for g, ra in zip(g8s, recv_a)]
        lands = [lax.empty(t.shape, BF16) for t in s_rem]
        return stage, g8s, recv_a, _ici_start(s_rem, lands, _distance_slot, f"rs_start_{stage}")

    def end_rs(pending_rs, after):
        stage, g8s, recv_a, (send_sems, recv_sems, arrays, _) = pending_rs
        n = len(g8s)
        arrays = _ici_wait(send_sems, recv_sems, arrays, n, False, _distance_slot, after, f"rs_wait_{stage}")
        for k, g8, ra, rb in zip(stage_keys(stage), g8s, recv_a, arrays[n:]):
            gfin[k] = _rs_add_final(g8, ra, rb, gfin[k], stage // 2, "rs_add_final")

    def after_rs_start(v, pending_rs):
        return v if pending_rs is None else v + pending_rs[3][3][0, 0]

    def grad_buf(k):
        return lax.empty(wf[k].shape, F32)

    pending_rs = None
    small = {k: [None] * nl for k in ("g_pre_mix", "g_post_mix", "g_pre_ffn", "g_post_ffn", "ssm_log_dt", "ssm_a_re",
                                      "ssm_a_im", "ssm_b_re", "ssm_b_im", "ssm_c_re", "ssm_c_im", "ssm_d", "b_glu",
                                      "conv_mix_w", "b_gate", "ffn_conv_w", "dmod")}
    for l in reversed(range(nl)):
        s = saved[l]
        sh1, sc1, gt1, sh2, sc2, gt2 = [mods[l, q] for q in range(6)]
        proj = s["proj"]
        wf = wfs[l]
        gw = {}
        (dy_ffn,), (dgt2, dg_post_ffn) = _ew_bwd(
            _fn_residual, [(s["x_mid"], 0, d), (s["y_ffn"], 0, d)], [after_rs_start(gt2, pending_rs), row(g_post_ffn[l])],
            [[dxl]], [None, BF16], name="residual_bwd", width=d)
        dact = _mm_nt(dy_ffn, wf["w_down"], 0, name="w_down_bwd_x", out_dtype=BF16)
        gw["w_down"] = _mm_tn(s["act"], dy_ffn, grad_buf("w_down"), 0, name="w_down_bwd_w")
        dup_a, dup_b, dwa, dwb = _ffn_act_bwd(s["up"], ffn_w_full[l], dact)
        dup = jnp.concatenate([dup_a, dup_b], axis=1)
        small["ffn_conv_w"][l] = jnp.concatenate([dwa, dwb], axis=1)
        dh2 = _mm_nt(dup, wf["w_up"], 0, name="w_up_bwd_x")
        gw["w_up"] = _mm_tn(s["h2"], dup, grad_buf("w_up"), 0, name="w_up_bwd_w")
        (dx_mid,), (dg_pre_ffn, dsc2, dsh2) = _ew_bwd(
            _fn_norm_mod, [(s["x_mid"], 0, d)], [row(g_pre_ffn[l]), sc2, sh2], [[dh2]], [F32], name="norm_mod_bwd", width=d,
            dx_add={0: dxl})
        if pending_rs is not None:
            end_rs(pending_rs, dx_mid)
        pending_rs = begin_rs(2 * l + 1, gw)
        (dy_mix,), (dgt1, dg_post_mix) = _ew_bwd(
            _fn_residual, [(s["x_in"], 0, d), (s["y_mix"], 0, d)], [after_rs_start(gt1, pending_rs), row(g_post_mix[l])],
            [[dx_mid]], [None, BF16], name="residual_bwd", width=d)
        dmerged = _mm_nt(dy_mix, wf["w_o"], 0, name="w_o_bwd_x")
        gw["w_o"] = _mm_tn(s["merged"], dy_mix, grad_buf("w_o"), 0, name="w_o_bwd_w")
        bg = b_gate[l].reshape(3, 1, d)
        gate_xs = [(proj, off_gate + q * d, d) for q in range(3)] + [(s["y_ssm"], 0, d), (s["y_attn"], 0, d),
                                                                     (s["y_conv"], 0, d)]
        (dp0, dp1, dp2, dy_ssm, dy_attn, dy_conv), dbg = _ew_bwd(
            _fn_gates, gate_xs, [bg[0], bg[1], bg[2]], [[dmerged]], [BF16] * 6, name="gates_bwd", width=d,
            tw=_tile(sw, 512))
        small["b_gate"][l] = jnp.concatenate(dbg, axis=1)[0]
        ds_ssm = _mm_nt(dy_ssm, wf["w_ssm_out"], 0, name="w_branch_out_bwd_x")
        gw["w_ssm_out"] = _mm_tn(s["s_ssm"], dy_ssm, grad_buf("w_ssm_out"), 0, name="w_branch_out_bwd_w")
        (dg1, dz), (db_glu,) = _ew_bwd(_fn_glu, [(s["gact"], 0, sw), (s["z"], 0, sw)], [row(b_glu[l])], [[ds_ssm]],
                                       [F32, BF16], name="glu_bwd", width=sw, tw=_tile(sw, 512))
        dg2 = _mm_nt(dz, wf["w_glu"], 0, name="w_glu_bwd_x")
        gw["w_glu"] = _mm_tn(s["gact"], dz, grad_buf("w_glu"), 0, name="w_glu_bwd_w")
        (dy_pre, du_skip), (dd_skip,) = _ew_bwd(_fn_gelu, [(s["y_ssm_pre"], 0, sw), (proj, 0, sw)], [row(ssm_d[l])],
                                               [[dg1, dg2]], [BF16, F32], name="gelu_bwd", width=sw, tw=_tile(sw, 512))
        dhcat = _mm_nt(dy_pre, s["ccat"], 0, name="ssm_c_bwd_x")
        dccat = _mm_tn(s["hcat"], dy_pre, lax.empty((1, 1, 2 * gn, sw), F32), 0, name="ssm_c_bwd_w")[0, 0]
        gcat, dlam = _ssm_scan_bwd(dhcat, s["hcat"], s["lam"])
        du_b = _mm_nt(gcat, s["bcat"], 0, name="ssm_b_bwd_x")
        dbcat = _mm_tn(proj, gcat, lax.empty((1, 1, sw, 2 * gn), F32), 0, name="ssm_b_bwd_w")[0, 0]
        dlr, dli = _deinterleave(dlam, scan_c)
        dbre, dbim = _deinterleave(dbcat, scan_c)
        dbbr_t = jnp.transpose(_diag_blocks(dbre, groups), (1, 0, 2))
        dbbi_t = jnp.transpose(_diag_blocks(dbim, groups), (1, 0, 2))
        gld, gar, gai, gbr_t, gbi_t = _ssm_disc_bwd(*s["disc_in"], dlr.reshape(groups, SSM_STATE),
                                                    dli.reshape(groups, SSM_STATE), dbbr_t, dbbi_t)
        dcre_t, dcim_t = _deinterleave(dccat.T, scan_c)
        small["ssm_c_re"][l] = _diag_blocks(dcre_t, groups)
        small["ssm_c_im"][l] = -_diag_blocks(dcim_t, groups)
        small["ssm_log_dt"][l] = gld.reshape(groups)
        small["ssm_a_re"][l], small["ssm_a_im"][l] = gar, gai
        small["ssm_b_re"][l] = jnp.transpose(gbr_t, (1, 2, 0))
        small["ssm_b_im"][l] = jnp.transpose(gbi_t, (1, 2, 0))
        small["ssm_d"][l], small["b_glu"][l] = dd_skip[0], db_glu[0]
        du = (du_skip + du_b).astype(BF16)
        do_attn = _mm_nt(dy_attn, wf["w_attn_out"], 0, name="w_branch_out_bwd_x")
        gw["w_attn_out"] = _mm_tn(s["o_attn"], dy_attn, grad_buf("w_attn_out"), 0, name="w_branch_out_bwd_w")
        delta = _attn_delta(do_attn, s["o_attn"], head_ones)
        dqs, dks, dvs = [], [], []
        for p, (_, dl) in enumerate(DSWA_PATTERNS):
            dq_p, dk_p, dv_p = _attn_bwd(proj, do_attn, s["lse_attn"], delta, qkv_offs, p, dl, sw, slopes[p])
            dqs.append(dq_p)
            dks.append(dk_p)
            dvs.append(dv_p)
        dcv = _mm_nt(dy_conv, wf["w_conv_out"], 0, name="w_branch_out_bwd_x", out_dtype=BF16)
        gw["w_conv_out"] = _mm_tn(s["cv"], dy_conv, grad_buf("w_conv_out"), 0, name="w_branch_out_bwd_w")
        dcb, dcc, dch, dconv_w = _gconv_bwd(proj, off_conv, sw, conv_w_full[l], dcv)
        small["conv_mix_w"][l] = dconv_w
        dproj = jnp.concatenate([du] + dqs + dks + dvs + [dcb, dcc, dch, dp0, dp1, dp2], axis=1)
        dh1 = _mm_nt(dproj, wf["w_in"], 0, name="w_in_bwd_x")
        gw["w_in"] = _mm_tn(s["h1"], dproj, grad_buf("w_in"), 0, name="w_in_bwd_w")
        (dx_in,), (dg_pre_mix, dsc1, dsh1) = _ew_bwd(
            _fn_norm_mod, [(s["x_in"], 0, d)], [row(g_pre_mix[l]), sc1, sh1], [[dh1]], [F32], name="norm_mod_bwd", width=d,
            dx_add={0: dx_mid})
        end_rs(pending_rs, dx_in)
        pending_rs = begin_rs(2 * l, gw)
        small["g_pre_mix"][l], small["g_post_mix"][l] = dg_pre_mix[0], dg_post_mix[0]
        small["g_pre_ffn"][l], small["g_post_ffn"][l] = dg_pre_ffn[0], dg_post_ffn[0]
        small["dmod"][l] = jnp.concatenate([dsh1, dsc1, dgt1, dsh2, dsc2, dgt2], axis=1)[0]
        dxl = dx_in

    grad_x = dxl.reshape(x.shape)

    small = {k: jnp.stack(v) for k, v in small.items()}
    order = sorted(small)
    flat = jnp.concatenate([small[k].reshape(-1) for k in order])
    n_small = flat.shape[0]
    pack_w = 8 * LANE
    pack_r = -(-n_small // (pack_w * SUBLANE)) * SUBLANE
    flat = jnp.concatenate([flat, jnp.zeros((pack_r * pack_w - n_small,), F32)])
    gathered = _ag8(flat.reshape(1, 1, pack_r, pack_w), False, "ag_small_grads")[0]
    summed = _sum8(gathered, "sum_small_grads").reshape(-1)
    sgrad, at = {}, 0
    for k in order:
        size = small[k].size
        sgrad[k] = summed[at:at + size].reshape(small[k].shape)
        at += size
    dmod_off = sum(small[k].size for k in order[:order.index("dmod")])
    dmod_all = gathered.reshape(N_DEV, -1)[:, dmod_off:dmod_off + nl * 6 * d].reshape(N_DEV, nl, 6 * d)
    dmod_loc = lax.dynamic_slice_in_dim(jnp.transpose(dmod_all, (1, 0, 2)), chip * mcols, mcols, axis=2)

    grads = dict(sgrad)
    grads["b_mod"] = grads.pop("dmod")
    grads["conv_mix_w"] = lax.dynamic_slice_in_dim(sgrad["conv_mix_w"], chip * conv_mix_w.shape[2], conv_mix_w.shape[2], axis=2)
    grads["ffn_conv_w"] = lax.dynamic_slice_in_dim(sgrad["ffn_conv_w"], chip * ffn_conv_w.shape[2], ffn_conv_w.shape[2], axis=2)

    end_rs(pending_rs, summed)
    for k in mix_keys + ffn_keys:
        grads[k] = _share_halves(gfin[k], "rs_share").reshape(weights[k].shape)

    delta_w, new_m, new_v = {}, {}, {}
    c_t = jnp.pad(jnp.transpose(c_all), ((0, 0), (0, LANE - N_DEV)))
    dmod_pad = jnp.pad(dmod_loc, ((0, 0), (0, LANE - N_DEV), (0, 0)))
    grads["w_mod"], delta_w["w_mod"], new_m["w_mod"], new_v["w_mod"] = _wmod_grad_adam(
        c_t, dmod_pad, w_mod, m_w_mod, v_w_mod)
    for k in names:
        if k == "w_mod":
            continue
        delta_w[k], new_m[k], new_v[k] = _adam(weights[k], grads[k], mom_m[k], mom_v[k], "adamw")

    return (loss, grad_x, *[grads[k] for k in names], *[delta_w[k] for k in names], *[new_m[k] for k in names],
            *[new_v[k] for k in names])
```

```python
import functools
import math

import numpy as np
import jax
import jax.numpy as jnp
from jax import lax
from jax.experimental import pallas as pl
from jax.experimental.pallas import tpu as pltpu

F32 = jnp.float32
BF16 = jnp.bfloat16
MESH = pl.DeviceIdType.MESH
ANY = pl.BlockSpec(memory_space=pl.ANY)

VMEM_LIMIT_BYTES = 48 * 1024 * 1024
LANE = 128
SUBLANE = 8

RMS_EPS = 1e-6
NEG_INF = -1e30
SSM_GROUP = 16
SSM_STATE = 64
HEAD_DIM = 64
DSWA_PATTERNS = ((128, 1), (512, 4), (2048, 16))
ATTN_BLOCK = 128
N_DEV = 8
N_CHIP = 4

ADAM_LR = 0.001
ADAM_B1 = 0.9
ADAM_B2 = 0.999
ADAM_EPS = 1e-08
ADAM_WD = 0.01
ADAM_STEP = 10


def _cp(sem=None):
    return pltpu.CompilerParams(dimension_semantics=sem, vmem_limit_bytes=VMEM_LIMIT_BYTES)


def _tile(n, pref, align=LANE):
    if n <= pref:
        return n
    t = (pref // align) * align
    while t >= align:
        if n % t == 0:
            return t
        t -= align
    return n


def _sds(shape, dtype):
    return jax.ShapeDtypeStruct(tuple(shape), dtype)


MM_VMEM_BUDGET = 34 * 1024 * 1024
MM_MAX_CONTRACT = 2048


def _divisors(n, align):
    if n % align:
        return [n]
    return [t for t in range(n, 0, -align) if n % t == 0]


def _halvings(n, align, floor=256):
    out = [n]
    while out[-1] % (2 * align) == 0 and out[-1] // 2 >= floor:
        out.append(out[-1] // 2)
    return out


def _pick_tiles(rows, cols, fixed_bytes, row_bytes, col_bytes, cell_bytes):
    best = None
    for tr in rows:
        for tc in cols:
            if fixed_bytes + row_bytes * tr + col_bytes * tc + cell_bytes * tr * tc <= MM_VMEM_BUDGET:
                if best is None or tr * tc > best[0] * best[1]:
                    best = (tr, tc)
                break
    assert best is not None
    return best


def _accumulate(step, n_steps, part, o_ref, acc_ref):
    if n_steps == 1:
        o_ref[...] = part.astype(o_ref.dtype)
        return
    acc = o_ref if acc_ref is None else acc_ref

    @pl.when(step == 0)
    def _():
        acc[...] = part

    @pl.when(step > 0)
    def _():
        acc[...] += part

    if acc_ref is not None:
        @pl.when(step == n_steps - 1)
        def _():
            o_ref[...] = acc_ref[...].astype(o_ref.dtype)


def _mm_nn(a, w, layer, *, name, k_dim=None, a_col0=0, out_dtype=F32, a_fn=None):
    m = a.shape[0]
    _, nb, kw, n = w.shape
    k_dim = kw if k_dim is None else k_dim
    assert k_dim == kw
    tk = _tile(k_dim, MM_MAX_CONTRACT)
    nk = k_dim // tk
    sa, so = a.dtype.itemsize, jnp.dtype(out_dtype).itemsize
    use_acc = nk > 1 and so != 4
    row_bytes = tk * (2 * sa + (2 if sa == 4 else 0) + (4 if a_fn is not None else 0))
    tm, tn = _pick_tiles(_halvings(m, SUBLANE), _divisors(n, LANE), 0, row_bytes, 2 * tk * w.dtype.itemsize,
                         2 * so + 4 + (4 if use_acc else 0))
    assert a_col0 % tk == 0
    npb = n // tn
    a0 = a_col0 // tk

    def body(a_ref, w_ref, o_ref, *scratch):
        av = a_ref[...]
        if a_fn is not None:
            av = a_fn(av.astype(F32))
        part = jnp.dot(av.astype(BF16), w_ref[...].astype(BF16), preferred_element_type=F32)
        _accumulate(pl.program_id(2), nk, part, o_ref, scratch[0] if use_acc else None)

    return pl.pallas_call(
        body,
        out_shape=_sds((m, nb * n), out_dtype),
        grid=(m // tm, nb * npb, nk),
        in_specs=[pl.BlockSpec((tm, tk), lambda i, j, k: (i, a0 + k)),
                  pl.BlockSpec((None, None, tk, tn), lambda i, j, k: (layer, j // npb, k, j % npb))],
        out_specs=pl.BlockSpec((tm, tn), lambda i, j, k: (i, j)),
        scratch_shapes=[pltpu.VMEM((tm, tn), F32)] if use_acc else [],
        compiler_params=_cp(("parallel", "parallel", "arbitrary")),
        name=name,
    )(a, w)


def _mm_nt(g, w, layer, *, name, out_dtype=F32):
    m = g.shape[0]
    _, nb, k_dim, n = w.shape
    assert g.shape[1] == nb * n
    tko = _tile(k_dim, MM_MAX_CONTRACT)
    sg, so = g.dtype.itemsize, jnp.dtype(out_dtype).itemsize
    use_acc = so != 4
    res_row = tko * (2 * so + 4 + (4 if use_acc else 0))
    tm, tc = _pick_tiles(_halvings(m, SUBLANE), _divisors(n, LANE), 0, res_row, 2 * tko * w.dtype.itemsize,
                         2 * sg + (2 if sg == 4 else 0))
    npb = n // tc
    nr = nb * npb
    use_acc = use_acc and nr > 1

    def body(g_ref, w_ref, o_ref, *scratch):
        part = lax.dot_general(g_ref[...].astype(BF16), w_ref[...].astype(BF16),
                               (((1,), (1,)), ((), ())), preferred_element_type=F32)
        _accumulate(pl.program_id(2), nr, part, o_ref, scratch[0] if use_acc else None)

    return pl.pallas_call(
        body,
        out_shape=_sds((m, k_dim), out_dtype),
        grid=(m // tm, k_dim // tko, nr),
        in_specs=[pl.BlockSpec((tm, tc), lambda i, kk, r: (i, r)),
                  pl.BlockSpec((None, None, tko, tc), lambda i, kk, r: (layer, r // npb, kk, r % npb))],
        out_specs=pl.BlockSpec((tm, tko), lambda i, kk, r: (i, kk)),
        scratch_shapes=[pltpu.VMEM((tm, tko), F32)] if use_acc else [],
        compiler_params=_cp(("parallel", "parallel", "arbitrary")),
        name=name,
    )(g, w)


def _mm_tn(a, g, out_buf, layer, *, name, a_col0=0):
    m = a.shape[0]
    _, nb, k_dim, n = out_buf.shape
    assert g.shape == (m, nb * n)
    tm = _tile(m, MM_MAX_CONTRACT, SUBLANE)
    nr = m // tm
    sa, sg = a.dtype.itemsize, g.dtype.itemsize
    tk, tn = _pick_tiles(_halvings(k_dim, LANE), _divisors(n, LANE), 0, tm * (2 * sa + (2 if sa == 4 else 0) + 2),
                         tm * (2 * sg + (2 if sg == 4 else 0)), 2 * 4 + 4)
    assert a_col0 % tk == 0
    a0 = a_col0 // tk
    npb = n // tn

    def body(a_ref, g_ref, buf_ref, o_ref):
        del buf_ref
        part = lax.dot_general(a_ref[...].astype(BF16), g_ref[...].astype(BF16),
                               (((0,), (0,)), ((), ())), preferred_element_type=F32)
        _accumulate(pl.program_id(2), nr, part, o_ref, None)

    return pl.pallas_call(
        body,
        out_shape=_sds(out_buf.shape, F32),
        grid=(k_dim // tk, nb * npb, nr),
        in_specs=[pl.BlockSpec((tm, tk), lambda kk, j, r: (r, a0 + kk)),
                  pl.BlockSpec((tm, tn), lambda kk, j, r: (r, j)),
                  ANY],
        out_specs=pl.BlockSpec((None, None, tk, tn), lambda kk, j, r: (layer, j // npb, kk, j % npb)),
        input_output_aliases={2: 0},
        compiler_params=_cp(("parallel", "parallel", "arbitrary")),
        name=name,
    )(a, g, out_buf)


def _ew_fwd(fn, xs, ps, out_dtypes, *, name, width, tw=None, tm=256):
    rows = xs[0][0].shape[0]
    tm = _tile(rows, tm, SUBLANE)
    tw = width if tw is None else tw
    nx, n_p = len(xs), len(ps)

    def body(*refs):
        xv = [r[...].astype(F32) for r in refs[:nx]]
        pv = [r[...].astype(F32) for r in refs[nx:nx + n_p]]
        outs = fn(*xv, *pv)
        if not isinstance(outs, (tuple, list)):
            outs = (outs,)
        for o_ref, o in zip(refs[nx + n_p:], outs):
            o_ref[...] = o.astype(o_ref.dtype)

    in_specs = []
    for arr, c0, w in xs:
        assert w == width and c0 % tw == 0
        in_specs.append(pl.BlockSpec((tm, tw), functools.partial(lambda i, j, b: (i, b + j), b=c0 // tw)))
    for p in ps:
        assert p.shape == (1, width)
        in_specs.append(pl.BlockSpec((1, tw), lambda i, j: (0, j)))
    outs = pl.pallas_call(
        body,
        out_shape=[_sds((rows, width), d) for d in out_dtypes],
        grid=(rows // tm, width // tw),
        in_specs=in_specs,
        out_specs=[pl.BlockSpec((tm, tw), lambda i, j: (i, j)) for _ in out_dtypes],
        compiler_params=_cp(("parallel", "parallel")),
        name=name,
    )(*[x[0] for x in xs], *ps)
    return outs


def _ew_bwd(fn, xs, ps, cts, dx_dtypes, *, name, width, tw=None, tm=256, dx_add=None):
    rows = xs[0][0].shape[0]
    tm = _tile(rows, tm, SUBLANE)
    tw = width if tw is None else tw
    nx, n_p = len(xs), len(ps)
    dx_add = dx_add or {}
    flat_cts = [c for group in cts for c in group]
    add_keys = sorted(dx_add)
    n_in = nx + n_p + len(flat_cts) + len(add_keys)
    dx_idx = [i for i, d in enumerate(dx_dtypes) if d is not None]

    def body(*refs):
        i = pl.program_id(1)
        xv = [r[...].astype(F32) for r in refs[:nx]]
        pv = [r[...].astype(F32) for r in refs[nx:nx + n_p]]
        pos = nx + n_p
        ct_vals = []
        for group in cts:
            acc = refs[pos][...].astype(F32)
            pos += 1
            for _ in group[1:]:
                acc = acc + refs[pos][...].astype(F32)
                pos += 1
            ct_vals.append(acc)
        add_vals = {}
        for key in add_keys:
            add_vals[key] = refs[pos][...].astype(F32)
            pos += 1
        out_refs = refs[n_in:]
        outs, vjp = jax.vjp(fn, *xv, *pv)
        grads = vjp(tuple(ct_vals) if isinstance(outs, (tuple, list)) else ct_vals[0])
        o = 0
        for idx in dx_idx:
            gval = grads[idx]
            if idx in add_vals:
                gval = gval + add_vals[idx]
            out_refs[o][...] = gval.astype(out_refs[o].dtype)
            o += 1
        for q in range(n_p):
            gp = grads[nx + q]
            ref = out_refs[o + q]

            @pl.when(i == 0)
            def _(ref=ref, gp=gp):
                ref[...] = gp

            @pl.when(i > 0)
            def _(ref=ref, gp=gp):
                ref[...] += gp

    tile_spec = pl.BlockSpec((tm, tw), lambda j, i: (i, j))
    in_specs = []
    for arr, c0, w in xs:
        assert w == width and c0 % tw == 0
        in_specs.append(pl.BlockSpec((tm, tw), functools.partial(lambda j, i, b: (i, b + j), b=c0 // tw)))
    for p in ps:
        in_specs.append(pl.BlockSpec((1, tw), lambda j, i: (0, j)))
    in_specs += [tile_spec] * (len(flat_cts) + len(add_keys))
    out_shape = [_sds((rows, width), dx_dtypes[idx]) for idx in dx_idx] + [_sds((1, width), F32)] * n_p
    out_specs = [tile_spec] * len(dx_idx) + [pl.BlockSpec((1, tw), lambda j, i: (0, j))] * n_p
    outs = pl.pallas_call(
        body,
        out_shape=out_shape,
        grid=(width // tw, rows // tm),
        in_specs=in_specs,
        out_specs=out_specs,
        compiler_params=_cp(("parallel", "arbitrary")),
        name=name,
    )(*[x[0] for x in xs], *ps, *flat_cts, *[dx_add[k] for k in add_keys])
    return outs[:len(dx_idx)], outs[len(dx_idx):]


def _rms(x):
    return x * lax.rsqrt(jnp.mean(x * x, axis=-1, keepdims=True) + RMS_EPS)


def _fn_norm_mod(x, g, sc, sh):
    return (_rms(x) * g) * (1.0 + sc) + sh


def _fn_residual(x, y, gt, g):
    return x + gt * (_rms(y) * g)


def _fn_gelu(y, u, d):
    return jax.nn.gelu(y + d * u)


def _fn_glu(g, z, b):
    return g * jax.nn.sigmoid(z + b)


def _fn_gates(p0, p1, p2, ys, ya, yc, b0, b1, b2):
    return (jax.nn.sigmoid(p0 + b0) * ys + jax.nn.sigmoid(p1 + b1) * ya + jax.nn.sigmoid(p2 + b2) * yc)


def _fn_disc(log_dt, ar, ai, br_t, bi_t):
    dt = jnp.exp(log_dt)
    mag = jnp.exp(ar * dt)
    lr, li = mag * jnp.cos(ai * dt), mag * jnp.sin(ai * dt)
    den = ar * ar + ai * ai
    fr = ((lr - 1.0) * ar + li * ai) / den
    fi = (li * ar - (lr - 1.0) * ai) / den
    bbr = fr[None] * br_t - fi[None] * bi_t
    bbi = fr[None] * bi_t + fi[None] * br_t
    return lr, li, bbr, bbi


def _ssm_disc_fwd(log_dt, ar, ai, br_t, bi_t):
    g, n = ar.shape

    def body(ld_ref, ar_ref, ai_ref, br_ref, bi_ref, lr_ref, li_ref, bbr_ref, bbi_ref):
        lr, li, bbr, bbi = _fn_disc(ld_ref[...], ar_ref[...], ai_ref[...], br_ref[...], bi_ref[...])
        lr_ref[...] = lr
        li_ref[...] = li
        bbr_ref[...] = bbr
        bbi_ref[...] = bbi

    return pl.pallas_call(
        body,
        out_shape=[_sds((g, n), F32), _sds((g, n), F32), _sds(br_t.shape, F32), _sds(br_t.shape, F32)],
        compiler_params=_cp(),
        name="ssm_disc_fwd",
    )(log_dt, ar, ai, br_t, bi_t)


def _ssm_disc_bwd(log_dt, ar, ai, br_t, bi_t, dlr, dli, dbbr, dbbi):
    g, n = ar.shape

    def body(ld_ref, ar_ref, ai_ref, br_ref, bi_ref, dlr_ref, dli_ref, dbbr_ref, dbbi_ref,
             gld_ref, gar_ref, gai_ref, gbr_ref, gbi_ref):
        _, vjp = jax.vjp(_fn_disc, ld_ref[...], ar_ref[...], ai_ref[...], br_ref[...], bi_ref[...])
        gld, gar, gai, gbr, gbi = vjp((dlr_ref[...], dli_ref[...], dbbr_ref[...], dbbi_ref[...]))
        gld_ref[...] = gld
        gar_ref[...] = gar
        gai_ref[...] = gai
        gbr_ref[...] = gbr
        gbi_ref[...] = gbi

    return pl.pallas_call(
        body,
        out_shape=[_sds((g, 1), F32), _sds((g, n), F32), _sds((g, n), F32), _sds(br_t.shape, F32),
                   _sds(br_t.shape, F32)],
        compiler_params=_cp(),
        name="ssm_disc_bwd",
    )(log_dt, ar, ai, br_t, bi_t, dlr, dli, dbbr, dbbi)


def _cmul(ar, ai, br, bi):
    return ar * br - ai * bi, ar * bi + ai * br


def _scan_tables(lr, li, reverse):
    c = lr.shape[-1]
    p1 = (jnp.broadcast_to(lr, (SUBLANE, c)), jnp.broadcast_to(li, (SUBLANE, c)))
    p2 = _cmul(*p1, *p1)
    p4 = _cmul(*p2, *p2)
    p8 = _cmul(*p4, *p4)
    row = lax.broadcasted_iota(jnp.int32, (SUBLANE, c), 0)
    dist = (SUBLANE - row) if reverse else (row + 1)
    pr, pi = jnp.ones((SUBLANE, c), F32), jnp.zeros((SUBLANE, c), F32)
    for bit, pw in ((1, p1), (2, p2), (4, p4), (8, p8)):
        qr, qi = _cmul(pr, pi, *pw)
        take = (dist & bit) != 0
        pr, pi = jnp.where(take, qr, pr), jnp.where(take, qi, pi)
    return row, (p1, p2, p4), (pr, pi)


def _shift_rows(x, s, row, reverse):
    if reverse:
        return jnp.where(row < SUBLANE - s, pltpu.roll(x, SUBLANE - s, 0), 0.0)
    return jnp.where(row >= s, pltpu.roll(x, s, 0), 0.0)


def _scan_tile(xr, xi, carry, row, pows, carry_pow, reverse):
    for s, pw in zip((1, 2, 4), pows):
        sr, si = _shift_rows(xr, s, row, reverse), _shift_rows(xi, s, row, reverse)
        tr, ti = _cmul(*pw, sr, si)
        xr, xi = xr + tr, xi + ti
    tr, ti = _cmul(*carry_pow, *carry)
    hr, hi = xr + tr, xi + ti
    edge = 0 if reverse else SUBLANE - 1
    c = hr.shape[-1]
    new_carry = (jnp.broadcast_to(hr[edge:edge + 1, :], (SUBLANE, c)),
                 jnp.broadcast_to(hi[edge:edge + 1, :], (SUBLANE, c)))
    return hr, hi, new_carry


def _scan_cols(gn):
    return _tile(gn, 256)


def _ssm_scan_fwd(xcat, lam):
    rows, gn2 = xcat.shape
    c = _scan_cols(gn2 // 2)
    n_tiles = rows // SUBLANE

    def body(lam_ref, x_ref, h_ref):
        lr, li = lam_ref[:, :c], lam_ref[:, c:]
        row, pows, carry_pow = _scan_tables(lr, li, False)

        def step(k, carry):
            t0 = pl.multiple_of(k * SUBLANE, SUBLANE)
            hr, hi, carry = _scan_tile(x_ref[pl.ds(t0, SUBLANE), :c], x_ref[pl.ds(t0, SUBLANE), c:], carry,
                                       row, pows, carry_pow, False)
            h_ref[pl.ds(t0, SUBLANE), :c] = hr
            h_ref[pl.ds(t0, SUBLANE), c:] = hi
            return carry

        zero = jnp.zeros((SUBLANE, c), F32)
        lax.fori_loop(0, n_tiles, step, (zero, zero))

    return pl.pallas_call(
        body,
        out_shape=_sds((rows, gn2), F32),
        grid=(gn2 // (2 * c),),
        in_specs=[pl.BlockSpec((1, 2 * c), lambda j: (0, j)), pl.BlockSpec((rows, 2 * c), lambda j: (0, j))],
        out_specs=pl.BlockSpec((rows, 2 * c), lambda j: (0, j)),
        compiler_params=_cp(("parallel",)),
        name="ssm_scan_fwd",
    )(lam, xcat)


def _ssm_scan_bwd(dhcat, hcat, lam):
    rows, gn2 = dhcat.shape
    c = _scan_cols(gn2 // 2)
    n_tiles = rows // SUBLANE

    def body(lam_ref, dh_ref, h_ref, g_ref, dlam_ref):
        lr, li = lam_ref[:, :c], -lam_ref[:, c:]
        row, pows, carry_pow = _scan_tables(lr, li, True)

        def step(k, state):
            carry, acc_r, acc_i = state
            kk = n_tiles - 1 - k
            t0 = pl.multiple_of(kk * SUBLANE, SUBLANE)
            gr, gi, carry = _scan_tile(dh_ref[pl.ds(t0, SUBLANE), :c], dh_ref[pl.ds(t0, SUBLANE), c:], carry,
                                       row, pows, carry_pow, True)
            g_ref[pl.ds(t0, SUBLANE), :c] = gr
            g_ref[pl.ds(t0, SUBLANE), c:] = gi
            tp = pl.multiple_of(jnp.maximum(kk - 1, 0) * SUBLANE, SUBLANE)
            has_prev = (kk > 0).astype(F32)
            prev_r = pltpu.roll(h_ref[pl.ds(tp, SUBLANE), :c], 1, 0) * has_prev
            prev_i = pltpu.roll(h_ref[pl.ds(tp, SUBLANE), c:], 1, 0) * has_prev
            hpr = jnp.where(row >= 1, pltpu.roll(h_ref[pl.ds(t0, SUBLANE), :c], 1, 0), prev_r)
            hpi = jnp.where(row >= 1, pltpu.roll(h_ref[pl.ds(t0, SUBLANE), c:], 1, 0), prev_i)
            acc_r = acc_r + gr * hpr + gi * hpi
            acc_i = acc_i + gi * hpr - gr * hpi
            return carry, acc_r, acc_i

        zero = jnp.zeros((SUBLANE, c), F32)
        _, acc_r, acc_i = lax.fori_loop(0, n_tiles, step, ((zero, zero), zero, zero))
        dlam_ref[:, :c] = jnp.sum(acc_r, axis=0, keepdims=True)
        dlam_ref[:, c:] = jnp.sum(acc_i, axis=0, keepdims=True)

    blk = pl.BlockSpec((rows, 2 * c), lambda j: (0, j))
    return pl.pallas_call(
        body,
        out_shape=[_sds((rows, gn2), F32), _sds((1, gn2), F32)],
        grid=(gn2 // (2 * c),),
        in_specs=[pl.BlockSpec((1, 2 * c), lambda j: (0, j)), blk, blk],
        out_specs=[blk, pl.BlockSpec((1, 2 * c), lambda j: (0, j))],
        compiler_params=_cp(("parallel",)),
        name="ssm_scan_bwd",
    )(lam, dhcat, hcat)


def _shift_down(x, k, row):
    return x if k == 0 else jnp.where(row >= k, pltpu.roll(x, k, 0), 0.0)


def _shift_up(x, k, row):
    n = x.shape[0]
    return x if k == 0 else jnp.where(row < n - k, pltpu.roll(x, n - k, 0), 0.0)


def _taps(w_ref):
    return [w_ref[k:k + 1, :] for k in range(3)]


def _conv3(x, w, row):
    return sum(w[k] * _shift_down(x, k, row) for k in range(3))


def _conv3_bwd(x, w, dy, row):
    dx = sum(w[k] * _shift_up(dy, k, row) for k in range(3))
    dw = [jnp.sum(dy * _shift_down(x, k, row), axis=0, keepdims=True) for k in range(3)]
    return dx, dw


def _gconv_fwd(proj, off, cw, w):
    rows = proj.shape[0]
    tc = _tile(cw, LANE)
    nb = cw // tc

    def body(b_ref, c_ref, h_ref, w_ref, o_ref):
        row = lax.broadcasted_iota(jnp.int32, (rows, tc), 0)
        o_ref[...] = (b_ref[...] * _conv3(c_ref[...] * h_ref[...], _taps(w_ref), row)).astype(o_ref.dtype)

    specs = [pl.BlockSpec((rows, tc), functools.partial(lambda j, b: (0, b + j), b=(off + q * cw) // tc))
             for q in range(3)]
    return pl.pallas_call(
        body,
        out_shape=_sds((rows, cw), BF16),
        grid=(nb,),
        in_specs=specs + [pl.BlockSpec((3, tc), lambda j: (0, j))],
        out_specs=pl.BlockSpec((rows, tc), lambda j: (0, j)),
        compiler_params=_cp(("parallel",)),
        name="gconv_fwd",
    )(proj, proj, proj, w)


def _gconv_bwd(proj, off, cw, w, dy):
    rows = proj.shape[0]
    tc = _tile(cw, LANE)
    nb = cw // tc

    def body(b_ref, c_ref, h_ref, w_ref, dy_ref, db_ref, dc_ref, dh_ref, dw_ref):
        row = lax.broadcasted_iota(jnp.int32, (rows, tc), 0)
        cv, hv, dyv = c_ref[...], h_ref[...], dy_ref[...].astype(F32)
        t = cv * hv
        db_ref[...] = (dyv * _conv3(t, _taps(w_ref), row)).astype(db_ref.dtype)
        dt, dw = _conv3_bwd(t, _taps(w_ref), dyv * b_ref[...], row)
        dc_ref[...] = (dt * hv).astype(dc_ref.dtype)
        dh_ref[...] = (dt * cv).astype(dh_ref.dtype)
        for k in range(3):
            dw_ref[k:k + 1, :] = dw[k]

    specs = [pl.BlockSpec((rows, tc), functools.partial(lambda j, b: (0, b + j), b=(off + q * cw) // tc))
             for q in range(3)]
    col = pl.BlockSpec((rows, tc), lambda j: (0, j))
    wspec = pl.BlockSpec((3, tc), lambda j: (0, j))
    return pl.pallas_call(
        body,
        out_shape=[_sds((rows, cw), BF16)] * 3 + [_sds((3, cw), F32)],
        grid=(nb,),
        in_specs=specs + [wspec, col],
        out_specs=[col, col, col, wspec],
        compiler_params=_cp(("parallel",)),
        name="gconv_bwd",
    )(proj, proj, proj, w, dy)


def _ffn_act_fwd(up, w):
    rows, f2 = up.shape
    f = f2 // 2
    tc = _tile(f, LANE)
    nb = f // tc

    def body(a_ref, b_ref, wa_ref, wb_ref, o_ref):
        row = lax.broadcasted_iota(jnp.int32, (rows, tc), 0)
        a = _conv3(a_ref[...], _taps(wa_ref), row)
        b = _conv3(b_ref[...], _taps(wb_ref), row)
        o_ref[...] = (jax.nn.silu(a) * b).astype(o_ref.dtype)

    return pl.pallas_call(
        body,
        out_shape=_sds((rows, f), BF16),
        grid=(nb,),
        in_specs=[pl.BlockSpec((rows, tc), lambda j: (0, j)), pl.BlockSpec((rows, tc), lambda j: (0, nb + j)),
                  pl.BlockSpec((3, tc), lambda j: (0, j)), pl.BlockSpec((3, tc), lambda j: (0, nb + j))],
        out_specs=pl.BlockSpec((rows, tc), lambda j: (0, j)),
        compiler_params=_cp(("parallel",)),
        name="ffn_act_fwd",
    )(up, up, w, w)


def _ffn_act_bwd(up, w, dact):
    rows, f2 = up.shape
    f = f2 // 2
    tc = _tile(f, LANE)
    nb = f // tc

    def body(a_ref, b_ref, wa_ref, wb_ref, d_ref, da_ref, db_ref, dwa_ref, dwb_ref):
        row = lax.broadcasted_iota(jnp.int32, (rows, tc), 0)
        av, bv, dv = a_ref[...], b_ref[...], d_ref[...].astype(F32)
        ac = _conv3(av, _taps(wa_ref), row)
        bc = _conv3(bv, _taps(wb_ref), row)
        _, vjp = jax.vjp(lambda p, q: jax.nn.silu(p) * q, ac, bc)
        dac, dbc = vjp(dv)
        dxa, dwa = _conv3_bwd(av, _taps(wa_ref), dac, row)
        dxb, dwb = _conv3_bwd(bv, _taps(wb_ref), dbc, row)
        da_ref[...] = dxa.astype(da_ref.dtype)
        db_ref[...] = dxb.astype(db_ref.dtype)
        for k in range(3):
            dwa_ref[k:k + 1, :] = dwa[k]
            dwb_ref[k:k + 1, :] = dwb[k]

    col = pl.BlockSpec((rows, tc), lambda j: (0, j))
    wspec = pl.BlockSpec((3, tc), lambda j: (0, j))
    return pl.pallas_call(
        body,
        out_shape=[_sds((rows, f), BF16)] * 2 + [_sds((3, f), F32)] * 2,
        grid=(nb,),
        in_specs=[col, pl.BlockSpec((rows, tc), lambda j: (0, nb + j)), wspec,
                  pl.BlockSpec((3, tc), lambda j: (0, nb + j)), col],
        out_specs=[col, col, wspec, wspec],
        compiler_params=_cp(("parallel",)),
        name="ffn_act_bwd",
    )(up, up, w, w, dact)


def _attn_scores(q, kc, kp, slope, dilation, has_prev):
    scale = HEAD_DIM ** -0.5
    nt = (((1,), (1,)), ((), ()))
    s_c = lax.dot_general(q, kc, nt, preferred_element_type=F32) * scale
    s_p = lax.dot_general(q, kp, nt, preferred_element_type=F32) * scale
    qi = lax.broadcasted_iota(jnp.int32, (ATTN_BLOCK, ATTN_BLOCK), 0)
    kj = lax.broadcasted_iota(jnp.int32, (ATTN_BLOCK, ATTN_BLOCK), 1)
    dist_c = qi - kj
    dist_p = dist_c + ATTN_BLOCK
    s_c = jnp.where(dist_c >= 0, s_c - slope * (dist_c * dilation).astype(F32), NEG_INF)
    s_p = jnp.where((dist_p <= ATTN_BLOCK) & has_prev, s_p - slope * (dist_p * dilation).astype(F32), NEG_INF)
    return s_c, s_p


def _slab(seq, col0):
    assert col0 % LANE == 0
    return pl.BlockSpec((seq, LANE), lambda hh, r, s: (0, col0 // LANE + hh))


def _residue_rows(r, block, dilation):
    if dilation == 1:
        return pl.ds(pl.multiple_of(block * ATTN_BLOCK, ATTN_BLOCK), ATTN_BLOCK)
    return pl.ds(r + dilation * ATTN_BLOCK * block, ATTN_BLOCK, stride=dilation)


def _head_col(x, mask):
    return jnp.max(jnp.where(mask, x, -jnp.inf), axis=-1, keepdims=True)


def _attn_fwd(proj, offs, pattern, dilation, sw, slopes):
    seq, _ = proj.shape
    nb = seq // dilation // ATTN_BLOCK
    pairs = sw // LANE

    def body(q_ref, k_ref, v_ref, s0_ref, s1_ref, o_ref, lse_ref):
        r, i = pl.program_id(1), pl.program_id(2)
        cur, prev = _residue_rows(r, i, dilation), _residue_rows(r, jnp.maximum(i - 1, 0), dilation)
        first = lax.broadcasted_iota(jnp.int32, (ATTN_BLOCK, LANE), 1) < HEAD_DIM
        q2 = q_ref[cur, :]
        kc, kp = k_ref[cur, :].astype(BF16), k_ref[prev, :].astype(BF16)
        vc, vp = v_ref[cur, :].astype(BF16), v_ref[prev, :].astype(BF16)
        res = []
        for mask, sl_ref in ((first, s0_ref), (~first, s1_ref)):
            qh = jnp.where(mask, q2, 0.0).astype(BF16)
            s_c, s_p = _attn_scores(qh, kc, kp, sl_ref[:, :1], dilation, i > 0)
            mx = jnp.maximum(jnp.max(s_c, axis=-1, keepdims=True), jnp.max(s_p, axis=-1, keepdims=True))
            p_c, p_p = jnp.exp(s_c - mx), jnp.exp(s_p - mx)
            den = jnp.sum(p_c, axis=-1, keepdims=True) + jnp.sum(p_p, axis=-1, keepdims=True)
            o = (jnp.dot(p_c.astype(BF16), vc, preferred_element_type=F32)
                 + jnp.dot(p_p.astype(BF16), vp, preferred_element_type=F32))
            res.append((o / den, mx + jnp.log(den)))
        o_ref[cur, :] = jnp.where(first, res[0][0], res[1][0])
        lse_ref[cur, :] = jnp.where(first, res[0][1], res[1][1])

    slope = pl.BlockSpec((None, 1, LANE), lambda hh, r, s: (hh, 0, 0))
    return pl.pallas_call(
        body,
        out_shape=[_sds((seq, sw), F32)] * 2,
        grid=(pairs, dilation, nb),
        in_specs=[_slab(seq, offs[0] + pattern * sw), _slab(seq, offs[1] + pattern * sw),
                  _slab(seq, offs[2] + pattern * sw), slope, slope],
        out_specs=[_slab(seq, 0), _slab(seq, 0)],
        compiler_params=_cp(("parallel", "arbitrary", "arbitrary")),
        name=f"attn_fwd_d{dilation}",
    )(proj, proj, proj, *slopes)


def _attn_bwd(proj, do, lse, delta, offs, pattern, dilation, sw, slopes):
    seq, _ = proj.shape
    nb = seq // dilation // ATTN_BLOCK
    pairs = sw // LANE

    def body(q_ref, k_ref, v_ref, do_ref, lse_ref, dl_ref, s0_ref, s1_ref, dq_ref, dk_ref, dv_ref, ck_ref, cv_ref):
        r, step = pl.program_id(1), pl.program_id(2)
        i = nb - 1 - step
        cur, prev = _residue_rows(r, i, dilation), _residue_rows(r, jnp.maximum(i - 1, 0), dilation)
        scale = HEAD_DIM ** -0.5
        nt = (((1,), (1,)), ((), ()))
        first = lax.broadcasted_iota(jnp.int32, (ATTN_BLOCK, LANE), 1) < HEAD_DIM
        q2, do2, lse2, dl2 = q_ref[cur, :], do_ref[cur, :], lse_ref[cur, :], dl_ref[cur, :]
        kc, kp = k_ref[cur, :].astype(BF16), k_ref[prev, :].astype(BF16)
        vc, vp = v_ref[cur, :].astype(BF16), v_ref[prev, :].astype(BF16)

        @pl.when(step == 0)
        def _():
            ck_ref[...] = jnp.zeros_like(ck_ref)
            cv_ref[...] = jnp.zeros_like(cv_ref)

        dq, dk_c, dv_c, dk_p, dv_p = [], 0.0, 0.0, 0.0, 0.0
        for mask, sl_ref in ((first, s0_ref), (~first, s1_ref)):
            qh = jnp.where(mask, q2, 0.0).astype(BF16)
            doh = jnp.where(mask, do2, 0.0).astype(BF16)
            lse_col, dl_col = _head_col(lse2, mask), _head_col(dl2, mask)
            s_c, s_p = _attn_scores(qh, kc, kp, sl_ref[:, :1], dilation, i > 0)
            p_c, p_p = jnp.exp(s_c - lse_col), jnp.exp(s_p - lse_col)
            ds_c = p_c * (lax.dot_general(doh, vc, nt, preferred_element_type=F32) - dl_col)
            ds_p = p_p * (lax.dot_general(doh, vp, nt, preferred_element_type=F32) - dl_col)
            dq.append(jnp.dot(ds_c.astype(BF16), kc, preferred_element_type=F32)
                      + jnp.dot(ds_p.astype(BF16), kp, preferred_element_type=F32))
            dk_c = dk_c + jnp.dot(ds_c.T.astype(BF16), qh, preferred_element_type=F32)
            dv_c = dv_c + jnp.dot(p_c.T.astype(BF16), doh, preferred_element_type=F32)
            dk_p = dk_p + jnp.dot(ds_p.T.astype(BF16), qh, preferred_element_type=F32)
            dv_p = dv_p + jnp.dot(p_p.T.astype(BF16), doh, preferred_element_type=F32)
        dq_ref[cur, :] = jnp.where(first, dq[0], dq[1]) * scale
        dk_ref[cur, :] = dk_c * scale + ck_ref[...]
        dv_ref[cur, :] = dv_c + cv_ref[...]
        ck_ref[...] = dk_p * scale
        cv_ref[...] = dv_p

    slope = pl.BlockSpec((None, 1, LANE), lambda hh, r, s: (hh, 0, 0))
    tok = _slab(seq, 0)
    return pl.pallas_call(
        body,
        out_shape=[_sds((seq, sw), F32)] * 3,
        grid=(pairs, dilation, nb),
        in_specs=[_slab(seq, offs[0] + pattern * sw), _slab(seq, offs[1] + pattern * sw),
                  _slab(seq, offs[2] + pattern * sw), tok, tok, tok, slope, slope],
        out_specs=[tok, tok, tok],
        scratch_shapes=[pltpu.VMEM((ATTN_BLOCK, LANE), F32), pltpu.VMEM((ATTN_BLOCK, LANE), F32)],
        compiler_params=_cp(("parallel", "arbitrary", "arbitrary")),
        name=f"attn_bwd_d{dilation}",
    )(proj, proj, proj, do, lse, delta, *slopes)


def _attn_merge(outs, lses):
    rows, aw = outs[0].shape
    tm = _tile(rows, 256, SUBLANE)

    def body(o0, o1, o2, l0, l1, l2, o_ref, lse_ref):
        lv = [l0[...], l1[...], l2[...]]
        mx = jnp.maximum(jnp.maximum(lv[0], lv[1]), lv[2])
        w = [jnp.exp(t - mx) for t in lv]
        den = w[0] + w[1] + w[2]
        o_ref[...] = (w[0] * o0[...] + w[1] * o1[...] + w[2] * o2[...]) / den
        lse_ref[...] = mx + jnp.log(den)

    spec = pl.BlockSpec((tm, aw), lambda i: (i, 0))
    return pl.pallas_call(
        body,
        out_shape=[_sds((rows, aw), F32)] * 2,
        grid=(rows // tm,),
        in_specs=[spec] * 6,
        out_specs=[spec, spec],
        compiler_params=_cp(("parallel",)),
        name="attn_merge",
    )(*outs, *lses)


def _attn_delta(do, o, head_ones):
    rows, aw = do.shape
    tm = _tile(rows, 256, SUBLANE)

    def body(do_ref, o_ref, e_ref, d_ref):
        d_ref[...] = jnp.dot(do_ref[...] * o_ref[...], e_ref[...], preferred_element_type=F32,
                             precision=lax.Precision.HIGHEST)

    spec = pl.BlockSpec((tm, aw), lambda i: (i, 0))
    return pl.pallas_call(
        body,
        out_shape=_sds((rows, aw), F32),
        grid=(rows // tm,),
        in_specs=[spec, spec, pl.BlockSpec((aw, aw), lambda i: (0, 0))],
        out_specs=spec,
        compiler_params=_cp(("parallel",)),
        name="attn_delta",
    )(do, o, head_ones)


def _alibi_slopes(pattern, hp):
    n_heads = hp * len(DSWA_PATTERNS)
    s = np.array([2.0 ** (-8.0 * (pattern * hp + h + 1) / n_heads) for h in range(hp)], dtype=np.float32)
    return [jnp.asarray(np.broadcast_to(s[par::2, None, None], (hp // 2, 1, LANE)).copy()) for par in (0, 1)]


def _loss_fwd_bwd(y, target):
    rows, d = y.shape
    tm = _tile(rows, 256, SUBLANE)

    def body(y_ref, t_ref, dy_ref, l_ref):
        i = pl.program_id(0)
        err = y_ref[...] - t_ref[...]
        dy_ref[...] = err * (1.0 / d)
        part = 0.5 * jnp.sum(jnp.mean(err * err, axis=-1, keepdims=True), axis=0, keepdims=True)

        @pl.when(i == 0)
        def _():
            l_ref[...] = jnp.zeros_like(l_ref)

        l_ref[...] += jnp.broadcast_to(part, l_ref.shape)

    spec = pl.BlockSpec((tm, d), lambda i: (i, 0))
    dy, loss = pl.pallas_call(
        body,
        out_shape=[_sds((rows, d), F32), _sds((SUBLANE, LANE), F32)],
        grid=(rows // tm,),
        in_specs=[spec, spec],
        out_specs=[spec, pl.BlockSpec((SUBLANE, LANE), lambda i: (0, 0))],
        compiler_params=_cp(("arbitrary",)),
        name="loss",
    )(y, target)
    return dy, loss[0, 0]


def _adam_math(w, g, m, v):
    m = ADAM_B1 * m + (1.0 - ADAM_B1) * g
    v = ADAM_B2 * v + (1.0 - ADAM_B2) * jnp.square(g)
    m_hat = m / (1.0 - ADAM_B1 ** ADAM_STEP)
    v_hat = v / (1.0 - ADAM_B2 ** ADAM_STEP)
    delta = -ADAM_LR * (m_hat / (jnp.sqrt(v_hat) + ADAM_EPS) + ADAM_WD * w)
    return delta, m, v


def _as2d(a):
    if a.ndim == 1:
        return a.reshape(1, -1)
    return a.reshape(-1, a.shape[-1])


def _adam(w, g, m, v, name):
    shape = w.shape
    w2, g2, m2, v2 = _as2d(w), _as2d(g), _as2d(m), _as2d(v)
    r, c = w2.shape
    tr = _tile(r, 512, SUBLANE)
    tc = _tile(c, 1024)

    def body(w_ref, g_ref, m_ref, v_ref, d_ref, mo_ref, vo_ref):
        delta, mn, vn = _adam_math(w_ref[...], g_ref[...], m_ref[...], v_ref[...])
        d_ref[...] = delta
        mo_ref[...] = mn
        vo_ref[...] = vn

    spec = pl.BlockSpec((tr, tc), lambda i, j: (i, j))
    outs = pl.pallas_call(
        body,
        out_shape=[_sds((r, c), F32)] * 3,
        grid=(r // tr, c // tc),
        in_specs=[spec] * 4,
        out_specs=[spec] * 3,
        compiler_params=_cp(("parallel", "parallel")),
        name=name,
    )(w2, g2, m2, v2)
    return [o.reshape(shape) for o in outs]


def _wmod_grad_adam(c_t, dmod, w, m, v):
    nl, d, cols = w.shape
    nex = c_t.shape[1]
    tr = _tile(d, 256, SUBLANE)
    tc = _tile(cols, 1024)

    def body(c_ref, dm_ref, w_ref, m_ref, v_ref, g_ref, d_ref, mo_ref, vo_ref):
        cond = jax.nn.silu(c_ref[...]).astype(BF16)
        g = jnp.dot(cond, dm_ref[...].astype(BF16), preferred_element_type=F32)
        delta, mn, vn = _adam_math(w_ref[...], g, m_ref[...], v_ref[...])
        g_ref[...] = g
        d_ref[...] = delta
        mo_ref[...] = mn
        vo_ref[...] = vn

    spec = pl.BlockSpec((None, tr, tc), lambda l, i, j: (l, i, j))
    return pl.pallas_call(
        body,
        out_shape=[_sds((nl, d, cols), F32)] * 4,
        grid=(nl, d // tr, cols // tc),
        in_specs=[pl.BlockSpec((tr, nex), lambda l, i, j: (i, 0)),
                  pl.BlockSpec((None, nex, tc), lambda l, i, j: (l, 0, j)), spec, spec, spec],
        out_specs=[spec] * 4,
        compiler_params=_cp(("parallel", "parallel", "parallel")),
        name="wmod_grad_adam",
    )(c_t, dmod, w, m, v)


def _my_pos():
    return lax.axis_index("x"), lax.axis_index("y"), lax.axis_index("c")


def _ag8(x4, select_half, name):
    a, s, r, c = x4.shape
    assert s == (2 if select_half else 1)

    def body(x_ref, out_ref, send_sems, recv_sems, local_sem):
        x, y, cc = _my_pos()
        me, sibling = (x, y, cc), (x, y, 1 - cc)
        chips = [(1 - x, y), (x, 1 - y), (1 - x, 1 - y)]
        src_mine = x_ref.at[:, pl.ds(cc if select_half else 0, 1)]

        def blk(px, py, pc):
            return out_ref.at[:, pl.ds(4 * px + 2 * py + pc, 1)]

        def copy(k, block, to, src=None):
            return pltpu.make_async_remote_copy(
                src_ref=blk(*block) if src is None else src, dst_ref=blk(*block),
                send_sem=send_sems.at[k], recv_sem=recv_sems.at[k], device_id=to, device_id_type=MESH)

        mine = pltpu.make_async_copy(src_mine, blk(*me), local_sem)
        mine.start()
        first = [copy(0, me, sibling, src=src_mine)]
        first += [copy(1 + j, me, (*chip, cc), src=src_mine) for j, chip in enumerate(chips)]
        for cp in first:
            cp.start()
        passed = [copy(4 + j, (*chip, cc), sibling) for j, chip in enumerate(chips)]
        for j, chip in enumerate(chips):
            copy(1 + j, (*chip, cc), me).wait_recv()
            passed[j].start()
        copy(0, sibling, me).wait_recv()
        for j, chip in enumerate(chips):
            copy(4 + j, (*chip, 1 - cc), me).wait_recv()
        for cp in first + passed:
            cp.wait_send()
        mine.wait()

    return pl.pallas_call(
        body,
        out_shape=_sds((a, N_DEV, r, c), x4.dtype),
        in_specs=[ANY],
        out_specs=ANY,
        scratch_shapes=[pltpu.SemaphoreType.DMA((7,)), pltpu.SemaphoreType.DMA((7,)), pltpu.SemaphoreType.DMA],
        name=name,
    )(x4)


def _chip_of(x, y, k):
    return (1 - x if k & 2 else x), (1 - y if k & 1 else y)


HBM = pl.BlockSpec(memory_space=pltpu.HBM)
SEM = pl.BlockSpec(memory_space=pltpu.SEMAPHORE)
DATAFLOW = pltpu.SideEffectType.DATAFLOW_SIDE_EFFECTING


def _own_block(k, chip, cc):
    del k
    return 2 * chip + cc


def _distance_slot(k, chip, cc):
    del chip, cc
    return k - 1


def _ici_copies(srcs, dsts, slot, send_sems, recv_sems):
    x, y, cc = _my_pos()
    chip = 2 * x + y
    copies = []
    for n, (s_ref, d_ref) in enumerate(zip(srcs, dsts)):
        for k in (1, 2, 3):
            px, py = _chip_of(x, y, k)
            at = slot(k, chip, cc)
            copies.append(pltpu.make_async_remote_copy(
                src_ref=s_ref.at[pl.ds(at, 1)], dst_ref=d_ref.at[pl.ds(at, 1)],
                send_sem=send_sems.at[3 * n + k - 1], recv_sem=recv_sems.at[3 * n + k - 1],
                device_id=(px, py, cc), device_id_type=MESH))
    return copies


def _ici_start(srcs, lands, slot, name):
    n = len(srcs)
    arrays = list(srcs) + ([] if lands is None else list(lands))
    na = len(arrays)

    def body(*refs):
        s_refs = refs[:n]
        d_refs = s_refs if lands is None else refs[n:na]
        send_sems, recv_sems, token = refs[na], refs[na + 1], refs[-1]
        for cp in _ici_copies(s_refs, d_refs, slot, send_sems, recv_sems):
            cp.start()
        token[...] = jnp.zeros_like(token)

    outs = pl.pallas_call(
        body,
        name=name,
        out_shape=(pltpu.SemaphoreType.DMA((3 * n,)), pltpu.SemaphoreType.DMA((3 * n,)),
                   *[pltpu.HBM(a.shape, a.dtype) for a in arrays], _sds((SUBLANE, LANE), F32)),
        in_specs=[HBM] * na,
        out_specs=(SEM, SEM, *([HBM] * na), pl.BlockSpec(memory_space=pltpu.VMEM)),
        input_output_aliases={i: 2 + i for i in range(na)},
        compiler_params=pltpu.CompilerParams(has_side_effects=DATAFLOW),
    )(*[pltpu.with_memory_space_constraint(a, pltpu.HBM) for a in arrays])
    return outs[0], outs[1], list(outs[2:2 + na]), outs[-1]


def _ici_wait(send_sems, recv_sems, arrays, n, shared, slot, after, name):
    na = len(arrays)

    def body(*refs):
        s_refs = refs[:n]
        d_refs = s_refs if shared else refs[n:na]
        for cp in _ici_copies(s_refs, d_refs, slot, refs[na], refs[na + 1]):
            cp.wait_send()
            cp.wait_recv()

    outs = pl.pallas_call(
        body,
        name=name,
        out_shape=tuple(pltpu.HBM(a.shape, a.dtype) for a in arrays),
        in_specs=[HBM] * na + [SEM, SEM, ANY],
        out_specs=tuple([HBM] * na),
        input_output_aliases={i: i for i in range(na)},
        compiler_params=pltpu.CompilerParams(has_side_effects=DATAFLOW),
    )(*arrays, send_sems, recv_sems, after)
    return list(outs)


def _my_chip():
    return 2 * lax.axis_index("x") + lax.axis_index("y")


def _cast_own(w, layer, name):
    _, r, cols = w.shape
    tr = _tile(r, 512, 2 * SUBLANE)
    tc = _tile(cols, 1024)

    def body(w_ref, o_ref):
        o_ref[...] = w_ref[...].astype(o_ref.dtype)

    return pl.pallas_call(
        body,
        out_shape=_sds((N_CHIP, r, cols), BF16),
        grid=(r // tr, cols // tc),
        in_specs=[pl.BlockSpec((None, tr, tc), lambda i, j: (layer, i, j))],
        out_specs=pl.BlockSpec((None, tr, tc), lambda i, j: (_my_chip(), i, j)),
        compiler_params=_cp(("parallel", "parallel")),
        name=name,
    )(w)


def _forward_halves(bufs, name):
    n = len(bufs)

    def body(*refs):
        ins, outs = refs[:n], refs[n:2 * n]
        send_sems, recv_sems = refs[2 * n], refs[2 * n + 1]
        x, y, cc = _my_pos()
        chip = 2 * x + y
        copies = []
        for i in range(n):
            for k in (1, 2, 3):
                at = 2 * (chip ^ k) + cc
                copies.append(pltpu.make_async_remote_copy(
                    src_ref=ins[i].at[pl.ds(at, 1)], dst_ref=outs[i].at[pl.ds(at, 1)],
                    send_sem=send_sems.at[3 * i + k - 1], recv_sem=recv_sems.at[3 * i + k - 1],
                    device_id=(x, y, 1 - cc), device_id_type=MESH))
        for cp in copies:
            cp.start()
        for cp in copies:
            cp.wait()

    return pl.pallas_call(
        body,
        out_shape=[_sds(b.shape, b.dtype) for b in bufs],
        in_specs=[ANY] * n,
        out_specs=[ANY] * n,
        scratch_shapes=[pltpu.SemaphoreType.DMA((3 * n,)), pltpu.SemaphoreType.DMA((3 * n,))],
        input_output_aliases={i: i for i in range(n)},
        name=name,
    )(*bufs)


def _rs_sibling(g8s, name):
    n = len(g8s)
    g4s = [g.reshape(N_CHIP, 2, g.shape[1], g.shape[2]) for g in g8s]

    def body(*refs):
        ins, outs = refs[:n], refs[n:2 * n]
        send_sems, recv_sems = refs[2 * n], refs[2 * n + 1]
        x, y, cc = _my_pos()
        copies = [pltpu.make_async_remote_copy(
            src_ref=ins[i].at[:, pl.ds(1 - cc, 1)], dst_ref=outs[i], send_sem=send_sems.at[i],
            recv_sem=recv_sems.at[i], device_id=(x, y, 1 - cc), device_id_type=MESH) for i in range(n)]
        for cp in copies:
            cp.start()
        for cp in copies:
            cp.wait()

    return pl.pallas_call(
        body,
        out_shape=[_sds((N_CHIP, 1, g.shape[2], g.shape[3]), g.dtype) for g in g4s],
        in_specs=[ANY] * n,
        out_specs=[ANY] * n,
        scratch_shapes=[pltpu.SemaphoreType.DMA((n,)), pltpu.SemaphoreType.DMA((n,))],
        name=name,
    )(*g4s)


def _share_halves(halves, name):
    def body(in_ref, out_ref, send_sem, recv_sem):
        x, y, cc = _my_pos()
        cp = pltpu.make_async_remote_copy(
            src_ref=in_ref.at[:, pl.ds(cc, 1)], dst_ref=out_ref.at[:, pl.ds(cc, 1)], send_sem=send_sem,
            recv_sem=recv_sem, device_id=(x, y, 1 - cc), device_id_type=MESH)
        cp.start()
        cp.wait()

    return pl.pallas_call(
        body,
        out_shape=_sds(halves.shape, halves.dtype),
        in_specs=[ANY],
        out_specs=ANY,
        scratch_shapes=[pltpu.SemaphoreType.DMA, pltpu.SemaphoreType.DMA],
        input_output_aliases={0: 0},
        name=name,
    )(halves)


def _rs_add_remote(g8, recv_a, name):
    _, r, c = g8.shape
    ra = recv_a.reshape(N_CHIP, r, c)
    tr = _tile(r, 512, SUBLANE)
    tc = _tile(c, 1024)

    def body(g_ref, r_ref, o_ref):
        o_ref[...] = (g_ref[...] + r_ref[...]).astype(o_ref.dtype)

    return pl.pallas_call(
        body,
        out_shape=_sds((3, r, c), BF16),
        grid=(3, r // tr, c // tc),
        in_specs=[pl.BlockSpec((None, tr, tc),
                               lambda k, i, j: (2 * (_my_chip() ^ (k + 1)) + lax.axis_index("c"), i, j)),
                  pl.BlockSpec((None, tr, tc), lambda k, i, j: (_my_chip() ^ (k + 1), i, j))],
        out_specs=pl.BlockSpec((None, tr, tc), lambda k, i, j: (k, i, j)),
        compiler_params=_cp(("parallel",) * 3),
        name=name,
    )(g8, ra)


def _rs_add_final(g8, recv_a, recv_b, out_buf, layer, name):
    _, r, c = g8.shape
    ra = recv_a.reshape(N_CHIP, r, c)
    tr = _tile(r, 512, SUBLANE)
    tc = _tile(c, 1024)

    def body(g_ref, r_ref, b0_ref, b1_ref, b2_ref, buf_ref, o_ref):
        del buf_ref
        o_ref[...] = (((g_ref[...] + r_ref[...]) + b0_ref[...].astype(F32)) + b1_ref[...].astype(F32)
                      ) + b2_ref[...].astype(F32)

    def bspec(k):
        return pl.BlockSpec((None, tr, tc), functools.partial(lambda i, j, k: (k, i, j), k=k))

    return pl.pallas_call(
        body,
        out_shape=_sds(out_buf.shape, F32),
        grid=(r // tr, c // tc),
        in_specs=[pl.BlockSpec((None, tr, tc), lambda i, j: (2 * _my_chip() + lax.axis_index("c"), i, j)),
                  pl.BlockSpec((None, tr, tc), lambda i, j: (_my_chip(), i, j)),
                  bspec(0), bspec(1), bspec(2), ANY],
        out_specs=pl.BlockSpec((None, None, tr, tc), lambda i, j: (layer, lax.axis_index("c"), i, j)),
        input_output_aliases={5: 0},
        compiler_params=_cp(("parallel",) * 2),
        name=name,
    )(g8, ra, recv_b, recv_b, recv_b, out_buf)


def _sum8(x8, name):
    _, r, c = x8.shape
    tr = _tile(r, 256, SUBLANE)

    def body(x_ref, o_ref):
        acc = x_ref[0]
        for b in range(1, N_DEV):
            acc = acc + x_ref[b]
        o_ref[...] = acc

    return pl.pallas_call(
        body,
        out_shape=_sds((r, c), F32),
        grid=(r // tr,),
        in_specs=[pl.BlockSpec((N_DEV, tr, c), lambda i: (0, i, 0))],
        out_specs=pl.BlockSpec((tr, c), lambda i: (i, 0)),
        compiler_params=_cp(("parallel",)),
        name=name,
    )(x8)


def _block_diag(t):
    g, p, q = t.shape
    eye = jnp.eye(g, dtype=t.dtype)
    return (t[:, :, None, :] * eye[:, None, :, None]).reshape(g * p, g * q)


def _diag_blocks(mat, g):
    p, q = mat.shape[0] // g, mat.shape[1] // g
    eye = jnp.eye(g, dtype=mat.dtype)
    return jnp.sum(mat.reshape(g, p, g, q) * eye[:, None, :, None], axis=2)


def _interleave(re, im, c):
    lead = re.shape[:-1]
    gn = re.shape[-1]
    return jnp.stack([re.reshape(*lead, gn // c, c), im.reshape(*lead, gn // c, c)], axis=-2).reshape(*lead, 2 * gn)


def _deinterleave(cat, c):
    lead = cat.shape[:-1]
    gn = cat.shape[-1] // 2
    t = cat.reshape(*lead, gn // c, 2, c)
    return t[..., 0, :].reshape(*lead, gn), t[..., 1, :].reshape(*lead, gn)


def kernel(x, c, w_mod, b_mod, g_pre_mix, g_post_mix, g_pre_ffn, g_post_ffn, w_in, ssm_log_dt, ssm_a_re, ssm_a_im, ssm_b_re, ssm_b_im, ssm_c_re, ssm_c_im, ssm_d, w_glu, b_glu, conv_mix_w, w_ssm_out, w_attn_out, w_conv_out, b_gate, w_o, w_up, ffn_conv_w, w_down, loss_target, m_w_mod, m_b_mod, m_g_pre_mix, m_g_post_mix, m_g_pre_ffn, m_g_post_ffn, m_w_in, m_ssm_log_dt, m_ssm_a_re, m_ssm_a_im, m_ssm_b_re, m_ssm_b_im, m_ssm_c_re, m_ssm_c_im, m_ssm_d, m_w_glu, m_b_glu, m_conv_mix_w, m_w_ssm_out, m_w_attn_out, m_w_conv_out, m_b_gate, m_w_o, m_w_up, m_ffn_conv_w, m_w_down, v_w_mod, v_b_mod, v_g_pre_mix, v_g_post_mix, v_g_pre_ffn, v_g_post_ffn, v_w_in, v_ssm_log_dt, v_ssm_a_re, v_ssm_a_im, v_ssm_b_re, v_ssm_b_im, v_ssm_c_re, v_ssm_c_im, v_ssm_d, v_w_glu, v_b_glu, v_conv_mix_w, v_w_ssm_out, v_w_attn_out, v_w_conv_out, v_b_gate, v_w_o, v_w_up, v_ffn_conv_w, v_w_down):
    weights = dict(w_mod=w_mod, b_mod=b_mod, g_pre_mix=g_pre_mix, g_post_mix=g_post_mix, g_pre_ffn=g_pre_ffn, g_post_ffn=g_post_ffn, w_in=w_in, ssm_log_dt=ssm_log_dt, ssm_a_re=ssm_a_re, ssm_a_im=ssm_a_im, ssm_b_re=ssm_b_re, ssm_b_im=ssm_b_im, ssm_c_re=ssm_c_re, ssm_c_im=ssm_c_im, ssm_d=ssm_d, w_glu=w_glu, b_glu=b_glu, conv_mix_w=conv_mix_w, w_ssm_out=w_ssm_out, w_attn_out=w_attn_out, w_conv_out=w_conv_out, b_gate=b_gate, w_o=w_o, w_up=w_up, ffn_conv_w=ffn_conv_w, w_down=w_down)
    mom_m = dict(w_mod=m_w_mod, b_mod=m_b_mod, g_pre_mix=m_g_pre_mix, g_post_mix=m_g_post_mix, g_pre_ffn=m_g_pre_ffn, g_post_ffn=m_g_post_ffn, w_in=m_w_in, ssm_log_dt=m_ssm_log_dt, ssm_a_re=m_ssm_a_re, ssm_a_im=m_ssm_a_im, ssm_b_re=m_ssm_b_re, ssm_b_im=m_ssm_b_im, ssm_c_re=m_ssm_c_re, ssm_c_im=m_ssm_c_im, ssm_d=m_ssm_d, w_glu=m_w_glu, b_glu=m_b_glu, conv_mix_w=m_conv_mix_w, w_ssm_out=m_w_ssm_out, w_attn_out=m_w_attn_out, w_conv_out=m_w_conv_out, b_gate=m_b_gate, w_o=m_w_o, w_up=m_w_up, ffn_conv_w=m_ffn_conv_w, w_down=m_w_down)
    mom_v = dict(w_mod=v_w_mod, b_mod=v_b_mod, g_pre_mix=v_g_pre_mix, g_post_mix=v_g_post_mix, g_pre_ffn=v_g_pre_ffn, g_post_ffn=v_g_post_ffn, w_in=v_w_in, ssm_log_dt=v_ssm_log_dt, ssm_a_re=v_ssm_a_re, ssm_a_im=v_ssm_a_im, ssm_b_re=v_ssm_b_re, ssm_b_im=v_ssm_b_im, ssm_c_re=v_ssm_c_re, ssm_c_im=v_ssm_c_im, ssm_d=v_ssm_d, w_glu=v_w_glu, b_glu=v_b_glu, conv_mix_w=v_conv_mix_w, w_ssm_out=v_w_ssm_out, w_attn_out=v_w_attn_out, w_conv_out=v_w_conv_out, b_gate=v_b_gate, w_o=v_w_o, w_up=v_w_up, ffn_conv_w=v_ffn_conv_w, w_down=v_w_down)
    names = list(weights)

    nl = w_in.shape[0]
    seq, d = x.shape[1], x.shape[2]
    sw = d // 4
    groups = sw // SSM_GROUP
    gn = groups * SSM_STATE
    hp = sw // HEAD_DIM
    qw = 3 * sw
    off_q, off_k, off_v = sw, sw + qw, sw + 2 * qw
    off_conv = sw + 3 * qw
    off_gate = off_conv + 3 * sw
    n_in = off_gate + 3 * d
    f = w_down.shape[1] * N_CHIP
    scan_c = _scan_cols(gn)
    assert seq % (ATTN_BLOCK * DSWA_PATTERNS[-1][1]) == 0 and all(w // dl == ATTN_BLOCK for w, dl in DSWA_PATTERNS)

    px, py, pc = _my_pos()
    chip = 2 * px + py
    dev = 2 * chip + pc

    x2 = x.reshape(seq, d)
    target2 = loss_target.reshape(seq, d)

    mix_keys = ("w_in", "w_glu", "w_ssm_out", "w_attn_out", "w_conv_out", "w_o")
    ffn_keys = ("w_up", "w_down")
    col_sharded = ("w_in", "w_ssm_out", "w_attn_out", "w_conv_out", "w_up")
    n_stages = 2 * nl

    def stage_keys(stage):
        return ffn_keys if stage % 2 else mix_keys

    def blocked(k, buf8):
        r, cols = weights[k].shape[1:]
        return buf8.reshape(1, N_CHIP, r, cols) if k in col_sharded else buf8.reshape(1, 1, N_CHIP * r, cols)

    def begin_gather(stage):
        keys = stage_keys(stage)
        bufs = []
        for k in keys:
            r, cols = weights[k].shape[1:]
            bufs.append(_cast_own(weights[k], stage // 2, "cast_own").reshape(N_DEV, r // 2, cols))
        return _ici_start(bufs, None, _own_block, f"gather_start_{stage}")

    def end_gather(stage, pending, after):
        send_sems, recv_sems, bufs, _ = pending
        bufs = _ici_wait(send_sems, recv_sems, bufs, len(bufs), True, _own_block, after, f"gather_wait_{stage}")
        bufs = _forward_halves(bufs, "gather_forward")
        return {k: blocked(k, b) for k, b in zip(stage_keys(stage), bufs)}

    c_all = _ag8(c.reshape(1, 1, 1, d), False, "ag_cond").reshape(N_DEV, d)
    c_pad = jnp.concatenate([c_all, jnp.zeros((SUBLANE, d), F32)], axis=0)
    mcols = w_mod.shape[2]
    w_mod4 = w_mod.reshape(nl, 1, d, mcols)
    mod_loc = jnp.stack([_mm_nn(c_pad, w_mod4, l, name="mod_fwd", a_fn=jax.nn.silu) for l in range(nl)])
    mod_all = _ag8(mod_loc.reshape(nl, 1, 2 * SUBLANE, mcols), False, "ag_mod")
    mod_rows = lax.dynamic_slice_in_dim(mod_all[:, 0::2], dev, 1, axis=2)
    mod = mod_rows.reshape(nl, N_CHIP * mcols) + b_mod
    mods = mod.reshape(nl, 6, 1, d)

    taps = jnp.concatenate([conv_mix_w.reshape(-1), ffn_conv_w.reshape(-1)])
    tap_w = 8 * LANE
    tap_r = -(-taps.shape[0] // (tap_w * SUBLANE)) * SUBLANE
    taps = jnp.concatenate([taps, jnp.zeros((tap_r * tap_w - taps.shape[0],), F32)])
    taps_all = _ag8(taps.reshape(1, 1, tap_r, tap_w), False, "ag_small_weight")[0, 0::2].reshape(N_CHIP, -1)

    def whole(w, at):
        _, r, cols = w.shape
        got = taps_all[:, at:at + w.size].reshape(N_CHIP, nl, r, cols)
        return got.transpose(1, 2, 0, 3).reshape(nl, r, N_CHIP * cols)

    conv_w_full = whole(conv_mix_w, 0)
    ffn_w_full = whole(ffn_conv_w, conv_mix_w.size)
    wfs = [dict() for _ in range(nl)]
    pending = begin_gather(0)
    wfs[0].update(end_gather(0, pending, mods))
    pending = begin_gather(1)

    head_ones = jnp.asarray(np.kron(np.eye(hp, dtype=np.float32), np.ones((HEAD_DIM, HEAD_DIM), np.float32)))
    slopes = [_alibi_slopes(p, hp) for p in range(len(DSWA_PATTERNS))]
    qkv_offs = (off_q, off_k, off_v)

    def row(v):
        return v.reshape(1, -1)

    saved = []
    xl = x2
    for l in range(nl):
        sh1, sc1, gt1, sh2, sc2, gt2 = [mods[l, q] for q in range(6)]
        s = dict(x_in=xl)
        wf = wfs[l]
        (h1,) = _ew_fwd(_fn_norm_mod, [(xl, 0, d)], [row(g_pre_mix[l]), sc1, sh1 + pending[3][0, 0]], [BF16],
                        name="norm_mod_fwd", width=d)
        proj = _mm_nn(h1, wf["w_in"], 0, name="w_in_fwd")
        br_t = jnp.transpose(ssm_b_re[l], (2, 0, 1))
        bi_t = jnp.transpose(ssm_b_im[l], (2, 0, 1))
        disc_in = (ssm_log_dt[l].reshape(groups, 1), ssm_a_re[l], ssm_a_im[l], br_t, bi_t)
        lr, li, bbr_t, bbi_t = _ssm_disc_fwd(*disc_in)
        lam = _interleave(lr.reshape(1, gn), li.reshape(1, gn), scan_c)
        bcat = _interleave(_block_diag(jnp.transpose(bbr_t, (1, 0, 2))), _block_diag(jnp.transpose(bbi_t, (1, 0, 2))),
                           scan_c).astype(BF16).reshape(1, 1, sw, 2 * gn)
        cre = _block_diag(jnp.transpose(ssm_c_re[l], (0, 2, 1)))
        cim = _block_diag(jnp.transpose(ssm_c_im[l], (0, 2, 1)))
        ccat = jnp.transpose(_interleave(cre.T, -cim.T, scan_c)).astype(BF16).reshape(1, 1, 2 * gn, sw)
        xcat = _mm_nn(proj, bcat, 0, name="ssm_b_fwd", k_dim=sw)
        hcat = _ssm_scan_fwd(xcat, lam)
        y_ssm_pre = _mm_nn(hcat, ccat, 0, name="ssm_c_fwd")
        (gact,) = _ew_fwd(_fn_gelu, [(y_ssm_pre, 0, sw), (proj, 0, sw)], [row(ssm_d[l])], [F32], name="gelu_fwd", width=sw,
                          tw=_tile(sw, 512))
        z = _mm_nn(gact, wf["w_glu"], 0, name="w_glu_fwd")
        (s_ssm,) = _ew_fwd(_fn_glu, [(gact, 0, sw), (z, 0, sw)], [row(b_glu[l])], [BF16], name="glu_fwd", width=sw,
                           tw=_tile(sw, 512))
        y_ssm = _mm_nn(s_ssm, wf["w_ssm_out"], 0, name="w_branch_out_fwd")
        outs, lses = [], []
        for p, (_, dl) in enumerate(DSWA_PATTERNS):
            o_p, lse_p = _attn_fwd(proj, qkv_offs, p, dl, sw, slopes[p])
            outs.append(o_p)
            lses.append(lse_p)
        o_attn, lse_attn = _attn_merge(outs, lses)
        y_attn = _mm_nn(o_attn, wf["w_attn_out"], 0, name="w_branch_out_fwd")
        cv = _gconv_fwd(proj, off_conv, sw, conv_w_full[l])
        y_conv = _mm_nn(cv, wf["w_conv_out"], 0, name="w_branch_out_fwd")
        bg = b_gate[l].reshape(3, 1, d)
        gate_xs = [(proj, off_gate + q * d, d) for q in range(3)] + [(y_ssm, 0, d), (y_attn, 0, d), (y_conv, 0, d)]
        (merged,) = _ew_fwd(_fn_gates, gate_xs, [bg[0], bg[1], bg[2]], [BF16], name="gates_fwd", width=d,
                            tw=_tile(sw, 512))
        y_mix = _mm_nn(merged, wf["w_o"], 0, name="w_o_fwd")
        (x_mid,) = _ew_fwd(_fn_residual, [(xl, 0, d), (y_mix, 0, d)], [gt1, row(g_post_mix[l])], [F32], name="residual_fwd",
                           width=d)
        wf.update(end_gather(2 * l + 1, pending, x_mid))
        pending = begin_gather(2 * l + 2) if l + 1 < nl else None
        sh2_t = sh2 if pending is None else sh2 + pending[3][0, 0]
        (h2,) = _ew_fwd(_fn_norm_mod, [(x_mid, 0, d)], [row(g_pre_ffn[l]), sc2, sh2_t], [BF16], name="norm_mod_fwd",
                        width=d)
        up = _mm_nn(h2, wf["w_up"], 0, name="w_up_fwd")
        act = _ffn_act_fwd(up, ffn_w_full[l])
        y_ffn = _mm_nn(act, wf["w_down"], 0, name="w_down_fwd")
        (x_out,) = _ew_fwd(_fn_residual, [(x_mid, 0, d), (y_ffn, 0, d)], [gt2, row(g_post_ffn[l])], [F32],
                           name="residual_fwd", width=d)
        if pending is not None:
            wfs[l + 1].update(end_gather(2 * l + 2, pending, x_out))
            pending = begin_gather(2 * l + 3)
        s.update(h1=h1, proj=proj, disc_in=disc_in, lam=lam, bcat=bcat, ccat=ccat, hcat=hcat, y_ssm_pre=y_ssm_pre,
                 gact=gact, z=z, s_ssm=s_ssm, y_ssm=y_ssm, o_attn=o_attn, lse_attn=lse_attn, y_attn=y_attn,
                 cv=cv, y_conv=y_conv, merged=merged, y_mix=y_mix, x_mid=x_mid, h2=h2, up=up, act=act, y_ffn=y_ffn)
        saved.append(s)
        xl = x_out

    dxl, loss_local = _loss_fwd_bwd(xl, target2)
    loss = lax.psum(loss_local, ("x", "y", "c"))

    gfin = {k: lax.empty((nl, 2, weights[k].shape[1] // 2, weights[k].shape[2]), F32) for k in mix_keys + ffn_keys}

    def begin_rs(stage, grads_4d):
        keys = stage_keys(stage)
        g8s = [grads_4d[k].reshape(N_DEV, weights[k].shape[1] // 2, weights[k].shape[2]) for k in keys]
        recv_a = _rs_sibling(g8s, "rs_sibling")
        s_rem = [_rs_add_remote(g, ra, "rs_add_remote") for g, ra in zip(g8s, recv_a)]
        lands = [lax.empty(t.shape, BF16) for t in s_rem]
        return stage, g8s, recv_a, _ici_start(s_rem, lands, _distance_slot, f"rs_start_{stage}")

    def end_rs(pending_rs, after):
        stage, g8s, recv_a, (send_sems, recv_sems, arrays, _) = pending_rs
        n = len(g8s)
        arrays = _ici_wait(send_sems, recv_sems, arrays, n, False, _distance_slot, after, f"rs_wait_{stage}")
        for k, g8, ra, rb in zip(stage_keys(stage), g8s, recv_a, arrays[n:]):
            gfin[k] = _rs_add_final(g8, ra, rb, gfin[k], stage // 2, "rs_add_final")

    def after_rs_start(v, pending_rs):
        return v if pending_rs is None else v + pending_rs[3][3][0, 0]

    def grad_buf(k):
        return lax.empty(wf[k].shape, F32)

    pending_rs = None
    small = {k: [None] * nl for k in ("g_pre_mix", "g_post_mix", "g_pre_ffn", "g_post_ffn", "ssm_log_dt", "ssm_a_re",
                                      "ssm_a_im", "ssm_b_re", "ssm_b_im", "ssm_c_re", "ssm_c_im", "ssm_d", "b_glu",
                                      "conv_mix_w", "b_gate", "ffn_conv_w", "dmod")}
    for l in reversed(range(nl)):
        s = saved[l]
        sh1, sc1, gt1, sh2, sc2, gt2 = [mods[l, q] for q in range(6)]
        proj = s["proj"]
        wf = wfs[l]
        gw = {}
        (dy_ffn,), (dgt2, dg_post_ffn) = _ew_bwd(
            _fn_residual, [(s["x_mid"], 0, d), (s["y_ffn"], 0, d)], [after_rs_start(gt2, pending_rs), row(g_post_ffn[l])],
            [[dxl]], [None, BF16], name="residual_bwd", width=d)
        dact = _mm_nt(dy_ffn, wf["w_down"], 0, name="w_down_bwd_x", out_dtype=BF16)
        gw["w_down"] = _mm_tn(s["act"], dy_ffn, grad_buf("w_down"), 0, name="w_down_bwd_w")
        dup_a, dup_b, dwa, dwb = _ffn_act_bwd(s["up"], ffn_w_full[l], dact)
        dup = jnp.concatenate([dup_a, dup_b], axis=1)
        small["ffn_conv_w"][l] = jnp.concatenate([dwa, dwb], axis=1)
        dh2 = _mm_nt(dup, wf["w_up"], 0, name="w_up_bwd_x")
        gw["w_up"] = _mm_tn(s["h2"], dup, grad_buf("w_up"), 0, name="w_up_bwd_w")
        (dx_mid,), (dg_pre_ffn, dsc2, dsh2) = _ew_bwd(
            _fn_norm_mod, [(s["x_mid"], 0, d)], [row(g_pre_ffn[l]), sc2, sh2], [[dh2]], [F32], name="norm_mod_bwd", width=d,
            dx_add={0: dxl})
        if pending_rs is not None:
            end_rs(pending_rs, dx_mid)
        pending_rs = begin_rs(2 * l + 1, gw)
        (dy_mix,), (dgt1, dg_post_mix) = _ew_bwd(
            _fn_residual, [(s["x_in"], 0, d), (s["y_mix"], 0, d)], [after_rs_start(gt1, pending_rs), row(g_post_mix[l])],
            [[dx_mid]], [None, BF16], name="residual_bwd", width=d)
        dmerged = _mm_nt(dy_mix, wf["w_o"], 0, name="w_o_bwd_x")
        gw["w_o"] = _mm_tn(s["merged"], dy_mix, grad_buf("w_o"), 0, name="w_o_bwd_w")
        bg = b_gate[l].reshape(3, 1, d)
        gate_xs = [(proj, off_gate + q * d, d) for q in range(3)] + [(s["y_ssm"], 0, d), (s["y_attn"], 0, d),
                                                                     (s["y_conv"], 0, d)]
        (dp0, dp1, dp2, dy_ssm, dy_attn, dy_conv), dbg = _ew_bwd(
            _fn_gates, gate_xs, [bg[0], bg[1], bg[2]], [[dmerged]], [BF16] * 6, name="gates_bwd", width=d,
            tw=_tile(sw, 512))
        small["b_gate"][l] = jnp.concatenate(dbg, axis=1)[0]
        ds_ssm = _mm_nt(dy_ssm, wf["w_ssm_out"], 0, name="w_branch_out_bwd_x")
        gw["w_ssm_out"] = _mm_tn(s["s_ssm"], dy_ssm, grad_buf("w_ssm_out"), 0, name="w_branch_out_bwd_w")
        (dg1, dz), (db_glu,) = _ew_bwd(_fn_glu, [(s["gact"], 0, sw), (s["z"], 0, sw)], [row(b_glu[l])], [[ds_ssm]],
                                       [F32, BF16], name="glu_bwd", width=sw, tw=_tile(sw, 512))
        dg2 = _mm_nt(dz, wf["w_glu"], 0, name="w_glu_bwd_x")
        gw["w_glu"] = _mm_tn(s["gact"], dz, grad_buf("w_glu"), 0, name="w_glu_bwd_w")
        (dy_pre, du_skip), (dd_skip,) = _ew_bwd(_fn_gelu, [(s["y_ssm_pre"], 0, sw), (proj, 0, sw)], [row(ssm_d[l])],
                                               [[dg1, dg2]], [BF16, F32], name="gelu_bwd", width=sw, tw=_tile(sw, 512))
        dhcat = _mm_nt(dy_pre, s["ccat"], 0, name="ssm_c_bwd_x")
        dccat = _mm_tn(s["hcat"], dy_pre, lax.empty((1, 1, 2 * gn, sw), F32), 0, name="ssm_c_bwd_w")[0, 0]
        gcat, dlam = _ssm_scan_bwd(dhcat, s["hcat"], s["lam"])
        du_b = _mm_nt(gcat, s["bcat"], 0, name="ssm_b_bwd_x")
        dbcat = _mm_tn(proj, gcat, lax.empty((1, 1, sw, 2 * gn), F32), 0, name="ssm_b_bwd_w")[0, 0]
        dlr, dli = _deinterleave(dlam, scan_c)
        dbre, dbim = _deinterleave(dbcat, scan_c)
        dbbr_t = jnp.transpose(_diag_blocks(dbre, groups), (1, 0, 2))
        dbbi_t = jnp.transpose(_diag_blocks(dbim, groups), (1, 0, 2))
        gld, gar, gai, gbr_t, gbi_t = _ssm_disc_bwd(*s["disc_in"], dlr.reshape(groups, SSM_STATE),
                                                    dli.reshape(groups, SSM_STATE), dbbr_t, dbbi_t)
        dcre_t, dcim_t = _deinterleave(dccat.T, scan_c)
        small["ssm_c_re"][l] = _diag_blocks(dcre_t, groups)
        small["ssm_c_im"][l] = -_diag_blocks(dcim_t, groups)
        small["ssm_log_dt"][l] = gld.reshape(groups)
        small["ssm_a_re"][l], small["ssm_a_im"][l] = gar, gai
        small["ssm_b_re"][l] = jnp.transpose(gbr_t, (1, 2, 0))
        small["ssm_b_im"][l] = jnp.transpose(gbi_t, (1, 2, 0))
        small["ssm_d"][l], small["b_glu"][l] = dd_skip[0], db_glu[0]
        du = (du_skip + du_b).astype(BF16)
        do_attn = _mm_nt(dy_attn, wf["w_attn_out"], 0, name="w_branch_out_bwd_x")
        gw["w_attn_out"] = _mm_tn(s["o_attn"], dy_attn, grad_buf("w_attn_out"), 0, name="w_branch_out_bwd_w")
        delta = _attn_delta(do_attn, s["o_attn"], head_ones)
        dqs, dks, dvs = [], [], []
        for p, (_, dl) in enumerate(DSWA_PATTERNS):
            dq_p, dk_p, dv_p = _attn_bwd(proj, do_attn, s["lse_attn"], delta, qkv_offs, p, dl, sw, slopes[p])
            dqs.append(dq_p)
            dks.append(dk_p)
            dvs.append(dv_p)
        dcv = _mm_nt(dy_conv, wf["w_conv_out"], 0, name="w_branch_out_bwd_x", out_dtype=BF16)
        gw["w_conv_out"] = _mm_tn(s["cv"], dy_conv, grad_buf("w_conv_out"), 0, name="w_branch_out_bwd_w")
        dcb, dcc, dch, dconv_w = _gconv_bwd(proj, off_conv, sw, conv_w_full[l], dcv)
        small["conv_mix_w"][l] = dconv_w
        dproj = jnp.concatenate([du] + [t.astype(BF16) for t in dqs + dks + dvs] + [dcb, dcc, dch, dp0, dp1, dp2],
                                axis=1)
        dh1 = _mm_nt(dproj, wf["w_in"], 0, name="w_in_bwd_x")
        gw["w_in"] = _mm_tn(s["h1"], dproj, grad_buf("w_in"), 0, name="w_in_bwd_w")
        (dx_in,), (dg_pre_mix, dsc1, dsh1) = _ew_bwd(
            _fn_norm_mod, [(s["x_in"], 0, d)], [row(g_pre_mix[l]), sc1, sh1], [[dh1]], [F32], name="norm_mod_bwd", width=d,
            dx_add={0: dx_mid})
        end_rs(pending_rs, dx_in)
        pending_rs = begin_rs(2 * l, gw)
        small["g_pre_mix"][l], small["g_post_mix"][l] = dg_pre_mix[0], dg_post_mix[0]
        small["g_pre_ffn"][l], small["g_post_ffn"][l] = dg_pre_ffn[0], dg_post_ffn[0]
        small["dmod"][l] = jnp.concatenate([dsh1, dsc1, dgt1, dsh2, dsc2, dgt2], axis=1)[0]
        dxl = dx_in

    grad_x = dxl.reshape(x.shape)

    small = {k: jnp.stack(v) for k, v in small.items()}
    order = sorted(small)
    flat = jnp.concatenate([small[k].reshape(-1) for k in order])
    n_small = flat.shape[0]
    pack_w = 8 * LANE
    pack_r = -(-n_small // (pack_w * SUBLANE)) * SUBLANE
    flat = jnp.concatenate([flat, jnp.zeros((pack_r * pack_w - n_small,), F32)])
    gathered = _ag8(flat.reshape(1, 1, pack_r, pack_w), False, "ag_small_grads")[0]
    summed = _sum8(gathered, "sum_small_grads").reshape(-1)
    sgrad, at = {}, 0
    for k in order:
        size = small[k].size
        sgrad[k] = summed[at:at + size].reshape(small[k].shape)
        at += size
    dmod_off = sum(small[k].size for k in order[:order.index("dmod")])
    dmod_all = gathered.reshape(N_DEV, -1)[:, dmod_off:dmod_off + nl * 6 * d].reshape(N_DEV, nl, 6 * d)
    dmod_loc = lax.dynamic_slice_in_dim(jnp.transpose(dmod_all, (1, 0, 2)), chip * mcols, mcols, axis=2)

    grads = dict(sgrad)
    grads["b_mod"] = grads.pop("dmod")
    grads["conv_mix_w"] = lax.dynamic_slice_in_dim(sgrad["conv_mix_w"], chip * conv_mix_w.shape[2], conv_mix_w.shape[2], axis=2)
    grads["ffn_conv_w"] = lax.dynamic_slice_in_dim(sgrad["ffn_conv_w"], chip * ffn_conv_w.shape[2], ffn_conv_w.shape[2], axis=2)

    end_rs(pending_rs, summed)
    for k in mix_keys + ffn_keys:
        grads[k] = _share_halves(gfin[k], "rs_share").reshape(weights[k].shape)

    delta_w, new_m, new_v = {}, {}, {}
    c_t = jnp.pad(jnp.transpose(c_all), ((0, 0), (0, LANE - N_DEV)))
    dmod_pad = jnp.pad(dmod_loc, ((0, 0), (0, LANE - N_DEV), (0, 0)))
    grads["w_mod"], delta_w["w_mod"], new_m["w_mod"], new_v["w_mod"] = _wmod_grad_adam(
        c_t, dmod_pad, w_mod, m_w_mod, v_w_mod)
    for k in names:
        if k == "w_mod":
            continue
        delta_w[k], new_m[k], new_v[k] = _adam(weights[k], grads[k], mom_m[k], mom_v[k], "adamw")

    return (loss, grad_x, *[grads[k] for k in names], *[delta_w[k] for k in names], *[new_m[k] for k in names],
            *[new_v[k] for k in names])
```

```python
import functools
import math

import numpy as np
import jax
import jax.numpy as jnp
from jax import lax
from jax.experimental import pallas as pl
from jax.experimental.pallas import tpu as pltpu

F32 = jnp.float32
BF16 = jnp.bfloat16
MESH = pl.DeviceIdType.MESH
ANY = pl.BlockSpec(memory_space=pl.ANY)

VMEM_LIMIT_BYTES = 48 * 1024 * 1024
LANE = 128
SUBLANE = 8

RMS_EPS = 1e-6
NEG_INF = -1e30
SSM_GROUP = 16
SSM_STATE = 64
HEAD_DIM = 64
DSWA_PATTERNS = ((128, 1), (512, 4), (2048, 16))
ATTN_BLOCK = 128
N_DEV = 8
N_CHIP = 4

ADAM_LR = 0.001
ADAM_B1 = 0.9
ADAM_B2 = 0.999
ADAM_EPS = 1e-08
ADAM_WD = 0.01
ADAM_STEP = 10


def _cp(sem=None):
    return pltpu.CompilerParams(dimension_semantics=sem, vmem_limit_bytes=VMEM_LIMIT_BYTES)


def _tile(n, pref, align=LANE):
    if n <= pref:
        return n
    t = (pref // align) * align
    while t >= align:
        if n % t == 0:
            return t
        t -= align
    return n


def _sds(shape, dtype):
    return jax.ShapeDtypeStruct(tuple(shape), dtype)


MM_VMEM_BUDGET = 34 * 1024 * 1024
MM_MAX_CONTRACT = 2048


def _divisors(n, align):
    if n % align:
        return [n]
    return [t for t in range(n, 0, -align) if n % t == 0]


def _halvings(n, align, floor=256):
    out = [n]
    while out[-1] % (2 * align) == 0 and out[-1] // 2 >= floor:
        out.append(out[-1] // 2)
    return out


def _pick_tiles(rows, cols, fixed_bytes, row_bytes, col_bytes, cell_bytes):
    best = None
    for tr in rows:
        for tc in cols:
            if fixed_bytes + row_bytes * tr + col_bytes * tc + cell_bytes * tr * tc <= MM_VMEM_BUDGET:
                if best is None or tr * tc > best[0] * best[1]:
                    best = (tr, tc)
                break
    assert best is not None
    return best


def _accumulate(step, n_steps, part, o_ref, acc_ref):
    if n_steps == 1:
        o_ref[...] = part.astype(o_ref.dtype)
        return
    acc = o_ref if acc_ref is None else acc_ref

    @pl.when(step == 0)
    def _():
        acc[...] = part

    @pl.when(step > 0)
    def _():
        acc[...] += part

    if acc_ref is not None:
        @pl.when(step == n_steps - 1)
        def _():
            o_ref[...] = acc_ref[...].astype(o_ref.dtype)


def _mm_nn(a, w, layer, *, name, k_dim=None, a_col0=0, out_dtype=F32, a_fn=None):
    m = a.shape[0]
    _, nb, kw, n = w.shape
    k_dim = kw if k_dim is None else k_dim
    assert k_dim == kw
    tk = _tile(k_dim, MM_MAX_CONTRACT)
    nk = k_dim // tk
    sa, so = a.dtype.itemsize, jnp.dtype(out_dtype).itemsize
    use_acc = nk > 1 and so != 4
    row_bytes = tk * (2 * sa + (2 if sa == 4 else 0) + (4 if a_fn is not None else 0))
    tm, tn = _pick_tiles(_halvings(m, SUBLANE), _divisors(n, LANE), 0, row_bytes, 2 * tk * w.dtype.itemsize,
                         2 * so + 4 + (4 if use_acc else 0))
    assert a_col0 % tk == 0
    npb = n // tn
    a0 = a_col0 // tk

    def body(a_ref, w_ref, o_ref, *scratch):
        av = a_ref[...]
        if a_fn is not None:
            av = a_fn(av.astype(F32))
        part = jnp.dot(av.astype(BF16), w_ref[...].astype(BF16), preferred_element_type=F32)
        _accumulate(pl.program_id(2), nk, part, o_ref, scratch[0] if use_acc else None)

    return pl.pallas_call(
        body,
        out_shape=_sds((m, nb * n), out_dtype),
        grid=(m // tm, nb * npb, nk),
        in_specs=[pl.BlockSpec((tm, tk), lambda i, j, k: (i, a0 + k)),
                  pl.BlockSpec((None, None, tk, tn), lambda i, j, k: (layer, j // npb, k, j % npb))],
        out_specs=pl.BlockSpec((tm, tn), lambda i, j, k: (i, j)),
        scratch_shapes=[pltpu.VMEM((tm, tn), F32)] if use_acc else [],
        compiler_params=_cp(("parallel", "parallel", "arbitrary")),
        name=name,
    )(a, w)


def _mm_nt(g, w, layer, *, name, out_dtype=F32):
    m = g.shape[0]
    _, nb, k_dim, n = w.shape
    assert g.shape[1] == nb * n
    tko = _tile(k_dim, MM_MAX_CONTRACT)
    sg, so = g.dtype.itemsize, jnp.dtype(out_dtype).itemsize
    use_acc = so != 4
    res_row = tko * (2 * so + 4 + (4 if use_acc else 0))
    tm, tc = _pick_tiles(_halvings(m, SUBLANE), _divisors(n, LANE), 0, res_row, 2 * tko * w.dtype.itemsize,
                         2 * sg + (2 if sg == 4 else 0))
    npb = n // tc
    nr = nb * npb
    use_acc = use_acc and nr > 1

    def body(g_ref, w_ref, o_ref, *scratch):
        part = lax.dot_general(g_ref[...].astype(BF16), w_ref[...].astype(BF16),
                               (((1,), (1,)), ((), ())), preferred_element_type=F32)
        _accumulate(pl.program_id(2), nr, part, o_ref, scratch[0] if use_acc else None)

    return pl.pallas_call(
        body,
        out_shape=_sds((m, k_dim), out_dtype),
        grid=(m // tm, k_dim // tko, nr),
        in_specs=[pl.BlockSpec((tm, tc), lambda i, kk, r: (i, r)),
                  pl.BlockSpec((None, None, tko, tc), lambda i, kk, r: (layer, r // npb, kk, r % npb))],
        out_specs=pl.BlockSpec((tm, tko), lambda i, kk, r: (i, kk)),
        scratch_shapes=[pltpu.VMEM((tm, tko), F32)] if use_acc else [],
        compiler_params=_cp(("parallel", "parallel", "arbitrary")),
        name=name,
    )(g, w)


def _mm_tn(a, g, out_buf, layer, *, name, a_col0=0):
    m = a.shape[0]
    _, nb, k_dim, n = out_buf.shape
    assert g.shape == (m, nb * n)
    tm = _tile(m, MM_MAX_CONTRACT, SUBLANE)
    nr = m // tm
    sa, sg = a.dtype.itemsize, g.dtype.itemsize
    tk, tn = _pick_tiles(_halvings(k_dim, LANE), _divisors(n, LANE), 0, tm * (2 * sa + (2 if sa == 4 else 0) + 2),
                         tm * (2 * sg + (2 if sg == 4 else 0)), 2 * 4 + 4)
    assert a_col0 % tk == 0
    a0 = a_col0 // tk
    npb = n // tn

    def body(a_ref, g_ref, buf_ref, o_ref):
        del buf_ref
        part = lax.dot_general(a_ref[...].astype(BF16), g_ref[...].astype(BF16),
                               (((0,), (0,)), ((), ())), preferred_element_type=F32)
        _accumulate(pl.program_id(2), nr, part, o_ref, None)

    return pl.pallas_call(
        body,
        out_shape=_sds(out_buf.shape, F32),
        grid=(k_dim // tk, nb * npb, nr),
        in_specs=[pl.BlockSpec((tm, tk), lambda kk, j, r: (r, a0 + kk)),
                  pl.BlockSpec((tm, tn), lambda kk, j, r: (r, j)),
                  ANY],
        out_specs=pl.BlockSpec((None, None, tk, tn), lambda kk, j, r: (layer, j // npb, kk, j % npb)),
        input_output_aliases={2: 0},
        compiler_params=_cp(("parallel", "parallel", "arbitrary")),
        name=name,
    )(a, g, out_buf)


def _ew_fwd(fn, xs, ps, out_dtypes, *, name, width, tw=None, tm=256):
    rows = xs[0][0].shape[0]
    tm = _tile(rows, tm, SUBLANE)
    tw = width if tw is None else tw
    nx, n_p = len(xs), len(ps)

    def body(*refs):
        xv = [r[...].astype(F32) for r in refs[:nx]]
        pv = [r[...].astype(F32) for r in refs[nx:nx + n_p]]
        outs = fn(*xv, *pv)
        if not isinstance(outs, (tuple, list)):
            outs = (outs,)
        for o_ref, o in zip(refs[nx + n_p:], outs):
            o_ref[...] = o.astype(o_ref.dtype)

    in_specs = []
    for arr, c0, w in xs:
        assert w == width and c0 % tw == 0
        in_specs.append(pl.BlockSpec((tm, tw), functools.partial(lambda i, j, b: (i, b + j), b=c0 // tw)))
    for p in ps:
        assert p.shape == (1, width)
        in_specs.append(pl.BlockSpec((1, tw), lambda i, j: (0, j)))
    outs = pl.pallas_call(
        body,
        out_shape=[_sds((rows, width), d) for d in out_dtypes],
        grid=(rows // tm, width // tw),
        in_specs=in_specs,
        out_specs=[pl.BlockSpec((tm, tw), lambda i, j: (i, j)) for _ in out_dtypes],
        compiler_params=_cp(("parallel", "parallel")),
        name=name,
    )(*[x[0] for x in xs], *ps)
    return outs


def _ew_bwd(fn, xs, ps, cts, dx_dtypes, *, name, width, tw=None, tm=256, dx_add=None):
    rows = xs[0][0].shape[0]
    tm = _tile(rows, tm, SUBLANE)
    tw = width if tw is None else tw
    nx, n_p = len(xs), len(ps)
    dx_add = dx_add or {}
    flat_cts = [c for group in cts for c in group]
    add_keys = sorted(dx_add)
    n_in = nx + n_p + len(flat_cts) + len(add_keys)
    dx_idx = [i for i, d in enumerate(dx_dtypes) if d is not None]

    def body(*refs):
        i = pl.program_id(1)
        xv = [r[...].astype(F32) for r in refs[:nx]]
        pv = [r[...].astype(F32) for r in refs[nx:nx + n_p]]
        pos = nx + n_p
        ct_vals = []
        for group in cts:
            acc = refs[pos][...].astype(F32)
            pos += 1
            for _ in group[1:]:
                acc = acc + refs[pos][...].astype(F32)
                pos += 1
            ct_vals.append(acc)
        add_vals = {}
        for key in add_keys:
            add_vals[key] = refs[pos][...].astype(F32)
            pos += 1
        out_refs = refs[n_in:]
        outs, vjp = jax.vjp(fn, *xv, *pv)
        grads = vjp(tuple(ct_vals) if isinstance(outs, (tuple, list)) else ct_vals[0])
        o = 0
        for idx in dx_idx:
            gval = grads[idx]
            if idx in add_vals:
                gval = gval + add_vals[idx]
            out_refs[o][...] = gval.astype(out_refs[o].dtype)
            o += 1
        for q in range(n_p):
            gp = grads[nx + q]
            ref = out_refs[o + q]

            @pl.when(i == 0)
            def _(ref=ref, gp=gp):
                ref[...] = gp

            @pl.when(i > 0)
            def _(ref=ref, gp=gp):
                ref[...] += gp

    tile_spec = pl.BlockSpec((tm, tw), lambda j, i: (i, j))
    in_specs = []
    for arr, c0, w in xs:
        assert w == width and c0 % tw == 0
        in_specs.append(pl.BlockSpec((tm, tw), functools.partial(lambda j, i, b: (i, b + j), b=c0 // tw)))
    for p in ps:
        in_specs.append(pl.BlockSpec((1, tw), lambda j, i: (0, j)))
    in_specs += [tile_spec] * (len(flat_cts) + len(add_keys))
    out_shape = [_sds((rows, width), dx_dtypes[idx]) for idx in dx_idx] + [_sds((1, width), F32)] * n_p
    out_specs = [tile_spec] * len(dx_idx) + [pl.BlockSpec((1, tw), lambda j, i: (0, j))] * n_p
    outs = pl.pallas_call(
        body,
        out_shape=out_shape,
        grid=(width // tw, rows // tm),
        in_specs=in_specs,
        out_specs=out_specs,
        compiler_params=_cp(("parallel", "arbitrary")),
        name=name,
    )(*[x[0] for x in xs], *ps, *flat_cts, *[dx_add[k] for k in add_keys])
    return outs[:len(dx_idx)], outs[len(dx_idx):]


def _rms(x):
    return x * lax.rsqrt(jnp.mean(x * x, axis=-1, keepdims=True) + RMS_EPS)


def _fn_norm_mod(x, g, sc, sh):
    return (_rms(x) * g) * (1.0 + sc) + sh


def _fn_residual(x, y, gt, g):
    return x + gt * (_rms(y) * g)


def _fn_gelu(y, u, d):
    return jax.nn.gelu(y + d * u)


def _fn_glu(g, z, b):
    return g * jax.nn.sigmoid(z + b)


def _fn_gates(p0, p1, p2, ys, ya, yc, b0, b1, b2):
    return (jax.nn.sigmoid(p0 + b0) * ys + jax.nn.sigmoid(p1 + b1) * ya + jax.nn.sigmoid(p2 + b2) * yc)


def _fn_disc(log_dt, ar, ai, br_t, bi_t):
    dt = jnp.exp(log_dt)
    mag = jnp.exp(ar * dt)
    lr, li = mag * jnp.cos(ai * dt), mag * jnp.sin(ai * dt)
    den = ar * ar + ai * ai
    fr = ((lr - 1.0) * ar + li * ai) / den
    fi = (li * ar - (lr - 1.0) * ai) / den
    bbr = fr[None] * br_t - fi[None] * bi_t
    bbi = fr[None] * bi_t + fi[None] * br_t
    return lr, li, bbr, bbi


def _ssm_disc_fwd(log_dt, ar, ai, br_t, bi_t):
    g, n = ar.shape

    def body(ld_ref, ar_ref, ai_ref, br_ref, bi_ref, lr_ref, li_ref, bbr_ref, bbi_ref):
        lr, li, bbr, bbi = _fn_disc(ld_ref[...], ar_ref[...], ai_ref[...], br_ref[...], bi_ref[...])
        lr_ref[...] = lr
        li_ref[...] = li
        bbr_ref[...] = bbr
        bbi_ref[...] = bbi

    return pl.pallas_call(
        body,
        out_shape=[_sds((g, n), F32), _sds((g, n), F32), _sds(br_t.shape, F32), _sds(br_t.shape, F32)],
        compiler_params=_cp(),
        name="ssm_disc_fwd",
    )(log_dt, ar, ai, br_t, bi_t)


def _ssm_disc_bwd(log_dt, ar, ai, br_t, bi_t, dlr, dli, dbbr, dbbi):
    g, n = ar.shape

    def body(ld_ref, ar_ref, ai_ref, br_ref, bi_ref, dlr_ref, dli_ref, dbbr_ref, dbbi_ref,
             gld_ref, gar_ref, gai_ref, gbr_ref, gbi_ref):
        _, vjp = jax.vjp(_fn_disc, ld_ref[...], ar_ref[...], ai_ref[...], br_ref[...], bi_ref[...])
        gld, gar, gai, gbr, gbi = vjp((dlr_ref[...], dli_ref[...], dbbr_ref[...], dbbi_ref[...]))
        gld_ref[...] = gld
        gar_ref[...] = gar
        gai_ref[...] = gai
        gbr_ref[...] = gbr
        gbi_ref[...] = gbi

    return pl.pallas_call(
        body,
        out_shape=[_sds((g, 1), F32), _sds((g, n), F32), _sds((g, n), F32), _sds(br_t.shape, F32),
                   _sds(br_t.shape, F32)],
        compiler_params=_cp(),
        name="ssm_disc_bwd",
    )(log_dt, ar, ai, br_t, bi_t, dlr, dli, dbbr, dbbi)


def _cmul(ar, ai, br, bi):
    return ar * br - ai * bi, ar * bi + ai * br


def _scan_tables(lr, li, reverse):
    c = lr.shape[-1]
    p1 = (jnp.broadcast_to(lr, (SUBLANE, c)), jnp.broadcast_to(li, (SUBLANE, c)))
    p2 = _cmul(*p1, *p1)
    p4 = _cmul(*p2, *p2)
    p8 = _cmul(*p4, *p4)
    row = lax.broadcasted_iota(jnp.int32, (SUBLANE, c), 0)
    dist = (SUBLANE - row) if reverse else (row + 1)
    pr, pi = jnp.ones((SUBLANE, c), F32), jnp.zeros((SUBLANE, c), F32)
    for bit, pw in ((1, p1), (2, p2), (4, p4), (8, p8)):
        qr, qi = _cmul(pr, pi, *pw)
        take = (dist & bit) != 0
        pr, pi = jnp.where(take, qr, pr), jnp.where(take, qi, pi)
    return row, (p1, p2, p4), (pr, pi)


def _shift_rows(x, s, row, reverse):
    if reverse:
        return jnp.where(row < SUBLANE - s, pltpu.roll(x, SUBLANE - s, 0), 0.0)
    return jnp.where(row >= s, pltpu.roll(x, s, 0), 0.0)


def _scan_tile(xr, xi, carry, row, pows, carry_pow, reverse):
    for s, pw in zip((1, 2, 4), pows):
        sr, si = _shift_rows(xr, s, row, reverse), _shift_rows(xi, s, row, reverse)
        tr, ti = _cmul(*pw, sr, si)
        xr, xi = xr + tr, xi + ti
    tr, ti = _cmul(*carry_pow, *carry)
    hr, hi = xr + tr, xi + ti
    edge = 0 if reverse else SUBLANE - 1
    c = hr.shape[-1]
    new_carry = (jnp.broadcast_to(hr[edge:edge + 1, :], (SUBLANE, c)),
                 jnp.broadcast_to(hi[edge:edge + 1, :], (SUBLANE, c)))
    return hr, hi, new_carry


def _scan_cols(gn):
    return _tile(gn, 256)


def _ssm_scan_fwd(xcat, lam):
    rows, gn2 = xcat.shape
    c = _scan_cols(gn2 // 2)
    n_tiles = rows // SUBLANE

    def body(lam_ref, x_ref, h_ref):
        lr, li = lam_ref[:, :c], lam_ref[:, c:]
        row, pows, carry_pow = _scan_tables(lr, li, False)

        def step(k, carry):
            t0 = pl.multiple_of(k * SUBLANE, SUBLANE)
            hr, hi, carry = _scan_tile(x_ref[pl.ds(t0, SUBLANE), :c], x_ref[pl.ds(t0, SUBLANE), c:], carry,
                                       row, pows, carry_pow, False)
            h_ref[pl.ds(t0, SUBLANE), :c] = hr
            h_ref[pl.ds(t0, SUBLANE), c:] = hi
            return carry

        zero = jnp.zeros((SUBLANE, c), F32)
        lax.fori_loop(0, n_tiles, step, (zero, zero))

    return pl.pallas_call(
        body,
        out_shape=_sds((rows, gn2), F32),
        grid=(gn2 // (2 * c),),
        in_specs=[pl.BlockSpec((1, 2 * c), lambda j: (0, j)), pl.BlockSpec((rows, 2 * c), lambda j: (0, j))],
        out_specs=pl.BlockSpec((rows, 2 * c), lambda j: (0, j)),
        compiler_params=_cp(("parallel",)),
        name="ssm_scan_fwd",
    )(lam, xcat)


def _ssm_scan_bwd(dhcat, hcat, lam):
    rows, gn2 = dhcat.shape
    c = _scan_cols(gn2 // 2)
    n_tiles = rows // SUBLANE

    def body(lam_ref, dh_ref, h_ref, g_ref, dlam_ref):
        lr, li = lam_ref[:, :c], -lam_ref[:, c:]
        row, pows, carry_pow = _scan_tables(lr, li, True)

        def step(k, state):
            carry, acc_r, acc_i = state
            kk = n_tiles - 1 - k
            t0 = pl.multiple_of(kk * SUBLANE, SUBLANE)
            gr, gi, carry = _scan_tile(dh_ref[pl.ds(t0, SUBLANE), :c], dh_ref[pl.ds(t0, SUBLANE), c:], carry,
                                       row, pows, carry_pow, True)
            g_ref[pl.ds(t0, SUBLANE), :c] = gr
            g_ref[pl.ds(t0, SUBLANE), c:] = gi
            tp = pl.multiple_of(jnp.maximum(kk - 1, 0) * SUBLANE, SUBLANE)
            has_prev = (kk > 0).astype(F32)
            prev_r = pltpu.roll(h_ref[pl.ds(tp, SUBLANE), :c], 1, 0) * has_prev
            prev_i = pltpu.roll(h_ref[pl.ds(tp, SUBLANE), c:], 1, 0) * has_prev
            hpr = jnp.where(row >= 1, pltpu.roll(h_ref[pl.ds(t0, SUBLANE), :c], 1, 0), prev_r)
            hpi = jnp.where(row >= 1, pltpu.roll(h_ref[pl.ds(t0, SUBLANE), c:], 1, 0), prev_i)
            acc_r = acc_r + gr * hpr + gi * hpi
            acc_i = acc_i + gi * hpr - gr * hpi
            return carry, acc_r, acc_i

        zero = jnp.zeros((SUBLANE, c), F32)
        _, acc_r, acc_i = lax.fori_loop(0, n_tiles, step, ((zero, zero), zero, zero))
        dlam_ref[:, :c] = jnp.sum(acc_r, axis=0, keepdims=True)
        dlam_ref[:, c:] = jnp.sum(acc_i, axis=0, keepdims=True)

    blk = pl.BlockSpec((rows, 2 * c), lambda j: (0, j))
    return pl.pallas_call(
        body,
        out_shape=[_sds((rows, gn2), F32), _sds((1, gn2), F32)],
        grid=(gn2 // (2 * c),),
        in_specs=[pl.BlockSpec((1, 2 * c), lambda j: (0, j)), blk, blk],
        out_specs=[blk, pl.BlockSpec((1, 2 * c), lambda j: (0, j))],
        compiler_params=_cp(("parallel",)),
        name="ssm_scan_bwd",
    )(lam, dhcat, hcat)


def _shift_down(x, k, row):
    return x if k == 0 else jnp.where(row >= k, pltpu.roll(x, k, 0), 0.0)


def _shift_up(x, k, row):
    n = x.shape[0]
    return x if k == 0 else jnp.where(row < n - k, pltpu.roll(x, n - k, 0), 0.0)


def _taps(w_ref):
    return [w_ref[k:k + 1, :] for k in range(3)]


def _conv3(x, w, row):
    return sum(w[k] * _shift_down(x, k, row) for k in range(3))


def _conv3_bwd(x, w, dy, row):
    dx = sum(w[k] * _shift_up(dy, k, row) for k in range(3))
    dw = [jnp.sum(dy * _shift_down(x, k, row), axis=0, keepdims=True) for k in range(3)]
    return dx, dw


def _gconv_fwd(proj, off, cw, w):
    rows = proj.shape[0]
    tc = _tile(cw, LANE)
    nb = cw // tc

    def body(b_ref, c_ref, h_ref, w_ref, o_ref):
        row = lax.broadcasted_iota(jnp.int32, (rows, tc), 0)
        o_ref[...] = (b_ref[...] * _conv3(c_ref[...] * h_ref[...], _taps(w_ref), row)).astype(o_ref.dtype)

    specs = [pl.BlockSpec((rows, tc), functools.partial(lambda j, b: (0, b + j), b=(off + q * cw) // tc))
             for q in range(3)]
    return pl.pallas_call(
        body,
        out_shape=_sds((rows, cw), BF16),
        grid=(nb,),
        in_specs=specs + [pl.BlockSpec((3, tc), lambda j: (0, j))],
        out_specs=pl.BlockSpec((rows, tc), lambda j: (0, j)),
        compiler_params=_cp(("parallel",)),
        name="gconv_fwd",
    )(proj, proj, proj, w)


def _gconv_bwd(proj, off, cw, w, dy):
    rows = proj.shape[0]
    tc = _tile(cw, LANE)
    nb = cw // tc

    def body(b_ref, c_ref, h_ref, w_ref, dy_ref, db_ref, dc_ref, dh_ref, dw_ref):
        row = lax.broadcasted_iota(jnp.int32, (rows, tc), 0)
        cv, hv, dyv = c_ref[...], h_ref[...], dy_ref[...].astype(F32)
        t = cv * hv
        db_ref[...] = (dyv * _conv3(t, _taps(w_ref), row)).astype(db_ref.dtype)
        dt, dw = _conv3_bwd(t, _taps(w_ref), dyv * b_ref[...], row)
        dc_ref[...] = (dt * hv).astype(dc_ref.dtype)
        dh_ref[...] = (dt * cv).astype(dh_ref.dtype)
        for k in range(3):
            dw_ref[k:k + 1, :] = dw[k]

    specs = [pl.BlockSpec((rows, tc), functools.partial(lambda j, b: (0, b + j), b=(off + q * cw) // tc))
             for q in range(3)]
    col = pl.BlockSpec((rows, tc), lambda j: (0, j))
    wspec = pl.BlockSpec((3, tc), lambda j: (0, j))
    return pl.pallas_call(
        body,
        out_shape=[_sds((rows, cw), BF16)] * 3 + [_sds((3, cw), F32)],
        grid=(nb,),
        in_specs=specs + [wspec, col],
        out_specs=[col, col, col, wspec],
        compiler_params=_cp(("parallel",)),
        name="gconv_bwd",
    )(proj, proj, proj, w, dy)


def _ffn_act_fwd(up, w):
    rows, f2 = up.shape
    f = f2 // 2
    tc = _tile(f, LANE)
    nb = f // tc

    def body(a_ref, b_ref, wa_ref, wb_ref, o_ref):
        row = lax.broadcasted_iota(jnp.int32, (rows, tc), 0)
        a = _conv3(a_ref[...], _taps(wa_ref), row)
        b = _conv3(b_ref[...], _taps(wb_ref), row)
        o_ref[...] = (jax.nn.silu(a) * b).astype(o_ref.dtype)

    return pl.pallas_call(
        body,
        out_shape=_sds((rows, f), BF16),
        grid=(nb,),
        in_specs=[pl.BlockSpec((rows, tc), lambda j: (0, j)), pl.BlockSpec((rows, tc), lambda j: (0, nb + j)),
                  pl.BlockSpec((3, tc), lambda j: (0, j)), pl.BlockSpec((3, tc), lambda j: (0, nb + j))],
        out_specs=pl.BlockSpec((rows, tc), lambda j: (0, j)),
        compiler_params=_cp(("parallel",)),
        name="ffn_act_fwd",
    )(up, up, w, w)


def _ffn_act_bwd(up, w, dact):
    rows, f2 = up.shape
    f = f2 // 2
    tc = _tile(f, LANE)
    nb = f // tc

    def body(a_ref, b_ref, wa_ref, wb_ref, d_ref, da_ref, db_ref, dwa_ref, dwb_ref):
        row = lax.broadcasted_iota(jnp.int32, (rows, tc), 0)
        av, bv, dv = a_ref[...], b_ref[...], d_ref[...].astype(F32)
        ac = _conv3(av, _taps(wa_ref), row)
        bc = _conv3(bv, _taps(wb_ref), row)
        _, vjp = jax.vjp(lambda p, q: jax.nn.silu(p) * q, ac, bc)
        dac, dbc = vjp(dv)
        dxa, dwa = _conv3_bwd(av, _taps(wa_ref), dac, row)
        dxb, dwb = _conv3_bwd(bv, _taps(wb_ref), dbc, row)
        da_ref[...] = dxa.astype(da_ref.dtype)
        db_ref[...] = dxb.astype(db_ref.dtype)
        for k in range(3):
            dwa_ref[k:k + 1, :] = dwa[k]
            dwb_ref[k:k + 1, :] = dwb[k]

    col = pl.BlockSpec((rows, tc), lambda j: (0, j))
    wspec = pl.BlockSpec((3, tc), lambda j: (0, j))
    return pl.pallas_call(
        body,
        out_shape=[_sds((rows, f), BF16)] * 2 + [_sds((3, f), F32)] * 2,
        grid=(nb,),
        in_specs=[col, pl.BlockSpec((rows, tc), lambda j: (0, nb + j)), wspec,
                  pl.BlockSpec((3, tc), lambda j: (0, nb + j)), col],
        out_specs=[col, col, wspec, wspec],
        compiler_params=_cp(("parallel",)),
        name="ffn_act_bwd",
    )(up, up, w, w, dact)


def _attn_scores(q, kc, kp, slope, dilation, has_prev):
    scale = HEAD_DIM ** -0.5
    nt = (((1,), (1,)), ((), ()))
    s_c = lax.dot_general(q, kc, nt, preferred_element_type=F32) * scale
    s_p = lax.dot_general(q, kp, nt, preferred_element_type=F32) * scale
    qi = lax.broadcasted_iota(jnp.int32, (ATTN_BLOCK, ATTN_BLOCK), 0)
    kj = lax.broadcasted_iota(jnp.int32, (ATTN_BLOCK, ATTN_BLOCK), 1)
    dist_c = qi - kj
    dist_p = dist_c + ATTN_BLOCK
    s_c = jnp.where(dist_c >= 0, s_c - slope * (dist_c * dilation).astype(F32), NEG_INF)
    s_p = jnp.where((dist_p <= ATTN_BLOCK) & has_prev, s_p - slope * (dist_p * dilation).astype(F32), NEG_INF)
    return s_c, s_p


def _slab(seq, col0):
    assert col0 % LANE == 0
    return pl.BlockSpec((seq, LANE), lambda hh, r, s: (0, col0 // LANE + hh))


def _residue_rows(r, block, dilation):
    if dilation == 1:
        return pl.ds(pl.multiple_of(block * ATTN_BLOCK, ATTN_BLOCK), ATTN_BLOCK)
    return pl.ds(r + dilation * ATTN_BLOCK * block, ATTN_BLOCK, stride=dilation)


def _head_col(x, mask):
    return jnp.max(jnp.where(mask, x, -jnp.inf), axis=-1, keepdims=True)


def _attn_fwd(proj, offs, pattern, dilation, sw, slopes):
    seq, _ = proj.shape
    nb = seq // dilation // ATTN_BLOCK
    pairs = sw // LANE

    def body(q_ref, k_ref, v_ref, s0_ref, s1_ref, o_ref, lse_ref):
        r, i = pl.program_id(1), pl.program_id(2)
        cur, prev = _residue_rows(r, i, dilation), _residue_rows(r, jnp.maximum(i - 1, 0), dilation)
        first = lax.broadcasted_iota(jnp.int32, (ATTN_BLOCK, LANE), 1) < HEAD_DIM
        q2 = q_ref[cur, :]
        kc, kp = k_ref[cur, :].astype(BF16), k_ref[prev, :].astype(BF16)
        vc, vp = v_ref[cur, :].astype(BF16), v_ref[prev, :].astype(BF16)
        res = []
        for mask, sl_ref in ((first, s0_ref), (~first, s1_ref)):
            qh = jnp.where(mask, q2, 0.0).astype(BF16)
            s_c, s_p = _attn_scores(qh, kc, kp, sl_ref[:, :1], dilation, i > 0)
            mx = jnp.maximum(jnp.max(s_c, axis=-1, keepdims=True), jnp.max(s_p, axis=-1, keepdims=True))
            p_c, p_p = jnp.exp(s_c - mx), jnp.exp(s_p - mx)
            den = jnp.sum(p_c, axis=-1, keepdims=True) + jnp.sum(p_p, axis=-1, keepdims=True)
            o = (jnp.dot(p_c.astype(BF16), vc, preferred_element_type=F32)
                 + jnp.dot(p_p.astype(BF16), vp, preferred_element_type=F32))
            res.append((o / den, mx + jnp.log(den)))
        o_ref[cur, :] = jnp.where(first, res[0][0], res[1][0])
        lse_ref[cur, :] = jnp.where(first, res[0][1], res[1][1])

    slope = pl.BlockSpec((None, 1, LANE), lambda hh, r, s: (hh, 0, 0))
    return pl.pallas_call(
        body,
        out_shape=[_sds((seq, sw), F32)] * 2,
        grid=(pairs, dilation, nb),
        in_specs=[_slab(seq, offs[0] + pattern * sw), _slab(seq, offs[1] + pattern * sw),
                  _slab(seq, offs[2] + pattern * sw), slope, slope],
        out_specs=[_slab(seq, 0), _slab(seq, 0)],
        compiler_params=_cp(("parallel", "arbitrary", "arbitrary")),
        name=f"attn_fwd_d{dilation}",
    )(proj, proj, proj, *slopes)


def _attn_bwd(proj, do, lse, delta, offs, pattern, dilation, sw, slopes):
    seq, _ = proj.shape
    nb = seq // dilation // ATTN_BLOCK
    pairs = sw // LANE

    def body(q_ref, k_ref, v_ref, do_ref, lse_ref, dl_ref, s0_ref, s1_ref, dq_ref, dk_ref, dv_ref, ck_ref, cv_ref):
        r, step = pl.program_id(1), pl.program_id(2)
        i = nb - 1 - step
        cur, prev = _residue_rows(r, i, dilation), _residue_rows(r, jnp.maximum(i - 1, 0), dilation)
        scale = HEAD_DIM ** -0.5
        nt = (((1,), (1,)), ((), ()))
        first = lax.broadcasted_iota(jnp.int32, (ATTN_BLOCK, LANE), 1) < HEAD_DIM
        q2, do2, lse2, dl2 = q_ref[cur, :], do_ref[cur, :], lse_ref[cur, :], dl_ref[cur, :]
        kc, kp = k_ref[cur, :].astype(BF16), k_ref[prev, :].astype(BF16)
        vc, vp = v_ref[cur, :].astype(BF16), v_ref[prev, :].astype(BF16)

        @pl.when(step == 0)
        def _():
            ck_ref[...] = jnp.zeros_like(ck_ref)
            cv_ref[...] = jnp.zeros_like(cv_ref)

        dq, dk_c, dv_c, dk_p, dv_p = [], 0.0, 0.0, 0.0, 0.0
        for mask, sl_ref in ((first, s0_ref), (~first, s1_ref)):
            qh = jnp.where(mask, q2, 0.0).astype(BF16)
            doh = jnp.where(mask, do2, 0.0).astype(BF16)
            lse_col, dl_col = _head_col(lse2, mask), _head_col(dl2, mask)
            s_c, s_p = _attn_scores(qh, kc, kp, sl_ref[:, :1], dilation, i > 0)
            p_c, p_p = jnp.exp(s_c - lse_col), jnp.exp(s_p - lse_col)
            ds_c = p_c * (lax.dot_general(doh, vc, nt, preferred_element_type=F32) - dl_col)
            ds_p = p_p * (lax.dot_general(doh, vp, nt, preferred_element_type=F32) - dl_col)
            dq.append(jnp.dot(ds_c.astype(BF16), kc, preferred_element_type=F32)
                      + jnp.dot(ds_p.astype(BF16), kp, preferred_element_type=F32))
            dk_c = dk_c + jnp.dot(ds_c.T.astype(BF16), qh, preferred_element_type=F32)
            dv_c = dv_c + jnp.dot(p_c.T.astype(BF16), doh, preferred_element_type=F32)
            dk_p = dk_p + jnp.dot(ds_p.T.astype(BF16), qh, preferred_element_type=F32)
            dv_p = dv_p + jnp.dot(p_p.T.astype(BF16), doh, preferred_element_type=F32)
        dq_ref[cur, :] = jnp.where(first, dq[0], dq[1]) * scale
        dk_ref[cur, :] = dk_c * scale + ck_ref[...]
        dv_ref[cur, :] = dv_c + cv_ref[...]
        ck_ref[...] = dk_p * scale
        cv_ref[...] = dv_p

    slope = pl.BlockSpec((None, 1, LANE), lambda hh, r, s: (hh, 0, 0))
    tok = _slab(seq, 0)
    return pl.pallas_call(
        body,
        out_shape=[_sds((seq, sw), F32)] * 3,
        grid=(pairs, dilation, nb),
        in_specs=[_slab(seq, offs[0] + pattern * sw), _slab(seq, offs[1] + pattern * sw),
                  _slab(seq, offs[2] + pattern * sw), tok, tok, tok, slope, slope],
        out_specs=[tok, tok, tok],
        scratch_shapes=[pltpu.VMEM((ATTN_BLOCK, LANE), F32), pltpu.VMEM((ATTN_BLOCK, LANE), F32)],
        compiler_params=_cp(("parallel", "arbitrary", "arbitrary")),
        name=f"attn_bwd_d{dilation}",
    )(proj, proj, proj, do, lse, delta, *slopes)


def _attn_merge(outs, lses):
    rows, aw = outs[0].shape
    tm = _tile(rows, 256, SUBLANE)

    def body(o0, o1, o2, l0, l1, l2, o_ref, lse_ref):
        lv = [l0[...], l1[...], l2[...]]
        mx = jnp.maximum(jnp.maximum(lv[0], lv[1]), lv[2])
        w = [jnp.exp(t - mx) for t in lv]
        den = w[0] + w[1] + w[2]
        o_ref[...] = (w[0] * o0[...] + w[1] * o1[...] + w[2] * o2[...]) / den
        lse_ref[...] = mx + jnp.log(den)

    spec = pl.BlockSpec((tm, aw), lambda i: (i, 0))
    return pl.pallas_call(
        body,
        out_shape=[_sds((rows, aw), F32)] * 2,
        grid=(rows // tm,),
        in_specs=[spec] * 6,
        out_specs=[spec, spec],
        compiler_params=_cp(("parallel",)),
        name="attn_merge",
    )(*outs, *lses)


def _attn_delta(do, o, head_ones):
    rows, aw = do.shape
    tm = _tile(rows, 256, SUBLANE)

    def body(do_ref, o_ref, e_ref, d_ref):
        d_ref[...] = jnp.dot(do_ref[...] * o_ref[...], e_ref[...], preferred_element_type=F32,
                             precision=lax.Precision.HIGHEST)

    spec = pl.BlockSpec((tm, aw), lambda i: (i, 0))
    return pl.pallas_call(
        body,
        out_shape=_sds((rows, aw), F32),
        grid=(rows // tm,),
        in_specs=[spec, spec, pl.BlockSpec((aw, aw), lambda i: (0, 0))],
        out_specs=spec,
        compiler_params=_cp(("parallel",)),
        name="attn_delta",
    )(do, o, head_ones)


def _alibi_slopes(pattern, hp):
    n_heads = hp * len(DSWA_PATTERNS)
    s = np.array([2.0 ** (-8.0 * (pattern * hp + h + 1) / n_heads) for h in range(hp)], dtype=np.float32)
    return [jnp.asarray(np.broadcast_to(s[par::2, None, None], (hp // 2, 1, LANE)).copy()) for par in (0, 1)]


def _loss_fwd_bwd(y, target):
    rows, d = y.shape
    tm = _tile(rows, 256, SUBLANE)

    def body(y_ref, t_ref, dy_ref, l_ref):
        i = pl.program_id(0)
        err = y_ref[...] - t_ref[...]
        dy_ref[...] = err * (1.0 / d)
        part = 0.5 * jnp.sum(jnp.mean(err * err, axis=-1, keepdims=True), axis=0, keepdims=True)

        @pl.when(i == 0)
        def _():
            l_ref[...] = jnp.zeros_like(l_ref)

        l_ref[...] += jnp.broadcast_to(part, l_ref.shape)

    spec = pl.BlockSpec((tm, d), lambda i: (i, 0))
    dy, loss = pl.pallas_call(
        body,
        out_shape=[_sds((rows, d), F32), _sds((SUBLANE, LANE), F32)],
        grid=(rows // tm,),
        in_specs=[spec, spec],
        out_specs=[spec, pl.BlockSpec((SUBLANE, LANE), lambda i: (0, 0))],
        compiler_params=_cp(("arbitrary",)),
        name="loss",
    )(y, target)
    return dy, loss[0, 0]


def _adam_math(w, g, m, v):
    m = ADAM_B1 * m + (1.0 - ADAM_B1) * g
    v = ADAM_B2 * v + (1.0 - ADAM_B2) * jnp.square(g)
    m_hat = m / (1.0 - ADAM_B1 ** ADAM_STEP)
    v_hat = v / (1.0 - ADAM_B2 ** ADAM_STEP)
    delta = -ADAM_LR * (m_hat / (jnp.sqrt(v_hat) + ADAM_EPS) + ADAM_WD * w)
    return delta, m, v


def _as2d(a):
    if a.ndim == 1:
        return a.reshape(1, -1)
    return a.reshape(-1, a.shape[-1])


def _adam(w, g, m, v, name):
    shape = w.shape
    w2, g2, m2, v2 = _as2d(w), _as2d(g), _as2d(m), _as2d(v)
    r, c = w2.shape
    tr = _tile(r, 512, SUBLANE)
    tc = _tile(c, 1024)

    def body(w_ref, g_ref, m_ref, v_ref, d_ref, mo_ref, vo_ref):
        delta, mn, vn = _adam_math(w_ref[...], g_ref[...], m_ref[...], v_ref[...])
        d_ref[...] = delta
        mo_ref[...] = mn
        vo_ref[...] = vn

    spec = pl.BlockSpec((tr, tc), lambda i, j: (i, j))
    outs = pl.pallas_call(
        body,
        out_shape=[_sds((r, c), F32)] * 3,
        grid=(r // tr, c // tc),
        in_specs=[spec] * 4,
        out_specs=[spec] * 3,
        compiler_params=_cp(("parallel", "parallel")),
        name=name,
    )(w2, g2, m2, v2)
    return [o.reshape(shape) for o in outs]


def _wmod_grad_adam(c_t, dmod, w, m, v):
    nl, d, cols = w.shape
    nex = c_t.shape[1]
    tr = _tile(d, 256, SUBLANE)
    tc = _tile(cols, 1024)

    def body(c_ref, dm_ref, w_ref, m_ref, v_ref, g_ref, d_ref, mo_ref, vo_ref):
        cond = jax.nn.silu(c_ref[...]).astype(BF16)
        g = jnp.dot(cond, dm_ref[...].astype(BF16), preferred_element_type=F32)
        delta, mn, vn = _adam_math(w_ref[...], g, m_ref[...], v_ref[...])
        g_ref[...] = g
        d_ref[...] = delta
        mo_ref[...] = mn
        vo_ref[...] = vn

    spec = pl.BlockSpec((None, tr, tc), lambda l, i, j: (l, i, j))
    return pl.pallas_call(
        body,
        out_shape=[_sds((nl, d, cols), F32)] * 4,
        grid=(nl, d // tr, cols // tc),
        in_specs=[pl.BlockSpec((tr, nex), lambda l, i, j: (i, 0)),
                  pl.BlockSpec((None, nex, tc), lambda l, i, j: (l, 0, j)), spec, spec, spec],
        out_specs=[spec] * 4,
        compiler_params=_cp(("parallel", "parallel", "parallel")),
        name="wmod_grad_adam",
    )(c_t, dmod, w, m, v)


def _my_pos():
    return lax.axis_index("x"), lax.axis_index("y"), lax.axis_index("c")


def _ag8(x4, select_half, name):
    a, s, r, c = x4.shape
    assert s == (2 if select_half else 1)

    def body(x_ref, out_ref, send_sems, recv_sems, local_sem):
        x, y, cc = _my_pos()
        me, sibling = (x, y, cc), (x, y, 1 - cc)
        chips = [(1 - x, y), (x, 1 - y), (1 - x, 1 - y)]
        src_mine = x_ref.at[:, pl.ds(cc if select_half else 0, 1)]

        def blk(px, py, pc):
            return out_ref.at[:, pl.ds(4 * px + 2 * py + pc, 1)]

        def copy(k, block, to, src=None):
            return pltpu.make_async_remote_copy(
                src_ref=blk(*block) if src is None else src, dst_ref=blk(*block),
                send_sem=send_sems.at[k], recv_sem=recv_sems.at[k], device_id=to, device_id_type=MESH)

        mine = pltpu.make_async_copy(src_mine, blk(*me), local_sem)
        mine.start()
        first = [copy(0, me, sibling, src=src_mine)]
        first += [copy(1 + j, me, (*chip, cc), src=src_mine) for j, chip in enumerate(chips)]
        for cp in first:
            cp.start()
        passed = [copy(4 + j, (*chip, cc), sibling) for j, chip in enumerate(chips)]
        for j, chip in enumerate(chips):
            copy(1 + j, (*chip, cc), me).wait_recv()
            passed[j].start()
        copy(0, sibling, me).wait_recv()
        for j, chip in enumerate(chips):
            copy(4 + j, (*chip, 1 - cc), me).wait_recv()
        for cp in first + passed:
            cp.wait_send()
        mine.wait()

    return pl.pallas_call(
        body,
        out_shape=_sds((a, N_DEV, r, c), x4.dtype),
        in_specs=[ANY],
        out_specs=ANY,
        scratch_shapes=[pltpu.SemaphoreType.DMA((7,)), pltpu.SemaphoreType.DMA((7,)), pltpu.SemaphoreType.DMA],
        name=name,
    )(x4)


def _chip_of(x, y, k):
    return (1 - x if k & 2 else x), (1 - y if k & 1 else y)


HBM = pl.BlockSpec(memory_space=pltpu.HBM)
SEM = pl.BlockSpec(memory_space=pltpu.SEMAPHORE)
DATAFLOW = pltpu.SideEffectType.DATAFLOW_SIDE_EFFECTING


def _own_block(k, chip, cc):
    del k
    return 2 * chip + cc


def _distance_slot(k, chip, cc):
    del chip, cc
    return k - 1


def _ici_copies(srcs, dsts, slot, send_sems, recv_sems):
    x, y, cc = _my_pos()
    chip = 2 * x + y
    copies = []
    for n, (s_ref, d_ref) in enumerate(zip(srcs, dsts)):
        for k in (1, 2, 3):
            px, py = _chip_of(x, y, k)
            at = slot(k, chip, cc)
            copies.append(pltpu.make_async_remote_copy(
                src_ref=s_ref.at[pl.ds(at, 1)], dst_ref=d_ref.at[pl.ds(at, 1)],
                send_sem=send_sems.at[3 * n + k - 1], recv_sem=recv_sems.at[3 * n + k - 1],
                device_id=(px, py, cc), device_id_type=MESH))
    return copies


def _ici_start(srcs, lands, slot, after, name):
    n = len(srcs)
    arrays = list(srcs) + ([] if lands is None else list(lands))
    na = len(arrays)

    def body(*refs):
        s_refs = refs[:n]
        d_refs = s_refs if lands is None else refs[n:na]
        send_sems, recv_sems, token = refs[na + 1], refs[na + 2], refs[-1]
        for cp in _ici_copies(s_refs, d_refs, slot, send_sems, recv_sems):
            cp.start()
        token[...] = jnp.zeros_like(token)

    outs = pl.pallas_call(
        body,
        name=name,
        out_shape=(pltpu.SemaphoreType.DMA((3 * n,)), pltpu.SemaphoreType.DMA((3 * n,)),
                   *[pltpu.HBM(a.shape, a.dtype) for a in arrays], _sds((SUBLANE, LANE), F32)),
        in_specs=[HBM] * na + [ANY],
        out_specs=(SEM, SEM, *([HBM] * na), pl.BlockSpec(memory_space=pltpu.VMEM)),
        input_output_aliases={i: 2 + i for i in range(na)},
        compiler_params=pltpu.CompilerParams(has_side_effects=DATAFLOW),
    )(*[pltpu.with_memory_space_constraint(a, pltpu.HBM) for a in arrays], after)
    return outs[0], outs[1], list(outs[2:2 + na]), outs[-1]


def _ici_wait(send_sems, recv_sems, arrays, n, shared, slot, after, name):
    na = len(arrays)

    def body(*refs):
        s_refs = refs[:n]
        d_refs = s_refs if shared else refs[n:na]
        for cp in _ici_copies(s_refs, d_refs, slot, refs[na], refs[na + 1]):
            cp.wait_send()
            cp.wait_recv()

    outs = pl.pallas_call(
        body,
        name=name,
        out_shape=tuple(pltpu.HBM(a.shape, a.dtype) for a in arrays),
        in_specs=[HBM] * na + [SEM, SEM, ANY],
        out_specs=tuple([HBM] * na),
        input_output_aliases={i: i for i in range(na)},
        compiler_params=pltpu.CompilerParams(has_side_effects=DATAFLOW),
    )(*arrays, send_sems, recv_sems, after)
    return list(outs)


def _my_chip():
    return 2 * lax.axis_index("x") + lax.axis_index("y")


def _cast_own(w, layer, name):
    _, r, cols = w.shape
    tr = _tile(r, 512, 2 * SUBLANE)
    tc = _tile(cols, 1024)

    def body(w_ref, o_ref):
        o_ref[...] = w_ref[...].astype(o_ref.dtype)

    return pl.pallas_call(
        body,
        out_shape=_sds((N_CHIP, r, cols), BF16),
        grid=(r // tr, cols // tc),
        in_specs=[pl.BlockSpec((None, tr, tc), lambda i, j: (layer, i, j))],
        out_specs=pl.BlockSpec((None, tr, tc), lambda i, j: (_my_chip(), i, j)),
        compiler_params=_cp(("parallel", "parallel")),
        name=name,
    )(w)


def _forward_halves(bufs, name):
    n = len(bufs)

    def body(*refs):
        ins, outs = refs[:n], refs[n:2 * n]
        send_sems, recv_sems = refs[2 * n], refs[2 * n + 1]
        x, y, cc = _my_pos()
        chip = 2 * x + y
        copies = []
        for i in range(n):
            for k in (1, 2, 3):
                at = 2 * (chip ^ k) + cc
                copies.append(pltpu.make_async_remote_copy(
                    src_ref=ins[i].at[pl.ds(at, 1)], dst_ref=outs[i].at[pl.ds(at, 1)],
                    send_sem=send_sems.at[3 * i + k - 1], recv_sem=recv_sems.at[3 * i + k - 1],
                    device_id=(x, y, 1 - cc), device_id_type=MESH))
        for cp in copies:
            cp.start()
        for cp in copies:
            cp.wait()

    return pl.pallas_call(
        body,
        out_shape=[_sds(b.shape, b.dtype) for b in bufs],
        in_specs=[ANY] * n,
        out_specs=[ANY] * n,
        scratch_shapes=[pltpu.SemaphoreType.DMA((3 * n,)), pltpu.SemaphoreType.DMA((3 * n,))],
        input_output_aliases={i: i for i in range(n)},
        name=name,
    )(*bufs)


def _rs_sibling(g8s, name):
    n = len(g8s)
    g4s = [g.reshape(N_CHIP, 2, g.shape[1], g.shape[2]) for g in g8s]

    def body(*refs):
        ins, outs = refs[:n], refs[n:2 * n]
        send_sems, recv_sems = refs[2 * n], refs[2 * n + 1]
        x, y, cc = _my_pos()
        copies = [pltpu.make_async_remote_copy(
            src_ref=ins[i].at[:, pl.ds(1 - cc, 1)], dst_ref=outs[i], send_sem=send_sems.at[i],
            recv_sem=recv_sems.at[i], device_id=(x, y, 1 - cc), device_id_type=MESH) for i in range(n)]
        for cp in copies:
            cp.start()
        for cp in copies:
            cp.wait()

    return pl.pallas_call(
        body,
        out_shape=[_sds((N_CHIP, 1, g.shape[2], g.shape[3]), g.dtype) for g in g4s],
        in_specs=[ANY] * n,
        out_specs=[ANY] * n,
        scratch_shapes=[pltpu.SemaphoreType.DMA((n,)), pltpu.SemaphoreType.DMA((n,))],
        name=name,
    )(*g4s)


def _share_halves(halves, name):
    def body(in_ref, out_ref, send_sem, recv_sem):
        x, y, cc = _my_pos()
        cp = pltpu.make_async_remote_copy(
            src_ref=in_ref.at[:, pl.ds(cc, 1)], dst_ref=out_ref.at[:, pl.ds(cc, 1)], send_sem=send_sem,
            recv_sem=recv_sem, device_id=(x, y, 1 - cc), device_id_type=MESH)
        cp.start()
        cp.wait()

    return pl.pallas_call(
        body,
        out_shape=_sds(halves.shape, halves.dtype),
        in_specs=[ANY],
        out_specs=ANY,
        scratch_shapes=[pltpu.SemaphoreType.DMA, pltpu.SemaphoreType.DMA],
        input_output_aliases={0: 0},
        name=name,
    )(halves)


def _rs_add_remote(g8, recv_a, name):
    _, r, c = g8.shape
    ra = recv_a.reshape(N_CHIP, r, c)
    tr = _tile(r, 512, SUBLANE)
    tc = _tile(c, 1024)

    def body(g_ref, r_ref, o_ref):
        o_ref[...] = (g_ref[...] + r_ref[...]).astype(o_ref.dtype)

    return pl.pallas_call(
        body,
        out_shape=_sds((3, r, c), BF16),
        grid=(3, r // tr, c // tc),
        in_specs=[pl.BlockSpec((None, tr, tc),
                               lambda k, i, j: (2 * (_my_chip() ^ (k + 1)) + lax.axis_index("c"), i, j)),
                  pl.BlockSpec((None, tr, tc), lambda k, i, j: (_my_chip() ^ (k + 1), i, j))],
        out_specs=pl.BlockSpec((None, tr, tc), lambda k, i, j: (k, i, j)),
        compiler_params=_cp(("parallel",) * 3),
        name=name,
    )(g8, ra)


def _rs_add_final(g8, recv_a, recv_b, out_buf, layer, name):
    _, r, c = g8.shape
    ra = recv_a.reshape(N_CHIP, r, c)
    tr = _tile(r, 512, SUBLANE)
    tc = _tile(c, 1024)

    def body(g_ref, r_ref, b0_ref, b1_ref, b2_ref, buf_ref, o_ref):
        del buf_ref
        o_ref[...] = (((g_ref[...] + r_ref[...]) + b0_ref[...].astype(F32)) + b1_ref[...].astype(F32)
                      ) + b2_ref[...].astype(F32)

    def bspec(k):
        return pl.BlockSpec((None, tr, tc), functools.partial(lambda i, j, k: (k, i, j), k=k))

    return pl.pallas_call(
        body,
        out_shape=_sds(out_buf.shape, F32),
        grid=(r // tr, c // tc),
        in_specs=[pl.BlockSpec((None, tr, tc), lambda i, j: (2 * _my_chip() + lax.axis_index("c"), i, j)),
                  pl.BlockSpec((None, tr, tc), lambda i, j: (_my_chip(), i, j)),
                  bspec(0), bspec(1), bspec(2), ANY],
        out_specs=pl.BlockSpec((None, None, tr, tc), lambda i, j: (layer, lax.axis_index("c"), i, j)),
        input_output_aliases={5: 0},
        compiler_params=_cp(("parallel",) * 2),
        name=name,
    )(g8, ra, recv_b, recv_b, recv_b, out_buf)


def _sum8(x8, name):
    _, r, c = x8.shape
    tr = _tile(r, 256, SUBLANE)

    def body(x_ref, o_ref):
        acc = x_ref[0]
        for b in range(1, N_DEV):
            acc = acc + x_ref[b]
        o_ref[...] = acc

    return pl.pallas_call(
        body,
        out_shape=_sds((r, c), F32),
        grid=(r // tr,),
        in_specs=[pl.BlockSpec((N_DEV, tr, c), lambda i: (0, i, 0))],
        out_specs=pl.BlockSpec((tr, c), lambda i: (i, 0)),
        compiler_params=_cp(("parallel",)),
        name=name,
    )(x8)


def _block_diag(t):
    g, p, q = t.shape
    eye = jnp.eye(g, dtype=t.dtype)
    return (t[:, :, None, :] * eye[:, None, :, None]).reshape(g * p, g * q)


def _diag_blocks(mat, g):
    p, q = mat.shape[0] // g, mat.shape[1] // g
    eye = jnp.eye(g, dtype=mat.dtype)
    return jnp.sum(mat.reshape(g, p, g, q) * eye[:, None, :, None], axis=2)


def _interleave(re, im, c):
    lead = re.shape[:-1]
    gn = re.shape[-1]
    return jnp.stack([re.reshape(*lead, gn // c, c), im.reshape(*lead, gn // c, c)], axis=-2).reshape(*lead, 2 * gn)


def _deinterleave(cat, c):
    lead = cat.shape[:-1]
    gn = cat.shape[-1] // 2
    t = cat.reshape(*lead, gn // c, 2, c)
    return t[..., 0, :].reshape(*lead, gn), t[..., 1, :].reshape(*lead, gn)


def kernel(x, c, w_mod, b_mod, g_pre_mix, g_post_mix, g_pre_ffn, g_post_ffn, w_in, ssm_log_dt, ssm_a_re, ssm_a_im, ssm_b_re, ssm_b_im, ssm_c_re, ssm_c_im, ssm_d, w_glu, b_glu, conv_mix_w, w_ssm_out, w_attn_out, w_conv_out, b_gate, w_o, w_up, ffn_conv_w, w_down, loss_target, m_w_mod, m_b_mod, m_g_pre_mix, m_g_post_mix, m_g_pre_ffn, m_g_post_ffn, m_w_in, m_ssm_log_dt, m_ssm_a_re, m_ssm_a_im, m_ssm_b_re, m_ssm_b_im, m_ssm_c_re, m_ssm_c_im, m_ssm_d, m_w_glu, m_b_glu, m_conv_mix_w, m_w_ssm_out, m_w_attn_out, m_w_conv_out, m_b_gate, m_w_o, m_w_up, m_ffn_conv_w, m_w_down, v_w_mod, v_b_mod, v_g_pre_mix, v_g_post_mix, v_g_pre_ffn, v_g_post_ffn, v_w_in, v_ssm_log_dt, v_ssm_a_re, v_ssm_a_im, v_ssm_b_re, v_ssm_b_im, v_ssm_c_re, v_ssm_c_im, v_ssm_d, v_w_glu, v_b_glu, v_conv_mix_w, v_w_ssm_out, v_w_attn_out, v_w_conv_out, v_b_gate, v_w_o, v_w_up, v_ffn_conv_w, v_w_down):
    weights = dict(w_mod=w_mod, b_mod=b_mod, g_pre_mix=g_pre_mix, g_post_mix=g_post_mix, g_pre_ffn=g_pre_ffn, g_post_ffn=g_post_ffn, w_in=w_in, ssm_log_dt=ssm_log_dt, ssm_a_re=ssm_a_re, ssm_a_im=ssm_a_im, ssm_b_re=ssm_b_re, ssm_b_im=ssm_b_im, ssm_c_re=ssm_c_re, ssm_c_im=ssm_c_im, ssm_d=ssm_d, w_glu=w_glu, b_glu=b_glu, conv_mix_w=conv_mix_w, w_ssm_out=w_ssm_out, w_attn_out=w_attn_out, w_conv_out=w_conv_out, b_gate=b_gate, w_o=w_o, w_up=w_up, ffn_conv_w=ffn_conv_w, w_down=w_down)
    mom_m = dict(w_mod=m_w_mod, b_mod=m_b_mod, g_pre_mix=m_g_pre_mix, g_post_mix=m_g_post_mix, g_pre_ffn=m_g_pre_ffn, g_post_ffn=m_g_post_ffn, w_in=m_w_in, ssm_log_dt=m_ssm_log_dt, ssm_a_re=m_ssm_a_re, ssm_a_im=m_ssm_a_im, ssm_b_re=m_ssm_b_re, ssm_b_im=m_ssm_b_im, ssm_c_re=m_ssm_c_re, ssm_c_im=m_ssm_c_im, ssm_d=m_ssm_d, w_glu=m_w_glu, b_glu=m_b_glu, conv_mix_w=m_conv_mix_w, w_ssm_out=m_w_ssm_out, w_attn_out=m_w_attn_out, w_conv_out=m_w_conv_out, b_gate=m_b_gate, w_o=m_w_o, w_up=m_w_up, ffn_conv_w=m_ffn_conv_w, w_down=m_w_down)
    mom_v = dict(w_mod=v_w_mod, b_mod=v_b_mod, g_pre_mix=v_g_pre_mix, g_post_mix=v_g_post_mix, g_pre_ffn=v_g_pre_ffn, g_post_ffn=v_g_post_ffn, w_in=v_w_in, ssm_log_dt=v_ssm_log_dt, ssm_a_re=v_ssm_a_re, ssm_a_im=v_ssm_a_im, ssm_b_re=v_ssm_b_re, ssm_b_im=v_ssm_b_im, ssm_c_re=v_ssm_c_re, ssm_c_im=v_ssm_c_im, ssm_d=v_ssm_d, w_glu=v_w_glu, b_glu=v_b_glu, conv_mix_w=v_conv_mix_w, w_ssm_out=v_w_ssm_out, w_attn_out=v_w_attn_out, w_conv_out=v_w_conv_out, b_gate=v_b_gate, w_o=v_w_o, w_up=v_w_up, ffn_conv_w=v_ffn_conv_w, w_down=v_w_down)
    names = list(weights)

    nl = w_in.shape[0]
    seq, d = x.shape[1], x.shape[2]
    sw = d // 4
    groups = sw // SSM_GROUP
    gn = groups * SSM_STATE
    hp = sw // HEAD_DIM
    qw = 3 * sw
    off_q, off_k, off_v = sw, sw + qw, sw + 2 * qw
    off_conv = sw + 3 * qw
    off_gate = off_conv + 3 * sw
    n_in = off_gate + 3 * d
    f = w_down.shape[1] * N_CHIP
    scan_c = _scan_cols(gn)
    assert seq % (ATTN_BLOCK * DSWA_PATTERNS[-1][1]) == 0 and all(w // dl == ATTN_BLOCK for w, dl in DSWA_PATTERNS)

    px, py, pc = _my_pos()
    chip = 2 * px + py
    dev = 2 * chip + pc

    x2 = x.reshape(seq, d)
    target2 = loss_target.reshape(seq, d)

    mix_keys = ("w_in", "w_glu", "w_ssm_out", "w_attn_out", "w_conv_out", "w_o")
    ffn_keys = ("w_up", "w_down")
    col_sharded = ("w_in", "w_ssm_out", "w_attn_out", "w_conv_out", "w_up")
    n_stages = 2 * nl

    def stage_keys(stage):
        return ffn_keys if stage % 2 else mix_keys

    def blocked(k, buf8):
        r, cols = weights[k].shape[1:]
        return buf8.reshape(1, N_CHIP, r, cols) if k in col_sharded else buf8.reshape(1, 1, N_CHIP * r, cols)

    def begin_gather(stage, after):
        keys = stage_keys(stage)
        bufs = []
        for k in keys:
            r, cols = weights[k].shape[1:]
            bufs.append(_cast_own(weights[k], stage // 2, "cast_own").reshape(N_DEV, r // 2, cols))
        return _ici_start(bufs, None, _own_block, after, f"gather_start_{stage}")

    def end_gather(stage, pending, after):
        send_sems, recv_sems, bufs, _ = pending
        bufs = _ici_wait(send_sems, recv_sems, bufs, len(bufs), True, _own_block, after, f"gather_wait_{stage}")
        bufs = _forward_halves(bufs, "gather_forward")
        return {k: blocked(k, b) for k, b in zip(stage_keys(stage), bufs)}

    c_all = _ag8(c.reshape(1, 1, 1, d), False, "ag_cond").reshape(N_DEV, d)
    c_pad = jnp.concatenate([c_all, jnp.zeros((SUBLANE, d), F32)], axis=0)
    mcols = w_mod.shape[2]
    w_mod4 = w_mod.reshape(nl, 1, d, mcols)
    mod_loc = jnp.stack([_mm_nn(c_pad, w_mod4, l, name="mod_fwd", a_fn=jax.nn.silu) for l in range(nl)])
    mod_all = _ag8(mod_loc.reshape(nl, 1, 2 * SUBLANE, mcols), False, "ag_mod")
    mod_rows = lax.dynamic_slice_in_dim(mod_all[:, 0::2], dev, 1, axis=2)
    mod = mod_rows.reshape(nl, N_CHIP * mcols) + b_mod
    mods = mod.reshape(nl, 6, 1, d)

    taps = jnp.concatenate([conv_mix_w.reshape(-1), ffn_conv_w.reshape(-1)])
    tap_w = 8 * LANE
    tap_r = -(-taps.shape[0] // (tap_w * SUBLANE)) * SUBLANE
    taps = jnp.concatenate([taps, jnp.zeros((tap_r * tap_w - taps.shape[0],), F32)])
    taps_all = _ag8(taps.reshape(1, 1, tap_r, tap_w), False, "ag_small_weight")[0, 0::2].reshape(N_CHIP, -1)

    def whole(w, at):
        _, r, cols = w.shape
        got = taps_all[:, at:at + w.size].reshape(N_CHIP, nl, r, cols)
        return got.transpose(1, 2, 0, 3).reshape(nl, r, N_CHIP * cols)

    conv_w_full = whole(conv_mix_w, 0)
    ffn_w_full = whole(ffn_conv_w, conv_mix_w.size)
    wfs = [dict() for _ in range(nl)]
    pendings, order = [], mods[0, 0, :, :1] + taps_all[:1, :1]
    for stage in range(n_stages):
        pendings.append(begin_gather(stage, order))
        order = pendings[-1][3]
    wfs[0].update(end_gather(0, pendings[0], mods))

    head_ones = jnp.asarray(np.kron(np.eye(hp, dtype=np.float32), np.ones((HEAD_DIM, HEAD_DIM), np.float32)))
    slopes = [_alibi_slopes(p, hp) for p in range(len(DSWA_PATTERNS))]
    qkv_offs = (off_q, off_k, off_v)

    def row(v):
        return v.reshape(1, -1)

    saved = []
    xl = x2
    for l in range(nl):
        sh1, sc1, gt1, sh2, sc2, gt2 = [mods[l, q] for q in range(6)]
        s = dict(x_in=xl)
        wf = wfs[l]
        (h1,) = _ew_fwd(_fn_norm_mod, [(xl, 0, d)], [row(g_pre_mix[l]), sc1, sh1], [BF16], name="norm_mod_fwd", width=d)
        proj = _mm_nn(h1, wf["w_in"], 0, name="w_in_fwd")
        br_t = jnp.transpose(ssm_b_re[l], (2, 0, 1))
        bi_t = jnp.transpose(ssm_b_im[l], (2, 0, 1))
        disc_in = (ssm_log_dt[l].reshape(groups, 1), ssm_a_re[l], ssm_a_im[l], br_t, bi_t)
        lr, li, bbr_t, bbi_t = _ssm_disc_fwd(*disc_in)
        lam = _interleave(lr.reshape(1, gn), li.reshape(1, gn), scan_c)
        bcat = _interleave(_block_diag(jnp.transpose(bbr_t, (1, 0, 2))), _block_diag(jnp.transpose(bbi_t, (1, 0, 2))),
                           scan_c).astype(BF16).reshape(1, 1, sw, 2 * gn)
        cre = _block_diag(jnp.transpose(ssm_c_re[l], (0, 2, 1)))
        cim = _block_diag(jnp.transpose(ssm_c_im[l], (0, 2, 1)))
        ccat = jnp.transpose(_interleave(cre.T, -cim.T, scan_c)).astype(BF16).reshape(1, 1, 2 * gn, sw)
        xcat = _mm_nn(proj, bcat, 0, name="ssm_b_fwd", k_dim=sw)
        hcat = _ssm_scan_fwd(xcat, lam)
        y_ssm_pre = _mm_nn(hcat, ccat, 0, name="ssm_c_fwd")
        (gact,) = _ew_fwd(_fn_gelu, [(y_ssm_pre, 0, sw), (proj, 0, sw)], [row(ssm_d[l])], [F32], name="gelu_fwd", width=sw,
                          tw=_tile(sw, 512))
        z = _mm_nn(gact, wf["w_glu"], 0, name="w_glu_fwd")
        (s_ssm,) = _ew_fwd(_fn_glu, [(gact, 0, sw), (z, 0, sw)], [row(b_glu[l])], [BF16], name="glu_fwd", width=sw,
                           tw=_tile(sw, 512))
        y_ssm = _mm_nn(s_ssm, wf["w_ssm_out"], 0, name="w_branch_out_fwd")
        outs, lses = [], []
        for p, (_, dl) in enumerate(DSWA_PATTERNS):
            o_p, lse_p = _attn_fwd(proj, qkv_offs, p, dl, sw, slopes[p])
            outs.append(o_p)
            lses.append(lse_p)
        o_attn, lse_attn = _attn_merge(outs, lses)
        y_attn = _mm_nn(o_attn, wf["w_attn_out"], 0, name="w_branch_out_fwd")
        cv = _gconv_fwd(proj, off_conv, sw, conv_w_full[l])
        y_conv = _mm_nn(cv, wf["w_conv_out"], 0, name="w_branch_out_fwd")
        bg = b_gate[l].reshape(3, 1, d)
        gate_xs = [(proj, off_gate + q * d, d) for q in range(3)] + [(y_ssm, 0, d), (y_attn, 0, d), (y_conv, 0, d)]
        (merged,) = _ew_fwd(_fn_gates, gate_xs, [bg[0], bg[1], bg[2]], [BF16], name="gates_fwd", width=d,
                            tw=_tile(sw, 512))
        y_mix = _mm_nn(merged, wf["w_o"], 0, name="w_o_fwd")
        (x_mid,) = _ew_fwd(_fn_residual, [(xl, 0, d), (y_mix, 0, d)], [gt1, row(g_post_mix[l])], [F32], name="residual_fwd",
                           width=d)
        wf.update(end_gather(2 * l + 1, pendings[2 * l + 1], x_mid))
        (h2,) = _ew_fwd(_fn_norm_mod, [(x_mid, 0, d)], [row(g_pre_ffn[l]), sc2, sh2], [BF16], name="norm_mod_fwd", width=d)
        up = _mm_nn(h2, wf["w_up"], 0, name="w_up_fwd")
        act = _ffn_act_fwd(up, ffn_w_full[l])
        y_ffn = _mm_nn(act, wf["w_down"], 0, name="w_down_fwd")
        (x_out,) = _ew_fwd(_fn_residual, [(x_mid, 0, d), (y_ffn, 0, d)], [gt2, row(g_post_ffn[l])], [F32],
                           name="residual_fwd", width=d)
        if l + 1 < nl:
            wfs[l + 1].update(end_gather(2 * l + 2, pendings[2 * l + 2], x_out))
        s.update(h1=h1, proj=proj, disc_in=disc_in, lam=lam, bcat=bcat, ccat=ccat, hcat=hcat, y_ssm_pre=y_ssm_pre,
                 gact=gact, z=z, s_ssm=s_ssm, y_ssm=y_ssm, o_attn=o_attn, lse_attn=lse_attn, y_attn=y_attn,
                 cv=cv, y_conv=y_conv, merged=merged, y_mix=y_mix, x_mid=x_mid, h2=h2, up=up, act=act, y_ffn=y_ffn)
        saved.append(s)
        xl = x_out

    dxl, loss_local = _loss_fwd_bwd(xl, target2)
    loss = lax.psum(loss_local, ("x", "y", "c"))

    gfin = {k: lax.empty((nl, 2, weights[k].shape[1] // 2, weights[k].shape[2]), F32) for k in mix_keys + ffn_keys}

    def begin_rs(stage, grads_4d):
        keys = stage_keys(stage)
        g8s = [grads_4d[k].reshape(N_DEV, weights[k].shape[1] // 2, weights[k].shape[2]) for k in keys]
        recv_a = _rs_sibling(g8s, "rs_sibling")
        s_rem = [_rs_add_remote(g, ra, "rs_add_remote") for g, ra in zip(g8s, recv_a)]
        lands = [lax.empty(t.shape, BF16) for t in s_rem]
        return stage, g8s, recv_a, _ici_start(s_rem, lands, _distance_slot, recv_a[0], f"rs_start_{stage}")

    def end_rs(pending_rs, after):
        stage, g8s, recv_a, (send_sems, recv_sems, arrays, _) = pending_rs
        n = len(g8s)
        arrays = _ici_wait(send_sems, recv_sems, arrays, n, False, _distance_slot, after, f"rs_wait_{stage}")
        for k, g8, ra, rb in zip(stage_keys(stage), g8s, recv_a, arrays[n:]):
            gfin[k] = _rs_add_final(g8, ra, rb, gfin[k], stage // 2, "rs_add_final")

    def after_rs_start(v, pending_rs):
        return v if pending_rs is None else v + pending_rs[3][3][0, 0]

    def grad_buf(k):
        return lax.empty(wf[k].shape, F32)

    pending_rs = None
    small = {k: [None] * nl for k in ("g_pre_mix", "g_post_mix", "g_pre_ffn", "g_post_ffn", "ssm_log_dt", "ssm_a_re",
                                      "ssm_a_im", "ssm_b_re", "ssm_b_im", "ssm_c_re", "ssm_c_im", "ssm_d", "b_glu",
                                      "conv_mix_w", "b_gate", "ffn_conv_w", "dmod")}
    for l in reversed(range(nl)):
        s = saved[l]
        sh1, sc1, gt1, sh2, sc2, gt2 = [mods[l, q] for q in range(6)]
        proj = s["proj"]
        wf = wfs[l]
        gw = {}
        (dy_ffn,), (dgt2, dg_post_ffn) = _ew_bwd(
            _fn_residual, [(s["x_mid"], 0, d), (s["y_ffn"], 0, d)], [after_rs_start(gt2, pending_rs), row(g_post_ffn[l])],
            [[dxl]], [None, BF16], name="residual_bwd", width=d)
        dact = _mm_nt(dy_ffn, wf["w_down"], 0, name="w_down_bwd_x", out_dtype=BF16)
        gw["w_down"] = _mm_tn(s["act"], dy_ffn, grad_buf("w_down"), 0, name="w_down_bwd_w")
        dup_a, dup_b, dwa, dwb = _ffn_act_bwd(s["up"], ffn_w_full[l], dact)
        dup = jnp.concatenate([dup_a, dup_b], axis=1)
        small["ffn_conv_w"][l] = jnp.concatenate([dwa, dwb], axis=1)
        dh2 = _mm_nt(dup, wf["w_up"], 0, name="w_up_bwd_x")
        gw["w_up"] = _mm_tn(s["h2"], dup, grad_buf("w_up"), 0, name="w_up_bwd_w")
        (dx_mid,), (dg_pre_ffn, dsc2, dsh2) = _ew_bwd(
            _fn_norm_mod, [(s["x_mid"], 0, d)], [row(g_pre_ffn[l]), sc2, sh2], [[dh2]], [F32], name="norm_mod_bwd", width=d,
            dx_add={0: dxl})
        if pending_rs is not None:
            end_rs(pending_rs, dx_mid)
        pending_rs = begin_rs(2 * l + 1, gw)
        (dy_mix,), (dgt1, dg_post_mix) = _ew_bwd(
            _fn_residual, [(s["x_in"], 0, d), (s["y_mix"], 0, d)], [after_rs_start(gt1, pending_rs), row(g_post_mix[l])],
            [[dx_mid]], [None, BF16], name="residual_bwd", width=d)
        dmerged = _mm_nt(dy_mix, wf["w_o"], 0, name="w_o_bwd_x")
        gw["w_o"] = _mm_tn(s["merged"], dy_mix, grad_buf("w_o"), 0, name="w_o_bwd_w")
        bg = b_gate[l].reshape(3, 1, d)
        gate_xs = [(proj, off_gate + q * d, d) for q in range(3)] + [(s["y_ssm"], 0, d), (s["y_attn"], 0, d),
                                                                     (s["y_conv"], 0, d)]
        (dp0, dp1, dp2, dy_ssm, dy_attn, dy_conv), dbg = _ew_bwd(
            _fn_gates, gate_xs, [bg[0], bg[1], bg[2]], [[dmerged]], [BF16] * 6, name="gates_bwd", width=d,
            tw=_tile(sw, 512))
        small["b_gate"][l] = jnp.concatenate(dbg, axis=1)[0]
        ds_ssm = _mm_nt(dy_ssm, wf["w_ssm_out"], 0, name="w_branch_out_bwd_x")
        gw["w_ssm_out"] = _mm_tn(s["s_ssm"], dy_ssm, grad_buf("w_ssm_out"), 0, name="w_branch_out_bwd_w")
        (dg1, dz), (db_glu,) = _ew_bwd(_fn_glu, [(s["gact"], 0, sw), (s["z"], 0, sw)], [row(b_glu[l])], [[ds_ssm]],
                                       [F32, BF16], name="glu_bwd", width=sw, tw=_tile(sw, 512))
        dg2 = _mm_nt(dz, wf["w_glu"], 0, name="w_glu_bwd_x")
        gw["w_glu"] = _mm_tn(s["gact"], dz, grad_buf("w_glu"), 0, name="w_glu_bwd_w")
        (dy_pre, du_skip), (dd_skip,) = _ew_bwd(_fn_gelu, [(s["y_ssm_pre"], 0, sw), (proj, 0, sw)], [row(ssm_d[l])],
                                               [[dg1, dg2]], [BF16, F32], name="gelu_bwd", width=sw, tw=_tile(sw, 512))
        dhcat = _mm_nt(dy_pre, s["ccat"], 0, name="ssm_c_bwd_x")
        dccat = _mm_tn(s["hcat"], dy_pre, lax.empty((1, 1, 2 * gn, sw), F32), 0, name="ssm_c_bwd_w")[0, 0]
        gcat, dlam = _ssm_scan_bwd(dhcat, s["hcat"], s["lam"])
        du_b = _mm_nt(gcat, s["bcat"], 0, name="ssm_b_bwd_x")
        dbcat = _mm_tn(proj, gcat, lax.empty((1, 1, sw, 2 * gn), F32), 0, name="ssm_b_bwd_w")[0, 0]
        dlr, dli = _deinterleave(dlam, scan_c)
        dbre, dbim = _deinterleave(dbcat, scan_c)
        dbbr_t = jnp.transpose(_diag_blocks(dbre, groups), (1, 0, 2))
        dbbi_t = jnp.transpose(_diag_blocks(dbim, groups), (1, 0, 2))
        gld, gar, gai, gbr_t, gbi_t = _ssm_disc_bwd(*s["disc_in"], dlr.reshape(groups, SSM_STATE),
                                                    dli.reshape(groups, SSM_STATE), dbbr_t, dbbi_t)
        dcre_t, dcim_t = _deinterleave(dccat.T, scan_c)
        small["ssm_c_re"][l] = _diag_blocks(dcre_t, groups)
        small["ssm_c_im"][l] = -_diag_blocks(dcim_t, groups)
        small["ssm_log_dt"][l] = gld.reshape(groups)
        small["ssm_a_re"][l], small["ssm_a_im"][l] = gar, gai
        small["ssm_b_re"][l] = jnp.transpose(gbr_t, (1, 2, 0))
        small["ssm_b_im"][l] = jnp.transpose(gbi_t, (1, 2, 0))
        small["ssm_d"][l], small["b_glu"][l] = dd_skip[0], db_glu[0]
        du = (du_skip + du_b).astype(BF16)
        do_attn = _mm_nt(dy_attn, wf["w_attn_out"], 0, name="w_branch_out_bwd_x")
        gw["w_attn_out"] = _mm_tn(s["o_attn"], dy_attn, grad_buf("w_attn_out"), 0, name="w_branch_out_bwd_w")
        delta = _attn_delta(do_attn, s["o_attn"], head_ones)
        dqs, dks, dvs = [], [], []
        for p, (_, dl) in enumerate(DSWA_PATTERNS):
            dq_p, dk_p, dv_p = _attn_bwd(proj, do_attn, s["lse_attn"], delta, qkv_offs, p, dl, sw, slopes[p])
            dqs.append(dq_p)
            dks.append(dk_p)
            dvs.append(dv_p)
        dcv = _mm_nt(dy_conv, wf["w_conv_out"], 0, name="w_branch_out_bwd_x", out_dtype=BF16)
        gw["w_conv_out"] = _mm_tn(s["cv"], dy_conv, grad_buf("w_conv_out"), 0, name="w_branch_out_bwd_w")
        dcb, dcc, dch, dconv_w = _gconv_bwd(proj, off_conv, sw, conv_w_full[l], dcv)
        small["conv_mix_w"][l] = dconv_w
        dproj = jnp.concatenate([du] + [t.astype(BF16) for t in dqs + dks + dvs] + [dcb, dcc, dch, dp0, dp1, dp2],
                                axis=1)
        dh1 = _mm_nt(dproj, wf["w_in"], 0, name="w_in_bwd_x")
        gw["w_in"] = _mm_tn(s["h1"], dproj, grad_buf("w_in"), 0, name="w_in_bwd_w")
        (dx_in,), (dg_pre_mix, dsc1, dsh1) = _ew_bwd(
            _fn_norm_mod, [(s["x_in"], 0, d)], [row(g_pre_mix[l]), sc1, sh1], [[dh1]], [F32], name="norm_mod_bwd", width=d,
            dx_add={0: dx_mid})
        end_rs(pending_rs, dx_in)
        pending_rs = begin_rs(2 * l, gw)
        small["g_pre_mix"][l], small["g_post_mix"][l] = dg_pre_mix[0], dg_post_mix[0]
        small["g_pre_ffn"][l], small["g_post_ffn"][l] = dg_pre_ffn[0], dg_post_ffn[0]
        small["dmod"][l] = jnp.concatenate([dsh1, dsc1, dgt1, dsh2, dsc2, dgt2], axis=1)[0]
        dxl = dx_in

    grad_x = dxl.reshape(x.shape)

    small = {k: jnp.stack(v) for k, v in small.items()}
    order = sorted(small)
    flat = jnp.concatenate([small[k].reshape(-1) for k in order])
    n_small = flat.shape[0]
    pack_w = 8 * LANE
    pack_r = -(-n_small // (pack_w * SUBLANE)) * SUBLANE
    flat = jnp.concatenate([flat, jnp.zeros((pack_r * pack_w - n_small,), F32)])
    gathered = _ag8(flat.reshape(1, 1, pack_r, pack_w), False, "ag_small_grads")[0]
    summed = _sum8(gathered, "sum_small_grads").reshape(-1)
    sgrad, at = {}, 0
    for k in order:
        size = small[k].size
        sgrad[k] = summed[at:at + size].reshape(small[k].shape)
        at += size
    dmod_off = sum(small[k].size for k in order[:order.index("dmod")])
    dmod_all = gathered.reshape(N_DEV, -1)[:, dmod_off:dmod_off + nl * 6 * d].reshape(N_DEV, nl, 6 * d)
    dmod_loc = lax.dynamic_slice_in_dim(jnp.transpose(dmod_all, (1, 0, 2)), chip * mcols, mcols, axis=2)

    grads = dict(sgrad)
    grads["b_mod"] = grads.pop("dmod")
    grads["conv_mix_w"] = lax.dynamic_slice_in_dim(sgrad["conv_mix_w"], chip * conv_mix_w.shape[2], conv_mix_w.shape[2], axis=2)
    grads["ffn_conv_w"] = lax.dynamic_slice_in_dim(sgrad["ffn_conv_w"], chip * ffn_conv_w.shape[2], ffn_conv_w.shape[2], axis=2)

    end_rs(pending_rs, summed)
    for k in mix_keys + ffn_keys:
        grads[k] = _share_halves(gfin[k], "rs_share").reshape(weights[k].shape)

    delta_w, new_m, new_v = {}, {}, {}
    c_t = jnp.pad(jnp.transpose(c_all), ((0, 0), (0, LANE - N_DEV)))
    dmod_pad = jnp.pad(dmod_loc, ((0, 0), (0, LANE - N_DEV), (0, 0)))
    grads["w_mod"], delta_w["w_mod"], new_m["w_mod"], new_v["w_mod"] = _wmod_grad_adam(
        c_t, dmod_pad, w_mod, m_w_mod, v_w_mod)
    for k in names:
        if k == "w_mod":
            continue
        delta_w[k], new_m[k], new_v[k] = _adam(weights[k], grads[k], mom_m[k], mom_v[k], "adamw")

    return (loss, grad_x, *[grads[k] for k in names], *[delta_w[k] for k in names], *[new_m[k] for k in names],
            *[new_v[k] for k in names])
```

```python
import functools
import math

import numpy as np
import jax
import jax.numpy as jnp
from jax import lax
from jax.experimental import pallas as pl
from jax.experimental.pallas import tpu as pltpu

F32 = jnp.float32
BF16 = jnp.bfloat16
MESH = pl.DeviceIdType.MESH
ANY = pl.BlockSpec(memory_space=pl.ANY)

VMEM_LIMIT_BYTES = 48 * 1024 * 1024
LANE = 128
SUBLANE = 8

RMS_EPS = 1e-6
NEG_INF = -1e30
SSM_GROUP = 16
SSM_STATE = 64
HEAD_DIM = 64
DSWA_PATTERNS = ((128, 1), (512, 4), (2048, 16))
ATTN_BLOCK = 128
N_DEV = 8
N_CHIP = 4

ADAM_LR = 0.001
ADAM_B1 = 0.9
ADAM_B2 = 0.999
ADAM_EPS = 1e-08
ADAM_WD = 0.01
ADAM_STEP = 10


def _cp(sem=None):
    return pltpu.CompilerParams(dimension_semantics=sem, vmem_limit_bytes=VMEM_LIMIT_BYTES)


def _tile(n, pref, align=LANE):
    if n <= pref:
        return n
    t = (pref // align) * align
    while t >= align:
        if n % t == 0:
            return t
        t -= align
    return n


def _sds(shape, dtype):
    return jax.ShapeDtypeStruct(tuple(shape), dtype)


MM_VMEM_BUDGET = 34 * 1024 * 1024
MM_MAX_CONTRACT = 2048


def _divisors(n, align):
    if n % align:
        return [n]
    return [t for t in range(n, 0, -align) if n % t == 0]


def _halvings(n, align, floor=256):
    out = [n]
    while out[-1] % (2 * align) == 0 and out[-1] // 2 >= floor:
        out.append(out[-1] // 2)
    return out


def _pick_tiles(rows, cols, fixed_bytes, row_bytes, col_bytes, cell_bytes):
    best = None
    for tr in rows:
        for tc in cols:
            if fixed_bytes + row_bytes * tr + col_bytes * tc + cell_bytes * tr * tc <= MM_VMEM_BUDGET:
                if best is None or tr * tc > best[0] * best[1]:
                    best = (tr, tc)
                break
    assert best is not None
    return best


def _accumulate(step, n_steps, part, o_ref, acc_ref):
    if n_steps == 1:
        o_ref[...] = part.astype(o_ref.dtype)
        return
    acc = o_ref if acc_ref is None else acc_ref

    @pl.when(step == 0)
    def _():
        acc[...] = part

    @pl.when(step > 0)
    def _():
        acc[...] += part

    if acc_ref is not None:
        @pl.when(step == n_steps - 1)
        def _():
            o_ref[...] = acc_ref[...].astype(o_ref.dtype)


def _mm_nn(a, w, layer, *, name, k_dim=None, a_col0=0, out_dtype=F32, a_fn=None):
    m = a.shape[0]
    _, nb, kw, n = w.shape
    k_dim = kw if k_dim is None else k_dim
    assert k_dim == kw
    tk = _tile(k_dim, MM_MAX_CONTRACT)
    nk = k_dim // tk
    sa, so = a.dtype.itemsize, jnp.dtype(out_dtype).itemsize
    use_acc = nk > 1 and so != 4
    row_bytes = tk * (2 * sa + (2 if sa == 4 else 0) + (4 if a_fn is not None else 0))
    tm, tn = _pick_tiles(_halvings(m, SUBLANE), _divisors(n, LANE), 0, row_bytes, 2 * tk * w.dtype.itemsize,
                         2 * so + 4 + (4 if use_acc else 0))
    assert a_col0 % tk == 0
    npb = n // tn
    a0 = a_col0 // tk

    def body(a_ref, w_ref, o_ref, *scratch):
        av = a_ref[...]
        if a_fn is not None:
            av = a_fn(av.astype(F32))
        part = jnp.dot(av.astype(BF16), w_ref[...].astype(BF16), preferred_element_type=F32)
        _accumulate(pl.program_id(2), nk, part, o_ref, scratch[0] if use_acc else None)

    return pl.pallas_call(
        body,
        out_shape=_sds((m, nb * n), out_dtype),
        grid=(m // tm, nb * npb, nk),
        in_specs=[pl.BlockSpec((tm, tk), lambda i, j, k: (i, a0 + k)),
                  pl.BlockSpec((None, None, tk, tn), lambda i, j, k: (layer, j // npb, k, j % npb))],
        out_specs=pl.BlockSpec((tm, tn), lambda i, j, k: (i, j)),
        scratch_shapes=[pltpu.VMEM((tm, tn), F32)] if use_acc else [],
        compiler_params=_cp(("parallel", "parallel", "arbitrary")),
        name=name,
    )(a, w)


def _mm_nt(g, w, layer, *, name, out_dtype=F32):
    m = g.shape[0]
    _, nb, k_dim, n = w.shape
    assert g.shape[1] == nb * n
    tko = _tile(k_dim, MM_MAX_CONTRACT)
    sg, so = g.dtype.itemsize, jnp.dtype(out_dtype).itemsize
    use_acc = so != 4
    res_row = tko * (2 * so + 4 + (4 if use_acc else 0))
    tm, tc = _pick_tiles(_halvings(m, SUBLANE), _divisors(n, LANE), 0, res_row, 2 * tko * w.dtype.itemsize,
                         2 * sg + (2 if sg == 4 else 0))
    npb = n // tc
    nr = nb * npb
    use_acc = use_acc and nr > 1

    def body(g_ref, w_ref, o_ref, *scratch):
        part = lax.dot_general(g_ref[...].astype(BF16), w_ref[...].astype(BF16),
                               (((1,), (1,)), ((), ())), preferred_element_type=F32)
        _accumulate(pl.program_id(2), nr, part, o_ref, scratch[0] if use_acc else None)

    return pl.pallas_call(
        body,
        out_shape=_sds((m, k_dim), out_dtype),
        grid=(m // tm, k_dim // tko, nr),
        in_specs=[pl.BlockSpec((tm, tc), lambda i, kk, r: (i, r)),
                  pl.BlockSpec((None, None, tko, tc), lambda i, kk, r: (layer, r // npb, kk, r % npb))],
        out_specs=pl.BlockSpec((tm, tko), lambda i, kk, r: (i, kk)),
        scratch_shapes=[pltpu.VMEM((tm, tko), F32)] if use_acc else [],
        compiler_params=_cp(("parallel", "parallel", "arbitrary")),
        name=name,
    )(g, w)


def _mm_tn(a, g, out_buf, layer, *, name, a_col0=0):
    m = a.shape[0]
    _, nb, k_dim, n = out_buf.shape
    assert g.shape == (m, nb * n)
    tm = _tile(m, MM_MAX_CONTRACT, SUBLANE)
    nr = m // tm
    sa, sg = a.dtype.itemsize, g.dtype.itemsize
    tk, tn = _pick_tiles(_halvings(k_dim, LANE), _divisors(n, LANE), 0, tm * (2 * sa + (2 if sa == 4 else 0) + 2),
                         tm * (2 * sg + (2 if sg == 4 else 0)), 2 * 4 + 4)
    assert a_col0 % tk == 0
    a0 = a_col0 // tk
    npb = n // tn

    def body(a_ref, g_ref, buf_ref, o_ref):
        del buf_ref
        part = lax.dot_general(a_ref[...].astype(BF16), g_ref[...].astype(BF16),
                               (((0,), (0,)), ((), ())), preferred_element_type=F32)
        _accumulate(pl.program_id(2), nr, part, o_ref, None)

    return pl.pallas_call(
        body,
        out_shape=_sds(out_buf.shape, F32),
        grid=(k_dim // tk, nb * npb, nr),
        in_specs=[pl.BlockSpec((tm, tk), lambda kk, j, r: (r, a0 + kk)),
                  pl.BlockSpec((tm, tn), lambda kk, j, r: (r, j)),
                  ANY],
        out_specs=pl.BlockSpec((None, None, tk, tn), lambda kk, j, r: (layer, j // npb, kk, j % npb)),
        input_output_aliases={2: 0},
        compiler_params=_cp(("parallel", "parallel", "arbitrary")),
        name=name,
    )(a, g, out_buf)


def _ew_fwd(fn, xs, ps, out_dtypes, *, name, width, tw=None, tm=256):
    rows = xs[0][0].shape[0]
    tm = _tile(rows, tm, SUBLANE)
    tw = width if tw is None else tw
    nx, n_p = len(xs), len(ps)

    def body(*refs):
        xv = [r[...].astype(F32) for r in refs[:nx]]
        pv = [r[...].astype(F32) for r in refs[nx:nx + n_p]]
        outs = fn(*xv, *pv)
        if not isinstance(outs, (tuple, list)):
            outs = (outs,)
        for o_ref, o in zip(refs[nx + n_p:], outs):
            o_ref[...] = o.astype(o_ref.dtype)

    in_specs = []
    for arr, c0, w in xs:
        assert w == width and c0 % tw == 0
        in_specs.append(pl.BlockSpec((tm, tw), functools.partial(lambda i, j, b: (i, b + j), b=c0 // tw)))
    for p in ps:
        assert p.shape == (1, width)
        in_specs.append(pl.BlockSpec((1, tw), lambda i, j: (0, j)))
    outs = pl.pallas_call(
        body,
        out_shape=[_sds((rows, width), d) for d in out_dtypes],
        grid=(rows // tm, width // tw),
        in_specs=in_specs,
        out_specs=[pl.BlockSpec((tm, tw), lambda i, j: (i, j)) for _ in out_dtypes],
        compiler_params=_cp(("parallel", "parallel")),
        name=name,
    )(*[x[0] for x in xs], *ps)
    return outs


def _ew_bwd(fn, xs, ps, cts, dx_dtypes, *, name, width, tw=None, tm=256, dx_add=None):
    rows = xs[0][0].shape[0]
    tm = _tile(rows, tm, SUBLANE)
    tw = width if tw is None else tw
    nx, n_p = len(xs), len(ps)
    dx_add = dx_add or {}
    flat_cts = [c for group in cts for c in group]
    add_keys = sorted(dx_add)
    n_in = nx + n_p + len(flat_cts) + len(add_keys)
    dx_idx = [i for i, d in enumerate(dx_dtypes) if d is not None]

    def body(*refs):
        i = pl.program_id(1)
        xv = [r[...].astype(F32) for r in refs[:nx]]
        pv = [r[...].astype(F32) for r in refs[nx:nx + n_p]]
        pos = nx + n_p
        ct_vals = []
        for group in cts:
            acc = refs[pos][...].astype(F32)
            pos += 1
            for _ in group[1:]:
                acc = acc + refs[pos][...].astype(F32)
                pos += 1
            ct_vals.append(acc)
        add_vals = {}
        for key in add_keys:
            add_vals[key] = refs[pos][...].astype(F32)
            pos += 1
        out_refs = refs[n_in:]
        outs, vjp = jax.vjp(fn, *xv, *pv)
        grads = vjp(tuple(ct_vals) if isinstance(outs, (tuple, list)) else ct_vals[0])
        o = 0
        for idx in dx_idx:
            gval = grads[idx]
            if idx in add_vals:
                gval = gval + add_vals[idx]
            out_refs[o][...] = gval.astype(out_refs[o].dtype)
            o += 1
        for q in range(n_p):
            gp = grads[nx + q]
            ref = out_refs[o + q]

            @pl.when(i == 0)
            def _(ref=ref, gp=gp):
                ref[...] = gp

            @pl.when(i > 0)
            def _(ref=ref, gp=gp):
                ref[...] += gp

    tile_spec = pl.BlockSpec((tm, tw), lambda j, i: (i, j))
    in_specs = []
    for arr, c0, w in xs:
        assert w == width and c0 % tw == 0
        in_specs.append(pl.BlockSpec((tm, tw), functools.partial(lambda j, i, b: (i, b + j), b=c0 // tw)))
    for p in ps:
        in_specs.append(pl.BlockSpec((1, tw), lambda j, i: (0, j)))
    in_specs += [tile_spec] * (len(flat_cts) + len(add_keys))
    out_shape = [_sds((rows, width), dx_dtypes[idx]) for idx in dx_idx] + [_sds((1, width), F32)] * n_p
    out_specs = [tile_spec] * len(dx_idx) + [pl.BlockSpec((1, tw), lambda j, i: (0, j))] * n_p
    outs = pl.pallas_call(
        body,
        out_shape=out_shape,
        grid=(width // tw, rows // tm),
        in_specs=in_specs,
        out_specs=out_specs,
        compiler_params=_cp(("parallel", "arbitrary")),
        name=name,
    )(*[x[0] for x in xs], *ps, *flat_cts, *[dx_add[k] for k in add_keys])
    return outs[:len(dx_idx)], outs[len(dx_idx):]


def _rms(x):
    return x * lax.rsqrt(jnp.mean(x * x, axis=-1, keepdims=True) + RMS_EPS)


def _fn_norm_mod(x, g, sc, sh):
    return (_rms(x) * g) * (1.0 + sc) + sh


def _fn_residual(x, y, gt, g):
    return x + gt * (_rms(y) * g)


def _fn_gelu(y, u, d):
    return jax.nn.gelu(y + d * u)


def _fn_glu(g, z, b):
    return g * jax.nn.sigmoid(z + b)


def _fn_gates(p0, p1, p2, ys, ya, yc, b0, b1, b2):
    return (jax.nn.sigmoid(p0 + b0) * ys + jax.nn.sigmoid(p1 + b1) * ya + jax.nn.sigmoid(p2 + b2) * yc)


def _fn_disc(log_dt, ar, ai, br_t, bi_t):
    dt = jnp.exp(log_dt)
    mag = jnp.exp(ar * dt)
    lr, li = mag * jnp.cos(ai * dt), mag * jnp.sin(ai * dt)
    den = ar * ar + ai * ai
    fr = ((lr - 1.0) * ar + li * ai) / den
    fi = (li * ar - (lr - 1.0) * ai) / den
    bbr = fr[None] * br_t - fi[None] * bi_t
    bbi = fr[None] * bi_t + fi[None] * br_t
    return lr, li, bbr, bbi


def _ssm_disc_fwd(log_dt, ar, ai, br_t, bi_t):
    g, n = ar.shape

    def body(ld_ref, ar_ref, ai_ref, br_ref, bi_ref, lr_ref, li_ref, bbr_ref, bbi_ref):
        lr, li, bbr, bbi = _fn_disc(ld_ref[...], ar_ref[...], ai_ref[...], br_ref[...], bi_ref[...])
        lr_ref[...] = lr
        li_ref[...] = li
        bbr_ref[...] = bbr
        bbi_ref[...] = bbi

    return pl.pallas_call(
        body,
        out_shape=[_sds((g, n), F32), _sds((g, n), F32), _sds(br_t.shape, F32), _sds(br_t.shape, F32)],
        compiler_params=_cp(),
        name="ssm_disc_fwd",
    )(log_dt, ar, ai, br_t, bi_t)


def _ssm_disc_bwd(log_dt, ar, ai, br_t, bi_t, dlr, dli, dbbr, dbbi):
    g, n = ar.shape

    def body(ld_ref, ar_ref, ai_ref, br_ref, bi_ref, dlr_ref, dli_ref, dbbr_ref, dbbi_ref,
             gld_ref, gar_ref, gai_ref, gbr_ref, gbi_ref):
        _, vjp = jax.vjp(_fn_disc, ld_ref[...], ar_ref[...], ai_ref[...], br_ref[...], bi_ref[...])
        gld, gar, gai, gbr, gbi = vjp((dlr_ref[...], dli_ref[...], dbbr_ref[...], dbbi_ref[...]))
        gld_ref[...] = gld
        gar_ref[...] = gar
        gai_ref[...] = gai
        gbr_ref[...] = gbr
        gbi_ref[...] = gbi

    return pl.pallas_call(
        body,
        out_shape=[_sds((g, 1), F32), _sds((g, n), F32), _sds((g, n), F32), _sds(br_t.shape, F32),
                   _sds(br_t.shape, F32)],
        compiler_params=_cp(),
        name="ssm_disc_bwd",
    )(log_dt, ar, ai, br_t, bi_t, dlr, dli, dbbr, dbbi)


def _cmul(ar, ai, br, bi):
    return ar * br - ai * bi, ar * bi + ai * br


def _scan_tables(lr, li, reverse):
    c = lr.shape[-1]
    p1 = (jnp.broadcast_to(lr, (SUBLANE, c)), jnp.broadcast_to(li, (SUBLANE, c)))
    p2 = _cmul(*p1, *p1)
    p4 = _cmul(*p2, *p2)
    p8 = _cmul(*p4, *p4)
    row = lax.broadcasted_iota(jnp.int32, (SUBLANE, c), 0)
    dist = (SUBLANE - row) if reverse else (row + 1)
    pr, pi = jnp.ones((SUBLANE, c), F32), jnp.zeros((SUBLANE, c), F32)
    for bit, pw in ((1, p1), (2, p2), (4, p4), (8, p8)):
        qr, qi = _cmul(pr, pi, *pw)
        take = (dist & bit) != 0
        pr, pi = jnp.where(take, qr, pr), jnp.where(take, qi, pi)
    return row, (p1, p2, p4), (pr, pi)


def _shift_rows(x, s, row, reverse):
    if reverse:
        return jnp.where(row < SUBLANE - s, pltpu.roll(x, SUBLANE - s, 0), 0.0)
    return jnp.where(row >= s, pltpu.roll(x, s, 0), 0.0)


def _scan_tile(xr, xi, carry, row, pows, carry_pow, reverse):
    for s, pw in zip((1, 2, 4), pows):
        sr, si = _shift_rows(xr, s, row, reverse), _shift_rows(xi, s, row, reverse)
        tr, ti = _cmul(*pw, sr, si)
        xr, xi = xr + tr, xi + ti
    tr, ti = _cmul(*carry_pow, *carry)
    hr, hi = xr + tr, xi + ti
    edge = 0 if reverse else SUBLANE - 1
    c = hr.shape[-1]
    new_carry = (jnp.broadcast_to(hr[edge:edge + 1, :], (SUBLANE, c)),
                 jnp.broadcast_to(hi[edge:edge + 1, :], (SUBLANE, c)))
    return hr, hi, new_carry


def _scan_cols(gn):
    return _tile(gn, 256)


def _ssm_scan_fwd(xcat, lam):
    rows, gn2 = xcat.shape
    c = _scan_cols(gn2 // 2)
    n_tiles = rows // SUBLANE

    def body(lam_ref, x_ref, h_ref):
        lr, li = lam_ref[:, :c], lam_ref[:, c:]
        row, pows, carry_pow = _scan_tables(lr, li, False)

        def step(k, carry):
            t0 = pl.multiple_of(k * SUBLANE, SUBLANE)
            hr, hi, carry = _scan_tile(x_ref[pl.ds(t0, SUBLANE), :c], x_ref[pl.ds(t0, SUBLANE), c:], carry,
                                       row, pows, carry_pow, False)
            h_ref[pl.ds(t0, SUBLANE), :c] = hr
            h_ref[pl.ds(t0, SUBLANE), c:] = hi
            return carry

        zero = jnp.zeros((SUBLANE, c), F32)
        lax.fori_loop(0, n_tiles, step, (zero, zero))

    return pl.pallas_call(
        body,
        out_shape=_sds((rows, gn2), F32),
        grid=(gn2 // (2 * c),),
        in_specs=[pl.BlockSpec((1, 2 * c), lambda j: (0, j)), pl.BlockSpec((rows, 2 * c), lambda j: (0, j))],
        out_specs=pl.BlockSpec((rows, 2 * c), lambda j: (0, j)),
        compiler_params=_cp(("parallel",)),
        name="ssm_scan_fwd",
    )(lam, xcat)


def _ssm_scan_bwd(dhcat, hcat, lam):
    rows, gn2 = dhcat.shape
    c = _scan_cols(gn2 // 2)
    n_tiles = rows // SUBLANE

    def body(lam_ref, dh_ref, h_ref, g_ref, dlam_ref):
        lr, li = lam_ref[:, :c], -lam_ref[:, c:]
        row, pows, carry_pow = _scan_tables(lr, li, True)

        def step(k, state):
            carry, acc_r, acc_i = state
            kk = n_tiles - 1 - k
            t0 = pl.multiple_of(kk * SUBLANE, SUBLANE)
            gr, gi, carry = _scan_tile(dh_ref[pl.ds(t0, SUBLANE), :c], dh_ref[pl.ds(t0, SUBLANE), c:], carry,
                                       row, pows, carry_pow, True)
            g_ref[pl.ds(t0, SUBLANE), :c] = gr
            g_ref[pl.ds(t0, SUBLANE), c:] = gi
            tp = pl.multiple_of(jnp.maximum(kk - 1, 0) * SUBLANE, SUBLANE)
            has_prev = (kk > 0).astype(F32)
            prev_r = pltpu.roll(h_ref[pl.ds(tp, SUBLANE), :c], 1, 0) * has_prev
            prev_i = pltpu.roll(h_ref[pl.ds(tp, SUBLANE), c:], 1, 0) * has_prev
            hpr = jnp.where(row >= 1, pltpu.roll(h_ref[pl.ds(t0, SUBLANE), :c], 1, 0), prev_r)
            hpi = jnp.where(row >= 1, pltpu.roll(h_ref[pl.ds(t0, SUBLANE), c:], 1, 0), prev_i)
            acc_r = acc_r + gr * hpr + gi * hpi
            acc_i = acc_i + gi * hpr - gr * hpi
            return carry, acc_r, acc_i

        zero = jnp.zeros((SUBLANE, c), F32)
        _, acc_r, acc_i = lax.fori_loop(0, n_tiles, step, ((zero, zero), zero, zero))
        dlam_ref[:, :c] = jnp.sum(acc_r, axis=0, keepdims=True)
        dlam_ref[:, c:] = jnp.sum(acc_i, axis=0, keepdims=True)

    blk = pl.BlockSpec((rows, 2 * c), lambda j: (0, j))
    return pl.pallas_call(
        body,
        out_shape=[_sds((rows, gn2), F32), _sds((1, gn2), F32)],
        grid=(gn2 // (2 * c),),
        in_specs=[pl.BlockSpec((1, 2 * c), lambda j: (0, j)), blk, blk],
        out_specs=[blk, pl.BlockSpec((1, 2 * c), lambda j: (0, j))],
        compiler_params=_cp(("parallel",)),
        name="ssm_scan_bwd",
    )(lam, dhcat, hcat)


def _shift_down(x, k, row):
    return x if k == 0 else jnp.where(row >= k, pltpu.roll(x, k, 0), 0.0)


def _shift_up(x, k, row):
    n = x.shape[0]
    return x if k == 0 else jnp.where(row < n - k, pltpu.roll(x, n - k, 0), 0.0)


def _taps(w_ref):
    return [w_ref[k:k + 1, :] for k in range(3)]


def _conv3(x, w, row):
    return sum(w[k] * _shift_down(x, k, row) for k in range(3))


def _conv3_bwd(x, w, dy, row):
    dx = sum(w[k] * _shift_up(dy, k, row) for k in range(3))
    dw = [jnp.sum(dy * _shift_down(x, k, row), axis=0, keepdims=True) for k in range(3)]
    return dx, dw


def _gconv_fwd(proj, off, cw, w):
    rows = proj.shape[0]
    tc = _tile(cw, LANE)
    nb = cw // tc

    def body(b_ref, c_ref, h_ref, w_ref, o_ref):
        row = lax.broadcasted_iota(jnp.int32, (rows, tc), 0)
        o_ref[...] = (b_ref[...] * _conv3(c_ref[...] * h_ref[...], _taps(w_ref), row)).astype(o_ref.dtype)

    specs = [pl.BlockSpec((rows, tc), functools.partial(lambda j, b: (0, b + j), b=(off + q * cw) // tc))
             for q in range(3)]
    return pl.pallas_call(
        body,
        out_shape=_sds((rows, cw), BF16),
        grid=(nb,),
        in_specs=specs + [pl.BlockSpec((3, tc), lambda j: (0, j))],
        out_specs=pl.BlockSpec((rows, tc), lambda j: (0, j)),
        compiler_params=_cp(("parallel",)),
        name="gconv_fwd",
    )(proj, proj, proj, w)


def _gconv_bwd(proj, off, cw, w, dy):
    rows = proj.shape[0]
    tc = _tile(cw, LANE)
    nb = cw // tc

    def body(b_ref, c_ref, h_ref, w_ref, dy_ref, db_ref, dc_ref, dh_ref, dw_ref):
        row = lax.broadcasted_iota(jnp.int32, (rows, tc), 0)
        cv, hv, dyv = c_ref[...], h_ref[...], dy_ref[...].astype(F32)
        t = cv * hv
        db_ref[...] = (dyv * _conv3(t, _taps(w_ref), row)).astype(db_ref.dtype)
        dt, dw = _conv3_bwd(t, _taps(w_ref), dyv * b_ref[...], row)
        dc_ref[...] = (dt * hv).astype(dc_ref.dtype)
        dh_ref[...] = (dt * cv).astype(dh_ref.dtype)
        for k in range(3):
            dw_ref[k:k + 1, :] = dw[k]

    specs = [pl.BlockSpec((rows, tc), functools.partial(lambda j, b: (0, b + j), b=(off + q * cw) // tc))
             for q in range(3)]
    col = pl.BlockSpec((rows, tc), lambda j: (0, j))
    wspec = pl.BlockSpec((3, tc), lambda j: (0, j))
    return pl.pallas_call(
        body,
        out_shape=[_sds((rows, cw), BF16)] * 3 + [_sds((3, cw), F32)],
        grid=(nb,),
        in_specs=specs + [wspec, col],
        out_specs=[col, col, col, wspec],
        compiler_params=_cp(("parallel",)),
        name="gconv_bwd",
    )(proj, proj, proj, w, dy)


def _ffn_act_fwd(up, w):
    rows, f2 = up.shape
    f = f2 // 2
    tc = _tile(f, LANE)
    nb = f // tc

    def body(a_ref, b_ref, wa_ref, wb_ref, o_ref):
        row = lax.broadcasted_iota(jnp.int32, (rows, tc), 0)
        a = _conv3(a_ref[...], _taps(wa_ref), row)
        b = _conv3(b_ref[...], _taps(wb_ref), row)
        o_ref[...] = (jax.nn.silu(a) * b).astype(o_ref.dtype)

    return pl.pallas_call(
        body,
        out_shape=_sds((rows, f), BF16),
        grid=(nb,),
        in_specs=[pl.BlockSpec((rows, tc), lambda j: (0, j)), pl.BlockSpec((rows, tc), lambda j: (0, nb + j)),
                  pl.BlockSpec((3, tc), lambda j: (0, j)), pl.BlockSpec((3, tc), lambda j: (0, nb + j))],
        out_specs=pl.BlockSpec((rows, tc), lambda j: (0, j)),
        compiler_params=_cp(("parallel",)),
        name="ffn_act_fwd",
    )(up, up, w, w)


def _ffn_act_bwd(up, w, dact):
    rows, f2 = up.shape
    f = f2 // 2
    tc = _tile(f, LANE)
    nb = f // tc

    def body(a_ref, b_ref, wa_ref, wb_ref, d_ref, da_ref, db_ref, dwa_ref, dwb_ref):
        row = lax.broadcasted_iota(jnp.int32, (rows, tc), 0)
        av, bv, dv = a_ref[...], b_ref[...], d_ref[...].astype(F32)
        ac = _conv3(av, _taps(wa_ref), row)
        bc = _conv3(bv, _taps(wb_ref), row)
        _, vjp = jax.vjp(lambda p, q: jax.nn.silu(p) * q, ac, bc)
        dac, dbc = vjp(dv)
        dxa, dwa = _conv3_bwd(av, _taps(wa_ref), dac, row)
        dxb, dwb = _conv3_bwd(bv, _taps(wb_ref), dbc, row)
        da_ref[...] = dxa.astype(da_ref.dtype)
        db_ref[...] = dxb.astype(db_ref.dtype)
        for k in range(3):
            dwa_ref[k:k + 1, :] = dwa[k]
            dwb_ref[k:k + 1, :] = dwb[k]

    col = pl.BlockSpec((rows, tc), lambda j: (0, j))
    wspec = pl.BlockSpec((3, tc), lambda j: (0, j))
    return pl.pallas_call(
        body,
        out_shape=[_sds((rows, f), BF16)] * 2 + [_sds((3, f), F32)] * 2,
        grid=(nb,),
        in_specs=[col, pl.BlockSpec((rows, tc), lambda j: (0, nb + j)), wspec,
                  pl.BlockSpec((3, tc), lambda j: (0, nb + j)), col],
        out_specs=[col, col, wspec, wspec],
        compiler_params=_cp(("parallel",)),
        name="ffn_act_bwd",
    )(up, up, w, w, dact)


def _attn_scores(q, kc, kp, slope, dilation, has_prev):
    scale = HEAD_DIM ** -0.5
    nt = (((1,), (1,)), ((), ()))
    s_c = lax.dot_general(q, kc, nt, preferred_element_type=F32) * scale
    s_p = lax.dot_general(q, kp, nt, preferred_element_type=F32) * scale
    qi = lax.broadcasted_iota(jnp.int32, (ATTN_BLOCK, ATTN_BLOCK), 0)
    kj = lax.broadcasted_iota(jnp.int32, (ATTN_BLOCK, ATTN_BLOCK), 1)
    dist_c = qi - kj
    dist_p = dist_c + ATTN_BLOCK
    s_c = jnp.where(dist_c >= 0, s_c - slope * (dist_c * dilation).astype(F32), NEG_INF)
    s_p = jnp.where((dist_p <= ATTN_BLOCK) & has_prev, s_p - slope * (dist_p * dilation).astype(F32), NEG_INF)
    return s_c, s_p


def _slab(seq, col0):
    assert col0 % LANE == 0
    return pl.BlockSpec((seq, LANE), lambda hh, r, s: (0, col0 // LANE + hh))


def _residue_rows(r, block, dilation):
    if dilation == 1:
        return pl.ds(pl.multiple_of(block * ATTN_BLOCK, ATTN_BLOCK), ATTN_BLOCK)
    return pl.ds(r + dilation * ATTN_BLOCK * block, ATTN_BLOCK, stride=dilation)


def _head_col(x, mask):
    return jnp.max(jnp.where(mask, x, -jnp.inf), axis=-1, keepdims=True)


def _attn_fwd(proj, offs, pattern, dilation, sw, slopes):
    seq, _ = proj.shape
    nb = seq // dilation // ATTN_BLOCK
    pairs = sw // LANE

    def body(q_ref, k_ref, v_ref, s0_ref, s1_ref, o_ref, lse_ref):
        r, i = pl.program_id(1), pl.program_id(2)
        cur, prev = _residue_rows(r, i, dilation), _residue_rows(r, jnp.maximum(i - 1, 0), dilation)
        first = lax.broadcasted_iota(jnp.int32, (ATTN_BLOCK, LANE), 1) < HEAD_DIM
        q2 = q_ref[cur, :]
        kc, kp = k_ref[cur, :].astype(BF16), k_ref[prev, :].astype(BF16)
        vc, vp = v_ref[cur, :].astype(BF16), v_ref[prev, :].astype(BF16)
        res = []
        for mask, sl_ref in ((first, s0_ref), (~first, s1_ref)):
            qh = jnp.where(mask, q2, 0.0).astype(BF16)
            s_c, s_p = _attn_scores(qh, kc, kp, sl_ref[:, :1], dilation, i > 0)
            mx = jnp.maximum(jnp.max(s_c, axis=-1, keepdims=True), jnp.max(s_p, axis=-1, keepdims=True))
            p_c, p_p = jnp.exp(s_c - mx), jnp.exp(s_p - mx)
            den = jnp.sum(p_c, axis=-1, keepdims=True) + jnp.sum(p_p, axis=-1, keepdims=True)
            o = (jnp.dot(p_c.astype(BF16), vc, preferred_element_type=F32)
                 + jnp.dot(p_p.astype(BF16), vp, preferred_element_type=F32))
            res.append((o / den, mx + jnp.log(den)))
        o_ref[cur, :] = jnp.where(first, res[0][0], res[1][0])
        lse_ref[cur, :] = jnp.where(first, res[0][1], res[1][1])

    slope = pl.BlockSpec((None, 1, LANE), lambda hh, r, s: (hh, 0, 0))
    return pl.pallas_call(
        body,
        out_shape=[_sds((seq, sw), F32)] * 2,
        grid=(pairs, dilation, nb),
        in_specs=[_slab(seq, offs[0] + pattern * sw), _slab(seq, offs[1] + pattern * sw),
                  _slab(seq, offs[2] + pattern * sw), slope, slope],
        out_specs=[_slab(seq, 0), _slab(seq, 0)],
        compiler_params=_cp(("parallel", "arbitrary", "arbitrary")),
        name=f"attn_fwd_d{dilation}",
    )(proj, proj, proj, *slopes)


def _attn_bwd(proj, do, lse, delta, offs, pattern, dilation, sw, slopes):
    seq, _ = proj.shape
    nb = seq // dilation // ATTN_BLOCK
    pairs = sw // LANE

    def body(q_ref, k_ref, v_ref, do_ref, lse_ref, dl_ref, s0_ref, s1_ref, dq_ref, dk_ref, dv_ref, ck_ref, cv_ref):
        r, step = pl.program_id(1), pl.program_id(2)
        i = nb - 1 - step
        cur, prev = _residue_rows(r, i, dilation), _residue_rows(r, jnp.maximum(i - 1, 0), dilation)
        scale = HEAD_DIM ** -0.5
        nt = (((1,), (1,)), ((), ()))
        first = lax.broadcasted_iota(jnp.int32, (ATTN_BLOCK, LANE), 1) < HEAD_DIM
        q2, do2, lse2, dl2 = q_ref[cur, :], do_ref[cur, :], lse_ref[cur, :], dl_ref[cur, :]
        kc, kp = k_ref[cur, :].astype(BF16), k_ref[prev, :].astype(BF16)
        vc, vp = v_ref[cur, :].astype(BF16), v_ref[prev, :].astype(BF16)

        @pl.when(step == 0)
        def _():
            ck_ref[...] = jnp.zeros_like(ck_ref)
            cv_ref[...] = jnp.zeros_like(cv_ref)

        dq, dk_c, dv_c, dk_p, dv_p = [], 0.0, 0.0, 0.0, 0.0
        for mask, sl_ref in ((first, s0_ref), (~first, s1_ref)):
            qh = jnp.where(mask, q2, 0.0).astype(BF16)
            doh = jnp.where(mask, do2, 0.0).astype(BF16)
            lse_col, dl_col = _head_col(lse2, mask), _head_col(dl2, mask)
            s_c, s_p = _attn_scores(qh, kc, kp, sl_ref[:, :1], dilation, i > 0)
            p_c, p_p = jnp.exp(s_c - lse_col), jnp.exp(s_p - lse_col)
            ds_c = p_c * (lax.dot_general(doh, vc, nt, preferred_element_type=F32) - dl_col)
            ds_p = p_p * (lax.dot_general(doh, vp, nt, preferred_element_type=F32) - dl_col)
            dq.append(jnp.dot(ds_c.astype(BF16), kc, preferred_element_type=F32)
                      + jnp.dot(ds_p.astype(BF16), kp, preferred_element_type=F32))
            dk_c = dk_c + jnp.dot(ds_c.T.astype(BF16), qh, preferred_element_type=F32)
            dv_c = dv_c + jnp.dot(p_c.T.astype(BF16), doh, preferred_element_type=F32)
            dk_p = dk_p + jnp.dot(ds_p.T.astype(BF16), qh, preferred_element_type=F32)
            dv_p = dv_p + jnp.dot(p_p.T.astype(BF16), doh, preferred_element_type=F32)
        dq_ref[cur, :] = jnp.where(first, dq[0], dq[1]) * scale
        dk_ref[cur, :] = dk_c * scale + ck_ref[...]
        dv_ref[cur, :] = dv_c + cv_ref[...]
        ck_ref[...] = dk_p * scale
        cv_ref[...] = dv_p

    slope = pl.BlockSpec((None, 1, LANE), lambda hh, r, s: (hh, 0, 0))
    tok = _slab(seq, 0)
    return pl.pallas_call(
        body,
        out_shape=[_sds((seq, sw), F32)] * 3,
        grid=(pairs, dilation, nb),
        in_specs=[_slab(seq, offs[0] + pattern * sw), _slab(seq, offs[1] + pattern * sw),
                  _slab(seq, offs[2] + pattern * sw), tok, tok, tok, slope, slope],
        out_specs=[tok, tok, tok],
        scratch_shapes=[pltpu.VMEM((ATTN_BLOCK, LANE), F32), pltpu.VMEM((ATTN_BLOCK, LANE), F32)],
        compiler_params=_cp(("parallel", "arbitrary", "arbitrary")),
        name=f"attn_bwd_d{dilation}",
    )(proj, proj, proj, do, lse, delta, *slopes)


def _attn_merge(outs, lses):
    rows, aw = outs[0].shape
    tm = _tile(rows, 256, SUBLANE)

    def body(o0, o1, o2, l0, l1, l2, o_ref, lse_ref):
        lv = [l0[...], l1[...], l2[...]]
        mx = jnp.maximum(jnp.maximum(lv[0], lv[1]), lv[2])
        w = [jnp.exp(t - mx) for t in lv]
        den = w[0] + w[1] + w[2]
        o_ref[...] = (w[0] * o0[...] + w[1] * o1[...] + w[2] * o2[...]) / den
        lse_ref[...] = mx + jnp.log(den)

    spec = pl.BlockSpec((tm, aw), lambda i: (i, 0))
    return pl.pallas_call(
        body,
        out_shape=[_sds((rows, aw), F32)] * 2,
        grid=(rows // tm,),
        in_specs=[spec] * 6,
        out_specs=[spec, spec],
        compiler_params=_cp(("parallel",)),
        name="attn_merge",
    )(*outs, *lses)


def _attn_delta(do, o, head_ones):
    rows, aw = do.shape
    tm = _tile(rows, 256, SUBLANE)

    def body(do_ref, o_ref, e_ref, d_ref):
        d_ref[...] = jnp.dot(do_ref[...] * o_ref[...], e_ref[...], preferred_element_type=F32,
                             precision=lax.Precision.HIGHEST)

    spec = pl.BlockSpec((tm, aw), lambda i: (i, 0))
    return pl.pallas_call(
        body,
        out_shape=_sds((rows, aw), F32),
        grid=(rows // tm,),
        in_specs=[spec, spec, pl.BlockSpec((aw, aw), lambda i: (0, 0))],
        out_specs=spec,
        compiler_params=_cp(("parallel",)),
        name="attn_delta",
    )(do, o, head_ones)


def _alibi_slopes(pattern, hp):
    n_heads = hp * len(DSWA_PATTERNS)
    s = np.array([2.0 ** (-8.0 * (pattern * hp + h + 1) / n_heads) for h in range(hp)], dtype=np.float32)
    return [jnp.asarray(np.broadcast_to(s[par::2, None, None], (hp // 2, 1, LANE)).copy()) for par in (0, 1)]


def _loss_fwd_bwd(y, target):
    rows, d = y.shape
    tm = _tile(rows, 256, SUBLANE)

    def body(y_ref, t_ref, dy_ref, l_ref):
        i = pl.program_id(0)
        err = y_ref[...] - t_ref[...]
        dy_ref[...] = err * (1.0 / d)
        part = 0.5 * jnp.sum(jnp.mean(err * err, axis=-1, keepdims=True), axis=0, keepdims=True)

        @pl.when(i == 0)
        def _():
            l_ref[...] = jnp.zeros_like(l_ref)

        l_ref[...] += jnp.broadcast_to(part, l_ref.shape)

    spec = pl.BlockSpec((tm, d), lambda i: (i, 0))
    dy, loss = pl.pallas_call(
        body,
        out_shape=[_sds((rows, d), F32), _sds((SUBLANE, LANE), F32)],
        grid=(rows // tm,),
        in_specs=[spec, spec],
        out_specs=[spec, pl.BlockSpec((SUBLANE, LANE), lambda i: (0, 0))],
        compiler_params=_cp(("arbitrary",)),
        name="loss",
    )(y, target)
    return dy, loss[0, 0]


def _adam_math(w, g, m, v):
    m = ADAM_B1 * m + (1.0 - ADAM_B1) * g
    v = ADAM_B2 * v + (1.0 - ADAM_B2) * jnp.square(g)
    m_hat = m / (1.0 - ADAM_B1 ** ADAM_STEP)
    v_hat = v / (1.0 - ADAM_B2 ** ADAM_STEP)
    delta = -ADAM_LR * (m_hat / (jnp.sqrt(v_hat) + ADAM_EPS) + ADAM_WD * w)
    return delta, m, v


def _as2d(a):
    if a.ndim == 1:
        return a.reshape(1, -1)
    return a.reshape(-1, a.shape[-1])


def _adam(w, g, m, v, name):
    shape = w.shape
    w2, g2, m2, v2 = _as2d(w), _as2d(g), _as2d(m), _as2d(v)
    r, c = w2.shape
    tr = _tile(r, 512, SUBLANE)
    tc = _tile(c, 1024)

    def body(w_ref, g_ref, m_ref, v_ref, d_ref, mo_ref, vo_ref):
        delta, mn, vn = _adam_math(w_ref[...], g_ref[...], m_ref[...], v_ref[...])
        d_ref[...] = delta
        mo_ref[...] = mn
        vo_ref[...] = vn

    spec = pl.BlockSpec((tr, tc), lambda i, j: (i, j))
    outs = pl.pallas_call(
        body,
        out_shape=[_sds((r, c), F32)] * 3,
        grid=(r // tr, c // tc),
        in_specs=[spec] * 4,
        out_specs=[spec] * 3,
        compiler_params=_cp(("parallel", "parallel")),
        name=name,
    )(w2, g2, m2, v2)
    return [o.reshape(shape) for o in outs]


def _wmod_grad_adam(c_t, dmod, w, m, v):
    nl, d, cols = w.shape
    nex = c_t.shape[1]
    tr = _tile(d, 256, SUBLANE)
    tc = _tile(cols, 1024)

    def body(c_ref, dm_ref, w_ref, m_ref, v_ref, g_ref, d_ref, mo_ref, vo_ref):
        cond = jax.nn.silu(c_ref[...]).astype(BF16)
        g = jnp.dot(cond, dm_ref[...].astype(BF16), preferred_element_type=F32)
        delta, mn, vn = _adam_math(w_ref[...], g, m_ref[...], v_ref[...])
        g_ref[...] = g
        d_ref[...] = delta
        mo_ref[...] = mn
        vo_ref[...] = vn

    spec = pl.BlockSpec((None, tr, tc), lambda l, i, j: (l, i, j))
    return pl.pallas_call(
        body,
        out_shape=[_sds((nl, d, cols), F32)] * 4,
        grid=(nl, d // tr, cols // tc),
        in_specs=[pl.BlockSpec((tr, nex), lambda l, i, j: (i, 0)),
                  pl.BlockSpec((None, nex, tc), lambda l, i, j: (l, 0, j)), spec, spec, spec],
        out_specs=[spec] * 4,
        compiler_params=_cp(("parallel", "parallel", "parallel")),
        name="wmod_grad_adam",
    )(c_t, dmod, w, m, v)


def _my_pos():
    return lax.axis_index("x"), lax.axis_index("y"), lax.axis_index("c")


def _ag8(x4, select_half, name):
    a, s, r, c = x4.shape
    assert s == (2 if select_half else 1)

    def body(x_ref, out_ref, send_sems, recv_sems, local_sem):
        x, y, cc = _my_pos()
        me, sibling = (x, y, cc), (x, y, 1 - cc)
        chips = [(1 - x, y), (x, 1 - y), (1 - x, 1 - y)]
        src_mine = x_ref.at[:, pl.ds(cc if select_half else 0, 1)]

        def blk(px, py, pc):
            return out_ref.at[:, pl.ds(4 * px + 2 * py + pc, 1)]

        def copy(k, block, to, src=None):
            return pltpu.make_async_remote_copy(
                src_ref=blk(*block) if src is None else src, dst_ref=blk(*block),
                send_sem=send_sems.at[k], recv_sem=recv_sems.at[k], device_id=to, device_id_type=MESH)

        mine = pltpu.make_async_copy(src_mine, blk(*me), local_sem)
        mine.start()
        first = [copy(0, me, sibling, src=src_mine)]
        first += [copy(1 + j, me, (*chip, cc), src=src_mine) for j, chip in enumerate(chips)]
        for cp in first:
            cp.start()
        passed = [copy(4 + j, (*chip, cc), sibling) for j, chip in enumerate(chips)]
        for j, chip in enumerate(chips):
            copy(1 + j, (*chip, cc), me).wait_recv()
            passed[j].start()
        copy(0, sibling, me).wait_recv()
        for j, chip in enumerate(chips):
            copy(4 + j, (*chip, 1 - cc), me).wait_recv()
        for cp in first + passed:
            cp.wait_send()
        mine.wait()

    return pl.pallas_call(
        body,
        out_shape=_sds((a, N_DEV, r, c), x4.dtype),
        in_specs=[ANY],
        out_specs=ANY,
        scratch_shapes=[pltpu.SemaphoreType.DMA((7,)), pltpu.SemaphoreType.DMA((7,)), pltpu.SemaphoreType.DMA],
        name=name,
    )(x4)


def _chip_of(x, y, k):
    return (1 - x if k & 2 else x), (1 - y if k & 1 else y)


HBM = pl.BlockSpec(memory_space=pltpu.HBM)
SEM = pl.BlockSpec(memory_space=pltpu.SEMAPHORE)
DATAFLOW = pltpu.SideEffectType.DATAFLOW_SIDE_EFFECTING


def _own_block(k, chip, cc):
    del k
    return 2 * chip + cc


def _distance_slot(k, chip, cc):
    del chip, cc
    return k - 1


def _ici_copies(srcs, dsts, slot, send_sems, recv_sems):
    x, y, cc = _my_pos()
    chip = 2 * x + y
    copies = []
    for n, (s_ref, d_ref) in enumerate(zip(srcs, dsts)):
        for k in (1, 2, 3):
            px, py = _chip_of(x, y, k)
            at = slot(k, chip, cc)
            copies.append(pltpu.make_async_remote_copy(
                src_ref=s_ref.at[pl.ds(at, 1)], dst_ref=d_ref.at[pl.ds(at, 1)],
                send_sem=send_sems.at[3 * n + k - 1], recv_sem=recv_sems.at[3 * n + k - 1],
                device_id=(px, py, cc), device_id_type=MESH))
    return copies


def _ici_start(srcs, lands, slot, after, name):
    n = len(srcs)
    arrays = list(srcs) + ([] if lands is None else list(lands))
    na = len(arrays)

    def body(*refs):
        s_refs = refs[:n]
        d_refs = s_refs if lands is None else refs[n:na]
        send_sems, recv_sems, token = refs[na + 1], refs[na + 2], refs[-1]
        for cp in _ici_copies(s_refs, d_refs, slot, send_sems, recv_sems):
            cp.start()
        token[...] = jnp.zeros_like(token)

    outs = pl.pallas_call(
        body,
        name=name,
        out_shape=(pltpu.SemaphoreType.DMA((3 * n,)), pltpu.SemaphoreType.DMA((3 * n,)),
                   *[pltpu.HBM(a.shape, a.dtype) for a in arrays], _sds((SUBLANE, LANE), F32)),
        in_specs=[HBM] * na + [ANY],
        out_specs=(SEM, SEM, *([HBM] * na), pl.BlockSpec(memory_space=pltpu.VMEM)),
        input_output_aliases={i: 2 + i for i in range(na)},
        compiler_params=pltpu.CompilerParams(has_side_effects=DATAFLOW),
    )(*[pltpu.with_memory_space_constraint(a, pltpu.HBM) for a in arrays], after)
    return outs[0], outs[1], list(outs[2:2 + na]), outs[-1]


def _ici_wait(send_sems, recv_sems, arrays, n, shared, slot, after, name):
    na = len(arrays)

    def body(*refs):
        s_refs = refs[:n]
        d_refs = s_refs if shared else refs[n:na]
        for cp in _ici_copies(s_refs, d_refs, slot, refs[na], refs[na + 1]):
            cp.wait_send()
            cp.wait_recv()

    outs = pl.pallas_call(
        body,
        name=name,
        out_shape=tuple(pltpu.HBM(a.shape, a.dtype) for a in arrays),
        in_specs=[HBM] * na + [SEM, SEM, ANY],
        out_specs=tuple([HBM] * na),
        input_output_aliases={i: i for i in range(na)},
        compiler_params=pltpu.CompilerParams(has_side_effects=DATAFLOW),
    )(*arrays, send_sems, recv_sems, after)
    return list(outs)


def _my_chip():
    return 2 * lax.axis_index("x") + lax.axis_index("y")


def _cast_own(w, layer, name):
    _, r, cols = w.shape
    tr = _tile(r, 512, 2 * SUBLANE)
    tc = _tile(cols, 1024)

    def body(w_ref, o_ref):
        o_ref[...] = w_ref[...].astype(o_ref.dtype)

    return pl.pallas_call(
        body,
        out_shape=_sds((N_CHIP, r, cols), BF16),
        grid=(r // tr, cols // tc),
        in_specs=[pl.BlockSpec((None, tr, tc), lambda i, j: (layer, i, j))],
        out_specs=pl.BlockSpec((None, tr, tc), lambda i, j: (_my_chip(), i, j)),
        compiler_params=_cp(("parallel", "parallel")),
        name=name,
    )(w)


def _forward_halves(bufs, name):
    n = len(bufs)

    def body(*refs):
        ins, outs = refs[:n], refs[n:2 * n]
        send_sems, recv_sems = refs[2 * n], refs[2 * n + 1]
        x, y, cc = _my_pos()
        chip = 2 * x + y
        copies = []
        for i in range(n):
            for k in (1, 2, 3):
                at = 2 * (chip ^ k) + cc
                copies.append(pltpu.make_async_remote_copy(
                    src_ref=ins[i].at[pl.ds(at, 1)], dst_ref=outs[i].at[pl.ds(at, 1)],
                    send_sem=send_sems.at[3 * i + k - 1], recv_sem=recv_sems.at[3 * i + k - 1],
                    device_id=(x, y, 1 - cc), device_id_type=MESH))
        for cp in copies:
            cp.start()
        for cp in copies:
            cp.wait()

    return pl.pallas_call(
        body,
        out_shape=[_sds(b.shape, b.dtype) for b in bufs],
        in_specs=[ANY] * n,
        out_specs=[ANY] * n,
        scratch_shapes=[pltpu.SemaphoreType.DMA((3 * n,)), pltpu.SemaphoreType.DMA((3 * n,))],
        input_output_aliases={i: i for i in range(n)},
        name=name,
    )(*bufs)


def _rs_sibling(g8s, name):
    n = len(g8s)
    g4s = [g.reshape(N_CHIP, 2, g.shape[1], g.shape[2]) for g in g8s]

    def body(*refs):
        ins, outs = refs[:n], refs[n:2 * n]
        send_sems, recv_sems = refs[2 * n], refs[2 * n + 1]
        x, y, cc = _my_pos()
        copies = [pltpu.make_async_remote_copy(
            src_ref=ins[i].at[:, pl.ds(1 - cc, 1)], dst_ref=outs[i], send_sem=send_sems.at[i],
            recv_sem=recv_sems.at[i], device_id=(x, y, 1 - cc), device_id_type=MESH) for i in range(n)]
        for cp in copies:
            cp.start()
        for cp in copies:
            cp.wait()

    return pl.pallas_call(
        body,
        out_shape=[_sds((N_CHIP, 1, g.shape[2], g.shape[3]), g.dtype) for g in g4s],
        in_specs=[ANY] * n,
        out_specs=[ANY] * n,
        scratch_shapes=[pltpu.SemaphoreType.DMA((n,)), pltpu.SemaphoreType.DMA((n,))],
        name=name,
    )(*g4s)


def _share_halves(halves, name):
    def body(in_ref, out_ref, send_sem, recv_sem):
        x, y, cc = _my_pos()
        cp = pltpu.make_async_remote_copy(
            src_ref=in_ref.at[:, pl.ds(cc, 1)], dst_ref=out_ref.at[:, pl.ds(cc, 1)], send_sem=send_sem,
            recv_sem=recv_sem, device_id=(x, y, 1 - cc), device_id_type=MESH)
        cp.start()
        cp.wait()

    return pl.pallas_call(
        body,
        out_shape=_sds(halves.shape, halves.dtype),
        in_specs=[ANY],
        out_specs=ANY,
        scratch_shapes=[pltpu.SemaphoreType.DMA, pltpu.SemaphoreType.DMA],
        input_output_aliases={0: 0},
        name=name,
    )(halves)


def _rs_add_remote(g8, recv_a, name):
    _, r, c = g8.shape
    ra = recv_a.reshape(N_CHIP, r, c)
    tr = _tile(r, 512, SUBLANE)
    tc = _tile(c, 1024)

    def body(g_ref, r_ref, o_ref):
        o_ref[...] = (g_ref[...] + r_ref[...]).astype(o_ref.dtype)

    return pl.pallas_call(
        body,
        out_shape=_sds((3, r, c), BF16),
        grid=(3, r // tr, c // tc),
        in_specs=[pl.BlockSpec((None, tr, tc),
                               lambda k, i, j: (2 * (_my_chip() ^ (k + 1)) + lax.axis_index("c"), i, j)),
                  pl.BlockSpec((None, tr, tc), lambda k, i, j: (_my_chip() ^ (k + 1), i, j))],
        out_specs=pl.BlockSpec((None, tr, tc), lambda k, i, j: (k, i, j)),
        compiler_params=_cp(("parallel",) * 3),
        name=name,
    )(g8, ra)


def _rs_add_final(g8, recv_a, recv_b, out_buf, layer, name):
    _, r, c = g8.shape
    ra = recv_a.reshape(N_CHIP, r, c)
    tr = _tile(r, 512, SUBLANE)
    tc = _tile(c, 1024)

    def body(g_ref, r_ref, b0_ref, b1_ref, b2_ref, buf_ref, o_ref):
        del buf_ref
        o_ref[...] = (((g_ref[...] + r_ref[...]) + b0_ref[...].astype(F32)) + b1_ref[...].astype(F32)
                      ) + b2_ref[...].astype(F32)

    def bspec(k):
        return pl.BlockSpec((None, tr, tc), functools.partial(lambda i, j, k: (k, i, j), k=k))

    return pl.pallas_call(
        body,
        out_shape=_sds(out_buf.shape, F32),
        grid=(r // tr, c // tc),
        in_specs=[pl.BlockSpec((None, tr, tc), lambda i, j: (2 * _my_chip() + lax.axis_index("c"), i, j)),
                  pl.BlockSpec((None, tr, tc), lambda i, j: (_my_chip(), i, j)),
                  bspec(0), bspec(1), bspec(2), ANY],
        out_specs=pl.BlockSpec((None, None, tr, tc), lambda i, j: (layer, lax.axis_index("c"), i, j)),
        input_output_aliases={5: 0},
        compiler_params=_cp(("parallel",) * 2),
        name=name,
    )(g8, ra, recv_b, recv_b, recv_b, out_buf)


def _sum8(x8, name):
    _, r, c = x8.shape
    tr = _tile(r, 256, SUBLANE)

    def body(x_ref, o_ref):
        acc = x_ref[0]
        for b in range(1, N_DEV):
            acc = acc + x_ref[b]
        o_ref[...] = acc

    return pl.pallas_call(
        body,
        out_shape=_sds((r, c), F32),
        grid=(r // tr,),
        in_specs=[pl.BlockSpec((N_DEV, tr, c), lambda i: (0, i, 0))],
        out_specs=pl.BlockSpec((tr, c), lambda i: (i, 0)),
        compiler_params=_cp(("parallel",)),
        name=name,
    )(x8)


def _block_diag(t):
    g, p, q = t.shape
    eye = jnp.eye(g, dtype=t.dtype)
    return (t[:, :, None, :] * eye[:, None, :, None]).reshape(g * p, g * q)


def _diag_blocks(mat, g):
    p, q = mat.shape[0] // g, mat.shape[1] // g
    eye = jnp.eye(g, dtype=mat.dtype)
    return jnp.sum(mat.reshape(g, p, g, q) * eye[:, None, :, None], axis=2)


def _interleave(re, im, c):
    lead = re.shape[:-1]
    gn = re.shape[-1]
    return jnp.stack([re.reshape(*lead, gn // c, c), im.reshape(*lead, gn // c, c)], axis=-2).reshape(*lead, 2 * gn)


def _deinterleave(cat, c):
    lead = cat.shape[:-1]
    gn = cat.shape[-1] // 2
    t = cat.reshape(*lead, gn // c, 2, c)
    return t[..., 0, :].reshape(*lead, gn), t[..., 1, :].reshape(*lead, gn)


def kernel(x, c, w_mod, b_mod, g_pre_mix, g_post_mix, g_pre_ffn, g_post_ffn, w_in, ssm_log_dt, ssm_a_re, ssm_a_im, ssm_b_re, ssm_b_im, ssm_c_re, ssm_c_im, ssm_d, w_glu, b_glu, conv_mix_w, w_ssm_out, w_attn_out, w_conv_out, b_gate, w_o, w_up, ffn_conv_w, w_down, loss_target, m_w_mod, m_b_mod, m_g_pre_mix, m_g_post_mix, m_g_pre_ffn, m_g_post_ffn, m_w_in, m_ssm_log_dt, m_ssm_a_re, m_ssm_a_im, m_ssm_b_re, m_ssm_b_im, m_ssm_c_re, m_ssm_c_im, m_ssm_d, m_w_glu, m_b_glu, m_conv_mix_w, m_w_ssm_out, m_w_attn_out, m_w_conv_out, m_b_gate, m_w_o, m_w_up, m_ffn_conv_w, m_w_down, v_w_mod, v_b_mod, v_g_pre_mix, v_g_post_mix, v_g_pre_ffn, v_g_post_ffn, v_w_in, v_ssm_log_dt, v_ssm_a_re, v_ssm_a_im, v_ssm_b_re, v_ssm_b_im, v_ssm_c_re, v_ssm_c_im, v_ssm_d, v_w_glu, v_b_glu, v_conv_mix_w, v_w_ssm_out, v_w_attn_out, v_w_conv_out, v_b_gate, v_w_o, v_w_up, v_ffn_conv_w, v_w_down):
    weights = dict(w_mod=w_mod, b_mod=b_mod, g_pre_mix=g_pre_mix, g_post_mix=g_post_mix, g_pre_ffn=g_pre_ffn, g_post_ffn=g_post_ffn, w_in=w_in, ssm_log_dt=ssm_log_dt, ssm_a_re=ssm_a_re, ssm_a_im=ssm_a_im, ssm_b_re=ssm_b_re, ssm_b_im=ssm_b_im, ssm_c_re=ssm_c_re, ssm_c_im=ssm_c_im, ssm_d=ssm_d, w_glu=w_glu, b_glu=b_glu, conv_mix_w=conv_mix_w, w_ssm_out=w_ssm_out, w_attn_out=w_attn_out, w_conv_out=w_conv_out, b_gate=b_gate, w_o=w_o, w_up=w_up, ffn_conv_w=ffn_conv_w, w_down=w_down)
    mom_m = dict(w_mod=m_w_mod, b_mod=m_b_mod, g_pre_mix=m_g_pre_mix, g_post_mix=m_g_post_mix, g_pre_ffn=m_g_pre_ffn, g_post_ffn=m_g_post_ffn, w_in=m_w_in, ssm_log_dt=m_ssm_log_dt, ssm_a_re=m_ssm_a_re, ssm_a_im=m_ssm_a_im, ssm_b_re=m_ssm_b_re, ssm_b_im=m_ssm_b_im, ssm_c_re=m_ssm_c_re, ssm_c_im=m_ssm_c_im, ssm_d=m_ssm_d, w_glu=m_w_glu, b_glu=m_b_glu, conv_mix_w=m_conv_mix_w, w_ssm_out=m_w_ssm_out, w_attn_out=m_w_attn_out, w_conv_out=m_w_conv_out, b_gate=m_b_gate, w_o=m_w_o, w_up=m_w_up, ffn_conv_w=m_ffn_conv_w, w_down=m_w_down)
    mom_v = dict(w_mod=v_w_mod, b_mod=v_b_mod, g_pre_mix=v_g_pre_mix, g_post_mix=v_g_post_mix, g_pre_ffn=v_g_pre_ffn, g_post_ffn=v_g_post_ffn, w_in=v_w_in, ssm_log_dt=v_ssm_log_dt, ssm_a_re=v_ssm_a_re, ssm_a_im=v_ssm_a_im, ssm_b_re=v_ssm_b_re, ssm_b_im=v_ssm_b_im, ssm_c_re=v_ssm_c_re, ssm_c_im=v_ssm_c_im, ssm_d=v_ssm_d, w_glu=v_w_glu, b_glu=v_b_glu, conv_mix_w=v_conv_mix_w, w_ssm_out=v_w_ssm_out, w_attn_out=v_w_attn_out, w_conv_out=v_w_conv_out, b_gate=v_b_gate, w_o=v_w_o, w_up=v_w_up, ffn_conv_w=v_ffn_conv_w, w_down=v_w_down)
    names = list(weights)

    nl = w_in.shape[0]
    seq, d = x.shape[1], x.shape[2]
    sw = d // 4
    groups = sw // SSM_GROUP
    gn = groups * SSM_STATE
    hp = sw // HEAD_DIM
    qw = 3 * sw
    off_q, off_k, off_v = sw, sw + qw, sw + 2 * qw
    off_conv = sw + 3 * qw
    off_gate = off_conv + 3 * sw
    n_in = off_gate + 3 * d
    f = w_down.shape[1] * N_CHIP
    scan_c = _scan_cols(gn)
    assert seq % (ATTN_BLOCK * DSWA_PATTERNS[-1][1]) == 0 and all(w // dl == ATTN_BLOCK for w, dl in DSWA_PATTERNS)

    px, py, pc = _my_pos()
    chip = 2 * px + py
    dev = 2 * chip + pc

    x2 = x.reshape(seq, d)
    target2 = loss_target.reshape(seq, d)

    mix_keys = ("w_in", "w_glu", "w_ssm_out", "w_attn_out", "w_conv_out", "w_o")
    ffn_keys = ("w_up", "w_down")
    col_sharded = ("w_in", "w_ssm_out", "w_attn_out", "w_conv_out", "w_up")
    n_stages = 2 * nl

    def stage_keys(stage):
        return ffn_keys if stage % 2 else mix_keys

    def blocked(k, buf8):
        r, cols = weights[k].shape[1:]
        return buf8.reshape(1, N_CHIP, r, cols) if k in col_sharded else buf8.reshape(1, 1, N_CHIP * r, cols)

    def begin_gather(stage, after):
        keys = stage_keys(stage)
        bufs = []
        for k in keys:
            r, cols = weights[k].shape[1:]
            bufs.append(_cast_own(weights[k], stage // 2, "cast_own").reshape(N_DEV, r // 2, cols))
        return _ici_start(bufs, None, _own_block, after, f"gather_start_{stage}")

    def end_gather(stage, pending, after):
        send_sems, recv_sems, bufs, _ = pending
        bufs = _ici_wait(send_sems, recv_sems, bufs, len(bufs), True, _own_block, after, f"gather_wait_{stage}")
        bufs = _forward_halves(bufs, "gather_forward")
        return {k: blocked(k, b) for k, b in zip(stage_keys(stage), bufs)}

    c_all = _ag8(c.reshape(1, 1, 1, d), False, "ag_cond").reshape(N_DEV, d)
    c_pad = jnp.concatenate([c_all, jnp.zeros((SUBLANE, d), F32)], axis=0)
    mcols = w_mod.shape[2]
    w_mod4 = w_mod.reshape(nl, 1, d, mcols)
    mod_loc = jnp.stack([_mm_nn(c_pad, w_mod4, l, name="mod_fwd", a_fn=jax.nn.silu) for l in range(nl)])
    mod_all = _ag8(mod_loc.reshape(nl, 1, 2 * SUBLANE, mcols), False, "ag_mod")
    mod_rows = lax.dynamic_slice_in_dim(mod_all[:, 0::2], dev, 1, axis=2)
    mod = mod_rows.reshape(nl, N_CHIP * mcols) + b_mod
    mods = mod.reshape(nl, 6, 1, d)

    taps = jnp.concatenate([conv_mix_w.reshape(-1), ffn_conv_w.reshape(-1)])
    tap_w = 8 * LANE
    tap_r = -(-taps.shape[0] // (tap_w * SUBLANE)) * SUBLANE
    taps = jnp.concatenate([taps, jnp.zeros((tap_r * tap_w - taps.shape[0],), F32)])
    taps_all = _ag8(taps.reshape(1, 1, tap_r, tap_w), False, "ag_small_weight")[0, 0::2].reshape(N_CHIP, -1)

    def whole(w, at):
        _, r, cols = w.shape
        got = taps_all[:, at:at + w.size].reshape(N_CHIP, nl, r, cols)
        return got.transpose(1, 2, 0, 3).reshape(nl, r, N_CHIP * cols)

    conv_w_full = whole(conv_mix_w, 0)
    ffn_w_full = whole(ffn_conv_w, conv_mix_w.size)
    wfs = [dict() for _ in range(nl)]
    pendings, order = [], mods[0, 0, :, :1] + taps_all[:1, :1]
    for stage in range(n_stages):
        pendings.append(begin_gather(stage, order))
        order = pendings[-1][3]
    wfs[0].update(end_gather(0, pendings[0], mods))

    head_ones = jnp.asarray(np.kron(np.eye(hp, dtype=np.float32), np.ones((HEAD_DIM, HEAD_DIM), np.float32)))
    slopes = [_alibi_slopes(p, hp) for p in range(len(DSWA_PATTERNS))]
    qkv_offs = (off_q, off_k, off_v)

    def row(v):
        return v.reshape(1, -1)

    saved = []
    xl = x2
    for l in range(nl):
        sh1, sc1, gt1, sh2, sc2, gt2 = [mods[l, q] for q in range(6)]
        s = dict(x_in=xl)
        wf = wfs[l]
        sh1_t = sh1 + order[0, 0] if l == 0 else sh1
        (h1,) = _ew_fwd(_fn_norm_mod, [(xl, 0, d)], [row(g_pre_mix[l]), sc1, sh1_t], [BF16], name="norm_mod_fwd", width=d)
        proj = _mm_nn(h1, wf["w_in"], 0, name="w_in_fwd")
        br_t = jnp.transpose(ssm_b_re[l], (2, 0, 1))
        bi_t = jnp.transpose(ssm_b_im[l], (2, 0, 1))
        disc_in = (ssm_log_dt[l].reshape(groups, 1), ssm_a_re[l], ssm_a_im[l], br_t, bi_t)
        lr, li, bbr_t, bbi_t = _ssm_disc_fwd(*disc_in)
        lam = _interleave(lr.reshape(1, gn), li.reshape(1, gn), scan_c)
        bcat = _interleave(_block_diag(jnp.transpose(bbr_t, (1, 0, 2))), _block_diag(jnp.transpose(bbi_t, (1, 0, 2))),
                           scan_c).astype(BF16).reshape(1, 1, sw, 2 * gn)
        cre = _block_diag(jnp.transpose(ssm_c_re[l], (0, 2, 1)))
        cim = _block_diag(jnp.transpose(ssm_c_im[l], (0, 2, 1)))
        ccat = jnp.transpose(_interleave(cre.T, -cim.T, scan_c)).astype(BF16).reshape(1, 1, 2 * gn, sw)
        xcat = _mm_nn(proj, bcat, 0, name="ssm_b_fwd", k_dim=sw)
        hcat = _ssm_scan_fwd(xcat, lam)
        y_ssm_pre = _mm_nn(hcat, ccat, 0, name="ssm_c_fwd")
        (gact,) = _ew_fwd(_fn_gelu, [(y_ssm_pre, 0, sw), (proj, 0, sw)], [row(ssm_d[l])], [F32], name="gelu_fwd", width=sw,
                          tw=_tile(sw, 512))
        z = _mm_nn(gact, wf["w_glu"], 0, name="w_glu_fwd")
        (s_ssm,) = _ew_fwd(_fn_glu, [(gact, 0, sw), (z, 0, sw)], [row(b_glu[l])], [BF16], name="glu_fwd", width=sw,
                           tw=_tile(sw, 512))
        y_ssm = _mm_nn(s_ssm, wf["w_ssm_out"], 0, name="w_branch_out_fwd")
        outs, lses = [], []
        for p, (_, dl) in enumerate(DSWA_PATTERNS):
            o_p, lse_p = _attn_fwd(proj, qkv_offs, p, dl, sw, slopes[p])
            outs.append(o_p)
            lses.append(lse_p)
        o_attn, lse_attn = _attn_merge(outs, lses)
        y_attn = _mm_nn(o_attn, wf["w_attn_out"], 0, name="w_branch_out_fwd")
        cv = _gconv_fwd(proj, off_conv, sw, conv_w_full[l])
        y_conv = _mm_nn(cv, wf["w_conv_out"], 0, name="w_branch_out_fwd")
        bg = b_gate[l].reshape(3, 1, d)
        gate_xs = [(proj, off_gate + q * d, d) for q in range(3)] + [(y_ssm, 0, d), (y_attn, 0, d), (y_conv, 0, d)]
        (merged,) = _ew_fwd(_fn_gates, gate_xs, [bg[0], bg[1], bg[2]], [BF16], name="gates_fwd", width=d,
                            tw=_tile(sw, 512))
        y_mix = _mm_nn(merged, wf["w_o"], 0, name="w_o_fwd")
        (x_mid,) = _ew_fwd(_fn_residual, [(xl, 0, d), (y_mix, 0, d)], [gt1, row(g_post_mix[l])], [F32], name="residual_fwd",
                           width=d)
        wf.update(end_gather(2 * l + 1, pendings[2 * l + 1], x_mid))
        (h2,) = _ew_fwd(_fn_norm_mod, [(x_mid, 0, d)], [row(g_pre_ffn[l]), sc2, sh2], [BF16], name="norm_mod_fwd", width=d)
        up = _mm_nn(h2, wf["w_up"], 0, name="w_up_fwd")
        act = _ffn_act_fwd(up, ffn_w_full[l])
        y_ffn = _mm_nn(act, wf["w_down"], 0, name="w_down_fwd")
        (x_out,) = _ew_fwd(_fn_residual, [(x_mid, 0, d), (y_ffn, 0, d)], [gt2, row(g_post_ffn[l])], [F32],
                           name="residual_fwd", width=d)
        if l + 1 < nl:
            wfs[l + 1].update(end_gather(2 * l + 2, pendings[2 * l + 2], x_out))
        s.update(h1=h1, proj=proj, disc_in=disc_in, lam=lam, bcat=bcat, ccat=ccat, hcat=hcat, y_ssm_pre=y_ssm_pre,
                 gact=gact, z=z, s_ssm=s_ssm, y_ssm=y_ssm, o_attn=o_attn, lse_attn=lse_attn, y_attn=y_attn,
                 cv=cv, y_conv=y_conv, merged=merged, y_mix=y_mix, x_mid=x_mid, h2=h2, up=up, act=act, y_ffn=y_ffn)
        saved.append(s)
        xl = x_out

    dxl, loss_local = _loss_fwd_bwd(xl, target2)
    loss = lax.psum(loss_local, ("x", "y", "c"))

    gfin = {k: lax.empty((nl, 2, weights[k].shape[1] // 2, weights[k].shape[2]), F32) for k in mix_keys + ffn_keys}

    def begin_rs(stage, grads_4d):
        keys = stage_keys(stage)
        g8s = [grads_4d[k].reshape(N_DEV, weights[k].shape[1] // 2, weights[k].shape[2]) for k in keys]
        recv_a = _rs_sibling(g8s, "rs_sibling")
        s_rem = [_rs_add_remote(g, ra, "rs_add_remote") for g, ra in zip(g8s, recv_a)]
        lands = [lax.empty(t.shape, BF16) for t in s_rem]
        return stage, g8s, recv_a, _ici_start(s_rem, lands, _distance_slot, recv_a[0], f"rs_start_{stage}")

    def end_rs(pending_rs, after):
        stage, g8s, recv_a, (send_sems, recv_sems, arrays, _) = pending_rs
        n = len(g8s)
        arrays = _ici_wait(send_sems, recv_sems, arrays, n, False, _distance_slot, after, f"rs_wait_{stage}")
        for k, g8, ra, rb in zip(stage_keys(stage), g8s, recv_a, arrays[n:]):
            gfin[k] = _rs_add_final(g8, ra, rb, gfin[k], stage // 2, "rs_add_final")

    def after_rs_start(v, pending_rs):
        return v if pending_rs is None else v + pending_rs[3][3][0, 0]

    def grad_buf(k):
        return lax.empty(wf[k].shape, F32)

    pending_rs = None
    small = {k: [None] * nl for k in ("g_pre_mix", "g_post_mix", "g_pre_ffn", "g_post_ffn", "ssm_log_dt", "ssm_a_re",
                                      "ssm_a_im", "ssm_b_re", "ssm_b_im", "ssm_c_re", "ssm_c_im", "ssm_d", "b_glu",
                                      "conv_mix_w", "b_gate", "ffn_conv_w", "dmod")}
    for l in reversed(range(nl)):
        s = saved[l]
        sh1, sc1, gt1, sh2, sc2, gt2 = [mods[l, q] for q in range(6)]
        proj = s["proj"]
        wf = wfs[l]
        gw = {}
        (dy_ffn,), (dgt2, dg_post_ffn) = _ew_bwd(
            _fn_residual, [(s["x_mid"], 0, d), (s["y_ffn"], 0, d)], [after_rs_start(gt2, pending_rs), row(g_post_ffn[l])],
            [[dxl]], [None, BF16], name="residual_bwd", width=d)
        dact = _mm_nt(dy_ffn, wf["w_down"], 0, name="w_down_bwd_x", out_dtype=BF16)
        gw["w_down"] = _mm_tn(s["act"], dy_ffn, grad_buf("w_down"), 0, name="w_down_bwd_w")
        dup_a, dup_b, dwa, dwb = _ffn_act_bwd(s["up"], ffn_w_full[l], dact)
        dup = jnp.concatenate([dup_a, dup_b], axis=1)
        small["ffn_conv_w"][l] = jnp.concatenate([dwa, dwb], axis=1)
        dh2 = _mm_nt(dup, wf["w_up"], 0, name="w_up_bwd_x")
        gw["w_up"] = _mm_tn(s["h2"], dup, grad_buf("w_up"), 0, name="w_up_bwd_w")
        (dx_mid,), (dg_pre_ffn, dsc2, dsh2) = _ew_bwd(
            _fn_norm_mod, [(s["x_mid"], 0, d)], [row(g_pre_ffn[l]), sc2, sh2], [[dh2]], [F32], name="norm_mod_bwd", width=d,
            dx_add={0: dxl})
        if pending_rs is not None:
            end_rs(pending_rs, dx_mid)
        pending_rs = begin_rs(2 * l + 1, gw)
        (dy_mix,), (dgt1, dg_post_mix) = _ew_bwd(
            _fn_residual, [(s["x_in"], 0, d), (s["y_mix"], 0, d)], [after_rs_start(gt1, pending_rs), row(g_post_mix[l])],
            [[dx_mid]], [None, BF16], name="residual_bwd", width=d)
        dmerged = _mm_nt(dy_mix, wf["w_o"], 0, name="w_o_bwd_x")
        gw["w_o"] = _mm_tn(s["merged"], dy_mix, grad_buf("w_o"), 0, name="w_o_bwd_w")
        bg = b_gate[l].reshape(3, 1, d)
        gate_xs = [(proj, off_gate + q * d, d) for q in range(3)] + [(s["y_ssm"], 0, d), (s["y_attn"], 0, d),
                                                                     (s["y_conv"], 0, d)]
        (dp0, dp1, dp2, dy_ssm, dy_attn, dy_conv), dbg = _ew_bwd(
            _fn_gates, gate_xs, [bg[0], bg[1], bg[2]], [[dmerged]], [BF16] * 6, name="gates_bwd", width=d,
            tw=_tile(sw, 512))
        small["b_gate"][l] = jnp.concatenate(dbg, axis=1)[0]
        ds_ssm = _mm_nt(dy_ssm, wf["w_ssm_out"], 0, name="w_branch_out_bwd_x")
        gw["w_ssm_out"] = _mm_tn(s["s_ssm"], dy_ssm, grad_buf("w_ssm_out"), 0, name="w_branch_out_bwd_w")
        (dg1, dz), (db_glu,) = _ew_bwd(_fn_glu, [(s["gact"], 0, sw), (s["z"], 0, sw)], [row(b_glu[l])], [[ds_ssm]],
                                       [F32, BF16], name="glu_bwd", width=sw, tw=_tile(sw, 512))
        dg2 = _mm_nt(dz, wf["w_glu"], 0, name="w_glu_bwd_x")
        gw["w_glu"] = _mm_tn(s["gact"], dz, grad_buf("w_glu"), 0, name="w_glu_bwd_w")
        (dy_pre, du_skip), (dd_skip,) = _ew_bwd(_fn_gelu, [(s["y_ssm_pre"], 0, sw), (proj, 0, sw)], [row(ssm_d[l])],
                                               [[dg1, dg2]], [BF16, F32], name="gelu_bwd", width=sw, tw=_tile(sw, 512))
        dhcat = _mm_nt(dy_pre, s["ccat"], 0, name="ssm_c_bwd_x")
        dccat = _mm_tn(s["hcat"], dy_pre, lax.empty((1, 1, 2 * gn, sw), F32), 0, name="ssm_c_bwd_w")[0, 0]
        gcat, dlam = _ssm_scan_bwd(dhcat, s["hcat"], s["lam"])
        du_b = _mm_nt(gcat, s["bcat"], 0, name="ssm_b_bwd_x")
        dbcat = _mm_tn(proj, gcat, lax.empty((1, 1, sw, 2 * gn), F32), 0, name="ssm_b_bwd_w")[0, 0]
        dlr, dli = _deinterleave(dlam, scan_c)
        dbre, dbim = _deinterleave(dbcat, scan_c)
        dbbr_t = jnp.transpose(_diag_blocks(dbre, groups), (1, 0, 2))
        dbbi_t = jnp.transpose(_diag_blocks(dbim, groups), (1, 0, 2))
        gld, gar, gai, gbr_t, gbi_t = _ssm_disc_bwd(*s["disc_in"], dlr.reshape(groups, SSM_STATE),
                                                    dli.reshape(groups, SSM_STATE), dbbr_t, dbbi_t)
        dcre_t, dcim_t = _deinterleave(dccat.T, scan_c)
        small["ssm_c_re"][l] = _diag_blocks(dcre_t, groups)
        small["ssm_c_im"][l] = -_diag_blocks(dcim_t, groups)
        small["ssm_log_dt"][l] = gld.reshape(groups)
        small["ssm_a_re"][l], small["ssm_a_im"][l] = gar, gai
        small["ssm_b_re"][l] = jnp.transpose(gbr_t, (1, 2, 0))
        small["ssm_b_im"][l] = jnp.transpose(gbi_t, (1, 2, 0))
        small["ssm_d"][l], small["b_glu"][l] = dd_skip[0], db_glu[0]
        du = (du_skip + du_b).astype(BF16)
        do_attn = _mm_nt(dy_attn, wf["w_attn_out"], 0, name="w_branch_out_bwd_x")
        gw["w_attn_out"] = _mm_tn(s["o_attn"], dy_attn, grad_buf("w_attn_out"), 0, name="w_branch_out_bwd_w")
        delta = _attn_delta(do_attn, s["o_attn"], head_ones)
        dqs, dks, dvs = [], [], []
        for p, (_, dl) in enumerate(DSWA_PATTERNS):
            dq_p, dk_p, dv_p = _attn_bwd(proj, do_attn, s["lse_attn"], delta, qkv_offs, p, dl, sw, slopes[p])
            dqs.append(dq_p)
            dks.append(dk_p)
            dvs.append(dv_p)
        dcv = _mm_nt(dy_conv, wf["w_conv_out"], 0, name="w_branch_out_bwd_x", out_dtype=BF16)
        gw["w_conv_out"] = _mm_tn(s["cv"], dy_conv, grad_buf("w_conv_out"), 0, name="w_branch_out_bwd_w")
        dcb, dcc, dch, dconv_w = _gconv_bwd(proj, off_conv, sw, conv_w_full[l], dcv)
        small["conv_mix_w"][l] = dconv_w
        dproj = jnp.concatenate([du] + [t.astype(BF16) for t in dqs + dks + dvs] + [dcb, dcc, dch, dp0, dp1, dp2],
                                axis=1)
        dh1 = _mm_nt(dproj, wf["w_in"], 0, name="w_in_bwd_x")
        gw["w_in"] = _mm_tn(s["h1"], dproj, grad_buf("w_in"), 0, name="w_in_bwd_w")
        (dx_in,), (dg_pre_mix, dsc1, dsh1) = _ew_bwd(
            _fn_norm_mod, [(s["x_in"], 0, d)], [row(g_pre_mix[l]), sc1, sh1], [[dh1]], [F32], name="norm_mod_bwd", width=d,
            dx_add={0: dx_mid})
        end_rs(pending_rs, dx_in)
        pending_rs = begin_rs(2 * l, gw)
        small["g_pre_mix"][l], small["g_post_mix"][l] = dg_pre_mix[0], dg_post_mix[0]
        small["g_pre_ffn"][l], small["g_post_ffn"][l] = dg_pre_ffn[0], dg_post_ffn[0]
        small["dmod"][l] = jnp.concatenate([dsh1, dsc1, dgt1, dsh2, dsc2, dgt2], axis=1)[0]
        dxl = dx_in

    grad_x = dxl.reshape(x.shape)

    small = {k: jnp.stack(v) for k, v in small.items()}
    order = sorted(small)
    flat = jnp.concatenate([small[k].reshape(-1) for k in order])
    n_small = flat.shape[0]
    pack_w = 8 * LANE
    pack_r = -(-n_small // (pack_w * SUBLANE)) * SUBLANE
    flat = jnp.concatenate([flat, jnp.zeros((pack_r * pack_w - n_small,), F32)])
    gathered = _ag8(flat.reshape(1, 1, pack_r, pack_w), False, "ag_small_grads")[0]
    summed = _sum8(gathered, "sum_small_grads").reshape(-1)
    sgrad, at = {}, 0
    for k in order:
        size = small[k].size
        sgrad[k] = summed[at:at + size].reshape(small[k].shape)
        at += size
    dmod_off = sum(small[k].size for k in order[:order.index("dmod")])
    dmod_all = gathered.reshape(N_DEV, -1)[:, dmod_off:dmod_off + nl * 6 * d].reshape(N_DEV, nl, 6 * d)
    dmod_loc = lax.dynamic_slice_in_dim(jnp.transpose(dmod_all, (1, 0, 2)), chip * mcols, mcols, axis=2)

    grads = dict(sgrad)
    grads["b_mod"] = grads.pop("dmod")
    grads["conv_mix_w"] = lax.dynamic_slice_in_dim(sgrad["conv_mix_w"], chip * conv_mix_w.shape[2], conv_mix_w.shape[2], axis=2)
    grads["ffn_conv_w"] = lax.dynamic_slice_in_dim(sgrad["ffn_conv_w"], chip * ffn_conv_w.shape[2], ffn_conv_w.shape[2], axis=2)

    end_rs(pending_rs, summed)
    for k in mix_keys + ffn_keys:
        grads[k] = _share_halves(gfin[k], "rs_share").reshape(weights[k].shape)

    delta_w, new_m, new_v = {}, {}, {}
    c_t = jnp.pad(jnp.transpose(c_all), ((0, 0), (0, LANE - N_DEV)))
    dmod_pad = jnp.pad(dmod_loc, ((0, 0), (0, LANE - N_DEV), (0, 0)))
    grads["w_mod"], delta_w["w_mod"], new_m["w_mod"], new_v["w_mod"] = _wmod_grad_adam(
        c_t, dmod_pad, w_mod, m_w_mod, v_w_mod)
    for k in names:
        if k == "w_mod":
            continue
        delta_w[k], new_m[k], new_v[k] = _adam(weights[k], grads[k], mom_m[k], mom_v[k], "adamw")

    return (loss, grad_x, *[grads[k] for k in names], *[delta_w[k] for k in names], *[new_m[k] for k in names],
            *[new_v[k] for k in names])
```

```python
import functools
import math

import numpy as np
import jax
import jax.numpy as jnp
from jax import lax
from jax.experimental import pallas as pl
from jax.experimental.pallas import tpu as pltpu

F32 = jnp.float32
BF16 = jnp.bfloat16
MESH = pl.DeviceIdType.MESH
ANY = pl.BlockSpec(memory_space=pl.ANY)

VMEM_LIMIT_BYTES = 48 * 1024 * 1024
LANE = 128
SUBLANE = 8

RMS_EPS = 1e-6
NEG_INF = -1e30
SSM_GROUP = 16
SSM_STATE = 64
HEAD_DIM = 64
DSWA_PATTERNS = ((128, 1), (512, 4), (2048, 16))
ATTN_BLOCK = 128
N_DEV = 8
N_CHIP = 4

ADAM_LR = 0.001
ADAM_B1 = 0.9
ADAM_B2 = 0.999
ADAM_EPS = 1e-08
ADAM_WD = 0.01
ADAM_STEP = 10


def _cp(sem=None):
    return pltpu.CompilerParams(dimension_semantics=sem, vmem_limit_bytes=VMEM_LIMIT_BYTES)


def _tile(n, pref, align=LANE):
    if n <= pref:
        return n
    t = (pref // align) * align
    while t >= align:
        if n % t == 0:
            return t
        t -= align
    return n


def _sds(shape, dtype):
    return jax.ShapeDtypeStruct(tuple(shape), dtype)


MM_VMEM_BUDGET = 34 * 1024 * 1024
MM_MAX_CONTRACT = 2048


def _divisors(n, align):
    if n % align:
        return [n]
    return [t for t in range(n, 0, -align) if n % t == 0]


def _halvings(n, align, floor=256):
    out = [n]
    while out[-1] % (2 * align) == 0 and out[-1] // 2 >= floor:
        out.append(out[-1] // 2)
    return out


def _pick_tiles(rows, cols, fixed_bytes, row_bytes, col_bytes, cell_bytes):
    best = None
    for tr in rows:
        for tc in cols:
            if fixed_bytes + row_bytes * tr + col_bytes * tc + cell_bytes * tr * tc <= MM_VMEM_BUDGET:
                if best is None or tr * tc > best[0] * best[1]:
                    best = (tr, tc)
                break
    assert best is not None
    return best


def _accumulate(step, n_steps, part, o_ref, acc_ref):
    if n_steps == 1:
        o_ref[...] = part.astype(o_ref.dtype)
        return
    acc = o_ref if acc_ref is None else acc_ref

    @pl.when(step == 0)
    def _():
        acc[...] = part

    @pl.when(step > 0)
    def _():
        acc[...] += part

    if acc_ref is not None:
        @pl.when(step == n_steps - 1)
        def _():
            o_ref[...] = acc_ref[...].astype(o_ref.dtype)


def _mm_nn(a, w, layer, *, name, k_dim=None, a_col0=0, out_dtype=F32, a_fn=None):
    m = a.shape[0]
    _, nb, kw, n = w.shape
    k_dim = kw if k_dim is None else k_dim
    assert k_dim == kw
    tk = _tile(k_dim, MM_MAX_CONTRACT)
    nk = k_dim // tk
    sa, so = a.dtype.itemsize, jnp.dtype(out_dtype).itemsize
    use_acc = nk > 1 and so != 4
    row_bytes = tk * (2 * sa + (2 if sa == 4 else 0) + (4 if a_fn is not None else 0))
    tm, tn = _pick_tiles(_halvings(m, SUBLANE), _divisors(n, LANE), 0, row_bytes, 2 * tk * w.dtype.itemsize,
                         2 * so + 4 + (4 if use_acc else 0))
    assert a_col0 % tk == 0
    npb = n // tn
    a0 = a_col0 // tk

    def body(a_ref, w_ref, o_ref, *scratch):
        av = a_ref[...]
        if a_fn is not None:
            av = a_fn(av.astype(F32))
        part = jnp.dot(av.astype(BF16), w_ref[...].astype(BF16), preferred_element_type=F32)
        _accumulate(pl.program_id(2), nk, part, o_ref, scratch[0] if use_acc else None)

    return pl.pallas_call(
        body,
        out_shape=_sds((m, nb * n), out_dtype),
        grid=(m // tm, nb * npb, nk),
        in_specs=[pl.BlockSpec((tm, tk), lambda i, j, k: (i, a0 + k)),
                  pl.BlockSpec((None, None, tk, tn), lambda i, j, k: (layer, j // npb, k, j % npb))],
        out_specs=pl.BlockSpec((tm, tn), lambda i, j, k: (i, j)),
        scratch_shapes=[pltpu.VMEM((tm, tn), F32)] if use_acc else [],
        compiler_params=_cp(("parallel", "parallel", "arbitrary")),
        name=name,
    )(a, w)


def _mm_nt(g, w, layer, *, name, out_dtype=F32):
    m = g.shape[0]
    _, nb, k_dim, n = w.shape
    assert g.shape[1] == nb * n
    tko = _tile(k_dim, MM_MAX_CONTRACT)
    sg, so = g.dtype.itemsize, jnp.dtype(out_dtype).itemsize
    use_acc = so != 4
    res_row = tko * (2 * so + 4 + (4 if use_acc else 0))
    tm, tc = _pick_tiles(_halvings(m, SUBLANE), _divisors(n, LANE), 0, res_row, 2 * tko * w.dtype.itemsize,
                         2 * sg + (2 if sg == 4 else 0))
    npb = n // tc
    nr = nb * npb
    use_acc = use_acc and nr > 1

    def body(g_ref, w_ref, o_ref, *scratch):
        part = lax.dot_general(g_ref[...].astype(BF16), w_ref[...].astype(BF16),
                               (((1,), (1,)), ((), ())), preferred_element_type=F32)
        _accumulate(pl.program_id(2), nr, part, o_ref, scratch[0] if use_acc else None)

    return pl.pallas_call(
        body,
        out_shape=_sds((m, k_dim), out_dtype),
        grid=(m // tm, k_dim // tko, nr),
        in_specs=[pl.BlockSpec((tm, tc), lambda i, kk, r: (i, r)),
                  pl.BlockSpec((None, None, tko, tc), lambda i, kk, r: (layer, r // npb, kk, r % npb))],
        out_specs=pl.BlockSpec((tm, tko), lambda i, kk, r: (i, kk)),
        scratch_shapes=[pltpu.VMEM((tm, tko), F32)] if use_acc else [],
        compiler_params=_cp(("parallel", "parallel", "arbitrary")),
        name=name,
    )(g, w)


def _mm_tn(a, g, out_buf, layer, *, name, a_col0=0):
    m = a.shape[0]
    _, nb, k_dim, n = out_buf.shape
    assert g.shape == (m, nb * n)
    tm = _tile(m, MM_MAX_CONTRACT, SUBLANE)
    nr = m // tm
    sa, sg = a.dtype.itemsize, g.dtype.itemsize
    tk, tn = _pick_tiles(_halvings(k_dim, LANE), _divisors(n, LANE), 0, tm * (2 * sa + (2 if sa == 4 else 0) + 2),
                         tm * (2 * sg + (2 if sg == 4 else 0)), 2 * 4 + 4)
    assert a_col0 % tk == 0
    a0 = a_col0 // tk
    npb = n // tn

    def body(a_ref, g_ref, buf_ref, o_ref):
        del buf_ref
        part = lax.dot_general(a_ref[...].astype(BF16), g_ref[...].astype(BF16),
                               (((0,), (0,)), ((), ())), preferred_element_type=F32)
        _accumulate(pl.program_id(2), nr, part, o_ref, None)

    return pl.pallas_call(
        body,
        out_shape=_sds(out_buf.shape, F32),
        grid=(k_dim // tk, nb * npb, nr),
        in_specs=[pl.BlockSpec((tm, tk), lambda kk, j, r: (r, a0 + kk)),
                  pl.BlockSpec((tm, tn), lambda kk, j, r: (r, j)),
                  ANY],
        out_specs=pl.BlockSpec((None, None, tk, tn), lambda kk, j, r: (layer, j // npb, kk, j % npb)),
        input_output_aliases={2: 0},
        compiler_params=_cp(("parallel", "parallel", "arbitrary")),
        name=name,
    )(a, g, out_buf)


def _ew_fwd(fn, xs, ps, out_dtypes, *, name, width, tw=None, tm=256):
    rows = xs[0][0].shape[0]
    tm = _tile(rows, tm, SUBLANE)
    tw = width if tw is None else tw
    nx, n_p = len(xs), len(ps)

    def body(*refs):
        xv = [r[...].astype(F32) for r in refs[:nx]]
        pv = [r[...].astype(F32) for r in refs[nx:nx + n_p]]
        outs = fn(*xv, *pv)
        if not isinstance(outs, (tuple, list)):
            outs = (outs,)
        for o_ref, o in zip(refs[nx + n_p:], outs):
            o_ref[...] = o.astype(o_ref.dtype)

    in_specs = []
    for arr, c0, w in xs:
        assert w == width and c0 % tw == 0
        in_specs.append(pl.BlockSpec((tm, tw), functools.partial(lambda i, j, b: (i, b + j), b=c0 // tw)))
    for p in ps:
        assert p.shape == (1, width)
        in_specs.append(pl.BlockSpec((1, tw), lambda i, j: (0, j)))
    outs = pl.pallas_call(
        body,
        out_shape=[_sds((rows, width), d) for d in out_dtypes],
        grid=(rows // tm, width // tw),
        in_specs=in_specs,
        out_specs=[pl.BlockSpec((tm, tw), lambda i, j: (i, j)) for _ in out_dtypes],
        compiler_params=_cp(("parallel", "parallel")),
        name=name,
    )(*[x[0] for x in xs], *ps)
    return outs


def _ew_bwd(fn, xs, ps, cts, dx_dtypes, *, name, width, tw=None, tm=256, dx_add=None):
    rows = xs[0][0].shape[0]
    tm = _tile(rows, tm, SUBLANE)
    tw = width if tw is None else tw
    nx, n_p = len(xs), len(ps)
    dx_add = dx_add or {}
    flat_cts = [c for group in cts for c in group]
    add_keys = sorted(dx_add)
    n_in = nx + n_p + len(flat_cts) + len(add_keys)
    dx_idx = [i for i, d in enumerate(dx_dtypes) if d is not None]

    def body(*refs):
        i = pl.program_id(1)
        xv = [r[...].astype(F32) for r in refs[:nx]]
        pv = [r[...].astype(F32) for r in refs[nx:nx + n_p]]
        pos = nx + n_p
        ct_vals = []
        for group in cts:
            acc = refs[pos][...].astype(F32)
            pos += 1
            for _ in group[1:]:
                acc = acc + refs[pos][...].astype(F32)
                pos += 1
            ct_vals.append(acc)
        add_vals = {}
        for key in add_keys:
            add_vals[key] = refs[pos][...].astype(F32)
            pos += 1
        out_refs = refs[n_in:]
        outs, vjp = jax.vjp(fn, *xv, *pv)
        grads = vjp(tuple(ct_vals) if isinstance(outs, (tuple, list)) else ct_vals[0])
        o = 0
        for idx in dx_idx:
            gval = grads[idx]
            if idx in add_vals:
                gval = gval + add_vals[idx]
            out_refs[o][...] = gval.astype(out_refs[o].dtype)
            o += 1
        for q in range(n_p):
            gp = grads[nx + q]
            ref = out_refs[o + q]

            @pl.when(i == 0)
            def _(ref=ref, gp=gp):
                ref[...] = gp

            @pl.when(i > 0)
            def _(ref=ref, gp=gp):
                ref[...] += gp

    tile_spec = pl.BlockSpec((tm, tw), lambda j, i: (i, j))
    in_specs = []
    for arr, c0, w in xs:
        assert w == width and c0 % tw == 0
        in_specs.append(pl.BlockSpec((tm, tw), functools.partial(lambda j, i, b: (i, b + j), b=c0 // tw)))
    for p in ps:
        in_specs.append(pl.BlockSpec((1, tw), lambda j, i: (0, j)))
    in_specs += [tile_spec] * (len(flat_cts) + len(add_keys))
    out_shape = [_sds((rows, width), dx_dtypes[idx]) for idx in dx_idx] + [_sds((1, width), F32)] * n_p
    out_specs = [tile_spec] * len(dx_idx) + [pl.BlockSpec((1, tw), lambda j, i: (0, j))] * n_p
    outs = pl.pallas_call(
        body,
        out_shape=out_shape,
        grid=(width // tw, rows // tm),
        in_specs=in_specs,
        out_specs=out_specs,
        compiler_params=_cp(("parallel", "arbitrary")),
        name=name,
    )(*[x[0] for x in xs], *ps, *flat_cts, *[dx_add[k] for k in add_keys])
    return outs[:len(dx_idx)], outs[len(dx_idx):]


def _rms(x):
    return x * lax.rsqrt(jnp.mean(x * x, axis=-1, keepdims=True) + RMS_EPS)


def _fn_norm_mod(x, g, sc, sh):
    return (_rms(x) * g) * (1.0 + sc) + sh


def _fn_residual(x, y, gt, g):
    return x + gt * (_rms(y) * g)


def _fn_gelu(y, u, d):
    return jax.nn.gelu(y + d * u)


def _fn_glu(g, z, b):
    return g * jax.nn.sigmoid(z + b)


def _fn_gates(p0, p1, p2, ys, ya, yc, b0, b1, b2):
    return (jax.nn.sigmoid(p0 + b0) * ys + jax.nn.sigmoid(p1 + b1) * ya + jax.nn.sigmoid(p2 + b2) * yc)


def _fn_disc(log_dt, ar, ai, br_t, bi_t):
    dt = jnp.exp(log_dt)
    mag = jnp.exp(ar * dt)
    lr, li = mag * jnp.cos(ai * dt), mag * jnp.sin(ai * dt)
    den = ar * ar + ai * ai
    fr = ((lr - 1.0) * ar + li * ai) / den
    fi = (li * ar - (lr - 1.0) * ai) / den
    bbr = fr[None] * br_t - fi[None] * bi_t
    bbi = fr[None] * bi_t + fi[None] * br_t
    return lr, li, bbr, bbi


def _ssm_disc_fwd(log_dt, ar, ai, br_t, bi_t):
    g, n = ar.shape

    def body(ld_ref, ar_ref, ai_ref, br_ref, bi_ref, lr_ref, li_ref, bbr_ref, bbi_ref):
        lr, li, bbr, bbi = _fn_disc(ld_ref[...], ar_ref[...], ai_ref[...], br_ref[...], bi_ref[...])
        lr_ref[...] = lr
        li_ref[...] = li
        bbr_ref[...] = bbr
        bbi_ref[...] = bbi

    return pl.pallas_call(
        body,
        out_shape=[_sds((g, n), F32), _sds((g, n), F32), _sds(br_t.shape, F32), _sds(br_t.shape, F32)],
        compiler_params=_cp(),
        name="ssm_disc_fwd",
    )(log_dt, ar, ai, br_t, bi_t)


def _ssm_disc_bwd(log_dt, ar, ai, br_t, bi_t, dlr, dli, dbbr, dbbi):
    g, n = ar.shape

    def body(ld_ref, ar_ref, ai_ref, br_ref, bi_ref, dlr_ref, dli_ref, dbbr_ref, dbbi_ref,
             gld_ref, gar_ref, gai_ref, gbr_ref, gbi_ref):
        _, vjp = jax.vjp(_fn_disc, ld_ref[...], ar_ref[...], ai_ref[...], br_ref[...], bi_ref[...])
        gld, gar, gai, gbr, gbi = vjp((dlr_ref[...], dli_ref[...], dbbr_ref[...], dbbi_ref[...]))
        gld_ref[...] = gld
        gar_ref[...] = gar
        gai_ref[...] = gai
        gbr_ref[...] = gbr
        gbi_ref[...] = gbi

    return pl.pallas_call(
        body,
        out_shape=[_sds((g, 1), F32), _sds((g, n), F32), _sds((g, n), F32), _sds(br_t.shape, F32),
                   _sds(br_t.shape, F32)],
        compiler_params=_cp(),
        name="ssm_disc_bwd",
    )(log_dt, ar, ai, br_t, bi_t, dlr, dli, dbbr, dbbi)


def _cmul(ar, ai, br, bi):
    return ar * br - ai * bi, ar * bi + ai * br


def _scan_tables(lr, li, reverse):
    c = lr.shape[-1]
    p1 = (jnp.broadcast_to(lr, (SUBLANE, c)), jnp.broadcast_to(li, (SUBLANE, c)))
    p2 = _cmul(*p1, *p1)
    p4 = _cmul(*p2, *p2)
    p8 = _cmul(*p4, *p4)
    row = lax.broadcasted_iota(jnp.int32, (SUBLANE, c), 0)
    dist = (SUBLANE - row) if reverse else (row + 1)
    pr, pi = jnp.ones((SUBLANE, c), F32), jnp.zeros((SUBLANE, c), F32)
    for bit, pw in ((1, p1), (2, p2), (4, p4), (8, p8)):
        qr, qi = _cmul(pr, pi, *pw)
        take = (dist & bit) != 0
        pr, pi = jnp.where(take, qr, pr), jnp.where(take, qi, pi)
    return row, (p1, p2, p4), (pr, pi)


def _shift_rows(x, s, row, reverse):
    if reverse:
        return jnp.where(row < SUBLANE - s, pltpu.roll(x, SUBLANE - s, 0), 0.0)
    return jnp.where(row >= s, pltpu.roll(x, s, 0), 0.0)


def _scan_tile(xr, xi, carry, row, pows, carry_pow, reverse):
    for s, pw in zip((1, 2, 4), pows):
        sr, si = _shift_rows(xr, s, row, reverse), _shift_rows(xi, s, row, reverse)
        tr, ti = _cmul(*pw, sr, si)
        xr, xi = xr + tr, xi + ti
    tr, ti = _cmul(*carry_pow, *carry)
    hr, hi = xr + tr, xi + ti
    edge = 0 if reverse else SUBLANE - 1
    c = hr.shape[-1]
    new_carry = (jnp.broadcast_to(hr[edge:edge + 1, :], (SUBLANE, c)),
                 jnp.broadcast_to(hi[edge:edge + 1, :], (SUBLANE, c)))
    return hr, hi, new_carry


def _scan_cols(gn):
    return _tile(gn, 256)


def _ssm_scan_fwd(xcat, lam):
    rows, gn2 = xcat.shape
    c = _scan_cols(gn2 // 2)
    n_tiles = rows // SUBLANE

    def body(lam_ref, x_ref, h_ref):
        lr, li = lam_ref[:, :c], lam_ref[:, c:]
        row, pows, carry_pow = _scan_tables(lr, li, False)

        def step(k, carry):
            t0 = pl.multiple_of(k * SUBLANE, SUBLANE)
            hr, hi, carry = _scan_tile(x_ref[pl.ds(t0, SUBLANE), :c], x_ref[pl.ds(t0, SUBLANE), c:], carry,
                                       row, pows, carry_pow, False)
            h_ref[pl.ds(t0, SUBLANE), :c] = hr
            h_ref[pl.ds(t0, SUBLANE), c:] = hi
            return carry

        zero = jnp.zeros((SUBLANE, c), F32)
        lax.fori_loop(0, n_tiles, step, (zero, zero))

    return pl.pallas_call(
        body,
        out_shape=_sds((rows, gn2), F32),
        grid=(gn2 // (2 * c),),
        in_specs=[pl.BlockSpec((1, 2 * c), lambda j: (0, j)), pl.BlockSpec((rows, 2 * c), lambda j: (0, j))],
        out_specs=pl.BlockSpec((rows, 2 * c), lambda j: (0, j)),
        compiler_params=_cp(("parallel",)),
        name="ssm_scan_fwd",
    )(lam, xcat)


def _ssm_scan_bwd(dhcat, hcat, lam):
    rows, gn2 = dhcat.shape
    c = _scan_cols(gn2 // 2)
    n_tiles = rows // SUBLANE

    def body(lam_ref, dh_ref, h_ref, g_ref, dlam_ref):
        lr, li = lam_ref[:, :c], -lam_ref[:, c:]
        row, pows, carry_pow = _scan_tables(lr, li, True)

        def step(k, state):
            carry, acc_r, acc_i = state
            kk = n_tiles - 1 - k
            t0 = pl.multiple_of(kk * SUBLANE, SUBLANE)
            gr, gi, carry = _scan_tile(dh_ref[pl.ds(t0, SUBLANE), :c], dh_ref[pl.ds(t0, SUBLANE), c:], carry,
                                       row, pows, carry_pow, True)
            g_ref[pl.ds(t0, SUBLANE), :c] = gr
            g_ref[pl.ds(t0, SUBLANE), c:] = gi
            tp = pl.multiple_of(jnp.maximum(kk - 1, 0) * SUBLANE, SUBLANE)
            has_prev = (kk > 0).astype(F32)
            prev_r = pltpu.roll(h_ref[pl.ds(tp, SUBLANE), :c], 1, 0) * has_prev
            prev_i = pltpu.roll(h_ref[pl.ds(tp, SUBLANE), c:], 1, 0) * has_prev
            hpr = jnp.where(row >= 1, pltpu.roll(h_ref[pl.ds(t0, SUBLANE), :c], 1, 0), prev_r)
            hpi = jnp.where(row >= 1, pltpu.roll(h_ref[pl.ds(t0, SUBLANE), c:], 1, 0), prev_i)
            acc_r = acc_r + gr * hpr + gi * hpi
            acc_i = acc_i + gi * hpr - gr * hpi
            return carry, acc_r, acc_i

        zero = jnp.zeros((SUBLANE, c), F32)
        _, acc_r, acc_i = lax.fori_loop(0, n_tiles, step, ((zero, zero), zero, zero))
        dlam_ref[:, :c] = jnp.sum(acc_r, axis=0, keepdims=True)
        dlam_ref[:, c:] = jnp.sum(acc_i, axis=0, keepdims=True)

    blk = pl.BlockSpec((rows, 2 * c), lambda j: (0, j))
    return pl.pallas_call(
        body,
        out_shape=[_sds((rows, gn2), F32), _sds((1, gn2), F32)],
        grid=(gn2 // (2 * c),),
        in_specs=[pl.BlockSpec((1, 2 * c), lambda j: (0, j)), blk, blk],
        out_specs=[blk, pl.BlockSpec((1, 2 * c), lambda j: (0, j))],
        compiler_params=_cp(("parallel",)),
        name="ssm_scan_bwd",
    )(lam, dhcat, hcat)


def _shift_down(x, k, row):
    return x if k == 0 else jnp.where(row >= k, pltpu.roll(x, k, 0), 0.0)


def _shift_up(x, k, row):
    n = x.shape[0]
    return x if k == 0 else jnp.where(row < n - k, pltpu.roll(x, n - k, 0), 0.0)


def _taps(w_ref):
    return [w_ref[k:k + 1, :] for k in range(3)]


def _conv3(x, w, row):
    return sum(w[k] * _shift_down(x, k, row) for k in range(3))


def _conv3_bwd(x, w, dy, row):
    dx = sum(w[k] * _shift_up(dy, k, row) for k in range(3))
    dw = [jnp.sum(dy * _shift_down(x, k, row), axis=0, keepdims=True) for k in range(3)]
    return dx, dw


def _gconv_fwd(proj, off, cw, w):
    rows = proj.shape[0]
    tc = _tile(cw, LANE)
    nb = cw // tc

    def body(b_ref, c_ref, h_ref, w_ref, o_ref):
        row = lax.broadcasted_iota(jnp.int32, (rows, tc), 0)
        o_ref[...] = (b_ref[...] * _conv3(c_ref[...] * h_ref[...], _taps(w_ref), row)).astype(o_ref.dtype)

    specs = [pl.BlockSpec((rows, tc), functools.partial(lambda j, b: (0, b + j), b=(off + q * cw) // tc))
             for q in range(3)]
    return pl.pallas_call(
        body,
        out_shape=_sds((rows, cw), BF16),
        grid=(nb,),
        in_specs=specs + [pl.BlockSpec((3, tc), lambda j: (0, j))],
        out_specs=pl.BlockSpec((rows, tc), lambda j: (0, j)),
        compiler_params=_cp(("parallel",)),
        name="gconv_fwd",
    )(proj, proj, proj, w)


def _gconv_bwd(proj, off, cw, w, dy):
    rows = proj.shape[0]
    tc = _tile(cw, LANE)
    nb = cw // tc

    def body(b_ref, c_ref, h_ref, w_ref, dy_ref, db_ref, dc_ref, dh_ref, dw_ref):
        row = lax.broadcasted_iota(jnp.int32, (rows, tc), 0)
        cv, hv, dyv = c_ref[...], h_ref[...], dy_ref[...].astype(F32)
        t = cv * hv
        db_ref[...] = (dyv * _conv3(t, _taps(w_ref), row)).astype(db_ref.dtype)
        dt, dw = _conv3_bwd(t, _taps(w_ref), dyv * b_ref[...], row)
        dc_ref[...] = (dt * hv).astype(dc_ref.dtype)
        dh_ref[...] = (dt * cv).astype(dh_ref.dtype)
        for k in range(3):
            dw_ref[k:k + 1, :] = dw[k]

    specs = [pl.BlockSpec((rows, tc), functools.partial(lambda j, b: (0, b + j), b=(off + q * cw) // tc))
             for q in range(3)]
    col = pl.BlockSpec((rows, tc), lambda j: (0, j))
    wspec = pl.BlockSpec((3, tc), lambda j: (0, j))
    return pl.pallas_call(
        body,
        out_shape=[_sds((rows, cw), BF16)] * 3 + [_sds((3, cw), F32)],
        grid=(nb,),
        in_specs=specs + [wspec, col],
        out_specs=[col, col, col, wspec],
        compiler_params=_cp(("parallel",)),
        name="gconv_bwd",
    )(proj, proj, proj, w, dy)


def _ffn_act_fwd(up, w):
    rows, f2 = up.shape
    f = f2 // 2
    tc = _tile(f, LANE)
    nb = f // tc

    def body(a_ref, b_ref, wa_ref, wb_ref, o_ref):
        row = lax.broadcasted_iota(jnp.int32, (rows, tc), 0)
        a = _conv3(a_ref[...], _taps(wa_ref), row)
        b = _conv3(b_ref[...], _taps(wb_ref), row)
        o_ref[...] = (jax.nn.silu(a) * b).astype(o_ref.dtype)

    return pl.pallas_call(
        body,
        out_shape=_sds((rows, f), BF16),
        grid=(nb,),
        in_specs=[pl.BlockSpec((rows, tc), lambda j: (0, j)), pl.BlockSpec((rows, tc), lambda j: (0, nb + j)),
                  pl.BlockSpec((3, tc), lambda j: (0, j)), pl.BlockSpec((3, tc), lambda j: (0, nb + j))],
        out_specs=pl.BlockSpec((rows, tc), lambda j: (0, j)),
        compiler_params=_cp(("parallel",)),
        name="ffn_act_fwd",
    )(up, up, w, w)


def _ffn_act_bwd(up, w, dact):
    rows, f2 = up.shape
    f = f2 // 2
    tc = _tile(f, LANE)
    nb = f // tc

    def body(a_ref, b_ref, wa_ref, wb_ref, d_ref, da_ref, db_ref, dwa_ref, dwb_ref):
        row = lax.broadcasted_iota(jnp.int32, (rows, tc), 0)
        av, bv, dv = a_ref[...], b_ref[...], d_ref[...].astype(F32)
        ac = _conv3(av, _taps(wa_ref), row)
        bc = _conv3(bv, _taps(wb_ref), row)
        _, vjp = jax.vjp(lambda p, q: jax.nn.silu(p) * q, ac, bc)
        dac, dbc = vjp(dv)
        dxa, dwa = _conv3_bwd(av, _taps(wa_ref), dac, row)
        dxb, dwb = _conv3_bwd(bv, _taps(wb_ref), dbc, row)
        da_ref[...] = dxa.astype(da_ref.dtype)
        db_ref[...] = dxb.astype(db_ref.dtype)
        for k in range(3):
            dwa_ref[k:k + 1, :] = dwa[k]
            dwb_ref[k:k + 1, :] = dwb[k]

    col = pl.BlockSpec((rows, tc), lambda j: (0, j))
    wspec = pl.BlockSpec((3, tc), lambda j: (0, j))
    return pl.pallas_call(
        body,
        out_shape=[_sds((rows, f), BF16)] * 2 + [_sds((3, f), F32)] * 2,
        grid=(nb,),
        in_specs=[col, pl.BlockSpec((rows, tc), lambda j: (0, nb + j)), wspec,
                  pl.BlockSpec((3, tc), lambda j: (0, nb + j)), col],
        out_specs=[col, col, wspec, wspec],
        compiler_params=_cp(("parallel",)),
        name="ffn_act_bwd",
    )(up, up, w, w, dact)


def _attn_scores(q, kc, kp, slope, dilation, has_prev):
    scale = HEAD_DIM ** -0.5
    nt = (((1,), (1,)), ((), ()))
    s_c = lax.dot_general(q, kc, nt, preferred_element_type=F32) * scale
    s_p = lax.dot_general(q, kp, nt, preferred_element_type=F32) * scale
    qi = lax.broadcasted_iota(jnp.int32, (ATTN_BLOCK, ATTN_BLOCK), 0)
    kj = lax.broadcasted_iota(jnp.int32, (ATTN_BLOCK, ATTN_BLOCK), 1)
    dist_c = qi - kj
    dist_p = dist_c + ATTN_BLOCK
    s_c = jnp.where(dist_c >= 0, s_c - slope * (dist_c * dilation).astype(F32), NEG_INF)
    s_p = jnp.where((dist_p <= ATTN_BLOCK) & has_prev, s_p - slope * (dist_p * dilation).astype(F32), NEG_INF)
    return s_c, s_p


def _slab(seq, col0):
    assert col0 % LANE == 0
    return pl.BlockSpec((seq, LANE), lambda hh, r, s: (0, col0 // LANE + hh))


def _residue_rows(r, block, dilation):
    if dilation == 1:
        return pl.ds(pl.multiple_of(block * ATTN_BLOCK, ATTN_BLOCK), ATTN_BLOCK)
    return pl.ds(r + dilation * ATTN_BLOCK * block, ATTN_BLOCK, stride=dilation)


def _head_col(x, mask):
    return jnp.max(jnp.where(mask, x, -jnp.inf), axis=-1, keepdims=True)


def _attn_fwd(proj, offs, pattern, dilation, sw, slopes):
    seq, _ = proj.shape
    nb = seq // dilation // ATTN_BLOCK
    pairs = sw // LANE

    def body(q_ref, k_ref, v_ref, s0_ref, s1_ref, o_ref, lse_ref):
        r, i = pl.program_id(1), pl.program_id(2)
        cur, prev = _residue_rows(r, i, dilation), _residue_rows(r, jnp.maximum(i - 1, 0), dilation)
        first = lax.broadcasted_iota(jnp.int32, (ATTN_BLOCK, LANE), 1) < HEAD_DIM
        q2 = q_ref[cur, :]
        kc, kp = k_ref[cur, :].astype(BF16), k_ref[prev, :].astype(BF16)
        vc, vp = v_ref[cur, :].astype(BF16), v_ref[prev, :].astype(BF16)
        res = []
        for mask, sl_ref in ((first, s0_ref), (~first, s1_ref)):
            qh = jnp.where(mask, q2, 0.0).astype(BF16)
            s_c, s_p = _attn_scores(qh, kc, kp, sl_ref[:, :1], dilation, i > 0)
            mx = jnp.maximum(jnp.max(s_c, axis=-1, keepdims=True), jnp.max(s_p, axis=-1, keepdims=True))
            p_c, p_p = jnp.exp(s_c - mx), jnp.exp(s_p - mx)
            den = jnp.sum(p_c, axis=-1, keepdims=True) + jnp.sum(p_p, axis=-1, keepdims=True)
            o = (jnp.dot(p_c.astype(BF16), vc, preferred_element_type=F32)
                 + jnp.dot(p_p.astype(BF16), vp, preferred_element_type=F32))
            res.append((o / den, mx + jnp.log(den)))
        o_ref[cur, :] = jnp.where(first, res[0][0], res[1][0])
        lse_ref[cur, :] = jnp.where(first, res[0][1], res[1][1])

    slope = pl.BlockSpec((None, 1, LANE), lambda hh, r, s: (hh, 0, 0))
    return pl.pallas_call(
        body,
        out_shape=[_sds((seq, sw), F32)] * 2,
        grid=(pairs, dilation, nb),
        in_specs=[_slab(seq, offs[0] + pattern * sw), _slab(seq, offs[1] + pattern * sw),
                  _slab(seq, offs[2] + pattern * sw), slope, slope],
        out_specs=[_slab(seq, 0), _slab(seq, 0)],
        compiler_params=_cp(("parallel", "arbitrary", "arbitrary")),
        name=f"attn_fwd_d{dilation}",
    )(proj, proj, proj, *slopes)


def _attn_bwd(proj, do, lse, delta, offs, pattern, dilation, sw, slopes):
    seq, _ = proj.shape
    nb = seq // dilation // ATTN_BLOCK
    pairs = sw // LANE

    def body(q_ref, k_ref, v_ref, do_ref, lse_ref, dl_ref, s0_ref, s1_ref, dq_ref, dk_ref, dv_ref, ck_ref, cv_ref):
        r, step = pl.program_id(1), pl.program_id(2)
        i = nb - 1 - step
        cur, prev = _residue_rows(r, i, dilation), _residue_rows(r, jnp.maximum(i - 1, 0), dilation)
        scale = HEAD_DIM ** -0.5
        nt = (((1,), (1,)), ((), ()))
        first = lax.broadcasted_iota(jnp.int32, (ATTN_BLOCK, LANE), 1) < HEAD_DIM
        q2, do2, lse2, dl2 = q_ref[cur, :], do_ref[cur, :], lse_ref[cur, :], dl_ref[cur, :]
        kc, kp = k_ref[cur, :].astype(BF16), k_ref[prev, :].astype(BF16)
        vc, vp = v_ref[cur, :].astype(BF16), v_ref[prev, :].astype(BF16)

        @pl.when(step == 0)
        def _():
            ck_ref[...] = jnp.zeros_like(ck_ref)
            cv_ref[...] = jnp.zeros_like(cv_ref)

        dq, dk_c, dv_c, dk_p, dv_p = [], 0.0, 0.0, 0.0, 0.0
        for mask, sl_ref in ((first, s0_ref), (~first, s1_ref)):
            qh = jnp.where(mask, q2, 0.0).astype(BF16)
            doh = jnp.where(mask, do2, 0.0).astype(BF16)
            lse_col, dl_col = _head_col(lse2, mask), _head_col(dl2, mask)
            s_c, s_p = _attn_scores(qh, kc, kp, sl_ref[:, :1], dilation, i > 0)
            p_c, p_p = jnp.exp(s_c - lse_col), jnp.exp(s_p - lse_col)
            ds_c = p_c * (lax.dot_general(doh, vc, nt, preferred_element_type=F32) - dl_col)
            ds_p = p_p * (lax.dot_general(doh, vp, nt, preferred_element_type=F32) - dl_col)
            dq.append(jnp.dot(ds_c.astype(BF16), kc, preferred_element_type=F32)
                      + jnp.dot(ds_p.astype(BF16), kp, preferred_element_type=F32))
            dk_c = dk_c + jnp.dot(ds_c.T.astype(BF16), qh, preferred_element_type=F32)
            dv_c = dv_c + jnp.dot(p_c.T.astype(BF16), doh, preferred_element_type=F32)
            dk_p = dk_p + jnp.dot(ds_p.T.astype(BF16), qh, preferred_element_type=F32)
            dv_p = dv_p + jnp.dot(p_p.T.astype(BF16), doh, preferred_element_type=F32)
        dq_ref[cur, :] = jnp.where(first, dq[0], dq[1]) * scale
        dk_ref[cur, :] = dk_c * scale + ck_ref[...]
        dv_ref[cur, :] = dv_c + cv_ref[...]
        ck_ref[...] = dk_p * scale
        cv_ref[...] = dv_p

    slope = pl.BlockSpec((None, 1, LANE), lambda hh, r, s: (hh, 0, 0))
    tok = _slab(seq, 0)
    return pl.pallas_call(
        body,
        out_shape=[_sds((seq, sw), F32)] * 3,
        grid=(pairs, dilation, nb),
        in_specs=[_slab(seq, offs[0] + pattern * sw), _slab(seq, offs[1] + pattern * sw),
                  _slab(seq, offs[2] + pattern * sw), tok, tok, tok, slope, slope],
        out_specs=[tok, tok, tok],
        scratch_shapes=[pltpu.VMEM((ATTN_BLOCK, LANE), F32), pltpu.VMEM((ATTN_BLOCK, LANE), F32)],
        compiler_params=_cp(("parallel", "arbitrary", "arbitrary")),
        name=f"attn_bwd_d{dilation}",
    )(proj, proj, proj, do, lse, delta, *slopes)


def _attn_merge(outs, lses):
    rows, aw = outs[0].shape
    tm = _tile(rows, 256, SUBLANE)

    def body(o0, o1, o2, l0, l1, l2, o_ref, lse_ref):
        lv = [l0[...], l1[...], l2[...]]
        mx = jnp.maximum(jnp.maximum(lv[0], lv[1]), lv[2])
        w = [jnp.exp(t - mx) for t in lv]
        den = w[0] + w[1] + w[2]
        o_ref[...] = (w[0] * o0[...] + w[1] * o1[...] + w[2] * o2[...]) / den
        lse_ref[...] = mx + jnp.log(den)

    spec = pl.BlockSpec((tm, aw), lambda i: (i, 0))
    return pl.pallas_call(
        body,
        out_shape=[_sds((rows, aw), F32)] * 2,
        grid=(rows // tm,),
        in_specs=[spec] * 6,
        out_specs=[spec, spec],
        compiler_params=_cp(("parallel",)),
        name="attn_merge",
    )(*outs, *lses)


def _attn_delta(do, o, head_ones):
    rows, aw = do.shape
    tm = _tile(rows, 256, SUBLANE)

    def body(do_ref, o_ref, e_ref, d_ref):
        d_ref[...] = jnp.dot(do_ref[...] * o_ref[...], e_ref[...], preferred_element_type=F32,
                             precision=lax.Precision.HIGHEST)

    spec = pl.BlockSpec((tm, aw), lambda i: (i, 0))
    return pl.pallas_call(
        body,
        out_shape=_sds((rows, aw), F32),
        grid=(rows // tm,),
        in_specs=[spec, spec, pl.BlockSpec((aw, aw), lambda i: (0, 0))],
        out_specs=spec,
        compiler_params=_cp(("parallel",)),
        name="attn_delta",
    )(do, o, head_ones)


def _alibi_slopes(pattern, hp):
    n_heads = hp * len(DSWA_PATTERNS)
    s = np.array([2.0 ** (-8.0 * (pattern * hp + h + 1) / n_heads) for h in range(hp)], dtype=np.float32)
    return [jnp.asarray(np.broadcast_to(s[par::2, None, None], (hp // 2, 1, LANE)).copy()) for par in (0, 1)]


def _loss_fwd_bwd(y, target):
    rows, d = y.shape
    tm = _tile(rows, 256, SUBLANE)

    def body(y_ref, t_ref, dy_ref, l_ref):
        i = pl.program_id(0)
        err = y_ref[...] - t_ref[...]
        dy_ref[...] = err * (1.0 / d)
        part = 0.5 * jnp.sum(jnp.mean(err * err, axis=-1, keepdims=True), axis=0, keepdims=True)

        @pl.when(i == 0)
        def _():
            l_ref[...] = jnp.zeros_like(l_ref)

        l_ref[...] += jnp.broadcast_to(part, l_ref.shape)

    spec = pl.BlockSpec((tm, d), lambda i: (i, 0))
    dy, loss = pl.pallas_call(
        body,
        out_shape=[_sds((rows, d), F32), _sds((SUBLANE, LANE), F32)],
        grid=(rows // tm,),
        in_specs=[spec, spec],
        out_specs=[spec, pl.BlockSpec((SUBLANE, LANE), lambda i: (0, 0))],
        compiler_params=_cp(("arbitrary",)),
        name="loss",
    )(y, target)
    return dy, loss[0, 0]


def _adam_math(w, g, m, v):
    m = ADAM_B1 * m + (1.0 - ADAM_B1) * g
    v = ADAM_B2 * v + (1.0 - ADAM_B2) * jnp.square(g)
    m_hat = m / (1.0 - ADAM_B1 ** ADAM_STEP)
    v_hat = v / (1.0 - ADAM_B2 ** ADAM_STEP)
    delta = -ADAM_LR * (m_hat / (jnp.sqrt(v_hat) + ADAM_EPS) + ADAM_WD * w)
    return delta, m, v


def _as2d(a):
    if a.ndim == 1:
        return a.reshape(1, -1)
    return a.reshape(-1, a.shape[-1])


def _adam(w, g, m, v, name):
    shape = w.shape
    w2, g2, m2, v2 = _as2d(w), _as2d(g), _as2d(m), _as2d(v)
    r, c = w2.shape
    tr = _tile(r, 512, SUBLANE)
    tc = _tile(c, 1024)

    def body(w_ref, g_ref, m_ref, v_ref, d_ref, mo_ref, vo_ref):
        delta, mn, vn = _adam_math(w_ref[...], g_ref[...], m_ref[...], v_ref[...])
        d_ref[...] = delta
        mo_ref[...] = mn
        vo_ref[...] = vn

    spec = pl.BlockSpec((tr, tc), lambda i, j: (i, j))
    outs = pl.pallas_call(
        body,
        out_shape=[_sds((r, c), F32)] * 3,
        grid=(r // tr, c // tc),
        in_specs=[spec] * 4,
        out_specs=[spec] * 3,
        compiler_params=_cp(("parallel", "parallel")),
        name=name,
    )(w2, g2, m2, v2)
    return [o.reshape(shape) for o in outs]


def _wmod_grad_adam(c_t, dmod, w, m, v):
    nl, d, cols = w.shape
    nex = c_t.shape[1]
    tr = _tile(d, 256, SUBLANE)
    tc = _tile(cols, 1024)

    def body(c_ref, dm_ref, w_ref, m_ref, v_ref, g_ref, d_ref, mo_ref, vo_ref):
        cond = jax.nn.silu(c_ref[...]).astype(BF16)
        g = jnp.dot(cond, dm_ref[...].astype(BF16), preferred_element_type=F32)
        delta, mn, vn = _adam_math(w_ref[...], g, m_ref[...], v_ref[...])
        g_ref[...] = g
        d_ref[...] = delta
        mo_ref[...] = mn
        vo_ref[...] = vn

    spec = pl.BlockSpec((None, tr, tc), lambda l, i, j: (l, i, j))
    return pl.pallas_call(
        body,
        out_shape=[_sds((nl, d, cols), F32)] * 4,
        grid=(nl, d // tr, cols // tc),
        in_specs=[pl.BlockSpec((tr, nex), lambda l, i, j: (i, 0)),
                  pl.BlockSpec((None, nex, tc), lambda l, i, j: (l, 0, j)), spec, spec, spec],
        out_specs=[spec] * 4,
        compiler_params=_cp(("parallel", "parallel", "parallel")),
        name="wmod_grad_adam",
    )(c_t, dmod, w, m, v)


def _my_pos():
    return lax.axis_index("x"), lax.axis_index("y"), lax.axis_index("c")


def _ag8(x4, select_half, name):
    a, s, r, c = x4.shape
    assert s == (2 if select_half else 1)

    def body(x_ref, out_ref, send_sems, recv_sems, local_sem):
        x, y, cc = _my_pos()
        me, sibling = (x, y, cc), (x, y, 1 - cc)
        chips = [(1 - x, y), (x, 1 - y), (1 - x, 1 - y)]
        src_mine = x_ref.at[:, pl.ds(cc if select_half else 0, 1)]

        def blk(px, py, pc):
            return out_ref.at[:, pl.ds(4 * px + 2 * py + pc, 1)]

        def copy(k, block, to, src=None):
            return pltpu.make_async_remote_copy(
                src_ref=blk(*block) if src is None else src, dst_ref=blk(*block),
                send_sem=send_sems.at[k], recv_sem=recv_sems.at[k], device_id=to, device_id_type=MESH)

        mine = pltpu.make_async_copy(src_mine, blk(*me), local_sem)
        mine.start()
        first = [copy(0, me, sibling, src=src_mine)]
        first += [copy(1 + j, me, (*chip, cc), src=src_mine) for j, chip in enumerate(chips)]
        for cp in first:
            cp.start()
        passed = [copy(4 + j, (*chip, cc), sibling) for j, chip in enumerate(chips)]
        for j, chip in enumerate(chips):
            copy(1 + j, (*chip, cc), me).wait_recv()
            passed[j].start()
        copy(0, sibling, me).wait_recv()
        for j, chip in enumerate(chips):
            copy(4 + j, (*chip, 1 - cc), me).wait_recv()
        for cp in first + passed:
            cp.wait_send()
        mine.wait()

    return pl.pallas_call(
        body,
        out_shape=_sds((a, N_DEV, r, c), x4.dtype),
        in_specs=[ANY],
        out_specs=ANY,
        scratch_shapes=[pltpu.SemaphoreType.DMA((7,)), pltpu.SemaphoreType.DMA((7,)), pltpu.SemaphoreType.DMA],
        name=name,
    )(x4)


def _chip_of(x, y, k):
    return (1 - x if k & 2 else x), (1 - y if k & 1 else y)


HBM = pl.BlockSpec(memory_space=pltpu.HBM)
SEM = pl.BlockSpec(memory_space=pltpu.SEMAPHORE)
DATAFLOW = pltpu.SideEffectType.DATAFLOW_SIDE_EFFECTING


def _own_block(k, chip, cc):
    del k
    return 2 * chip + cc


def _distance_slot(k, chip, cc):
    del chip, cc
    return k - 1


def _ici_copies(srcs, dsts, slot, send_sems, recv_sems):
    x, y, cc = _my_pos()
    if slot is None:
        return [pltpu.make_async_remote_copy(
            src_ref=s_ref.at[:, pl.ds(1 - cc, 1)], dst_ref=d_ref, send_sem=send_sems.at[n], recv_sem=recv_sems.at[n],
            device_id=(x, y, 1 - cc), device_id_type=MESH) for n, (s_ref, d_ref) in enumerate(zip(srcs, dsts))]
    chip = 2 * x + y
    copies = []
    for n, (s_ref, d_ref) in enumerate(zip(srcs, dsts)):
        for k in (1, 2, 3):
            px, py = _chip_of(x, y, k)
            at = slot(k, chip, cc)
            copies.append(pltpu.make_async_remote_copy(
                src_ref=s_ref.at[pl.ds(at, 1)], dst_ref=d_ref.at[pl.ds(at, 1)],
                send_sem=send_sems.at[3 * n + k - 1], recv_sem=recv_sems.at[3 * n + k - 1],
                device_id=(px, py, cc), device_id_type=MESH))
    return copies


def _ici_start(srcs, lands, slot, after, name):
    n = len(srcs)
    arrays = list(srcs) + ([] if lands is None else list(lands))
    na = len(arrays)
    n_sems = n if slot is None else 3 * n

    def body(*refs):
        s_refs = refs[:n]
        d_refs = s_refs if lands is None else refs[n:na]
        send_sems, recv_sems, token = refs[na + 1], refs[na + 2], refs[-1]
        for cp in _ici_copies(s_refs, d_refs, slot, send_sems, recv_sems):
            cp.start()
        token[...] = jnp.zeros_like(token)

    outs = pl.pallas_call(
        body,
        name=name,
        out_shape=(pltpu.SemaphoreType.DMA((n_sems,)), pltpu.SemaphoreType.DMA((n_sems,)),
                   *[pltpu.HBM(a.shape, a.dtype) for a in arrays], _sds((SUBLANE, LANE), F32)),
        in_specs=[HBM] * na + [ANY],
        out_specs=(SEM, SEM, *([HBM] * na), pl.BlockSpec(memory_space=pltpu.VMEM)),
        input_output_aliases={i: 2 + i for i in range(na)},
        compiler_params=pltpu.CompilerParams(has_side_effects=DATAFLOW),
    )(*[pltpu.with_memory_space_constraint(a, pltpu.HBM) for a in arrays], after)
    return outs[0], outs[1], list(outs[2:2 + na]), outs[-1]


def _ici_wait(send_sems, recv_sems, arrays, n, shared, slot, after, name):
    na = len(arrays)

    def body(*refs):
        s_refs = refs[:n]
        d_refs = s_refs if shared else refs[n:na]
        for cp in _ici_copies(s_refs, d_refs, slot, refs[na], refs[na + 1]):
            cp.wait_send()
            cp.wait_recv()

    outs = pl.pallas_call(
        body,
        name=name,
        out_shape=tuple(pltpu.HBM(a.shape, a.dtype) for a in arrays),
        in_specs=[HBM] * na + [SEM, SEM, ANY],
        out_specs=tuple([HBM] * na),
        input_output_aliases={i: i for i in range(na)},
        compiler_params=pltpu.CompilerParams(has_side_effects=DATAFLOW),
    )(*arrays, send_sems, recv_sems, after)
    return list(outs)


def _my_chip():
    return 2 * lax.axis_index("x") + lax.axis_index("y")


def _cast_own(w, layer, name):
    _, r, cols = w.shape
    tr = _tile(r, 512, 2 * SUBLANE)
    tc = _tile(cols, 1024)

    def body(w_ref, o_ref):
        o_ref[...] = w_ref[...].astype(o_ref.dtype)

    return pl.pallas_call(
        body,
        out_shape=_sds((N_CHIP, r, cols), BF16),
        grid=(r // tr, cols // tc),
        in_specs=[pl.BlockSpec((None, tr, tc), lambda i, j: (layer, i, j))],
        out_specs=pl.BlockSpec((None, tr, tc), lambda i, j: (_my_chip(), i, j)),
        compiler_params=_cp(("parallel", "parallel")),
        name=name,
    )(w)


def _forward_halves(bufs, name):
    n = len(bufs)

    def body(*refs):
        ins, outs = refs[:n], refs[n:2 * n]
        send_sems, recv_sems = refs[2 * n], refs[2 * n + 1]
        x, y, cc = _my_pos()
        chip = 2 * x + y
        copies = []
        for i in range(n):
            for k in (1, 2, 3):
                at = 2 * (chip ^ k) + cc
                copies.append(pltpu.make_async_remote_copy(
                    src_ref=ins[i].at[pl.ds(at, 1)], dst_ref=outs[i].at[pl.ds(at, 1)],
                    send_sem=send_sems.at[3 * i + k - 1], recv_sem=recv_sems.at[3 * i + k - 1],
                    device_id=(x, y, 1 - cc), device_id_type=MESH))
        for cp in copies:
            cp.start()
        for cp in copies:
            cp.wait()

    return pl.pallas_call(
        body,
        out_shape=[_sds(b.shape, b.dtype) for b in bufs],
        in_specs=[ANY] * n,
        out_specs=[ANY] * n,
        scratch_shapes=[pltpu.SemaphoreType.DMA((3 * n,)), pltpu.SemaphoreType.DMA((3 * n,))],
        input_output_aliases={i: i for i in range(n)},
        name=name,
    )(*bufs)


def _rs_sibling(g8s, name):
    n = len(g8s)
    g4s = [g.reshape(N_CHIP, 2, g.shape[1], g.shape[2]) for g in g8s]

    def body(*refs):
        ins, outs = refs[:n], refs[n:2 * n]
        send_sems, recv_sems = refs[2 * n], refs[2 * n + 1]
        x, y, cc = _my_pos()
        copies = [pltpu.make_async_remote_copy(
            src_ref=ins[i].at[:, pl.ds(1 - cc, 1)], dst_ref=outs[i], send_sem=send_sems.at[i],
            recv_sem=recv_sems.at[i], device_id=(x, y, 1 - cc), device_id_type=MESH) for i in range(n)]
        for cp in copies:
            cp.start()
        for cp in copies:
            cp.wait()

    return pl.pallas_call(
        body,
        out_shape=[_sds((N_CHIP, 1, g.shape[2], g.shape[3]), g.dtype) for g in g4s],
        in_specs=[ANY] * n,
        out_specs=[ANY] * n,
        scratch_shapes=[pltpu.SemaphoreType.DMA((n,)), pltpu.SemaphoreType.DMA((n,))],
        name=name,
    )(*g4s)


def _share_halves(halves, name):
    def body(in_ref, out_ref, send_sem, recv_sem):
        x, y, cc = _my_pos()
        cp = pltpu.make_async_remote_copy(
            src_ref=in_ref.at[:, pl.ds(cc, 1)], dst_ref=out_ref.at[:, pl.ds(cc, 1)], send_sem=send_sem,
            recv_sem=recv_sem, device_id=(x, y, 1 - cc), device_id_type=MESH)
        cp.start()
        cp.wait()

    return pl.pallas_call(
        body,
        out_shape=_sds(halves.shape, halves.dtype),
        in_specs=[ANY],
        out_specs=ANY,
        scratch_shapes=[pltpu.SemaphoreType.DMA, pltpu.SemaphoreType.DMA],
        input_output_aliases={0: 0},
        name=name,
    )(halves)


def _rs_add_remote(g8, recv_a, name):
    _, r, c = g8.shape
    ra = recv_a.reshape(N_CHIP, r, c)
    tr = _tile(r, 512, SUBLANE)
    tc = _tile(c, 1024)

    def body(g_ref, r_ref, o_ref):
        o_ref[...] = (g_ref[...] + r_ref[...]).astype(o_ref.dtype)

    return pl.pallas_call(
        body,
        out_shape=_sds((3, r, c), BF16),
        grid=(3, r // tr, c // tc),
        in_specs=[pl.BlockSpec((None, tr, tc),
                               lambda k, i, j: (2 * (_my_chip() ^ (k + 1)) + lax.axis_index("c"), i, j)),
                  pl.BlockSpec((None, tr, tc), lambda k, i, j: (_my_chip() ^ (k + 1), i, j))],
        out_specs=pl.BlockSpec((None, tr, tc), lambda k, i, j: (k, i, j)),
        compiler_params=_cp(("parallel",) * 3),
        name=name,
    )(g8, ra)


def _rs_add_final(g8, recv_a, recv_b, out_buf, layer, name):
    _, r, c = g8.shape
    ra = recv_a.reshape(N_CHIP, r, c)
    tr = _tile(r, 512, SUBLANE)
    tc = _tile(c, 1024)

    def body(g_ref, r_ref, b0_ref, b1_ref, b2_ref, buf_ref, o_ref):
        del buf_ref
        o_ref[...] = (((g_ref[...] + r_ref[...]) + b0_ref[...].astype(F32)) + b1_ref[...].astype(F32)
                      ) + b2_ref[...].astype(F32)

    def bspec(k):
        return pl.BlockSpec((None, tr, tc), functools.partial(lambda i, j, k: (k, i, j), k=k))

    return pl.pallas_call(
        body,
        out_shape=_sds(out_buf.shape, F32),
        grid=(r // tr, c // tc),
        in_specs=[pl.BlockSpec((None, tr, tc), lambda i, j: (2 * _my_chip() + lax.axis_index("c"), i, j)),
                  pl.BlockSpec((None, tr, tc), lambda i, j: (_my_chip(), i, j)),
                  bspec(0), bspec(1), bspec(2), ANY],
        out_specs=pl.BlockSpec((None, None, tr, tc), lambda i, j: (layer, lax.axis_index("c"), i, j)),
        input_output_aliases={5: 0},
        compiler_params=_cp(("parallel",) * 2),
        name=name,
    )(g8, ra, recv_b, recv_b, recv_b, out_buf)


def _sum8(x8, name):
    _, r, c = x8.shape
    tr = _tile(r, 256, SUBLANE)

    def body(x_ref, o_ref):
        acc = x_ref[0]
        for b in range(1, N_DEV):
            acc = acc + x_ref[b]
        o_ref[...] = acc

    return pl.pallas_call(
        body,
        out_shape=_sds((r, c), F32),
        grid=(r // tr,),
        in_specs=[pl.BlockSpec((N_DEV, tr, c), lambda i: (0, i, 0))],
        out_specs=pl.BlockSpec((tr, c), lambda i: (i, 0)),
        compiler_params=_cp(("parallel",)),
        name=name,
    )(x8)


def _block_diag(t):
    g, p, q = t.shape
    eye = jnp.eye(g, dtype=t.dtype)
    return (t[:, :, None, :] * eye[:, None, :, None]).reshape(g * p, g * q)


def _diag_blocks(mat, g):
    p, q = mat.shape[0] // g, mat.shape[1] // g
    eye = jnp.eye(g, dtype=mat.dtype)
    return jnp.sum(mat.reshape(g, p, g, q) * eye[:, None, :, None], axis=2)


def _interleave(re, im, c):
    lead = re.shape[:-1]
    gn = re.shape[-1]
    return jnp.stack([re.reshape(*lead, gn // c, c), im.reshape(*lead, gn // c, c)], axis=-2).reshape(*lead, 2 * gn)


def _deinterleave(cat, c):
    lead = cat.shape[:-1]
    gn = cat.shape[-1] // 2
    t = cat.reshape(*lead, gn // c, 2, c)
    return t[..., 0, :].reshape(*lead, gn), t[..., 1, :].reshape(*lead, gn)


def kernel(x, c, w_mod, b_mod, g_pre_mix, g_post_mix, g_pre_ffn, g_post_ffn, w_in, ssm_log_dt, ssm_a_re, ssm_a_im, ssm_b_re, ssm_b_im, ssm_c_re, ssm_c_im, ssm_d, w_glu, b_glu, conv_mix_w, w_ssm_out, w_attn_out, w_conv_out, b_gate, w_o, w_up, ffn_conv_w, w_down, loss_target, m_w_mod, m_b_mod, m_g_pre_mix, m_g_post_mix, m_g_pre_ffn, m_g_post_ffn, m_w_in, m_ssm_log_dt, m_ssm_a_re, m_ssm_a_im, m_ssm_b_re, m_ssm_b_im, m_ssm_c_re, m_ssm_c_im, m_ssm_d, m_w_glu, m_b_glu, m_conv_mix_w, m_w_ssm_out, m_w_attn_out, m_w_conv_out, m_b_gate, m_w_o, m_w_up, m_ffn_conv_w, m_w_down, v_w_mod, v_b_mod, v_g_pre_mix, v_g_post_mix, v_g_pre_ffn, v_g_post_ffn, v_w_in, v_ssm_log_dt, v_ssm_a_re, v_ssm_a_im, v_ssm_b_re, v_ssm_b_im, v_ssm_c_re, v_ssm_c_im, v_ssm_d, v_w_glu, v_b_glu, v_conv_mix_w, v_w_ssm_out, v_w_attn_out, v_w_conv_out, v_b_gate, v_w_o, v_w_up, v_ffn_conv_w, v_w_down):
    weights = dict(w_mod=w_mod, b_mod=b_mod, g_pre_mix=g_pre_mix, g_post_mix=g_post_mix, g_pre_ffn=g_pre_ffn, g_post_ffn=g_post_ffn, w_in=w_in, ssm_log_dt=ssm_log_dt, ssm_a_re=ssm_a_re, ssm_a_im=ssm_a_im, ssm_b_re=ssm_b_re, ssm_b_im=ssm_b_im, ssm_c_re=ssm_c_re, ssm_c_im=ssm_c_im, ssm_d=ssm_d, w_glu=w_glu, b_glu=b_glu, conv_mix_w=conv_mix_w, w_ssm_out=w_ssm_out, w_attn_out=w_attn_out, w_conv_out=w_conv_out, b_gate=b_gate, w_o=w_o, w_up=w_up, ffn_conv_w=ffn_conv_w, w_down=w_down)
    mom_m = dict(w_mod=m_w_mod, b_mod=m_b_mod, g_pre_mix=m_g_pre_mix, g_post_mix=m_g_post_mix, g_pre_ffn=m_g_pre_ffn, g_post_ffn=m_g_post_ffn, w_in=m_w_in, ssm_log_dt=m_ssm_log_dt, ssm_a_re=m_ssm_a_re, ssm_a_im=m_ssm_a_im, ssm_b_re=m_ssm_b_re, ssm_b_im=m_ssm_b_im, ssm_c_re=m_ssm_c_re, ssm_c_im=m_ssm_c_im, ssm_d=m_ssm_d, w_glu=m_w_glu, b_glu=m_b_glu, conv_mix_w=m_conv_mix_w, w_ssm_out=m_w_ssm_out, w_attn_out=m_w_attn_out, w_conv_out=m_w_conv_out, b_gate=m_b_gate, w_o=m_w_o, w_up=m_w_up, ffn_conv_w=m_ffn_conv_w, w_down=m_w_down)
    mom_v = dict(w_mod=v_w_mod, b_mod=v_b_mod, g_pre_mix=v_g_pre_mix, g_post_mix=v_g_post_mix, g_pre_ffn=v_g_pre_ffn, g_post_ffn=v_g_post_ffn, w_in=v_w_in, ssm_log_dt=v_ssm_log_dt, ssm_a_re=v_ssm_a_re, ssm_a_im=v_ssm_a_im, ssm_b_re=v_ssm_b_re, ssm_b_im=v_ssm_b_im, ssm_c_re=v_ssm_c_re, ssm_c_im=v_ssm_c_im, ssm_d=v_ssm_d, w_glu=v_w_glu, b_glu=v_b_glu, conv_mix_w=v_conv_mix_w, w_ssm_out=v_w_ssm_out, w_attn_out=v_w_attn_out, w_conv_out=v_w_conv_out, b_gate=v_b_gate, w_o=v_w_o, w_up=v_w_up, ffn_conv_w=v_ffn_conv_w, w_down=v_w_down)
    names = list(weights)

    nl = w_in.shape[0]
    seq, d = x.shape[1], x.shape[2]
    sw = d // 4
    groups = sw // SSM_GROUP
    gn = groups * SSM_STATE
    hp = sw // HEAD_DIM
    qw = 3 * sw
    off_q, off_k, off_v = sw, sw + qw, sw + 2 * qw
    off_conv = sw + 3 * qw
    off_gate = off_conv + 3 * sw
    n_in = off_gate + 3 * d
    f = w_down.shape[1] * N_CHIP
    scan_c = _scan_cols(gn)
    assert seq % (ATTN_BLOCK * DSWA_PATTERNS[-1][1]) == 0 and all(w // dl == ATTN_BLOCK for w, dl in DSWA_PATTERNS)

    px, py, pc = _my_pos()
    chip = 2 * px + py
    dev = 2 * chip + pc

    x2 = x.reshape(seq, d)
    target2 = loss_target.reshape(seq, d)

    mix_keys = ("w_in", "w_glu", "w_ssm_out", "w_attn_out", "w_conv_out", "w_o")
    ffn_keys = ("w_up", "w_down")
    col_sharded = ("w_in", "w_ssm_out", "w_attn_out", "w_conv_out", "w_up")
    n_stages = 2 * nl

    def stage_keys(stage):
        return ffn_keys if stage % 2 else mix_keys

    def blocked(k, buf8):
        r, cols = weights[k].shape[1:]
        return buf8.reshape(1, N_CHIP, r, cols) if k in col_sharded else buf8.reshape(1, 1, N_CHIP * r, cols)

    def begin_gather(stage, after):
        keys = stage_keys(stage)
        bufs = []
        for k in keys:
            r, cols = weights[k].shape[1:]
            bufs.append(_cast_own(weights[k], stage // 2, "cast_own").reshape(N_DEV, r // 2, cols))
        return _ici_start(bufs, None, _own_block, after, f"gather_start_{stage}")

    def end_gather(stage, pending, after):
        send_sems, recv_sems, bufs, _ = pending
        bufs = _ici_wait(send_sems, recv_sems, bufs, len(bufs), True, _own_block, after, f"gather_wait_{stage}")
        bufs = _forward_halves(bufs, "gather_forward")
        return {k: blocked(k, b) for k, b in zip(stage_keys(stage), bufs)}

    c_all = _ag8(c.reshape(1, 1, 1, d), False, "ag_cond").reshape(N_DEV, d)
    c_pad = jnp.concatenate([c_all, jnp.zeros((SUBLANE, d), F32)], axis=0)
    mcols = w_mod.shape[2]
    w_mod4 = w_mod.reshape(nl, 1, d, mcols)
    mod_loc = jnp.stack([_mm_nn(c_pad, w_mod4, l, name="mod_fwd", a_fn=jax.nn.silu) for l in range(nl)])
    mod_all = _ag8(mod_loc.reshape(nl, 1, 2 * SUBLANE, mcols), False, "ag_mod")
    mod_rows = lax.dynamic_slice_in_dim(mod_all[:, 0::2], dev, 1, axis=2)
    mod = mod_rows.reshape(nl, N_CHIP * mcols) + b_mod
    mods = mod.reshape(nl, 6, 1, d)

    taps = jnp.concatenate([conv_mix_w.reshape(-1), ffn_conv_w.reshape(-1)])
    tap_w = 8 * LANE
    tap_r = -(-taps.shape[0] // (tap_w * SUBLANE)) * SUBLANE
    taps = jnp.concatenate([taps, jnp.zeros((tap_r * tap_w - taps.shape[0],), F32)])
    taps_all = _ag8(taps.reshape(1, 1, tap_r, tap_w), False, "ag_small_weight")[0, 0::2].reshape(N_CHIP, -1)

    def whole(w, at):
        _, r, cols = w.shape
        got = taps_all[:, at:at + w.size].reshape(N_CHIP, nl, r, cols)
        return got.transpose(1, 2, 0, 3).reshape(nl, r, N_CHIP * cols)

    conv_w_full = whole(conv_mix_w, 0)
    ffn_w_full = whole(ffn_conv_w, conv_mix_w.size)
    wfs = [dict() for _ in range(nl)]
    pendings, order = [], mods[0, 0, :, :1] + taps_all[:1, :1]
    for stage in range(n_stages):
        pendings.append(begin_gather(stage, order))
        order = pendings[-1][3]
    wfs[0].update(end_gather(0, pendings[0], mods))

    head_ones = jnp.asarray(np.kron(np.eye(hp, dtype=np.float32), np.ones((HEAD_DIM, HEAD_DIM), np.float32)))
    slopes = [_alibi_slopes(p, hp) for p in range(len(DSWA_PATTERNS))]
    qkv_offs = (off_q, off_k, off_v)

    def row(v):
        return v.reshape(1, -1)

    saved = []
    xl = x2
    for l in range(nl):
        sh1, sc1, gt1, sh2, sc2, gt2 = [mods[l, q] for q in range(6)]
        s = dict(x_in=xl)
        wf = wfs[l]
        sh1_t = sh1 + order[0, 0] if l == 0 else sh1
        (h1,) = _ew_fwd(_fn_norm_mod, [(xl, 0, d)], [row(g_pre_mix[l]), sc1, sh1_t], [BF16], name="norm_mod_fwd", width=d)
        proj = _mm_nn(h1, wf["w_in"], 0, name="w_in_fwd")
        br_t = jnp.transpose(ssm_b_re[l], (2, 0, 1))
        bi_t = jnp.transpose(ssm_b_im[l], (2, 0, 1))
        disc_in = (ssm_log_dt[l].reshape(groups, 1), ssm_a_re[l], ssm_a_im[l], br_t, bi_t)
        lr, li, bbr_t, bbi_t = _ssm_disc_fwd(*disc_in)
        lam = _interleave(lr.reshape(1, gn), li.reshape(1, gn), scan_c)
        bcat = _interleave(_block_diag(jnp.transpose(bbr_t, (1, 0, 2))), _block_diag(jnp.transpose(bbi_t, (1, 0, 2))),
                           scan_c).astype(BF16).reshape(1, 1, sw, 2 * gn)
        cre = _block_diag(jnp.transpose(ssm_c_re[l], (0, 2, 1)))
        cim = _block_diag(jnp.transpose(ssm_c_im[l], (0, 2, 1)))
        ccat = jnp.transpose(_interleave(cre.T, -cim.T, scan_c)).astype(BF16).reshape(1, 1, 2 * gn, sw)
        xcat = _mm_nn(proj, bcat, 0, name="ssm_b_fwd", k_dim=sw)
        hcat = _ssm_scan_fwd(xcat, lam)
        y_ssm_pre = _mm_nn(hcat, ccat, 0, name="ssm_c_fwd")
        (gact,) = _ew_fwd(_fn_gelu, [(y_ssm_pre, 0, sw), (proj, 0, sw)], [row(ssm_d[l])], [F32], name="gelu_fwd", width=sw,
                          tw=_tile(sw, 512))
        z = _mm_nn(gact, wf["w_glu"], 0, name="w_glu_fwd")
        (s_ssm,) = _ew_fwd(_fn_glu, [(gact, 0, sw), (z, 0, sw)], [row(b_glu[l])], [BF16], name="glu_fwd", width=sw,
                           tw=_tile(sw, 512))
        y_ssm = _mm_nn(s_ssm, wf["w_ssm_out"], 0, name="w_branch_out_fwd")
        outs, lses = [], []
        for p, (_, dl) in enumerate(DSWA_PATTERNS):
            o_p, lse_p = _attn_fwd(proj, qkv_offs, p, dl, sw, slopes[p])
            outs.append(o_p)
            lses.append(lse_p)
        o_attn, lse_attn = _attn_merge(outs, lses)
        y_attn = _mm_nn(o_attn, wf["w_attn_out"], 0, name="w_branch_out_fwd")
        cv = _gconv_fwd(proj, off_conv, sw, conv_w_full[l])
        y_conv = _mm_nn(cv, wf["w_conv_out"], 0, name="w_branch_out_fwd")
        bg = b_gate[l].reshape(3, 1, d)
        gate_xs = [(proj, off_gate + q * d, d) for q in range(3)] + [(y_ssm, 0, d), (y_attn, 0, d), (y_conv, 0, d)]
        (merged,) = _ew_fwd(_fn_gates, gate_xs, [bg[0], bg[1], bg[2]], [BF16], name="gates_fwd", width=d,
                            tw=_tile(sw, 512))
        y_mix = _mm_nn(merged, wf["w_o"], 0, name="w_o_fwd")
        (x_mid,) = _ew_fwd(_fn_residual, [(xl, 0, d), (y_mix, 0, d)], [gt1, row(g_post_mix[l])], [F32], name="residual_fwd",
                           width=d)
        wf.update(end_gather(2 * l + 1, pendings[2 * l + 1], x_mid))
        (h2,) = _ew_fwd(_fn_norm_mod, [(x_mid, 0, d)], [row(g_pre_ffn[l]), sc2, sh2], [BF16], name="norm_mod_fwd", width=d)
        up = _mm_nn(h2, wf["w_up"], 0, name="w_up_fwd")
        act = _ffn_act_fwd(up, ffn_w_full[l])
        y_ffn = _mm_nn(act, wf["w_down"], 0, name="w_down_fwd")
        (x_out,) = _ew_fwd(_fn_residual, [(x_mid, 0, d), (y_ffn, 0, d)], [gt2, row(g_post_ffn[l])], [F32],
                           name="residual_fwd", width=d)
        if l + 1 < nl:
            wfs[l + 1].update(end_gather(2 * l + 2, pendings[2 * l + 2], x_out))
        s.update(h1=h1, proj=proj, disc_in=disc_in, lam=lam, bcat=bcat, ccat=ccat, hcat=hcat, y_ssm_pre=y_ssm_pre,
                 gact=gact, z=z, s_ssm=s_ssm, y_ssm=y_ssm, o_attn=o_attn, lse_attn=lse_attn, y_attn=y_attn,
                 cv=cv, y_conv=y_conv, merged=merged, y_mix=y_mix, x_mid=x_mid, h2=h2, up=up, act=act, y_ffn=y_ffn)
        saved.append(s)
        xl = x_out

    dxl, loss_local = _loss_fwd_bwd(xl, target2)
    loss = lax.psum(loss_local, ("x", "y", "c"))

    gfin = {k: lax.empty((nl, 2, weights[k].shape[1] // 2, weights[k].shape[2]), F32) for k in mix_keys + ffn_keys}

    rs = dict(sib=None, ici=None)

    def rs_stage_end(stage, grads_4d):
        g8s = [grads_4d[k].reshape(N_DEV, weights[k].shape[1] // 2, weights[k].shape[2]) for k in stage_keys(stage)]
        g4s = [g.reshape(N_CHIP, 2, g.shape[1], g.shape[2]) for g in g8s]
        lands = [lax.empty((N_CHIP, 1, g.shape[1], g.shape[2]), F32) for g in g8s]
        rs["sib"] = (stage, _ici_start(g4s, lands, None, g8s[0], f"rs_sibling_start_{stage}"))

    def rs_midpoint(after):
        if rs["ici"] is not None:
            stage, g8s, recv_a, (send_sems, recv_sems, arrays, _) = rs["ici"]
            n = len(g8s)
            arrays = _ici_wait(send_sems, recv_sems, arrays, n, False, _distance_slot, after, f"rs_wait_{stage}")
            for k, g8, ra, rb in zip(stage_keys(stage), g8s, recv_a, arrays[n:]):
                gfin[k] = _rs_add_final(g8, ra, rb, gfin[k], stage // 2, "rs_add_final")
            rs["ici"] = None
        if rs["sib"] is None:
            return 0.0
        stage, (send_sems, recv_sems, arrays, _) = rs["sib"]
        n = len(arrays) // 2
        arrays = _ici_wait(send_sems, recv_sems, arrays, n, False, None, after, f"rs_sibling_wait_{stage}")
        g8s = [a.reshape(N_DEV, a.shape[2], a.shape[3]) for a in arrays[:n]]
        recv_a = arrays[n:]
        s_rem = [_rs_add_remote(g, ra, "rs_add_remote") for g, ra in zip(g8s, recv_a)]
        lands = [lax.empty(t.shape, BF16) for t in s_rem]
        started = _ici_start(s_rem, lands, _distance_slot, recv_a[0], f"rs_start_{stage}")
        rs["sib"], rs["ici"] = None, (stage, g8s, recv_a, started)
        return started[3][0, 0]

    def after_rs(v):
        for flight in (rs["sib"], rs["ici"]):
            if flight is not None:
                v = v + flight[-1][3][0, 0]
        return v

    def grad_buf(k):
        return lax.empty(wf[k].shape, F32)

    small = {k: [None] * nl for k in ("g_pre_mix", "g_post_mix", "g_pre_ffn", "g_post_ffn", "ssm_log_dt", "ssm_a_re",
                                      "ssm_a_im", "ssm_b_re", "ssm_b_im", "ssm_c_re", "ssm_c_im", "ssm_d", "b_glu",
                                      "conv_mix_w", "b_gate", "ffn_conv_w", "dmod")}
    for l in reversed(range(nl)):
        s = saved[l]
        sh1, sc1, gt1, sh2, sc2, gt2 = [mods[l, q] for q in range(6)]
        proj = s["proj"]
        wf = wfs[l]
        gw = {}
        (dy_ffn,), (dgt2, dg_post_ffn) = _ew_bwd(
            _fn_residual, [(s["x_mid"], 0, d), (s["y_ffn"], 0, d)], [after_rs(gt2), row(g_post_ffn[l])],
            [[dxl]], [None, BF16], name="residual_bwd", width=d)
        dact = _mm_nt(dy_ffn, wf["w_down"], 0, name="w_down_bwd_x", out_dtype=BF16)
        gw["w_down"] = _mm_tn(s["act"], dy_ffn, grad_buf("w_down"), 0, name="w_down_bwd_w")
        behind = rs_midpoint(gw["w_down"])
        dup_a, dup_b, dwa, dwb = _ffn_act_bwd(s["up"], ffn_w_full[l] + behind, dact)
        dup = jnp.concatenate([dup_a, dup_b], axis=1)
        small["ffn_conv_w"][l] = jnp.concatenate([dwa, dwb], axis=1)
        dh2 = _mm_nt(dup, wf["w_up"], 0, name="w_up_bwd_x")
        gw["w_up"] = _mm_tn(s["h2"], dup, grad_buf("w_up"), 0, name="w_up_bwd_w")
        (dx_mid,), (dg_pre_ffn, dsc2, dsh2) = _ew_bwd(
            _fn_norm_mod, [(s["x_mid"], 0, d)], [row(g_pre_ffn[l]), sc2, sh2], [[dh2]], [F32], name="norm_mod_bwd", width=d,
            dx_add={0: dxl})
        rs_stage_end(2 * l + 1, gw)
        (dy_mix,), (dgt1, dg_post_mix) = _ew_bwd(
            _fn_residual, [(s["x_in"], 0, d), (s["y_mix"], 0, d)], [after_rs(gt1), row(g_post_mix[l])],
            [[dx_mid]], [None, BF16], name="residual_bwd", width=d)
        dmerged = _mm_nt(dy_mix, wf["w_o"], 0, name="w_o_bwd_x")
        gw["w_o"] = _mm_tn(s["merged"], dy_mix, grad_buf("w_o"), 0, name="w_o_bwd_w")
        bg = b_gate[l].reshape(3, 1, d)
        gate_xs = [(proj, off_gate + q * d, d) for q in range(3)] + [(s["y_ssm"], 0, d), (s["y_attn"], 0, d),
                                                                     (s["y_conv"], 0, d)]
        (dp0, dp1, dp2, dy_ssm, dy_attn, dy_conv), dbg = _ew_bwd(
            _fn_gates, gate_xs, [bg[0], bg[1], bg[2]], [[dmerged]], [BF16] * 6, name="gates_bwd", width=d,
            tw=_tile(sw, 512))
        small["b_gate"][l] = jnp.concatenate(dbg, axis=1)[0]
        behind = rs_midpoint(dy_conv)
        ds_ssm = _mm_nt(dy_ssm, wf["w_ssm_out"], 0, name="w_branch_out_bwd_x")
        gw["w_ssm_out"] = _mm_tn(s["s_ssm"], dy_ssm, grad_buf("w_ssm_out"), 0, name="w_branch_out_bwd_w")
        (dg1, dz), (db_glu,) = _ew_bwd(_fn_glu, [(s["gact"], 0, sw), (s["z"], 0, sw)], [row(b_glu[l]) + behind], [[ds_ssm]],
                                       [F32, BF16], name="glu_bwd", width=sw, tw=_tile(sw, 512))
        dg2 = _mm_nt(dz, wf["w_glu"], 0, name="w_glu_bwd_x")
        gw["w_glu"] = _mm_tn(s["gact"], dz, grad_buf("w_glu"), 0, name="w_glu_bwd_w")
        (dy_pre, du_skip), (dd_skip,) = _ew_bwd(_fn_gelu, [(s["y_ssm_pre"], 0, sw), (proj, 0, sw)], [row(ssm_d[l])],
                                               [[dg1, dg2]], [BF16, F32], name="gelu_bwd", width=sw, tw=_tile(sw, 512))
        dhcat = _mm_nt(dy_pre, s["ccat"], 0, name="ssm_c_bwd_x")
        dccat = _mm_tn(s["hcat"], dy_pre, lax.empty((1, 1, 2 * gn, sw), F32), 0, name="ssm_c_bwd_w")[0, 0]
        gcat, dlam = _ssm_scan_bwd(dhcat, s["hcat"], s["lam"])
        du_b = _mm_nt(gcat, s["bcat"], 0, name="ssm_b_bwd_x")
        dbcat = _mm_tn(proj, gcat, lax.empty((1, 1, sw, 2 * gn), F32), 0, name="ssm_b_bwd_w")[0, 0]
        dlr, dli = _deinterleave(dlam, scan_c)
        dbre, dbim = _deinterleave(dbcat, scan_c)
        dbbr_t = jnp.transpose(_diag_blocks(dbre, groups), (1, 0, 2))
        dbbi_t = jnp.transpose(_diag_blocks(dbim, groups), (1, 0, 2))
        gld, gar, gai, gbr_t, gbi_t = _ssm_disc_bwd(*s["disc_in"], dlr.reshape(groups, SSM_STATE),
                                                    dli.reshape(groups, SSM_STATE), dbbr_t, dbbi_t)
        dcre_t, dcim_t = _deinterleave(dccat.T, scan_c)
        small["ssm_c_re"][l] = _diag_blocks(dcre_t, groups)
        small["ssm_c_im"][l] = -_diag_blocks(dcim_t, groups)
        small["ssm_log_dt"][l] = gld.reshape(groups)
        small["ssm_a_re"][l], small["ssm_a_im"][l] = gar, gai
        small["ssm_b_re"][l] = jnp.transpose(gbr_t, (1, 2, 0))
        small["ssm_b_im"][l] = jnp.transpose(gbi_t, (1, 2, 0))
        small["ssm_d"][l], small["b_glu"][l] = dd_skip[0], db_glu[0]
        du = (du_skip + du_b).astype(BF16)
        do_attn = _mm_nt(dy_attn, wf["w_attn_out"], 0, name="w_branch_out_bwd_x")
        gw["w_attn_out"] = _mm_tn(s["o_attn"], dy_attn, grad_buf("w_attn_out"), 0, name="w_branch_out_bwd_w")
        delta = _attn_delta(do_attn, s["o_attn"], head_ones)
        dqs, dks, dvs = [], [], []
        for p, (_, dl) in enumerate(DSWA_PATTERNS):
            dq_p, dk_p, dv_p = _attn_bwd(proj, do_attn, s["lse_attn"], delta, qkv_offs, p, dl, sw, slopes[p])
            dqs.append(dq_p)
            dks.append(dk_p)
            dvs.append(dv_p)
        dcv = _mm_nt(dy_conv, wf["w_conv_out"], 0, name="w_branch_out_bwd_x", out_dtype=BF16)
        gw["w_conv_out"] = _mm_tn(s["cv"], dy_conv, grad_buf("w_conv_out"), 0, name="w_branch_out_bwd_w")
        dcb, dcc, dch, dconv_w = _gconv_bwd(proj, off_conv, sw, conv_w_full[l], dcv)
        small["conv_mix_w"][l] = dconv_w
        dproj = jnp.concatenate([du] + [t.astype(BF16) for t in dqs + dks + dvs] + [dcb, dcc, dch, dp0, dp1, dp2],
                                axis=1)
        dh1 = _mm_nt(dproj, wf["w_in"], 0, name="w_in_bwd_x")
        gw["w_in"] = _mm_tn(s["h1"], dproj, grad_buf("w_in"), 0, name="w_in_bwd_w")
        (dx_in,), (dg_pre_mix, dsc1, dsh1) = _ew_bwd(
            _fn_norm_mod, [(s["x_in"], 0, d)], [row(g_pre_mix[l]), sc1, sh1], [[dh1]], [F32], name="norm_mod_bwd", width=d,
            dx_add={0: dx_mid})
        rs_stage_end(2 * l, gw)
        small["g_pre_mix"][l], small["g_post_mix"][l] = dg_pre_mix[0], dg_post_mix[0]
        small["g_pre_ffn"][l], small["g_post_ffn"][l] = dg_pre_ffn[0], dg_post_ffn[0]
        small["dmod"][l] = jnp.concatenate([dsh1, dsc1, dgt1, dsh2, dsc2, dgt2], axis=1)[0]
        dxl = dx_in

    grad_x = dxl.reshape(x.shape)

    small = {k: jnp.stack(v) for k, v in small.items()}
    order = sorted(small)
    flat = jnp.concatenate([small[k].reshape(-1) for k in order])
    n_small = flat.shape[0]
    pack_w = 8 * LANE
    pack_r = -(-n_small // (pack_w * SUBLANE)) * SUBLANE
    flat = jnp.concatenate([flat, jnp.zeros((pack_r * pack_w - n_small,), F32)])
    gathered = _ag8(flat.reshape(1, 1, pack_r, pack_w), False, "ag_small_grads")[0]
    summed = _sum8(gathered, "sum_small_grads").reshape(-1)
    sgrad, at = {}, 0
    for k in order:
        size = small[k].size
        sgrad[k] = summed[at:at + size].reshape(small[k].shape)
        at += size
    dmod_off = sum(small[k].size for k in order[:order.index("dmod")])
    dmod_all = gathered.reshape(N_DEV, -1)[:, dmod_off:dmod_off + nl * 6 * d].reshape(N_DEV, nl, 6 * d)
    dmod_loc = lax.dynamic_slice_in_dim(jnp.transpose(dmod_all, (1, 0, 2)), chip * mcols, mcols, axis=2)

    grads = dict(sgrad)
    grads["b_mod"] = grads.pop("dmod")
    grads["conv_mix_w"] = lax.dynamic_slice_in_dim(sgrad["conv_mix_w"], chip * conv_mix_w.shape[2], conv_mix_w.shape[2], axis=2)
    grads["ffn_conv_w"] = lax.dynamic_slice_in_dim(sgrad["ffn_conv_w"], chip * ffn_conv_w.shape[2], ffn_conv_w.shape[2], axis=2)

    rs_midpoint(summed)
    rs_midpoint(summed)
    for k in mix_keys + ffn_keys:
        grads[k] = _share_halves(gfin[k], "rs_share").reshape(weights[k].shape)

    delta_w, new_m, new_v = {}, {}, {}
    c_t = jnp.pad(jnp.transpose(c_all), ((0, 0), (0, LANE - N_DEV)))
    dmod_pad = jnp.pad(dmod_loc, ((0, 0), (0, LANE - N_DEV), (0, 0)))
    grads["w_mod"], delta_w["w_mod"], new_m["w_mod"], new_v["w_mod"] = _wmod_grad_adam(
        c_t, dmod_pad, w_mod, m_w_mod, v_w_mod)
    for k in names:
        if k == "w_mod":
            continue
        delta_w[k], new_m[k], new_v[k] = _adam(weights[k], grads[k], mom_m[k], mom_v[k], "adamw")

    return (loss, grad_x, *[grads[k] for k in names], *[delta_w[k] for k in names], *[new_m[k] for k in names],
            *[new_v[k] for k in names])
```

```python
import functools
import math

import numpy as np
import jax
import jax.numpy as jnp
from jax import lax
from jax.experimental import pallas as pl
from jax.experimental.pallas import tpu as pltpu

F32 = jnp.float32
BF16 = jnp.bfloat16
MESH = pl.DeviceIdType.MESH
ANY = pl.BlockSpec(memory_space=pl.ANY)

VMEM_LIMIT_BYTES = 48 * 1024 * 1024
LANE = 128
SUBLANE = 8

RMS_EPS = 1e-6
NEG_INF = -1e30
SSM_GROUP = 16
SSM_STATE = 64
HEAD_DIM = 64
DSWA_PATTERNS = ((128, 1), (512, 4), (2048, 16))
ATTN_BLOCK = 128
N_DEV = 8
N_CHIP = 4

ADAM_LR = 0.001
ADAM_B1 = 0.9
ADAM_B2 = 0.999
ADAM_EPS = 1e-08
ADAM_WD = 0.01
ADAM_STEP = 10


def _cp(sem=None):
    return pltpu.CompilerParams(dimension_semantics=sem, vmem_limit_bytes=VMEM_LIMIT_BYTES)


def _tile(n, pref, align=LANE):
    if n <= pref:
        return n
    t = (pref // align) * align
    while t >= align:
        if n % t == 0:
            return t
        t -= align
    return n


def _sds(shape, dtype):
    return jax.ShapeDtypeStruct(tuple(shape), dtype)


MM_VMEM_BUDGET = 34 * 1024 * 1024
MM_MAX_CONTRACT = 2048


def _divisors(n, align):
    if n % align:
        return [n]
    return [t for t in range(n, 0, -align) if n % t == 0]


def _halvings(n, align, floor=256):
    out = [n]
    while out[-1] % (2 * align) == 0 and out[-1] // 2 >= floor:
        out.append(out[-1] // 2)
    return out


def _pick_tiles(rows, cols, fixed_bytes, row_bytes, col_bytes, cell_bytes):
    best = None
    for tr in rows:
        for tc in cols:
            if fixed_bytes + row_bytes * tr + col_bytes * tc + cell_bytes * tr * tc <= MM_VMEM_BUDGET:
                if best is None or tr * tc > best[0] * best[1]:
                    best = (tr, tc)
                break
    assert best is not None
    return best


def _accumulate(step, n_steps, part, o_ref, acc_ref):
    if n_steps == 1:
        o_ref[...] = part.astype(o_ref.dtype)
        return
    acc = o_ref if acc_ref is None else acc_ref

    @pl.when(step == 0)
    def _():
        acc[...] = part

    @pl.when(step > 0)
    def _():
        acc[...] += part

    if acc_ref is not None:
        @pl.when(step == n_steps - 1)
        def _():
            o_ref[...] = acc_ref[...].astype(o_ref.dtype)


def _mm_nn(a, w, layer, *, name, k_dim=None, a_col0=0, out_dtype=F32, a_fn=None):
    m = a.shape[0]
    _, nb, kw, n = w.shape
    k_dim = kw if k_dim is None else k_dim
    assert k_dim == kw
    tk = _tile(k_dim, MM_MAX_CONTRACT)
    nk = k_dim // tk
    sa, so = a.dtype.itemsize, jnp.dtype(out_dtype).itemsize
    use_acc = nk > 1 and so != 4
    row_bytes = tk * (2 * sa + (2 if sa == 4 else 0) + (4 if a_fn is not None else 0))
    tm, tn = _pick_tiles(_halvings(m, SUBLANE), _divisors(n, LANE), 0, row_bytes, 2 * tk * w.dtype.itemsize,
                         2 * so + 4 + (4 if use_acc else 0))
    assert a_col0 % tk == 0
    npb = n // tn
    a0 = a_col0 // tk

    def body(a_ref, w_ref, o_ref, *scratch):
        av = a_ref[...]
        if a_fn is not None:
            av = a_fn(av.astype(F32))
        part = jnp.dot(av.astype(BF16), w_ref[...].astype(BF16), preferred_element_type=F32)
        _accumulate(pl.program_id(2), nk, part, o_ref, scratch[0] if use_acc else None)

    return pl.pallas_call(
        body,
        out_shape=_sds((m, nb * n), out_dtype),
        grid=(m // tm, nb * npb, nk),
        in_specs=[pl.BlockSpec((tm, tk), lambda i, j, k: (i, a0 + k)),
                  pl.BlockSpec((None, None, tk, tn), lambda i, j, k: (layer, j // npb, k, j % npb))],
        out_specs=pl.BlockSpec((tm, tn), lambda i, j, k: (i, j)),
        scratch_shapes=[pltpu.VMEM((tm, tn), F32)] if use_acc else [],
        compiler_params=_cp(("parallel", "parallel", "arbitrary")),
        name=name,
    )(a, w)


def _mm_nt(g, w, layer, *, name, out_dtype=F32):
    m = g.shape[0]
    _, nb, k_dim, n = w.shape
    assert g.shape[1] == nb * n
    tko = _tile(k_dim, MM_MAX_CONTRACT)
    sg, so = g.dtype.itemsize, jnp.dtype(out_dtype).itemsize
    use_acc = so != 4
    res_row = tko * (2 * so + 4 + (4 if use_acc else 0))
    tm, tc = _pick_tiles(_halvings(m, SUBLANE), _divisors(n, LANE), 0, res_row, 2 * tko * w.dtype.itemsize,
                         2 * sg + (2 if sg == 4 else 0))
    npb = n // tc
    nr = nb * npb
    use_acc = use_acc and nr > 1

    def body(g_ref, w_ref, o_ref, *scratch):
        part = lax.dot_general(g_ref[...].astype(BF16), w_ref[...].astype(BF16),
                               (((1,), (1,)), ((), ())), preferred_element_type=F32)
        _accumulate(pl.program_id(2), nr, part, o_ref, scratch[0] if use_acc else None)

    return pl.pallas_call(
        body,
        out_shape=_sds((m, k_dim), out_dtype),
        grid=(m // tm, k_dim // tko, nr),
        in_specs=[pl.BlockSpec((tm, tc), lambda i, kk, r: (i, r)),
                  pl.BlockSpec((None, None, tko, tc), lambda i, kk, r: (layer, r // npb, kk, r % npb))],
        out_specs=pl.BlockSpec((tm, tko), lambda i, kk, r: (i, kk)),
        scratch_shapes=[pltpu.VMEM((tm, tko), F32)] if use_acc else [],
        compiler_params=_cp(("parallel", "parallel", "arbitrary")),
        name=name,
    )(g, w)


def _mm_tn(a, g, out_buf, layer, *, name, a_col0=0):
    m = a.shape[0]
    _, nb, k_dim, n = out_buf.shape
    assert g.shape == (m, nb * n)
    tm = _tile(m, MM_MAX_CONTRACT, SUBLANE)
    nr = m // tm
    sa, sg = a.dtype.itemsize, g.dtype.itemsize
    tk, tn = _pick_tiles(_halvings(k_dim, LANE), _divisors(n, LANE), 0, tm * (2 * sa + (2 if sa == 4 else 0) + 2),
                         tm * (2 * sg + (2 if sg == 4 else 0)), 2 * 4 + 4)
    assert a_col0 % tk == 0
    a0 = a_col0 // tk
    npb = n // tn

    def body(a_ref, g_ref, buf_ref, o_ref):
        del buf_ref
        part = lax.dot_general(a_ref[...].astype(BF16), g_ref[...].astype(BF16),
                               (((0,), (0,)), ((), ())), preferred_element_type=F32)
        _accumulate(pl.program_id(2), nr, part, o_ref, None)

    return pl.pallas_call(
        body,
        out_shape=_sds(out_buf.shape, F32),
        grid=(k_dim // tk, nb * npb, nr),
        in_specs=[pl.BlockSpec((tm, tk), lambda kk, j, r: (r, a0 + kk)),
                  pl.BlockSpec((tm, tn), lambda kk, j, r: (r, j)),
                  ANY],
        out_specs=pl.BlockSpec((None, None, tk, tn), lambda kk, j, r: (layer, j // npb, kk, j % npb)),
        input_output_aliases={2: 0},
        compiler_params=_cp(("parallel", "parallel", "arbitrary")),
        name=name,
    )(a, g, out_buf)


def _ew_fwd(fn, xs, ps, out_dtypes, *, name, width, tw=None, tm=256):
    rows = xs[0][0].shape[0]
    tm = _tile(rows, tm, SUBLANE)
    tw = width if tw is None else tw
    nx, n_p = len(xs), len(ps)

    def body(*refs):
        xv = [r[...].astype(F32) for r in refs[:nx]]
        pv = [r[...].astype(F32) for r in refs[nx:nx + n_p]]
        outs = fn(*xv, *pv)
        if not isinstance(outs, (tuple, list)):
            outs = (outs,)
        for o_ref, o in zip(refs[nx + n_p:], outs):
            o_ref[...] = o.astype(o_ref.dtype)

    in_specs = []
    for arr, c0, w in xs:
        assert w == width and c0 % tw == 0
        in_specs.append(pl.BlockSpec((tm, tw), functools.partial(lambda i, j, b: (i, b + j), b=c0 // tw)))
    for p in ps:
        assert p.shape == (1, width)
        in_specs.append(pl.BlockSpec((1, tw), lambda i, j: (0, j)))
    outs = pl.pallas_call(
        body,
        out_shape=[_sds((rows, width), d) for d in out_dtypes],
        grid=(rows // tm, width // tw),
        in_specs=in_specs,
        out_specs=[pl.BlockSpec((tm, tw), lambda i, j: (i, j)) for _ in out_dtypes],
        compiler_params=_cp(("parallel", "parallel")),
        name=name,
    )(*[x[0] for x in xs], *ps)
    return outs


def _ew_bwd(fn, xs, ps, cts, dx_dtypes, *, name, width, tw=None, tm=256, dx_add=None):
    rows = xs[0][0].shape[0]
    tm = _tile(rows, tm, SUBLANE)
    tw = width if tw is None else tw
    nx, n_p = len(xs), len(ps)
    dx_add = dx_add or {}
    flat_cts = [c for group in cts for c in group]
    add_keys = sorted(dx_add)
    n_in = nx + n_p + len(flat_cts) + len(add_keys)
    dx_idx = [i for i, d in enumerate(dx_dtypes) if d is not None]

    def body(*refs):
        i = pl.program_id(1)
        xv = [r[...].astype(F32) for r in refs[:nx]]
        pv = [r[...].astype(F32) for r in refs[nx:nx + n_p]]
        pos = nx + n_p
        ct_vals = []
        for group in cts:
            acc = refs[pos][...].astype(F32)
            pos += 1
            for _ in group[1:]:
                acc = acc + refs[pos][...].astype(F32)
                pos += 1
            ct_vals.append(acc)
        add_vals = {}
        for key in add_keys:
            add_vals[key] = refs[pos][...].astype(F32)
            pos += 1
        out_refs = refs[n_in:]
        outs, vjp = jax.vjp(fn, *xv, *pv)
        grads = vjp(tuple(ct_vals) if isinstance(outs, (tuple, list)) else ct_vals[0])
        o = 0
        for idx in dx_idx:
            gval = grads[idx]
            if idx in add_vals:
                gval = gval + add_vals[idx]
            out_refs[o][...] = gval.astype(out_refs[o].dtype)
            o += 1
        for q in range(n_p):
            gp = grads[nx + q]
            ref = out_refs[o + q]

            @pl.when(i == 0)
            def _(ref=ref, gp=gp):
                ref[...] = gp

            @pl.when(i > 0)
            def _(ref=ref, gp=gp):
                ref[...] += gp

    tile_spec = pl.BlockSpec((tm, tw), lambda j, i: (i, j))
    in_specs = []
    for arr, c0, w in xs:
        assert w == width and c0 % tw == 0
        in_specs.append(pl.BlockSpec((tm, tw), functools.partial(lambda j, i, b: (i, b + j), b=c0 // tw)))
    for p in ps:
        in_specs.append(pl.BlockSpec((1, tw), lambda j, i: (0, j)))
    in_specs += [tile_spec] * (len(flat_cts) + len(add_keys))
    out_shape = [_sds((rows, width), dx_dtypes[idx]) for idx in dx_idx] + [_sds((1, width), F32)] * n_p
    out_specs = [tile_spec] * len(dx_idx) + [pl.BlockSpec((1, tw), lambda j, i: (0, j))] * n_p
    outs = pl.pallas_call(
        body,
        out_shape=out_shape,
        grid=(width // tw, rows // tm),
        in_specs=in_specs,
        out_specs=out_specs,
        compiler_params=_cp(("parallel", "arbitrary")),
        name=name,
    )(*[x[0] for x in xs], *ps, *flat_cts, *[dx_add[k] for k in add_keys])
    return outs[:len(dx_idx)], outs[len(dx_idx):]


def _rms(x):
    return x * lax.rsqrt(jnp.mean(x * x, axis=-1, keepdims=True) + RMS_EPS)


def _fn_norm_mod(x, g, sc, sh):
    return (_rms(x) * g) * (1.0 + sc) + sh


def _fn_residual(x, y, gt, g):
    return x + gt * (_rms(y) * g)


def _fn_gelu(y, u, d):
    return jax.nn.gelu(y + d * u)


def _fn_glu(g, z, b):
    return g * jax.nn.sigmoid(z + b)


def _fn_gates(p0, p1, p2, ys, ya, yc, b0, b1, b2):
    return (jax.nn.sigmoid(p0 + b0) * ys + jax.nn.sigmoid(p1 + b1) * ya + jax.nn.sigmoid(p2 + b2) * yc)


def _fn_disc(log_dt, ar, ai, br_t, bi_t):
    dt = jnp.exp(log_dt)
    mag = jnp.exp(ar * dt)
    lr, li = mag * jnp.cos(ai * dt), mag * jnp.sin(ai * dt)
    den = ar * ar + ai * ai
    fr = ((lr - 1.0) * ar + li * ai) / den
    fi = (li * ar - (lr - 1.0) * ai) / den
    bbr = fr[None] * br_t - fi[None] * bi_t
    bbi = fr[None] * bi_t + fi[None] * br_t
    return lr, li, bbr, bbi


def _ssm_disc_fwd(log_dt, ar, ai, br_t, bi_t):
    g, n = ar.shape

    def body(ld_ref, ar_ref, ai_ref, br_ref, bi_ref, lr_ref, li_ref, bbr_ref, bbi_ref):
        lr, li, bbr, bbi = _fn_disc(ld_ref[...], ar_ref[...], ai_ref[...], br_ref[...], bi_ref[...])
        lr_ref[...] = lr
        li_ref[...] = li
        bbr_ref[...] = bbr
        bbi_ref[...] = bbi

    return pl.pallas_call(
        body,
        out_shape=[_sds((g, n), F32), _sds((g, n), F32), _sds(br_t.shape, F32), _sds(br_t.shape, F32)],
        compiler_params=_cp(),
        name="ssm_disc_fwd",
    )(log_dt, ar, ai, br_t, bi_t)


def _ssm_disc_bwd(log_dt, ar, ai, br_t, bi_t, dlr, dli, dbbr, dbbi):
    g, n = ar.shape

    def body(ld_ref, ar_ref, ai_ref, br_ref, bi_ref, dlr_ref, dli_ref, dbbr_ref, dbbi_ref,
             gld_ref, gar_ref, gai_ref, gbr_ref, gbi_ref):
        _, vjp = jax.vjp(_fn_disc, ld_ref[...], ar_ref[...], ai_ref[...], br_ref[...], bi_ref[...])
        gld, gar, gai, gbr, gbi = vjp((dlr_ref[...], dli_ref[...], dbbr_ref[...], dbbi_ref[...]))
        gld_ref[...] = gld
        gar_ref[...] = gar
        gai_ref[...] = gai
        gbr_ref[...] = gbr
        gbi_ref[...] = gbi

    return pl.pallas_call(
        body,
        out_shape=[_sds((g, 1), F32), _sds((g, n), F32), _sds((g, n), F32), _sds(br_t.shape, F32),
                   _sds(br_t.shape, F32)],
        compiler_params=_cp(),
        name="ssm_disc_bwd",
    )(log_dt, ar, ai, br_t, bi_t, dlr, dli, dbbr, dbbi)


def _cmul(ar, ai, br, bi):
    return ar * br - ai * bi, ar * bi + ai * br


def _scan_tables(lr, li, reverse):
    c = lr.shape[-1]
    p1 = (jnp.broadcast_to(lr, (SUBLANE, c)), jnp.broadcast_to(li, (SUBLANE, c)))
    p2 = _cmul(*p1, *p1)
    p4 = _cmul(*p2, *p2)
    p8 = _cmul(*p4, *p4)
    row = lax.broadcasted_iota(jnp.int32, (SUBLANE, c), 0)
    dist = (SUBLANE - row) if reverse else (row + 1)
    pr, pi = jnp.ones((SUBLANE, c), F32), jnp.zeros((SUBLANE, c), F32)
    for bit, pw in ((1, p1), (2, p2), (4, p4), (8, p8)):
        qr, qi = _cmul(pr, pi, *pw)
        take = (dist & bit) != 0
        pr, pi = jnp.where(take, qr, pr), jnp.where(take, qi, pi)
    return row, (p1, p2, p4), (pr, pi)


def _shift_rows(x, s, row, reverse):
    if reverse:
        return jnp.where(row < SUBLANE - s, pltpu.roll(x, SUBLANE - s, 0), 0.0)
    return jnp.where(row >= s, pltpu.roll(x, s, 0), 0.0)


def _scan_tile(xr, xi, carry, row, pows, carry_pow, reverse):
    for s, pw in zip((1, 2, 4), pows):
        sr, si = _shift_rows(xr, s, row, reverse), _shift_rows(xi, s, row, reverse)
        tr, ti = _cmul(*pw, sr, si)
        xr, xi = xr + tr, xi + ti
    tr, ti = _cmul(*carry_pow, *carry)
    hr, hi = xr + tr, xi + ti
    edge = 0 if reverse else SUBLANE - 1
    c = hr.shape[-1]
    new_carry = (jnp.broadcast_to(hr[edge:edge + 1, :], (SUBLANE, c)),
                 jnp.broadcast_to(hi[edge:edge + 1, :], (SUBLANE, c)))
    return hr, hi, new_carry


def _scan_cols(gn):
    return _tile(gn, 256)


def _ssm_scan_fwd(xcat, lam):
    rows, gn2 = xcat.shape
    c = _scan_cols(gn2 // 2)
    n_tiles = rows // SUBLANE

    def body(lam_ref, x_ref, h_ref):
        lr, li = lam_ref[:, :c], lam_ref[:, c:]
        row, pows, carry_pow = _scan_tables(lr, li, False)

        def step(k, carry):
            t0 = pl.multiple_of(k * SUBLANE, SUBLANE)
            hr, hi, carry = _scan_tile(x_ref[pl.ds(t0, SUBLANE), :c], x_ref[pl.ds(t0, SUBLANE), c:], carry,
                                       row, pows, carry_pow, False)
            h_ref[pl.ds(t0, SUBLANE), :c] = hr
            h_ref[pl.ds(t0, SUBLANE), c:] = hi
            return carry

        zero = jnp.zeros((SUBLANE, c), F32)
        lax.fori_loop(0, n_tiles, step, (zero, zero))

    return pl.pallas_call(
        body,
        out_shape=_sds((rows, gn2), F32),
        grid=(gn2 // (2 * c),),
        in_specs=[pl.BlockSpec((1, 2 * c), lambda j: (0, j)), pl.BlockSpec((rows, 2 * c), lambda j: (0, j))],
        out_specs=pl.BlockSpec((rows, 2 * c), lambda j: (0, j)),
        compiler_params=_cp(("parallel",)),
        name="ssm_scan_fwd",
    )(lam, xcat)


def _ssm_scan_bwd(dhcat, hcat, lam):
    rows, gn2 = dhcat.shape
    c = _scan_cols(gn2 // 2)
    n_tiles = rows // SUBLANE

    def body(lam_ref, dh_ref, h_ref, g_ref, dlam_ref):
        lr, li = lam_ref[:, :c], -lam_ref[:, c:]
        row, pows, carry_pow = _scan_tables(lr, li, True)

        def step(k, state):
            carry, acc_r, acc_i = state
            kk = n_tiles - 1 - k
            t0 = pl.multiple_of(kk * SUBLANE, SUBLANE)
            gr, gi, carry = _scan_tile(dh_ref[pl.ds(t0, SUBLANE), :c], dh_ref[pl.ds(t0, SUBLANE), c:], carry,
                                       row, pows, carry_pow, True)
            g_ref[pl.ds(t0, SUBLANE), :c] = gr
            g_ref[pl.ds(t0, SUBLANE), c:] = gi
            tp = pl.multiple_of(jnp.maximum(kk - 1, 0) * SUBLANE, SUBLANE)
            has_prev = (kk > 0).astype(F32)
            prev_r = pltpu.roll(h_ref[pl.ds(tp, SUBLANE), :c], 1, 0) * has_prev
            prev_i = pltpu.roll(h_ref[pl.ds(tp, SUBLANE), c:], 1, 0) * has_prev
            hpr = jnp.where(row >= 1, pltpu.roll(h_ref[pl.ds(t0, SUBLANE), :c], 1, 0), prev_r)
            hpi = jnp.where(row >= 1, pltpu.roll(h_ref[pl.ds(t0, SUBLANE), c:], 1, 0), prev_i)
            acc_r = acc_r + gr * hpr + gi * hpi
            acc_i = acc_i + gi * hpr - gr * hpi
            return carry, acc_r, acc_i

        zero = jnp.zeros((SUBLANE, c), F32)
        _, acc_r, acc_i = lax.fori_loop(0, n_tiles, step, ((zero, zero), zero, zero))
        dlam_ref[:, :c] = jnp.sum(acc_r, axis=0, keepdims=True)
        dlam_ref[:, c:] = jnp.sum(acc_i, axis=0, keepdims=True)

    blk = pl.BlockSpec((rows, 2 * c), lambda j: (0, j))
    return pl.pallas_call(
        body,
        out_shape=[_sds((rows, gn2), F32), _sds((1, gn2), F32)],
        grid=(gn2 // (2 * c),),
        in_specs=[pl.BlockSpec((1, 2 * c), lambda j: (0, j)), blk, blk],
        out_specs=[blk, pl.BlockSpec((1, 2 * c), lambda j: (0, j))],
        compiler_params=_cp(("parallel",)),
        name="ssm_scan_bwd",
    )(lam, dhcat, hcat)


def _shift_down(x, k, row):
    return x if k == 0 else jnp.where(row >= k, pltpu.roll(x, k, 0), 0.0)


def _shift_up(x, k, row):
    n = x.shape[0]
    return x if k == 0 else jnp.where(row < n - k, pltpu.roll(x, n - k, 0), 0.0)


def _taps(w_ref):
    return [w_ref[k:k + 1, :] for k in range(3)]


def _conv3(x, w, row):
    return sum(w[k] * _shift_down(x, k, row) for k in range(3))


def _conv3_bwd(x, w, dy, row):
    dx = sum(w[k] * _shift_up(dy, k, row) for k in range(3))
    dw = [jnp.sum(dy * _shift_down(x, k, row), axis=0, keepdims=True) for k in range(3)]
    return dx, dw


def _gconv_fwd(proj, off, cw, w):
    rows = proj.shape[0]
    tc = _tile(cw, LANE)
    nb = cw // tc

    def body(b_ref, c_ref, h_ref, w_ref, o_ref):
        row = lax.broadcasted_iota(jnp.int32, (rows, tc), 0)
        o_ref[...] = (b_ref[...] * _conv3(c_ref[...] * h_ref[...], _taps(w_ref), row)).astype(o_ref.dtype)

    specs = [pl.BlockSpec((rows, tc), functools.partial(lambda j, b: (0, b + j), b=(off + q * cw) // tc))
             for q in range(3)]
    return pl.pallas_call(
        body,
        out_shape=_sds((rows, cw), BF16),
        grid=(nb,),
        in_specs=specs + [pl.BlockSpec((3, tc), lambda j: (0, j))],
        out_specs=pl.BlockSpec((rows, tc), lambda j: (0, j)),
        compiler_params=_cp(("parallel",)),
        name="gconv_fwd",
    )(proj, proj, proj, w)


def _gconv_bwd(proj, off, cw, w, dy):
    rows = proj.shape[0]
    tc = _tile(cw, LANE)
    nb = cw // tc

    def body(b_ref, c_ref, h_ref, w_ref, dy_ref, db_ref, dc_ref, dh_ref, dw_ref):
        row = lax.broadcasted_iota(jnp.int32, (rows, tc), 0)
        cv, hv, dyv = c_ref[...], h_ref[...], dy_ref[...].astype(F32)
        t = cv * hv
        db_ref[...] = (dyv * _conv3(t, _taps(w_ref), row)).astype(db_ref.dtype)
        dt, dw = _conv3_bwd(t, _taps(w_ref), dyv * b_ref[...], row)
        dc_ref[...] = (dt * hv).astype(dc_ref.dtype)
        dh_ref[...] = (dt * cv).astype(dh_ref.dtype)
        for k in range(3):
            dw_ref[k:k + 1, :] = dw[k]

    specs = [pl.BlockSpec((rows, tc), functools.partial(lambda j, b: (0, b + j), b=(off + q * cw) // tc))
             for q in range(3)]
    col = pl.BlockSpec((rows, tc), lambda j: (0, j))
    wspec = pl.BlockSpec((3, tc), lambda j: (0, j))
    return pl.pallas_call(
        body,
        out_shape=[_sds((rows, cw), BF16)] * 3 + [_sds((3, cw), F32)],
        grid=(nb,),
        in_specs=specs + [wspec, col],
        out_specs=[col, col, col, wspec],
        compiler_params=_cp(("parallel",)),
        name="gconv_bwd",
    )(proj, proj, proj, w, dy)


def _ffn_act_fwd(up, w):
    rows, f2 = up.shape
    f = f2 // 2
    tc = _tile(f, LANE)
    nb = f // tc

    def body(a_ref, b_ref, wa_ref, wb_ref, o_ref):
        row = lax.broadcasted_iota(jnp.int32, (rows, tc), 0)
        a = _conv3(a_ref[...], _taps(wa_ref), row)
        b = _conv3(b_ref[...], _taps(wb_ref), row)
        o_ref[...] = (jax.nn.silu(a) * b).astype(o_ref.dtype)

    return pl.pallas_call(
        body,
        out_shape=_sds((rows, f), BF16),
        grid=(nb,),
        in_specs=[pl.BlockSpec((rows, tc), lambda j: (0, j)), pl.BlockSpec((rows, tc), lambda j: (0, nb + j)),
                  pl.BlockSpec((3, tc), lambda j: (0, j)), pl.BlockSpec((3, tc), lambda j: (0, nb + j))],
        out_specs=pl.BlockSpec((rows, tc), lambda j: (0, j)),
        compiler_params=_cp(("parallel",)),
        name="ffn_act_fwd",
    )(up, up, w, w)


def _ffn_act_bwd(up, w, dact):
    rows, f2 = up.shape
    f = f2 // 2
    tc = _tile(f, LANE)
    nb = f // tc

    def body(a_ref, b_ref, wa_ref, wb_ref, d_ref, da_ref, db_ref, dwa_ref, dwb_ref):
        row = lax.broadcasted_iota(jnp.int32, (rows, tc), 0)
        av, bv, dv = a_ref[...], b_ref[...], d_ref[...].astype(F32)
        ac = _conv3(av, _taps(wa_ref), row)
        bc = _conv3(bv, _taps(wb_ref), row)
        _, vjp = jax.vjp(lambda p, q: jax.nn.silu(p) * q, ac, bc)
        dac, dbc = vjp(dv)
        dxa, dwa = _conv3_bwd(av, _taps(wa_ref), dac, row)
        dxb, dwb = _conv3_bwd(bv, _taps(wb_ref), dbc, row)
        da_ref[...] = dxa.astype(da_ref.dtype)
        db_ref[...] = dxb.astype(db_ref.dtype)
        for k in range(3):
            dwa_ref[k:k + 1, :] = dwa[k]
            dwb_ref[k:k + 1, :] = dwb[k]

    col = pl.BlockSpec((rows, tc), lambda j: (0, j))
    wspec = pl.BlockSpec((3, tc), lambda j: (0, j))
    return pl.pallas_call(
        body,
        out_shape=[_sds((rows, f), BF16)] * 2 + [_sds((3, f), F32)] * 2,
        grid=(nb,),
        in_specs=[col, pl.BlockSpec((rows, tc), lambda j: (0, nb + j)), wspec,
                  pl.BlockSpec((3, tc), lambda j: (0, nb + j)), col],
        out_specs=[col, col, wspec, wspec],
        compiler_params=_cp(("parallel",)),
        name="ffn_act_bwd",
    )(up, up, w, w, dact)


def _attn_scores(q, kc, kp, slope, dilation, has_prev):
    scale = HEAD_DIM ** -0.5
    nt = (((1,), (1,)), ((), ()))
    s_c = lax.dot_general(q, kc, nt, preferred_element_type=F32) * scale
    s_p = lax.dot_general(q, kp, nt, preferred_element_type=F32) * scale
    qi = lax.broadcasted_iota(jnp.int32, (ATTN_BLOCK, ATTN_BLOCK), 0)
    kj = lax.broadcasted_iota(jnp.int32, (ATTN_BLOCK, ATTN_BLOCK), 1)
    dist_c = qi - kj
    dist_p = dist_c + ATTN_BLOCK
    s_c = jnp.where(dist_c >= 0, s_c - slope * (dist_c * dilation).astype(F32), NEG_INF)
    s_p = jnp.where((dist_p <= ATTN_BLOCK) & has_prev, s_p - slope * (dist_p * dilation).astype(F32), NEG_INF)
    return s_c, s_p


def _slab(seq, col0):
    assert col0 % LANE == 0
    return pl.BlockSpec((seq, LANE), lambda hh, r, s: (0, col0 // LANE + hh))


def _residue_rows(r, block, dilation):
    if dilation == 1:
        return pl.ds(pl.multiple_of(block * ATTN_BLOCK, ATTN_BLOCK), ATTN_BLOCK)
    return pl.ds(r + dilation * ATTN_BLOCK * block, ATTN_BLOCK, stride=dilation)


def _head_col(x, mask):
    return jnp.max(jnp.where(mask, x, -jnp.inf), axis=-1, keepdims=True)


def _attn_fwd(proj, offs, pattern, dilation, sw, slopes):
    seq, _ = proj.shape
    nb = seq // dilation // ATTN_BLOCK
    pairs = sw // LANE

    def body(q_ref, k_ref, v_ref, s0_ref, s1_ref, o_ref, lse_ref):
        r, i = pl.program_id(1), pl.program_id(2)
        cur, prev = _residue_rows(r, i, dilation), _residue_rows(r, jnp.maximum(i - 1, 0), dilation)
        first = lax.broadcasted_iota(jnp.int32, (ATTN_BLOCK, LANE), 1) < HEAD_DIM
        q2 = q_ref[cur, :]
        kc, kp = k_ref[cur, :].astype(BF16), k_ref[prev, :].astype(BF16)
        vc, vp = v_ref[cur, :].astype(BF16), v_ref[prev, :].astype(BF16)
        res = []
        for mask, sl_ref in ((first, s0_ref), (~first, s1_ref)):
            qh = jnp.where(mask, q2, 0.0).astype(BF16)
            s_c, s_p = _attn_scores(qh, kc, kp, sl_ref[:, :1], dilation, i > 0)
            mx = jnp.maximum(jnp.max(s_c, axis=-1, keepdims=True), jnp.max(s_p, axis=-1, keepdims=True))
            p_c, p_p = jnp.exp(s_c - mx), jnp.exp(s_p - mx)
            den = jnp.sum(p_c, axis=-1, keepdims=True) + jnp.sum(p_p, axis=-1, keepdims=True)
            o = (jnp.dot(p_c.astype(BF16), vc, preferred_element_type=F32)
                 + jnp.dot(p_p.astype(BF16), vp, preferred_element_type=F32))
            res.append((o / den, mx + jnp.log(den)))
        o_ref[cur, :] = jnp.where(first, res[0][0], res[1][0])
        lse_ref[cur, :] = jnp.where(first, res[0][1], res[1][1])

    slope = pl.BlockSpec((None, 1, LANE), lambda hh, r, s: (hh, 0, 0))
    return pl.pallas_call(
        body,
        out_shape=[_sds((seq, sw), F32)] * 2,
        grid=(pairs, dilation, nb),
        in_specs=[_slab(seq, offs[0] + pattern * sw), _slab(seq, offs[1] + pattern * sw),
                  _slab(seq, offs[2] + pattern * sw), slope, slope],
        out_specs=[_slab(seq, 0), _slab(seq, 0)],
        compiler_params=_cp(("parallel", "arbitrary", "arbitrary")),
        name=f"attn_fwd_d{dilation}",
    )(proj, proj, proj, *slopes)


def _attn_bwd(proj, do, lse, delta, offs, pattern, dilation, sw, slopes):
    seq, _ = proj.shape
    nb = seq // dilation // ATTN_BLOCK
    pairs = sw // LANE

    def body(q_ref, k_ref, v_ref, do_ref, lse_ref, dl_ref, s0_ref, s1_ref, dq_ref, dk_ref, dv_ref, ck_ref, cv_ref):
        r, step = pl.program_id(1), pl.program_id(2)
        i = nb - 1 - step
        cur, prev = _residue_rows(r, i, dilation), _residue_rows(r, jnp.maximum(i - 1, 0), dilation)
        scale = HEAD_DIM ** -0.5
        nt = (((1,), (1,)), ((), ()))
        first = lax.broadcasted_iota(jnp.int32, (ATTN_BLOCK, LANE), 1) < HEAD_DIM
        q2, do2, lse2, dl2 = q_ref[cur, :], do_ref[cur, :], lse_ref[cur, :], dl_ref[cur, :]
        kc, kp = k_ref[cur, :].astype(BF16), k_ref[prev, :].astype(BF16)
        vc, vp = v_ref[cur, :].astype(BF16), v_ref[prev, :].astype(BF16)

        @pl.when(step == 0)
        def _():
            ck_ref[...] = jnp.zeros_like(ck_ref)
            cv_ref[...] = jnp.zeros_like(cv_ref)

        dq, dk_c, dv_c, dk_p, dv_p = [], 0.0, 0.0, 0.0, 0.0
        for mask, sl_ref in ((first, s0_ref), (~first, s1_ref)):
            qh = jnp.where(mask, q2, 0.0).astype(BF16)
            doh = jnp.where(mask, do2, 0.0).astype(BF16)
            lse_col, dl_col = _head_col(lse2, mask), _head_col(dl2, mask)
            s_c, s_p = _attn_scores(qh, kc, kp, sl_ref[:, :1], dilation, i > 0)
            p_c, p_p = jnp.exp(s_c - lse_col), jnp.exp(s_p - lse_col)
            ds_c = p_c * (lax.dot_general(doh, vc, nt, preferred_element_type=F32) - dl_col)
            ds_p = p_p * (lax.dot_general(doh, vp, nt, preferred_element_type=F32) - dl_col)
            dq.append(jnp.dot(ds_c.astype(BF16), kc, preferred_element_type=F32)
                      + jnp.dot(ds_p.astype(BF16), kp, preferred_element_type=F32))
            dk_c = dk_c + jnp.dot(ds_c.T.astype(BF16), qh, preferred_element_type=F32)
            dv_c = dv_c + jnp.dot(p_c.T.astype(BF16), doh, preferred_element_type=F32)
            dk_p = dk_p + jnp.dot(ds_p.T.astype(BF16), qh, preferred_element_type=F32)
            dv_p = dv_p + jnp.dot(p_p.T.astype(BF16), doh, preferred_element_type=F32)
        dq_ref[cur, :] = jnp.where(first, dq[0], dq[1]) * scale
        dk_ref[cur, :] = dk_c * scale + ck_ref[...]
        dv_ref[cur, :] = dv_c + cv_ref[...]
        ck_ref[...] = dk_p * scale
        cv_ref[...] = dv_p

    slope = pl.BlockSpec((None, 1, LANE), lambda hh, r, s: (hh, 0, 0))
    tok = _slab(seq, 0)
    return pl.pallas_call(
        body,
        out_shape=[_sds((seq, sw), F32)] * 3,
        grid=(pairs, dilation, nb),
        in_specs=[_slab(seq, offs[0] + pattern * sw), _slab(seq, offs[1] + pattern * sw),
                  _slab(seq, offs[2] + pattern * sw), tok, tok, tok, slope, slope],
        out_specs=[tok, tok, tok],
        scratch_shapes=[pltpu.VMEM((ATTN_BLOCK, LANE), F32), pltpu.VMEM((ATTN_BLOCK, LANE), F32)],
        compiler_params=_cp(("parallel", "arbitrary", "arbitrary")),
        name=f"attn_bwd_d{dilation}",
    )(proj, proj, proj, do, lse, delta, *slopes)


def _attn_merge(outs, lses):
    rows, aw = outs[0].shape
    tm = _tile(rows, 256, SUBLANE)

    def body(o0, o1, o2, l0, l1, l2, o_ref, lse_ref):
        lv = [l0[...], l1[...], l2[...]]
        mx = jnp.maximum(jnp.maximum(lv[0], lv[1]), lv[2])
        w = [jnp.exp(t - mx) for t in lv]
        den = w[0] + w[1] + w[2]
        o_ref[...] = (w[0] * o0[...] + w[1] * o1[...] + w[2] * o2[...]) / den
        lse_ref[...] = mx + jnp.log(den)

    spec = pl.BlockSpec((tm, aw), lambda i: (i, 0))
    return pl.pallas_call(
        body,
        out_shape=[_sds((rows, aw), F32)] * 2,
        grid=(rows // tm,),
        in_specs=[spec] * 6,
        out_specs=[spec, spec],
        compiler_params=_cp(("parallel",)),
        name="attn_merge",
    )(*outs, *lses)


def _attn_delta(do, o, head_ones):
    rows, aw = do.shape
    tm = _tile(rows, 256, SUBLANE)

    def body(do_ref, o_ref, e_ref, d_ref):
        d_ref[...] = jnp.dot(do_ref[...] * o_ref[...], e_ref[...], preferred_element_type=F32,
                             precision=lax.Precision.HIGHEST)

    spec = pl.BlockSpec((tm, aw), lambda i: (i, 0))
    return pl.pallas_call(
        body,
        out_shape=_sds((rows, aw), F32),
        grid=(rows // tm,),
        in_specs=[spec, spec, pl.BlockSpec((aw, aw), lambda i: (0, 0))],
        out_specs=spec,
        compiler_params=_cp(("parallel",)),
        name="attn_delta",
    )(do, o, head_ones)


def _alibi_slopes(pattern, hp):
    n_heads = hp * len(DSWA_PATTERNS)
    s = np.array([2.0 ** (-8.0 * (pattern * hp + h + 1) / n_heads) for h in range(hp)], dtype=np.float32)
    return [jnp.asarray(np.broadcast_to(s[par::2, None, None], (hp // 2, 1, LANE)).copy()) for par in (0, 1)]


def _loss_fwd_bwd(y, target):
    rows, d = y.shape
    tm = _tile(rows, 256, SUBLANE)

    def body(y_ref, t_ref, dy_ref, l_ref):
        i = pl.program_id(0)
        err = y_ref[...] - t_ref[...]
        dy_ref[...] = err * (1.0 / d)
        part = 0.5 * jnp.sum(jnp.mean(err * err, axis=-1, keepdims=True), axis=0, keepdims=True)

        @pl.when(i == 0)
        def _():
            l_ref[...] = jnp.zeros_like(l_ref)

        l_ref[...] += jnp.broadcast_to(part, l_ref.shape)

    spec = pl.BlockSpec((tm, d), lambda i: (i, 0))
    dy, loss = pl.pallas_call(
        body,
        out_shape=[_sds((rows, d), F32), _sds((SUBLANE, LANE), F32)],
        grid=(rows // tm,),
        in_specs=[spec, spec],
        out_specs=[spec, pl.BlockSpec((SUBLANE, LANE), lambda i: (0, 0))],
        compiler_params=_cp(("arbitrary",)),
        name="loss",
    )(y, target)
    return dy, loss[0, 0]


def _adam_math(w, g, m, v):
    m = ADAM_B1 * m + (1.0 - ADAM_B1) * g
    v = ADAM_B2 * v + (1.0 - ADAM_B2) * jnp.square(g)
    m_hat = m / (1.0 - ADAM_B1 ** ADAM_STEP)
    v_hat = v / (1.0 - ADAM_B2 ** ADAM_STEP)
    delta = -ADAM_LR * (m_hat / (jnp.sqrt(v_hat) + ADAM_EPS) + ADAM_WD * w)
    return delta, m, v


def _as2d(a):
    if a.ndim == 1:
        return a.reshape(1, -1)
    return a.reshape(-1, a.shape[-1])


def _adam(w, g, m, v, name, after=None):
    shape = w.shape
    w2, g2, m2, v2 = _as2d(w), _as2d(g), _as2d(m), _as2d(v)
    r, c = w2.shape
    tr = _tile(r, 512, SUBLANE)
    tc = _tile(c, 1024)
    extra = [] if after is None else [after]

    def body(w_ref, g_ref, m_ref, v_ref, *rest):
        d_ref, mo_ref, vo_ref = rest[len(extra):]
        delta, mn, vn = _adam_math(w_ref[...], g_ref[...], m_ref[...], v_ref[...])
        d_ref[...] = delta
        mo_ref[...] = mn
        vo_ref[...] = vn

    spec = pl.BlockSpec((tr, tc), lambda i, j: (i, j))
    outs = pl.pallas_call(
        body,
        out_shape=[_sds((r, c), F32)] * 3,
        grid=(r // tr, c // tc),
        in_specs=[spec] * 4 + [ANY] * len(extra),
        out_specs=[spec] * 3,
        compiler_params=_cp(("parallel", "parallel")),
        name=name,
    )(w2, g2, m2, v2, *extra)
    return [o.reshape(shape) for o in outs]


def _wmod_grad_adam(c_t, dmod, w, m, v):
    nl, d, cols = w.shape
    nex = c_t.shape[1]
    tr = _tile(d, 256, SUBLANE)
    tc = _tile(cols, 1024)

    def body(c_ref, dm_ref, w_ref, m_ref, v_ref, g_ref, d_ref, mo_ref, vo_ref):
        cond = jax.nn.silu(c_ref[...]).astype(BF16)
        g = jnp.dot(cond, dm_ref[...].astype(BF16), preferred_element_type=F32)
        delta, mn, vn = _adam_math(w_ref[...], g, m_ref[...], v_ref[...])
        g_ref[...] = g
        d_ref[...] = delta
        mo_ref[...] = mn
        vo_ref[...] = vn

    spec = pl.BlockSpec((None, tr, tc), lambda l, i, j: (l, i, j))
    return pl.pallas_call(
        body,
        out_shape=[_sds((nl, d, cols), F32)] * 4,
        grid=(nl, d // tr, cols // tc),
        in_specs=[pl.BlockSpec((tr, nex), lambda l, i, j: (i, 0)),
                  pl.BlockSpec((None, nex, tc), lambda l, i, j: (l, 0, j)), spec, spec, spec],
        out_specs=[spec] * 4,
        compiler_params=_cp(("parallel", "parallel", "parallel")),
        name="wmod_grad_adam",
    )(c_t, dmod, w, m, v)


def _my_pos():
    return lax.axis_index("x"), lax.axis_index("y"), lax.axis_index("c")


def _ag8(x4, select_half, name):
    a, s, r, c = x4.shape
    assert s == (2 if select_half else 1)

    def body(x_ref, out_ref, send_sems, recv_sems, local_sem):
        x, y, cc = _my_pos()
        me, sibling = (x, y, cc), (x, y, 1 - cc)
        chips = [(1 - x, y), (x, 1 - y), (1 - x, 1 - y)]
        src_mine = x_ref.at[:, pl.ds(cc if select_half else 0, 1)]

        def blk(px, py, pc):
            return out_ref.at[:, pl.ds(4 * px + 2 * py + pc, 1)]

        def copy(k, block, to, src=None):
            return pltpu.make_async_remote_copy(
                src_ref=blk(*block) if src is None else src, dst_ref=blk(*block),
                send_sem=send_sems.at[k], recv_sem=recv_sems.at[k], device_id=to, device_id_type=MESH)

        mine = pltpu.make_async_copy(src_mine, blk(*me), local_sem)
        mine.start()
        first = [copy(0, me, sibling, src=src_mine)]
        first += [copy(1 + j, me, (*chip, cc), src=src_mine) for j, chip in enumerate(chips)]
        for cp in first:
            cp.start()
        passed = [copy(4 + j, (*chip, cc), sibling) for j, chip in enumerate(chips)]
        for j, chip in enumerate(chips):
            copy(1 + j, (*chip, cc), me).wait_recv()
            passed[j].start()
        copy(0, sibling, me).wait_recv()
        for j, chip in enumerate(chips):
            copy(4 + j, (*chip, 1 - cc), me).wait_recv()
        for cp in first + passed:
            cp.wait_send()
        mine.wait()

    return pl.pallas_call(
        body,
        out_shape=_sds((a, N_DEV, r, c), x4.dtype),
        in_specs=[ANY],
        out_specs=ANY,
        scratch_shapes=[pltpu.SemaphoreType.DMA((7,)), pltpu.SemaphoreType.DMA((7,)), pltpu.SemaphoreType.DMA],
        name=name,
    )(x4)


def _chip_of(x, y, k):
    return (1 - x if k & 2 else x), (1 - y if k & 1 else y)


HBM = pl.BlockSpec(memory_space=pltpu.HBM)
SEM = pl.BlockSpec(memory_space=pltpu.SEMAPHORE)
DATAFLOW = pltpu.SideEffectType.DATAFLOW_SIDE_EFFECTING


def _own_block(k, chip, cc):
    del k
    return 2 * chip + cc


def _distance_slot(k, chip, cc):
    del chip, cc
    return k - 1


def _ici_copies(srcs, dsts, slot, send_sems, recv_sems):
    x, y, cc = _my_pos()
    if slot is None:
        return [pltpu.make_async_remote_copy(
            src_ref=s_ref.at[:, pl.ds(1 - cc, 1)], dst_ref=d_ref, send_sem=send_sems.at[n], recv_sem=recv_sems.at[n],
            device_id=(x, y, 1 - cc), device_id_type=MESH) for n, (s_ref, d_ref) in enumerate(zip(srcs, dsts))]
    chip = 2 * x + y
    copies = []
    for n, (s_ref, d_ref) in enumerate(zip(srcs, dsts)):
        for k in (1, 2, 3):
            px, py = _chip_of(x, y, k)
            at = slot(k, chip, cc)
            copies.append(pltpu.make_async_remote_copy(
                src_ref=s_ref.at[pl.ds(at, 1)], dst_ref=d_ref.at[pl.ds(at, 1)],
                send_sem=send_sems.at[3 * n + k - 1], recv_sem=recv_sems.at[3 * n + k - 1],
                device_id=(px, py, cc), device_id_type=MESH))
    return copies


def _ici_start(srcs, lands, slot, after, name):
    n = len(srcs)
    arrays = list(srcs) + ([] if lands is None else list(lands))
    na = len(arrays)
    n_sems = n if slot is None else 3 * n

    def body(*refs):
        s_refs = refs[:n]
        d_refs = s_refs if lands is None else refs[n:na]
        send_sems, recv_sems, token = refs[na + 1], refs[na + 2], refs[-1]
        for cp in _ici_copies(s_refs, d_refs, slot, send_sems, recv_sems):
            cp.start()
        token[...] = jnp.zeros_like(token)

    outs = pl.pallas_call(
        body,
        name=name,
        out_shape=(pltpu.SemaphoreType.DMA((n_sems,)), pltpu.SemaphoreType.DMA((n_sems,)),
                   *[pltpu.HBM(a.shape, a.dtype) for a in arrays], _sds((SUBLANE, LANE), F32)),
        in_specs=[HBM] * na + [ANY],
        out_specs=(SEM, SEM, *([HBM] * na), pl.BlockSpec(memory_space=pltpu.VMEM)),
        input_output_aliases={i: 2 + i for i in range(na)},
        compiler_params=pltpu.CompilerParams(has_side_effects=DATAFLOW),
    )(*[pltpu.with_memory_space_constraint(a, pltpu.HBM) for a in arrays], after)
    return outs[0], outs[1], list(outs[2:2 + na]), outs[-1]


def _ici_wait(send_sems, recv_sems, arrays, n, shared, slot, after, name):
    na = len(arrays)

    def body(*refs):
        s_refs = refs[:n]
        d_refs = s_refs if shared else refs[n:na]
        for cp in _ici_copies(s_refs, d_refs, slot, refs[na], refs[na + 1]):
            cp.wait_send()
            cp.wait_recv()

    outs = pl.pallas_call(
        body,
        name=name,
        out_shape=tuple(pltpu.HBM(a.shape, a.dtype) for a in arrays),
        in_specs=[HBM] * na + [SEM, SEM, ANY],
        out_specs=tuple([HBM] * na),
        input_output_aliases={i: i for i in range(na)},
        compiler_params=pltpu.CompilerParams(has_side_effects=DATAFLOW),
    )(*arrays, send_sems, recv_sems, after)
    return list(outs)


def _my_chip():
    return 2 * lax.axis_index("x") + lax.axis_index("y")


def _cast_own(w, layer, name):
    _, r, cols = w.shape
    tr = _tile(r, 512, 2 * SUBLANE)
    tc = _tile(cols, 1024)

    def body(w_ref, o_ref):
        o_ref[...] = w_ref[...].astype(o_ref.dtype)

    return pl.pallas_call(
        body,
        out_shape=_sds((N_CHIP, r, cols), BF16),
        grid=(r // tr, cols // tc),
        in_specs=[pl.BlockSpec((None, tr, tc), lambda i, j: (layer, i, j))],
        out_specs=pl.BlockSpec((None, tr, tc), lambda i, j: (_my_chip(), i, j)),
        compiler_params=_cp(("parallel", "parallel")),
        name=name,
    )(w)


def _forward_halves(bufs, name):
    n = len(bufs)

    def body(*refs):
        ins, outs = refs[:n], refs[n:2 * n]
        send_sems, recv_sems = refs[2 * n], refs[2 * n + 1]
        x, y, cc = _my_pos()
        chip = 2 * x + y
        copies = []
        for i in range(n):
            for k in (1, 2, 3):
                at = 2 * (chip ^ k) + cc
                copies.append(pltpu.make_async_remote_copy(
                    src_ref=ins[i].at[pl.ds(at, 1)], dst_ref=outs[i].at[pl.ds(at, 1)],
                    send_sem=send_sems.at[3 * i + k - 1], recv_sem=recv_sems.at[3 * i + k - 1],
                    device_id=(x, y, 1 - cc), device_id_type=MESH))
        for cp in copies:
            cp.start()
        for cp in copies:
            cp.wait()

    return pl.pallas_call(
        body,
        out_shape=[_sds(b.shape, b.dtype) for b in bufs],
        in_specs=[ANY] * n,
        out_specs=[ANY] * n,
        scratch_shapes=[pltpu.SemaphoreType.DMA((3 * n,)), pltpu.SemaphoreType.DMA((3 * n,))],
        input_output_aliases={i: i for i in range(n)},
        name=name,
    )(*bufs)


def _rs_sibling(g8s, name):
    n = len(g8s)
    g4s = [g.reshape(N_CHIP, 2, g.shape[1], g.shape[2]) for g in g8s]

    def body(*refs):
        ins, outs = refs[:n], refs[n:2 * n]
        send_sems, recv_sems = refs[2 * n], refs[2 * n + 1]
        x, y, cc = _my_pos()
        copies = [pltpu.make_async_remote_copy(
            src_ref=ins[i].at[:, pl.ds(1 - cc, 1)], dst_ref=outs[i], send_sem=send_sems.at[i],
            recv_sem=recv_sems.at[i], device_id=(x, y, 1 - cc), device_id_type=MESH) for i in range(n)]
        for cp in copies:
            cp.start()
        for cp in copies:
            cp.wait()

    return pl.pallas_call(
        body,
        out_shape=[_sds((N_CHIP, 1, g.shape[2], g.shape[3]), g.dtype) for g in g4s],
        in_specs=[ANY] * n,
        out_specs=[ANY] * n,
        scratch_shapes=[pltpu.SemaphoreType.DMA((n,)), pltpu.SemaphoreType.DMA((n,))],
        name=name,
    )(*g4s)


def _share_halves(halves, name):
    def body(in_ref, out_ref, send_sem, recv_sem):
        x, y, cc = _my_pos()
        cp = pltpu.make_async_remote_copy(
            src_ref=in_ref.at[:, pl.ds(cc, 1)], dst_ref=out_ref.at[:, pl.ds(cc, 1)], send_sem=send_sem,
            recv_sem=recv_sem, device_id=(x, y, 1 - cc), device_id_type=MESH)
        cp.start()
        cp.wait()

    return pl.pallas_call(
        body,
        out_shape=_sds(halves.shape, halves.dtype),
        in_specs=[ANY],
        out_specs=ANY,
        scratch_shapes=[pltpu.SemaphoreType.DMA, pltpu.SemaphoreType.DMA],
        input_output_aliases={0: 0},
        name=name,
    )(halves)


def _rs_add_remote(g8, recv_a, name):
    _, r, c = g8.shape
    ra = recv_a.reshape(N_CHIP, r, c)
    tr = _tile(r, 512, SUBLANE)
    tc = _tile(c, 1024)

    def body(g_ref, r_ref, o_ref):
        o_ref[...] = (g_ref[...] + r_ref[...]).astype(o_ref.dtype)

    return pl.pallas_call(
        body,
        out_shape=_sds((3, r, c), BF16),
        grid=(3, r // tr, c // tc),
        in_specs=[pl.BlockSpec((None, tr, tc),
                               lambda k, i, j: (2 * (_my_chip() ^ (k + 1)) + lax.axis_index("c"), i, j)),
                  pl.BlockSpec((None, tr, tc), lambda k, i, j: (_my_chip() ^ (k + 1), i, j))],
        out_specs=pl.BlockSpec((None, tr, tc), lambda k, i, j: (k, i, j)),
        compiler_params=_cp(("parallel",) * 3),
        name=name,
    )(g8, ra)


def _rs_add_final(g8, recv_a, recv_b, out_buf, layer, name):
    _, r, c = g8.shape
    ra = recv_a.reshape(N_CHIP, r, c)
    tr = _tile(r, 512, SUBLANE)
    tc = _tile(c, 1024)

    def body(g_ref, r_ref, b0_ref, b1_ref, b2_ref, buf_ref, o_ref):
        del buf_ref
        o_ref[...] = (((g_ref[...] + r_ref[...]) + b0_ref[...].astype(F32)) + b1_ref[...].astype(F32)
                      ) + b2_ref[...].astype(F32)

    def bspec(k):
        return pl.BlockSpec((None, tr, tc), functools.partial(lambda i, j, k: (k, i, j), k=k))

    return pl.pallas_call(
        body,
        out_shape=_sds(out_buf.shape, F32),
        grid=(r // tr, c // tc),
        in_specs=[pl.BlockSpec((None, tr, tc), lambda i, j: (2 * _my_chip() + lax.axis_index("c"), i, j)),
                  pl.BlockSpec((None, tr, tc), lambda i, j: (_my_chip(), i, j)),
                  bspec(0), bspec(1), bspec(2), ANY],
        out_specs=pl.BlockSpec((None, None, tr, tc), lambda i, j: (layer, lax.axis_index("c"), i, j)),
        input_output_aliases={5: 0},
        compiler_params=_cp(("parallel",) * 2),
        name=name,
    )(g8, ra, recv_b, recv_b, recv_b, out_buf)


def _sum8(x8, name):
    _, r, c = x8.shape
    tr = _tile(r, 256, SUBLANE)

    def body(x_ref, o_ref):
        acc = x_ref[0]
        for b in range(1, N_DEV):
            acc = acc + x_ref[b]
        o_ref[...] = acc

    return pl.pallas_call(
        body,
        out_shape=_sds((r, c), F32),
        grid=(r // tr,),
        in_specs=[pl.BlockSpec((N_DEV, tr, c), lambda i: (0, i, 0))],
        out_specs=pl.BlockSpec((tr, c), lambda i: (i, 0)),
        compiler_params=_cp(("parallel",)),
        name=name,
    )(x8)


def _block_diag(t):
    g, p, q = t.shape
    eye = jnp.eye(g, dtype=t.dtype)
    return (t[:, :, None, :] * eye[:, None, :, None]).reshape(g * p, g * q)


def _diag_blocks(mat, g):
    p, q = mat.shape[0] // g, mat.shape[1] // g
    eye = jnp.eye(g, dtype=mat.dtype)
    return jnp.sum(mat.reshape(g, p, g, q) * eye[:, None, :, None], axis=2)


def _interleave(re, im, c):
    lead = re.shape[:-1]
    gn = re.shape[-1]
    return jnp.stack([re.reshape(*lead, gn // c, c), im.reshape(*lead, gn // c, c)], axis=-2).reshape(*lead, 2 * gn)


def _deinterleave(cat, c):
    lead = cat.shape[:-1]
    gn = cat.shape[-1] // 2
    t = cat.reshape(*lead, gn // c, 2, c)
    return t[..., 0, :].reshape(*lead, gn), t[..., 1, :].reshape(*lead, gn)


def kernel(x, c, w_mod, b_mod, g_pre_mix, g_post_mix, g_pre_ffn, g_post_ffn, w_in, ssm_log_dt, ssm_a_re, ssm_a_im, ssm_b_re, ssm_b_im, ssm_c_re, ssm_c_im, ssm_d, w_glu, b_glu, conv_mix_w, w_ssm_out, w_attn_out, w_conv_out, b_gate, w_o, w_up, ffn_conv_w, w_down, loss_target, m_w_mod, m_b_mod, m_g_pre_mix, m_g_post_mix, m_g_pre_ffn, m_g_post_ffn, m_w_in, m_ssm_log_dt, m_ssm_a_re, m_ssm_a_im, m_ssm_b_re, m_ssm_b_im, m_ssm_c_re, m_ssm_c_im, m_ssm_d, m_w_glu, m_b_glu, m_conv_mix_w, m_w_ssm_out, m_w_attn_out, m_w_conv_out, m_b_gate, m_w_o, m_w_up, m_ffn_conv_w, m_w_down, v_w_mod, v_b_mod, v_g_pre_mix, v_g_post_mix, v_g_pre_ffn, v_g_post_ffn, v_w_in, v_ssm_log_dt, v_ssm_a_re, v_ssm_a_im, v_ssm_b_re, v_ssm_b_im, v_ssm_c_re, v_ssm_c_im, v_ssm_d, v_w_glu, v_b_glu, v_conv_mix_w, v_w_ssm_out, v_w_attn_out, v_w_conv_out, v_b_gate, v_w_o, v_w_up, v_ffn_conv_w, v_w_down):
    weights = dict(w_mod=w_mod, b_mod=b_mod, g_pre_mix=g_pre_mix, g_post_mix=g_post_mix, g_pre_ffn=g_pre_ffn, g_post_ffn=g_post_ffn, w_in=w_in, ssm_log_dt=ssm_log_dt, ssm_a_re=ssm_a_re, ssm_a_im=ssm_a_im, ssm_b_re=ssm_b_re, ssm_b_im=ssm_b_im, ssm_c_re=ssm_c_re, ssm_c_im=ssm_c_im, ssm_d=ssm_d, w_glu=w_glu, b_glu=b_glu, conv_mix_w=conv_mix_w, w_ssm_out=w_ssm_out, w_attn_out=w_attn_out, w_conv_out=w_conv_out, b_gate=b_gate, w_o=w_o, w_up=w_up, ffn_conv_w=ffn_conv_w, w_down=w_down)
    mom_m = dict(w_mod=m_w_mod, b_mod=m_b_mod, g_pre_mix=m_g_pre_mix, g_post_mix=m_g_post_mix, g_pre_ffn=m_g_pre_ffn, g_post_ffn=m_g_post_ffn, w_in=m_w_in, ssm_log_dt=m_ssm_log_dt, ssm_a_re=m_ssm_a_re, ssm_a_im=m_ssm_a_im, ssm_b_re=m_ssm_b_re, ssm_b_im=m_ssm_b_im, ssm_c_re=m_ssm_c_re, ssm_c_im=m_ssm_c_im, ssm_d=m_ssm_d, w_glu=m_w_glu, b_glu=m_b_glu, conv_mix_w=m_conv_mix_w, w_ssm_out=m_w_ssm_out, w_attn_out=m_w_attn_out, w_conv_out=m_w_conv_out, b_gate=m_b_gate, w_o=m_w_o, w_up=m_w_up, ffn_conv_w=m_ffn_conv_w, w_down=m_w_down)
    mom_v = dict(w_mod=v_w_mod, b_mod=v_b_mod, g_pre_mix=v_g_pre_mix, g_post_mix=v_g_post_mix, g_pre_ffn=v_g_pre_ffn, g_post_ffn=v_g_post_ffn, w_in=v_w_in, ssm_log_dt=v_ssm_log_dt, ssm_a_re=v_ssm_a_re, ssm_a_im=v_ssm_a_im, ssm_b_re=v_ssm_b_re, ssm_b_im=v_ssm_b_im, ssm_c_re=v_ssm_c_re, ssm_c_im=v_ssm_c_im, ssm_d=v_ssm_d, w_glu=v_w_glu, b_glu=v_b_glu, conv_mix_w=v_conv_mix_w, w_ssm_out=v_w_ssm_out, w_attn_out=v_w_attn_out, w_conv_out=v_w_conv_out, b_gate=v_b_gate, w_o=v_w_o, w_up=v_w_up, ffn_conv_w=v_ffn_conv_w, w_down=v_w_down)
    names = list(weights)

    nl = w_in.shape[0]
    seq, d = x.shape[1], x.shape[2]
    sw = d // 4
    groups = sw // SSM_GROUP
    gn = groups * SSM_STATE
    hp = sw // HEAD_DIM
    qw = 3 * sw
    off_q, off_k, off_v = sw, sw + qw, sw + 2 * qw
    off_conv = sw + 3 * qw
    off_gate = off_conv + 3 * sw
    n_in = off_gate + 3 * d
    f = w_down.shape[1] * N_CHIP
    scan_c = _scan_cols(gn)
    assert seq % (ATTN_BLOCK * DSWA_PATTERNS[-1][1]) == 0 and all(w // dl == ATTN_BLOCK for w, dl in DSWA_PATTERNS)

    px, py, pc = _my_pos()
    chip = 2 * px + py
    dev = 2 * chip + pc

    x2 = x.reshape(seq, d)
    target2 = loss_target.reshape(seq, d)

    mix_keys = ("w_in", "w_glu", "w_ssm_out", "w_attn_out", "w_conv_out", "w_o")
    ffn_keys = ("w_up", "w_down")
    col_sharded = ("w_in", "w_ssm_out", "w_attn_out", "w_conv_out", "w_up")
    n_stages = 2 * nl

    def stage_keys(stage):
        return ffn_keys if stage % 2 else mix_keys

    def blocked(k, buf8):
        r, cols = weights[k].shape[1:]
        return buf8.reshape(1, N_CHIP, r, cols) if k in col_sharded else buf8.reshape(1, 1, N_CHIP * r, cols)

    def begin_gather(stage, after):
        keys = stage_keys(stage)
        bufs = []
        for k in keys:
            r, cols = weights[k].shape[1:]
            bufs.append(_cast_own(weights[k], stage // 2, "cast_own").reshape(N_DEV, r // 2, cols))
        return _ici_start(bufs, None, _own_block, after, f"gather_start_{stage}")

    def end_gather(stage, pending, after):
        send_sems, recv_sems, bufs, _ = pending
        bufs = _ici_wait(send_sems, recv_sems, bufs, len(bufs), True, _own_block, after, f"gather_wait_{stage}")
        bufs = _forward_halves(bufs, "gather_forward")
        return {k: blocked(k, b) for k, b in zip(stage_keys(stage), bufs)}

    c_all = _ag8(c.reshape(1, 1, 1, d), False, "ag_cond").reshape(N_DEV, d)
    c_pad = jnp.concatenate([c_all, jnp.zeros((SUBLANE, d), F32)], axis=0)
    mcols = w_mod.shape[2]
    w_mod4 = w_mod.reshape(nl, 1, d, mcols)
    mod_loc = jnp.stack([_mm_nn(c_pad, w_mod4, l, name="mod_fwd", a_fn=jax.nn.silu) for l in range(nl)])
    mod_all = _ag8(mod_loc.reshape(nl, 1, 2 * SUBLANE, mcols), False, "ag_mod")
    mod_rows = lax.dynamic_slice_in_dim(mod_all[:, 0::2], dev, 1, axis=2)
    mod = mod_rows.reshape(nl, N_CHIP * mcols) + b_mod
    mods = mod.reshape(nl, 6, 1, d)

    taps = jnp.concatenate([conv_mix_w.reshape(-1), ffn_conv_w.reshape(-1)])
    tap_w = 8 * LANE
    tap_r = -(-taps.shape[0] // (tap_w * SUBLANE)) * SUBLANE
    taps = jnp.concatenate([taps, jnp.zeros((tap_r * tap_w - taps.shape[0],), F32)])
    taps_all = _ag8(taps.reshape(1, 1, tap_r, tap_w), False, "ag_small_weight")[0, 0::2].reshape(N_CHIP, -1)

    def whole(w, at):
        _, r, cols = w.shape
        got = taps_all[:, at:at + w.size].reshape(N_CHIP, nl, r, cols)
        return got.transpose(1, 2, 0, 3).reshape(nl, r, N_CHIP * cols)

    conv_w_full = whole(conv_mix_w, 0)
    ffn_w_full = whole(ffn_conv_w, conv_mix_w.size)
    wfs = [dict() for _ in range(nl)]
    pendings, order = [], mods[0, 0, :, :1] + taps_all[:1, :1]
    for stage in range(n_stages):
        pendings.append(begin_gather(stage, order))
        order = pendings[-1][3]
    wfs[0].update(end_gather(0, pendings[0], mods))

    head_ones = jnp.asarray(np.kron(np.eye(hp, dtype=np.float32), np.ones((HEAD_DIM, HEAD_DIM), np.float32)))
    slopes = [_alibi_slopes(p, hp) for p in range(len(DSWA_PATTERNS))]
    qkv_offs = (off_q, off_k, off_v)

    def row(v):
        return v.reshape(1, -1)

    saved = []
    xl = x2
    for l in range(nl):
        sh1, sc1, gt1, sh2, sc2, gt2 = [mods[l, q] for q in range(6)]
        s = dict(x_in=xl)
        wf = wfs[l]
        sh1_t = sh1 + order[0, 0] if l == 0 else sh1
        (h1,) = _ew_fwd(_fn_norm_mod, [(xl, 0, d)], [row(g_pre_mix[l]), sc1, sh1_t], [BF16], name="norm_mod_fwd", width=d)
        proj = _mm_nn(h1, wf["w_in"], 0, name="w_in_fwd")
        br_t = jnp.transpose(ssm_b_re[l], (2, 0, 1))
        bi_t = jnp.transpose(ssm_b_im[l], (2, 0, 1))
        disc_in = (ssm_log_dt[l].reshape(groups, 1), ssm_a_re[l], ssm_a_im[l], br_t, bi_t)
        lr, li, bbr_t, bbi_t = _ssm_disc_fwd(*disc_in)
        lam = _interleave(lr.reshape(1, gn), li.reshape(1, gn), scan_c)
        bcat = _interleave(_block_diag(jnp.transpose(bbr_t, (1, 0, 2))), _block_diag(jnp.transpose(bbi_t, (1, 0, 2))),
                           scan_c).astype(BF16).reshape(1, 1, sw, 2 * gn)
        cre = _block_diag(jnp.transpose(ssm_c_re[l], (0, 2, 1)))
        cim = _block_diag(jnp.transpose(ssm_c_im[l], (0, 2, 1)))
        ccat = jnp.transpose(_interleave(cre.T, -cim.T, scan_c)).astype(BF16).reshape(1, 1, 2 * gn, sw)
        xcat = _mm_nn(proj, bcat, 0, name="ssm_b_fwd", k_dim=sw)
        hcat = _ssm_scan_fwd(xcat, lam)
        y_ssm_pre = _mm_nn(hcat, ccat, 0, name="ssm_c_fwd")
        (gact,) = _ew_fwd(_fn_gelu, [(y_ssm_pre, 0, sw), (proj, 0, sw)], [row(ssm_d[l])], [F32], name="gelu_fwd", width=sw,
                          tw=_tile(sw, 512))
        z = _mm_nn(gact, wf["w_glu"], 0, name="w_glu_fwd")
        (s_ssm,) = _ew_fwd(_fn_glu, [(gact, 0, sw), (z, 0, sw)], [row(b_glu[l])], [BF16], name="glu_fwd", width=sw,
                           tw=_tile(sw, 512))
        y_ssm = _mm_nn(s_ssm, wf["w_ssm_out"], 0, name="w_branch_out_fwd")
        outs, lses = [], []
        for p, (_, dl) in enumerate(DSWA_PATTERNS):
            o_p, lse_p = _attn_fwd(proj, qkv_offs, p, dl, sw, slopes[p])
            outs.append(o_p)
            lses.append(lse_p)
        o_attn, lse_attn = _attn_merge(outs, lses)
        y_attn = _mm_nn(o_attn, wf["w_attn_out"], 0, name="w_branch_out_fwd")
        cv = _gconv_fwd(proj, off_conv, sw, conv_w_full[l])
        y_conv = _mm_nn(cv, wf["w_conv_out"], 0, name="w_branch_out_fwd")
        bg = b_gate[l].reshape(3, 1, d)
        gate_xs = [(proj, off_gate + q * d, d) for q in range(3)] + [(y_ssm, 0, d), (y_attn, 0, d), (y_conv, 0, d)]
        (merged,) = _ew_fwd(_fn_gates, gate_xs, [bg[0], bg[1], bg[2]], [BF16], name="gates_fwd", width=d,
                            tw=_tile(sw, 512))
        y_mix = _mm_nn(merged, wf["w_o"], 0, name="w_o_fwd")
        (x_mid,) = _ew_fwd(_fn_residual, [(xl, 0, d), (y_mix, 0, d)], [gt1, row(g_post_mix[l])], [F32], name="residual_fwd",
                           width=d)
        wf.update(end_gather(2 * l + 1, pendings[2 * l + 1], x_mid))
        (h2,) = _ew_fwd(_fn_norm_mod, [(x_mid, 0, d)], [row(g_pre_ffn[l]), sc2, sh2], [BF16], name="norm_mod_fwd", width=d)
        up = _mm_nn(h2, wf["w_up"], 0, name="w_up_fwd")
        act = _ffn_act_fwd(up, ffn_w_full[l])
        y_ffn = _mm_nn(act, wf["w_down"], 0, name="w_down_fwd")
        (x_out,) = _ew_fwd(_fn_residual, [(x_mid, 0, d), (y_ffn, 0, d)], [gt2, row(g_post_ffn[l])], [F32],
                           name="residual_fwd", width=d)
        if l + 1 < nl:
            wfs[l + 1].update(end_gather(2 * l + 2, pendings[2 * l + 2], x_out))
        s.update(h1=h1, proj=proj, disc_in=disc_in, lam=lam, bcat=bcat, ccat=ccat, hcat=hcat, y_ssm_pre=y_ssm_pre,
                 gact=gact, z=z, s_ssm=s_ssm, y_ssm=y_ssm, o_attn=o_attn, lse_attn=lse_attn, y_attn=y_attn,
                 cv=cv, y_conv=y_conv, merged=merged, y_mix=y_mix, x_mid=x_mid, h2=h2, up=up, act=act, y_ffn=y_ffn)
        saved.append(s)
        xl = x_out

    dxl, loss_local = _loss_fwd_bwd(xl, target2)
    loss = lax.psum(loss_local, ("x", "y", "c"))

    gfin = {k: lax.empty((nl, 2, weights[k].shape[1] // 2, weights[k].shape[2]), F32) for k in mix_keys + ffn_keys}

    rs = dict(sib=None, ici=None)

    def rs_stage_end(stage, grads_4d):
        g8s = [grads_4d[k].reshape(N_DEV, weights[k].shape[1] // 2, weights[k].shape[2]) for k in stage_keys(stage)]
        g4s = [g.reshape(N_CHIP, 2, g.shape[1], g.shape[2]) for g in g8s]
        lands = [lax.empty((N_CHIP, 1, g.shape[1], g.shape[2]), F32) for g in g8s]
        rs["sib"] = (stage, _ici_start(g4s, lands, None, mods[0, 0, :, :1], f"rs_sibling_start_{stage}"))

    def rs_midpoint(after):
        if rs["ici"] is not None:
            stage, g8s, recv_a, (send_sems, recv_sems, arrays, _) = rs["ici"]
            n = len(g8s)
            arrays = _ici_wait(send_sems, recv_sems, arrays, n, False, _distance_slot, after, f"rs_wait_{stage}")
            for k, g8, ra, rb in zip(stage_keys(stage), g8s, recv_a, arrays[n:]):
                gfin[k] = _rs_add_final(g8, ra, rb, gfin[k], stage // 2, "rs_add_final")
            rs["ici"] = None
        if rs["sib"] is None:
            return 0.0
        stage, (send_sems, recv_sems, arrays, _) = rs["sib"]
        n = len(arrays) // 2
        arrays = _ici_wait(send_sems, recv_sems, arrays, n, False, None, after, f"rs_sibling_wait_{stage}")
        g8s = [a.reshape(N_DEV, a.shape[2], a.shape[3]) for a in arrays[:n]]
        recv_a = arrays[n:]
        s_rem = [_rs_add_remote(g, ra, "rs_add_remote") for g, ra in zip(g8s, recv_a)]
        lands = [lax.empty(t.shape, BF16) for t in s_rem]
        started = _ici_start(s_rem, lands, _distance_slot, recv_a[0], f"rs_start_{stage}")
        rs["sib"], rs["ici"] = None, (stage, g8s, recv_a, started)
        return started[3][0, 0]

    def after_rs(v):
        for flight in (rs["sib"], rs["ici"]):
            if flight is not None:
                v = v + flight[-1][3][0, 0]
        return v

    def grad_buf(k):
        return lax.empty(wf[k].shape, F32)

    small = {k: [None] * nl for k in ("g_pre_mix", "g_post_mix", "g_pre_ffn", "g_post_ffn", "ssm_log_dt", "ssm_a_re",
                                      "ssm_a_im", "ssm_b_re", "ssm_b_im", "ssm_c_re", "ssm_c_im", "ssm_d", "b_glu",
                                      "conv_mix_w", "b_gate", "ffn_conv_w", "dmod")}
    for l in reversed(range(nl)):
        s = saved[l]
        sh1, sc1, gt1, sh2, sc2, gt2 = [mods[l, q] for q in range(6)]
        proj = s["proj"]
        wf = wfs[l]
        gw = {}
        (dy_ffn,), (dgt2, dg_post_ffn) = _ew_bwd(
            _fn_residual, [(s["x_mid"], 0, d), (s["y_ffn"], 0, d)], [after_rs(gt2), row(g_post_ffn[l])],
            [[dxl]], [None, BF16], name="residual_bwd", width=d)
        dact = _mm_nt(dy_ffn, wf["w_down"], 0, name="w_down_bwd_x", out_dtype=BF16)
        gw["w_down"] = _mm_tn(s["act"], dy_ffn, grad_buf("w_down"), 0, name="w_down_bwd_w")
        behind = rs_midpoint(gw["w_down"])
        dup_a, dup_b, dwa, dwb = _ffn_act_bwd(s["up"], ffn_w_full[l] + behind, dact)
        dup = jnp.concatenate([dup_a, dup_b], axis=1)
        small["ffn_conv_w"][l] = jnp.concatenate([dwa, dwb], axis=1)
        dh2 = _mm_nt(dup, wf["w_up"], 0, name="w_up_bwd_x")
        gw["w_up"] = _mm_tn(s["h2"], dup, grad_buf("w_up"), 0, name="w_up_bwd_w")
        (dx_mid,), (dg_pre_ffn, dsc2, dsh2) = _ew_bwd(
            _fn_norm_mod, [(s["x_mid"], 0, d)], [row(g_pre_ffn[l]), sc2, sh2], [[dh2]], [F32], name="norm_mod_bwd", width=d,
            dx_add={0: dxl})
        rs_stage_end(2 * l + 1, gw)
        (dy_mix,), (dgt1, dg_post_mix) = _ew_bwd(
            _fn_residual, [(s["x_in"], 0, d), (s["y_mix"], 0, d)], [after_rs(gt1), row(g_post_mix[l])],
            [[dx_mid]], [None, BF16], name="residual_bwd", width=d)
        dmerged = _mm_nt(dy_mix, wf["w_o"], 0, name="w_o_bwd_x")
        gw["w_o"] = _mm_tn(s["merged"], dy_mix, grad_buf("w_o"), 0, name="w_o_bwd_w")
        bg = b_gate[l].reshape(3, 1, d)
        gate_xs = [(proj, off_gate + q * d, d) for q in range(3)] + [(s["y_ssm"], 0, d), (s["y_attn"], 0, d),
                                                                     (s["y_conv"], 0, d)]
        (dp0, dp1, dp2, dy_ssm, dy_attn, dy_conv), dbg = _ew_bwd(
            _fn_gates, gate_xs, [bg[0], bg[1], bg[2]], [[dmerged]], [BF16] * 6, name="gates_bwd", width=d,
            tw=_tile(sw, 512))
        small["b_gate"][l] = jnp.concatenate(dbg, axis=1)[0]
        behind = rs_midpoint(dy_conv)
        ds_ssm = _mm_nt(dy_ssm, wf["w_ssm_out"], 0, name="w_branch_out_bwd_x")
        gw["w_ssm_out"] = _mm_tn(s["s_ssm"], dy_ssm, grad_buf("w_ssm_out"), 0, name="w_branch_out_bwd_w")
        (dg1, dz), (db_glu,) = _ew_bwd(_fn_glu, [(s["gact"], 0, sw), (s["z"], 0, sw)], [row(b_glu[l]) + behind], [[ds_ssm]],
                                       [F32, BF16], name="glu_bwd", width=sw, tw=_tile(sw, 512))
        dg2 = _mm_nt(dz, wf["w_glu"], 0, name="w_glu_bwd_x")
        gw["w_glu"] = _mm_tn(s["gact"], dz, grad_buf("w_glu"), 0, name="w_glu_bwd_w")
        (dy_pre, du_skip), (dd_skip,) = _ew_bwd(_fn_gelu, [(s["y_ssm_pre"], 0, sw), (proj, 0, sw)], [row(ssm_d[l])],
                                               [[dg1, dg2]], [BF16, F32], name="gelu_bwd", width=sw, tw=_tile(sw, 512))
        dhcat = _mm_nt(dy_pre, s["ccat"], 0, name="ssm_c_bwd_x")
        dccat = _mm_tn(s["hcat"], dy_pre, lax.empty((1, 1, 2 * gn, sw), F32), 0, name="ssm_c_bwd_w")[0, 0]
        gcat, dlam = _ssm_scan_bwd(dhcat, s["hcat"], s["lam"])
        du_b = _mm_nt(gcat, s["bcat"], 0, name="ssm_b_bwd_x")
        dbcat = _mm_tn(proj, gcat, lax.empty((1, 1, sw, 2 * gn), F32), 0, name="ssm_b_bwd_w")[0, 0]
        dlr, dli = _deinterleave(dlam, scan_c)
        dbre, dbim = _deinterleave(dbcat, scan_c)
        dbbr_t = jnp.transpose(_diag_blocks(dbre, groups), (1, 0, 2))
        dbbi_t = jnp.transpose(_diag_blocks(dbim, groups), (1, 0, 2))
        gld, gar, gai, gbr_t, gbi_t = _ssm_disc_bwd(*s["disc_in"], dlr.reshape(groups, SSM_STATE),
                                                    dli.reshape(groups, SSM_STATE), dbbr_t, dbbi_t)
        dcre_t, dcim_t = _deinterleave(dccat.T, scan_c)
        small["ssm_c_re"][l] = _diag_blocks(dcre_t, groups)
        small["ssm_c_im"][l] = -_diag_blocks(dcim_t, groups)
        small["ssm_log_dt"][l] = gld.reshape(groups)
        small["ssm_a_re"][l], small["ssm_a_im"][l] = gar, gai
        small["ssm_b_re"][l] = jnp.transpose(gbr_t, (1, 2, 0))
        small["ssm_b_im"][l] = jnp.transpose(gbi_t, (1, 2, 0))
        small["ssm_d"][l], small["b_glu"][l] = dd_skip[0], db_glu[0]
        du = (du_skip + du_b).astype(BF16)
        do_attn = _mm_nt(dy_attn, wf["w_attn_out"], 0, name="w_branch_out_bwd_x")
        gw["w_attn_out"] = _mm_tn(s["o_attn"], dy_attn, grad_buf("w_attn_out"), 0, name="w_branch_out_bwd_w")
        delta = _attn_delta(do_attn, s["o_attn"], head_ones)
        dqs, dks, dvs = [], [], []
        for p, (_, dl) in enumerate(DSWA_PATTERNS):
            dq_p, dk_p, dv_p = _attn_bwd(proj, do_attn, s["lse_attn"], delta, qkv_offs, p, dl, sw, slopes[p])
            dqs.append(dq_p)
            dks.append(dk_p)
            dvs.append(dv_p)
        dcv = _mm_nt(dy_conv, wf["w_conv_out"], 0, name="w_branch_out_bwd_x", out_dtype=BF16)
        gw["w_conv_out"] = _mm_tn(s["cv"], dy_conv, grad_buf("w_conv_out"), 0, name="w_branch_out_bwd_w")
        dcb, dcc, dch, dconv_w = _gconv_bwd(proj, off_conv, sw, conv_w_full[l], dcv)
        small["conv_mix_w"][l] = dconv_w
        dproj = jnp.concatenate([du] + [t.astype(BF16) for t in dqs + dks + dvs] + [dcb, dcc, dch, dp0, dp1, dp2],
                                axis=1)
        dh1 = _mm_nt(dproj, wf["w_in"], 0, name="w_in_bwd_x")
        gw["w_in"] = _mm_tn(s["h1"], dproj, grad_buf("w_in"), 0, name="w_in_bwd_w")
        (dx_in,), (dg_pre_mix, dsc1, dsh1) = _ew_bwd(
            _fn_norm_mod, [(s["x_in"], 0, d)], [row(g_pre_mix[l]), sc1, sh1], [[dh1]], [F32], name="norm_mod_bwd", width=d,
            dx_add={0: dx_mid})
        rs_stage_end(2 * l, gw)
        small["g_pre_mix"][l], small["g_post_mix"][l] = dg_pre_mix[0], dg_post_mix[0]
        small["g_pre_ffn"][l], small["g_post_ffn"][l] = dg_pre_ffn[0], dg_post_ffn[0]
        small["dmod"][l] = jnp.concatenate([dsh1, dsc1, dgt1, dsh2, dsc2, dgt2], axis=1)[0]
        dxl = dx_in

    grad_x = dxl.reshape(x.shape)

    small = {k: jnp.stack(v) for k, v in small.items()}
    order = sorted(small)
    flat = jnp.concatenate([small[k].reshape(-1) for k in order])
    n_small = flat.shape[0]
    pack_w = 8 * LANE
    pack_r = -(-n_small // (pack_w * SUBLANE)) * SUBLANE
    flat = jnp.concatenate([flat, jnp.zeros((pack_r * pack_w - n_small,), F32)])
    gathered = _ag8(flat.reshape(1, 1, pack_r, pack_w), False, "ag_small_grads")[0]
    summed = _sum8(gathered, "sum_small_grads").reshape(-1)
    sgrad, at = {}, 0
    for k in order:
        size = small[k].size
        sgrad[k] = summed[at:at + size].reshape(small[k].shape)
        at += size
    dmod_off = sum(small[k].size for k in order[:order.index("dmod")])
    dmod_all = gathered.reshape(N_DEV, -1)[:, dmod_off:dmod_off + nl * 6 * d].reshape(N_DEV, nl, 6 * d)
    dmod_loc = lax.dynamic_slice_in_dim(jnp.transpose(dmod_all, (1, 0, 2)), chip * mcols, mcols, axis=2)

    grads = dict(sgrad)
    grads["b_mod"] = grads.pop("dmod")
    grads["conv_mix_w"] = lax.dynamic_slice_in_dim(sgrad["conv_mix_w"], chip * conv_mix_w.shape[2], conv_mix_w.shape[2], axis=2)
    grads["ffn_conv_w"] = lax.dynamic_slice_in_dim(sgrad["ffn_conv_w"], chip * ffn_conv_w.shape[2], ffn_conv_w.shape[2], axis=2)

    delta_w, new_m, new_v = {}, {}, {}

    def update(k, after=None):
        delta_w[k], new_m[k], new_v[k] = _adam(weights[k], grads[k], mom_m[k], mom_v[k], "adamw", after)

    behind = rs_midpoint(summed)
    in_flight = rs["ici"][3][3]
    c_t = jnp.pad(jnp.transpose(c_all), ((0, 0), (0, LANE - N_DEV))) + behind
    dmod_pad = jnp.pad(dmod_loc, ((0, 0), (0, LANE - N_DEV), (0, 0)))
    grads["w_mod"], delta_w["w_mod"], new_m["w_mod"], new_v["w_mod"] = _wmod_grad_adam(
        c_t, dmod_pad, w_mod, m_w_mod, v_w_mod)
    for k in ffn_keys:
        grads[k] = _share_halves(gfin[k], "rs_share").reshape(weights[k].shape)
        update(k, in_flight)
    for k in names:
        if k != "w_mod" and k not in mix_keys + ffn_keys:
            update(k)
    rs_midpoint(new_v["w_down"][0, :1, :1] + new_v["w_up"][0, :1, :1] + new_v["w_mod"][0, :1, :1])
    for k in mix_keys:
        grads[k] = _share_halves(gfin[k], "rs_share").reshape(weights[k].shape)
        update(k)

    return (loss, grad_x, *[grads[k] for k in names], *[delta_w[k] for k in names], *[new_m[k] for k in names],
            *[new_v[k] for k in names])
```

```python
import functools
import math

import numpy as np
import jax
import jax.numpy as jnp
from jax import lax
from jax.experimental import pallas as pl
from jax.experimental.pallas import tpu as pltpu

F32 = jnp.float32
BF16 = jnp.bfloat16
MESH = pl.DeviceIdType.MESH
ANY = pl.BlockSpec(memory_space=pl.ANY)

VMEM_LIMIT_BYTES = 48 * 1024 * 1024
LANE = 128
SUBLANE = 8

RMS_EPS = 1e-6
NEG_INF = -1e30
SSM_GROUP = 16
SSM_STATE = 64
HEAD_DIM = 64
DSWA_PATTERNS = ((128, 1), (512, 4), (2048, 16))
ATTN_BLOCK = 128
N_DEV = 8
N_CHIP = 4

ADAM_LR = 0.001
ADAM_B1 = 0.9
ADAM_B2 = 0.999
ADAM_EPS = 1e-08
ADAM_WD = 0.01
ADAM_STEP = 10


def _cp(sem=None):
    return pltpu.CompilerParams(dimension_semantics=sem, vmem_limit_bytes=VMEM_LIMIT_BYTES)


def _tile(n, pref, align=LANE):
    if n <= pref:
        return n
    t = (pref // align) * align
    while t >= align:
        if n % t == 0:
            return t
        t -= align
    return n


def _sds(shape, dtype):
    return jax.ShapeDtypeStruct(tuple(shape), dtype)


MM_VMEM_BUDGET = 40 * 1024 * 1024
MM_MAX_CONTRACT = 2048


def _divisors(n, align):
    if n % align:
        return [n]
    return [t for t in range(n, 0, -align) if n % t == 0]


def _halvings(n, align, floor=256):
    out = [n]
    while out[-1] % (2 * align) == 0 and out[-1] // 2 >= floor:
        out.append(out[-1] // 2)
    return out


def _pick_tiles(rows, cols, fixed_bytes, row_bytes, col_bytes, cell_bytes):
    best = None
    for tr in rows:
        for tc in cols:
            if fixed_bytes + row_bytes * tr + col_bytes * tc + cell_bytes * tr * tc <= MM_VMEM_BUDGET:
                if best is None or tr * tc > best[0] * best[1]:
                    best = (tr, tc)
                break
    assert best is not None
    return best


def _accumulate(step, n_steps, part, o_ref, acc_ref):
    if n_steps == 1:
        o_ref[...] = part.astype(o_ref.dtype)
        return
    acc = o_ref if acc_ref is None else acc_ref

    @pl.when(step == 0)
    def _():
        acc[...] = part

    @pl.when(step > 0)
    def _():
        acc[...] += part

    if acc_ref is not None:
        @pl.when(step == n_steps - 1)
        def _():
            o_ref[...] = acc_ref[...].astype(o_ref.dtype)


def _mm_nn(a, w, layer, *, name, k_dim=None, a_col0=0, out_dtype=F32, a_fn=None):
    m = a.shape[0]
    _, nb, kw, n = w.shape
    k_dim = kw if k_dim is None else k_dim
    assert k_dim == kw
    tk = _tile(k_dim, MM_MAX_CONTRACT)
    nk = k_dim // tk
    sa, so = a.dtype.itemsize, jnp.dtype(out_dtype).itemsize
    use_acc = nk > 1 and so != 4
    row_bytes = tk * (2 * sa + (2 if sa == 4 else 0) + (4 if a_fn is not None else 0))
    tm, tn = _pick_tiles(_halvings(m, SUBLANE), _divisors(n, LANE), 0, row_bytes, 2 * tk * w.dtype.itemsize,
                         2 * so + 4 + (4 if use_acc else 0))
    assert a_col0 % tk == 0
    npb = n // tn
    a0 = a_col0 // tk

    def body(a_ref, w_ref, o_ref, *scratch):
        av = a_ref[...]
        if a_fn is not None:
            av = a_fn(av.astype(F32))
        part = jnp.dot(av.astype(BF16), w_ref[...].astype(BF16), preferred_element_type=F32)
        _accumulate(pl.program_id(2), nk, part, o_ref, scratch[0] if use_acc else None)

    return pl.pallas_call(
        body,
        out_shape=_sds((m, nb * n), out_dtype),
        grid=(m // tm, nb * npb, nk),
        in_specs=[pl.BlockSpec((tm, tk), lambda i, j, k: (i, a0 + k)),
                  pl.BlockSpec((None, None, tk, tn), lambda i, j, k: (layer, j // npb, k, j % npb))],
        out_specs=pl.BlockSpec((tm, tn), lambda i, j, k: (i, j)),
        scratch_shapes=[pltpu.VMEM((tm, tn), F32)] if use_acc else [],
        compiler_params=_cp(("parallel", "parallel", "arbitrary")),
        name=name,
    )(a, w)


def _mm_nt(g, w, layer, *, name, out_dtype=F32):
    m = g.shape[0]
    _, nb, k_dim, n = w.shape
    assert g.shape[1] == nb * n
    tko = _tile(k_dim, MM_MAX_CONTRACT)
    sg, so = g.dtype.itemsize, jnp.dtype(out_dtype).itemsize
    use_acc = so != 4
    res_row = tko * (2 * so + 4 + (4 if use_acc else 0))
    tm, tc = _pick_tiles(_halvings(m, SUBLANE), _divisors(n, LANE), 0, res_row, 2 * tko * w.dtype.itemsize,
                         2 * sg + (2 if sg == 4 else 0))
    npb = n // tc
    nr = nb * npb
    use_acc = use_acc and nr > 1

    def body(g_ref, w_ref, o_ref, *scratch):
        part = lax.dot_general(g_ref[...].astype(BF16), w_ref[...].astype(BF16),
                               (((1,), (1,)), ((), ())), preferred_element_type=F32)
        _accumulate(pl.program_id(2), nr, part, o_ref, scratch[0] if use_acc else None)

    return pl.pallas_call(
        body,
        out_shape=_sds((m, k_dim), out_dtype),
        grid=(m // tm, k_dim // tko, nr),
        in_specs=[pl.BlockSpec((tm, tc), lambda i, kk, r: (i, r)),
                  pl.BlockSpec((None, None, tko, tc), lambda i, kk, r: (layer, r // npb, kk, r % npb))],
        out_specs=pl.BlockSpec((tm, tko), lambda i, kk, r: (i, kk)),
        scratch_shapes=[pltpu.VMEM((tm, tko), F32)] if use_acc else [],
        compiler_params=_cp(("parallel", "parallel", "arbitrary")),
        name=name,
    )(g, w)


def _mm_tn(a, g, out_buf, layer, *, name, a_col0=0):
    m = a.shape[0]
    _, nb, k_dim, n = out_buf.shape
    assert g.shape == (m, nb * n)
    tm = _tile(m, MM_MAX_CONTRACT, SUBLANE)
    nr = m // tm
    sa, sg = a.dtype.itemsize, g.dtype.itemsize
    tk, tn = _pick_tiles(_halvings(k_dim, LANE), _divisors(n, LANE), 0, tm * (2 * sa + (2 if sa == 4 else 0) + 2),
                         tm * (2 * sg + (2 if sg == 4 else 0)), 2 * 4 + 4)
    assert a_col0 % tk == 0
    a0 = a_col0 // tk
    npb = n // tn

    def body(a_ref, g_ref, buf_ref, o_ref):
        del buf_ref
        part = lax.dot_general(a_ref[...].astype(BF16), g_ref[...].astype(BF16),
                               (((0,), (0,)), ((), ())), preferred_element_type=F32)
        _accumulate(pl.program_id(2), nr, part, o_ref, None)

    return pl.pallas_call(
        body,
        out_shape=_sds(out_buf.shape, F32),
        grid=(k_dim // tk, nb * npb, nr),
        in_specs=[pl.BlockSpec((tm, tk), lambda kk, j, r: (r, a0 + kk)),
                  pl.BlockSpec((tm, tn), lambda kk, j, r: (r, j)),
                  ANY],
        out_specs=pl.BlockSpec((None, None, tk, tn), lambda kk, j, r: (layer, j // npb, kk, j % npb)),
        input_output_aliases={2: 0},
        compiler_params=_cp(("parallel", "parallel", "arbitrary")),
        name=name,
    )(a, g, out_buf)


def _ew_fwd(fn, xs, ps, out_dtypes, *, name, width, tw=None, tm=256):
    rows = xs[0][0].shape[0]
    tm = _tile(rows, tm, SUBLANE)
    tw = width if tw is None else tw
    nx, n_p = len(xs), len(ps)

    def body(*refs):
        xv = [r[...].astype(F32) for r in refs[:nx]]
        pv = [r[...].astype(F32) for r in refs[nx:nx + n_p]]
        outs = fn(*xv, *pv)
        if not isinstance(outs, (tuple, list)):
            outs = (outs,)
        for o_ref, o in zip(refs[nx + n_p:], outs):
            o_ref[...] = o.astype(o_ref.dtype)

    in_specs = []
    for arr, c0, w in xs:
        assert w == width and c0 % tw == 0
        in_specs.append(pl.BlockSpec((tm, tw), functools.partial(lambda i, j, b: (i, b + j), b=c0 // tw)))
    for p in ps:
        assert p.shape == (1, width)
        in_specs.append(pl.BlockSpec((1, tw), lambda i, j: (0, j)))
    outs = pl.pallas_call(
        body,
        out_shape=[_sds((rows, width), d) for d in out_dtypes],
        grid=(rows // tm, width // tw),
        in_specs=in_specs,
        out_specs=[pl.BlockSpec((tm, tw), lambda i, j: (i, j)) for _ in out_dtypes],
        compiler_params=_cp(("parallel", "parallel")),
        name=name,
    )(*[x[0] for x in xs], *ps)
    return outs


def _ew_bwd(fn, xs, ps, cts, dx_dtypes, *, name, width, tw=None, tm=256, dx_add=None):
    rows = xs[0][0].shape[0]
    tm = _tile(rows, tm, SUBLANE)
    tw = width if tw is None else tw
    nx, n_p = len(xs), len(ps)
    dx_add = dx_add or {}
    flat_cts = [c for group in cts for c in group]
    add_keys = sorted(dx_add)
    n_in = nx + n_p + len(flat_cts) + len(add_keys)
    dx_idx = [i for i, d in enumerate(dx_dtypes) if d is not None]

    def body(*refs):
        i = pl.program_id(1)
        xv = [r[...].astype(F32) for r in refs[:nx]]
        pv = [r[...].astype(F32) for r in refs[nx:nx + n_p]]
        pos = nx + n_p
        ct_vals = []
        for group in cts:
            acc = refs[pos][...].astype(F32)
            pos += 1
            for _ in group[1:]:
                acc = acc + refs[pos][...].astype(F32)
                pos += 1
            ct_vals.append(acc)
        add_vals = {}
        for key in add_keys:
            add_vals[key] = refs[pos][...].astype(F32)
            pos += 1
        out_refs = refs[n_in:]
        outs, vjp = jax.vjp(fn, *xv, *pv)
        grads = vjp(tuple(ct_vals) if isinstance(outs, (tuple, list)) else ct_vals[0])
        o = 0
        for idx in dx_idx:
            gval = grads[idx]
            if idx in add_vals:
                gval = gval + add_vals[idx]
            out_refs[o][...] = gval.astype(out_refs[o].dtype)
            o += 1
        for q in range(n_p):
            gp = grads[nx + q]
            ref = out_refs[o + q]

            @pl.when(i == 0)
            def _(ref=ref, gp=gp):
                ref[...] = gp

            @pl.when(i > 0)
            def _(ref=ref, gp=gp):
                ref[...] += gp

    tile_spec = pl.BlockSpec((tm, tw), lambda j, i: (i, j))
    in_specs = []
    for arr, c0, w in xs:
        assert w == width and c0 % tw == 0
        in_specs.append(pl.BlockSpec((tm, tw), functools.partial(lambda j, i, b: (i, b + j), b=c0 // tw)))
    for p in ps:
        in_specs.append(pl.BlockSpec((1, tw), lambda j, i: (0, j)))
    in_specs += [tile_spec] * (len(flat_cts) + len(add_keys))
    out_shape = [_sds((rows, width), dx_dtypes[idx]) for idx in dx_idx] + [_sds((1, width), F32)] * n_p
    out_specs = [tile_spec] * len(dx_idx) + [pl.BlockSpec((1, tw), lambda j, i: (0, j))] * n_p
    outs = pl.pallas_call(
        body,
        out_shape=out_shape,
        grid=(width // tw, rows // tm),
        in_specs=in_specs,
        out_specs=out_specs,
        compiler_params=_cp(("parallel", "arbitrary")),
        name=name,
    )(*[x[0] for x in xs], *ps, *flat_cts, *[dx_add[k] for k in add_keys])
    return outs[:len(dx_idx)], outs[len(dx_idx):]


def _rms(x):
    return x * lax.rsqrt(jnp.mean(x * x, axis=-1, keepdims=True) + RMS_EPS)


def _fn_norm_mod(x, g, sc, sh):
    return (_rms(x) * g) * (1.0 + sc) + sh


def _fn_residual(x, y, gt, g):
    return x + gt * (_rms(y) * g)


def _fn_gelu(y, u, d):
    return jax.nn.gelu(y + d * u)


def _fn_glu(g, z, b):
    return g * jax.nn.sigmoid(z + b)


def _fn_gates(p0, p1, p2, ys, ya, yc, b0, b1, b2):
    return (jax.nn.sigmoid(p0 + b0) * ys + jax.nn.sigmoid(p1 + b1) * ya + jax.nn.sigmoid(p2 + b2) * yc)


def _fn_disc(log_dt, ar, ai, br_t, bi_t):
    dt = jnp.exp(log_dt)
    mag = jnp.exp(ar * dt)
    lr, li = mag * jnp.cos(ai * dt), mag * jnp.sin(ai * dt)
    den = ar * ar + ai * ai
    fr = ((lr - 1.0) * ar + li * ai) / den
    fi = (li * ar - (lr - 1.0) * ai) / den
    bbr = fr[None] * br_t - fi[None] * bi_t
    bbi = fr[None] * bi_t + fi[None] * br_t
    return lr, li, bbr, bbi


def _ssm_disc_fwd(log_dt, ar, ai, br_t, bi_t):
    g, n = ar.shape

    def body(ld_ref, ar_ref, ai_ref, br_ref, bi_ref, lr_ref, li_ref, bbr_ref, bbi_ref):
        lr, li, bbr, bbi = _fn_disc(ld_ref[...], ar_ref[...], ai_ref[...], br_ref[...], bi_ref[...])
        lr_ref[...] = lr
        li_ref[...] = li
        bbr_ref[...] = bbr
        bbi_ref[...] = bbi

    return pl.pallas_call(
        body,
        out_shape=[_sds((g, n), F32), _sds((g, n), F32), _sds(br_t.shape, F32), _sds(br_t.shape, F32)],
        compiler_params=_cp(),
        name="ssm_disc_fwd",
    )(log_dt, ar, ai, br_t, bi_t)


def _ssm_disc_bwd(log_dt, ar, ai, br_t, bi_t, dlr, dli, dbbr, dbbi):
    g, n = ar.shape

    def body(ld_ref, ar_ref, ai_ref, br_ref, bi_ref, dlr_ref, dli_ref, dbbr_ref, dbbi_ref,
             gld_ref, gar_ref, gai_ref, gbr_ref, gbi_ref):
        _, vjp = jax.vjp(_fn_disc, ld_ref[...], ar_ref[...], ai_ref[...], br_ref[...], bi_ref[...])
        gld, gar, gai, gbr, gbi = vjp((dlr_ref[...], dli_ref[...], dbbr_ref[...], dbbi_ref[...]))
        gld_ref[...] = gld
        gar_ref[...] = gar
        gai_ref[...] = gai
        gbr_ref[...] = gbr
        gbi_ref[...] = gbi

    return pl.pallas_call(
        body,
        out_shape=[_sds((g, 1), F32), _sds((g, n), F32), _sds((g, n), F32), _sds(br_t.shape, F32),
                   _sds(br_t.shape, F32)],
        compiler_params=_cp(),
        name="ssm_disc_bwd",
    )(log_dt, ar, ai, br_t, bi_t, dlr, dli, dbbr, dbbi)


def _cmul(ar, ai, br, bi):
    return ar * br - ai * bi, ar * bi + ai * br


def _scan_tables(lr, li, reverse):
    c = lr.shape[-1]
    p1 = (jnp.broadcast_to(lr, (SUBLANE, c)), jnp.broadcast_to(li, (SUBLANE, c)))
    p2 = _cmul(*p1, *p1)
    p4 = _cmul(*p2, *p2)
    p8 = _cmul(*p4, *p4)
    row = lax.broadcasted_iota(jnp.int32, (SUBLANE, c), 0)
    dist = (SUBLANE - row) if reverse else (row + 1)
    pr, pi = jnp.ones((SUBLANE, c), F32), jnp.zeros((SUBLANE, c), F32)
    for bit, pw in ((1, p1), (2, p2), (4, p4), (8, p8)):
        qr, qi = _cmul(pr, pi, *pw)
        take = (dist & bit) != 0
        pr, pi = jnp.where(take, qr, pr), jnp.where(take, qi, pi)
    return row, (p1, p2, p4), (pr, pi)


def _shift_rows(x, s, row, reverse):
    if reverse:
        return jnp.where(row < SUBLANE - s, pltpu.roll(x, SUBLANE - s, 0), 0.0)
    return jnp.where(row >= s, pltpu.roll(x, s, 0), 0.0)


def _scan_tile(xr, xi, carry, row, pows, carry_pow, reverse):
    for s, pw in zip((1, 2, 4), pows):
        sr, si = _shift_rows(xr, s, row, reverse), _shift_rows(xi, s, row, reverse)
        tr, ti = _cmul(*pw, sr, si)
        xr, xi = xr + tr, xi + ti
    tr, ti = _cmul(*carry_pow, *carry)
    hr, hi = xr + tr, xi + ti
    edge = 0 if reverse else SUBLANE - 1
    c = hr.shape[-1]
    new_carry = (jnp.broadcast_to(hr[edge:edge + 1, :], (SUBLANE, c)),
                 jnp.broadcast_to(hi[edge:edge + 1, :], (SUBLANE, c)))
    return hr, hi, new_carry


def _scan_cols(gn):
    return _tile(gn, 256)


def _ssm_scan_fwd(xcat, lam):
    rows, gn2 = xcat.shape
    c = _scan_cols(gn2 // 2)
    n_tiles = rows // SUBLANE

    def body(lam_ref, x_ref, h_ref):
        lr, li = lam_ref[:, :c], lam_ref[:, c:]
        row, pows, carry_pow = _scan_tables(lr, li, False)

        def step(k, carry):
            t0 = pl.multiple_of(k * SUBLANE, SUBLANE)
            hr, hi, carry = _scan_tile(x_ref[pl.ds(t0, SUBLANE), :c], x_ref[pl.ds(t0, SUBLANE), c:], carry,
                                       row, pows, carry_pow, False)
            h_ref[pl.ds(t0, SUBLANE), :c] = hr
            h_ref[pl.ds(t0, SUBLANE), c:] = hi
            return carry

        zero = jnp.zeros((SUBLANE, c), F32)
        lax.fori_loop(0, n_tiles, step, (zero, zero))

    return pl.pallas_call(
        body,
        out_shape=_sds((rows, gn2), F32),
        grid=(gn2 // (2 * c),),
        in_specs=[pl.BlockSpec((1, 2 * c), lambda j: (0, j)), pl.BlockSpec((rows, 2 * c), lambda j: (0, j))],
        out_specs=pl.BlockSpec((rows, 2 * c), lambda j: (0, j)),
        compiler_params=_cp(("parallel",)),
        name="ssm_scan_fwd",
    )(lam, xcat)


def _ssm_scan_bwd(dhcat, hcat, lam):
    rows, gn2 = dhcat.shape
    c = _scan_cols(gn2 // 2)
    n_tiles = rows // SUBLANE

    def body(lam_ref, dh_ref, h_ref, g_ref, dlam_ref):
        lr, li = lam_ref[:, :c], -lam_ref[:, c:]
        row, pows, carry_pow = _scan_tables(lr, li, True)

        def step(k, state):
            carry, acc_r, acc_i = state
            kk = n_tiles - 1 - k
            t0 = pl.multiple_of(kk * SUBLANE, SUBLANE)
            gr, gi, carry = _scan_tile(dh_ref[pl.ds(t0, SUBLANE), :c], dh_ref[pl.ds(t0, SUBLANE), c:], carry,
                                       row, pows, carry_pow, True)
            g_ref[pl.ds(t0, SUBLANE), :c] = gr
            g_ref[pl.ds(t0, SUBLANE), c:] = gi
            tp = pl.multiple_of(jnp.maximum(kk - 1, 0) * SUBLANE, SUBLANE)
            has_prev = (kk > 0).astype(F32)
            prev_r = pltpu.roll(h_ref[pl.ds(tp, SUBLANE), :c], 1, 0) * has_prev
            prev_i = pltpu.roll(h_ref[pl.ds(tp, SUBLANE), c:], 1, 0) * has_prev
            hpr = jnp.where(row >= 1, pltpu.roll(h_ref[pl.ds(t0, SUBLANE), :c], 1, 0), prev_r)
            hpi = jnp.where(row >= 1, pltpu.roll(h_ref[pl.ds(t0, SUBLANE), c:], 1, 0), prev_i)
            acc_r = acc_r + gr * hpr + gi * hpi
            acc_i = acc_i + gi * hpr - gr * hpi
            return carry, acc_r, acc_i

        zero = jnp.zeros((SUBLANE, c), F32)
        _, acc_r, acc_i = lax.fori_loop(0, n_tiles, step, ((zero, zero), zero, zero))
        dlam_ref[:, :c] = jnp.sum(acc_r, axis=0, keepdims=True)
        dlam_ref[:, c:] = jnp.sum(acc_i, axis=0, keepdims=True)

    blk = pl.BlockSpec((rows, 2 * c), lambda j: (0, j))
    return pl.pallas_call(
        body,
        out_shape=[_sds((rows, gn2), F32), _sds((1, gn2), F32)],
        grid=(gn2 // (2 * c),),
        in_specs=[pl.BlockSpec((1, 2 * c), lambda j: (0, j)), blk, blk],
        out_specs=[blk, pl.BlockSpec((1, 2 * c), lambda j: (0, j))],
        compiler_params=_cp(("parallel",)),
        name="ssm_scan_bwd",
    )(lam, dhcat, hcat)


def _shift_down(x, k, row):
    return x if k == 0 else jnp.where(row >= k, pltpu.roll(x, k, 0), 0.0)


def _shift_up(x, k, row):
    n = x.shape[0]
    return x if k == 0 else jnp.where(row < n - k, pltpu.roll(x, n - k, 0), 0.0)


def _taps(w_ref):
    return [w_ref[k:k + 1, :] for k in range(3)]


def _conv3(x, w, row):
    return sum(w[k] * _shift_down(x, k, row) for k in range(3))


def _conv3_bwd(x, w, dy, row):
    dx = sum(w[k] * _shift_up(dy, k, row) for k in range(3))
    dw = [jnp.sum(dy * _shift_down(x, k, row), axis=0, keepdims=True) for k in range(3)]
    return dx, dw


def _gconv_fwd(proj, off, cw, w):
    rows = proj.shape[0]
    tc = _tile(cw, LANE)
    nb = cw // tc

    def body(b_ref, c_ref, h_ref, w_ref, o_ref):
        row = lax.broadcasted_iota(jnp.int32, (rows, tc), 0)
        o_ref[...] = (b_ref[...] * _conv3(c_ref[...] * h_ref[...], _taps(w_ref), row)).astype(o_ref.dtype)

    specs = [pl.BlockSpec((rows, tc), functools.partial(lambda j, b: (0, b + j), b=(off + q * cw) // tc))
             for q in range(3)]
    return pl.pallas_call(
        body,
        out_shape=_sds((rows, cw), BF16),
        grid=(nb,),
        in_specs=specs + [pl.BlockSpec((3, tc), lambda j: (0, j))],
        out_specs=pl.BlockSpec((rows, tc), lambda j: (0, j)),
        compiler_params=_cp(("parallel",)),
        name="gconv_fwd",
    )(proj, proj, proj, w)


def _gconv_bwd(proj, off, cw, w, dy):
    rows = proj.shape[0]
    tc = _tile(cw, LANE)
    nb = cw // tc

    def body(b_ref, c_ref, h_ref, w_ref, dy_ref, db_ref, dc_ref, dh_ref, dw_ref):
        row = lax.broadcasted_iota(jnp.int32, (rows, tc), 0)
        cv, hv, dyv = c_ref[...], h_ref[...], dy_ref[...].astype(F32)
        t = cv * hv
        db_ref[...] = (dyv * _conv3(t, _taps(w_ref), row)).astype(db_ref.dtype)
        dt, dw = _conv3_bwd(t, _taps(w_ref), dyv * b_ref[...], row)
        dc_ref[...] = (dt * hv).astype(dc_ref.dtype)
        dh_ref[...] = (dt * cv).astype(dh_ref.dtype)
        for k in range(3):
            dw_ref[k:k + 1, :] = dw[k]

    specs = [pl.BlockSpec((rows, tc), functools.partial(lambda j, b: (0, b + j), b=(off + q * cw) // tc))
             for q in range(3)]
    col = pl.BlockSpec((rows, tc), lambda j: (0, j))
    wspec = pl.BlockSpec((3, tc), lambda j: (0, j))
    return pl.pallas_call(
        body,
        out_shape=[_sds((rows, cw), BF16)] * 3 + [_sds((3, cw), F32)],
        grid=(nb,),
        in_specs=specs + [wspec, col],
        out_specs=[col, col, col, wspec],
        compiler_params=_cp(("parallel",)),
        name="gconv_bwd",
    )(proj, proj, proj, w, dy)


def _ffn_act_fwd(up, w):
    rows, f2 = up.shape
    f = f2 // 2
    tc = _tile(f, LANE)
    nb = f // tc

    def body(a_ref, b_ref, wa_ref, wb_ref, o_ref):
        row = lax.broadcasted_iota(jnp.int32, (rows, tc), 0)
        a = _conv3(a_ref[...], _taps(wa_ref), row)
        b = _conv3(b_ref[...], _taps(wb_ref), row)
        o_ref[...] = (jax.nn.silu(a) * b).astype(o_ref.dtype)

    return pl.pallas_call(
        body,
        out_shape=_sds((rows, f), BF16),
        grid=(nb,),
        in_specs=[pl.BlockSpec((rows, tc), lambda j: (0, j)), pl.BlockSpec((rows, tc), lambda j: (0, nb + j)),
                  pl.BlockSpec((3, tc), lambda j: (0, j)), pl.BlockSpec((3, tc), lambda j: (0, nb + j))],
        out_specs=pl.BlockSpec((rows, tc), lambda j: (0, j)),
        compiler_params=_cp(("parallel",)),
        name="ffn_act_fwd",
    )(up, up, w, w)


def _ffn_act_bwd(up, w, dact):
    rows, f2 = up.shape
    f = f2 // 2
    tc = _tile(f, LANE)
    nb = f // tc

    def body(a_ref, b_ref, wa_ref, wb_ref, d_ref, da_ref, db_ref, dwa_ref, dwb_ref):
        row = lax.broadcasted_iota(jnp.int32, (rows, tc), 0)
        av, bv, dv = a_ref[...], b_ref[...], d_ref[...].astype(F32)
        ac = _conv3(av, _taps(wa_ref), row)
        bc = _conv3(bv, _taps(wb_ref), row)
        _, vjp = jax.vjp(lambda p, q: jax.nn.silu(p) * q, ac, bc)
        dac, dbc = vjp(dv)
        dxa, dwa = _conv3_bwd(av, _taps(wa_ref), dac, row)
        dxb, dwb = _conv3_bwd(bv, _taps(wb_ref), dbc, row)
        da_ref[...] = dxa.astype(da_ref.dtype)
        db_ref[...] = dxb.astype(db_ref.dtype)
        for k in range(3):
            dwa_ref[k:k + 1, :] = dwa[k]
            dwb_ref[k:k + 1, :] = dwb[k]

    col = pl.BlockSpec((rows, tc), lambda j: (0, j))
    wspec = pl.BlockSpec((3, tc), lambda j: (0, j))
    return pl.pallas_call(
        body,
        out_shape=[_sds((rows, f), BF16)] * 2 + [_sds((3, f), F32)] * 2,
        grid=(nb,),
        in_specs=[col, pl.BlockSpec((rows, tc), lambda j: (0, nb + j)), wspec,
                  pl.BlockSpec((3, tc), lambda j: (0, nb + j)), col],
        out_specs=[col, col, wspec, wspec],
        compiler_params=_cp(("parallel",)),
        name="ffn_act_bwd",
    )(up, up, w, w, dact)


def _attn_scores(q, kc, kp, slope, dilation, has_prev):
    scale = HEAD_DIM ** -0.5
    nt = (((1,), (1,)), ((), ()))
    s_c = lax.dot_general(q, kc, nt, preferred_element_type=F32) * scale
    s_p = lax.dot_general(q, kp, nt, preferred_element_type=F32) * scale
    qi = lax.broadcasted_iota(jnp.int32, (ATTN_BLOCK, ATTN_BLOCK), 0)
    kj = lax.broadcasted_iota(jnp.int32, (ATTN_BLOCK, ATTN_BLOCK), 1)
    dist_c = qi - kj
    dist_p = dist_c + ATTN_BLOCK
    s_c = jnp.where(dist_c >= 0, s_c - slope * (dist_c * dilation).astype(F32), NEG_INF)
    s_p = jnp.where((dist_p <= ATTN_BLOCK) & has_prev, s_p - slope * (dist_p * dilation).astype(F32), NEG_INF)
    return s_c, s_p


def _slab(seq, col0):
    assert col0 % LANE == 0
    return pl.BlockSpec((seq, LANE), lambda hh, r, s: (0, col0 // LANE + hh))


def _residue_rows(r, block, dilation):
    if dilation == 1:
        return pl.ds(pl.multiple_of(block * ATTN_BLOCK, ATTN_BLOCK), ATTN_BLOCK)
    return pl.ds(r + dilation * ATTN_BLOCK * block, ATTN_BLOCK, stride=dilation)


def _head_col(x, mask):
    return jnp.max(jnp.where(mask, x, -jnp.inf), axis=-1, keepdims=True)


def _attn_fwd(proj, offs, pattern, dilation, sw, slopes):
    seq, _ = proj.shape
    nb = seq // dilation // ATTN_BLOCK
    pairs = sw // LANE

    def body(q_ref, k_ref, v_ref, s0_ref, s1_ref, o_ref, lse_ref):
        r, i = pl.program_id(1), pl.program_id(2)
        cur, prev = _residue_rows(r, i, dilation), _residue_rows(r, jnp.maximum(i - 1, 0), dilation)
        first = lax.broadcasted_iota(jnp.int32, (ATTN_BLOCK, LANE), 1) < HEAD_DIM
        q2 = q_ref[cur, :]
        kc, kp = k_ref[cur, :].astype(BF16), k_ref[prev, :].astype(BF16)
        vc, vp = v_ref[cur, :].astype(BF16), v_ref[prev, :].astype(BF16)
        res = []
        for mask, sl_ref in ((first, s0_ref), (~first, s1_ref)):
            qh = jnp.where(mask, q2, 0.0).astype(BF16)
            s_c, s_p = _attn_scores(qh, kc, kp, sl_ref[:, :1], dilation, i > 0)
            mx = jnp.maximum(jnp.max(s_c, axis=-1, keepdims=True), jnp.max(s_p, axis=-1, keepdims=True))
            p_c, p_p = jnp.exp(s_c - mx), jnp.exp(s_p - mx)
            den = jnp.sum(p_c, axis=-1, keepdims=True) + jnp.sum(p_p, axis=-1, keepdims=True)
            o = (jnp.dot(p_c.astype(BF16), vc, preferred_element_type=F32)
                 + jnp.dot(p_p.astype(BF16), vp, preferred_element_type=F32))
            res.append((o / den, mx + jnp.log(den)))
        o_ref[cur, :] = jnp.where(first, res[0][0], res[1][0])
        lse_ref[cur, :] = jnp.where(first, res[0][1], res[1][1])

    slope = pl.BlockSpec((None, 1, LANE), lambda hh, r, s: (hh, 0, 0))
    return pl.pallas_call(
        body,
        out_shape=[_sds((seq, sw), F32)] * 2,
        grid=(pairs, dilation, nb),
        in_specs=[_slab(seq, offs[0] + pattern * sw), _slab(seq, offs[1] + pattern * sw),
                  _slab(seq, offs[2] + pattern * sw), slope, slope],
        out_specs=[_slab(seq, 0), _slab(seq, 0)],
        compiler_params=_cp(("parallel", "arbitrary", "arbitrary")),
        name=f"attn_fwd_d{dilation}",
    )(proj, proj, proj, *slopes)


def _attn_bwd(proj, do, lse, delta, offs, pattern, dilation, sw, slopes):
    seq, _ = proj.shape
    nb = seq // dilation // ATTN_BLOCK
    pairs = sw // LANE

    def body(q_ref, k_ref, v_ref, do_ref, lse_ref, dl_ref, s0_ref, s1_ref, dq_ref, dk_ref, dv_ref, ck_ref, cv_ref):
        r, step = pl.program_id(1), pl.program_id(2)
        i = nb - 1 - step
        cur, prev = _residue_rows(r, i, dilation), _residue_rows(r, jnp.maximum(i - 1, 0), dilation)
        scale = HEAD_DIM ** -0.5
        nt = (((1,), (1,)), ((), ()))
        first = lax.broadcasted_iota(jnp.int32, (ATTN_BLOCK, LANE), 1) < HEAD_DIM
        q2, do2, lse2, dl2 = q_ref[cur, :], do_ref[cur, :], lse_ref[cur, :], dl_ref[cur, :]
        kc, kp = k_ref[cur, :].astype(BF16), k_ref[prev, :].astype(BF16)
        vc, vp = v_ref[cur, :].astype(BF16), v_ref[prev, :].astype(BF16)

        @pl.when(step == 0)
        def _():
            ck_ref[...] = jnp.zeros_like(ck_ref)
            cv_ref[...] = jnp.zeros_like(cv_ref)

        dq, dk_c, dv_c, dk_p, dv_p = [], 0.0, 0.0, 0.0, 0.0
        for mask, sl_ref in ((first, s0_ref), (~first, s1_ref)):
            qh = jnp.where(mask, q2, 0.0).astype(BF16)
            doh = jnp.where(mask, do2, 0.0).astype(BF16)
            lse_col, dl_col = _head_col(lse2, mask), _head_col(dl2, mask)
            s_c, s_p = _attn_scores(qh, kc, kp, sl_ref[:, :1], dilation, i > 0)
            p_c, p_p = jnp.exp(s_c - lse_col), jnp.exp(s_p - lse_col)
            ds_c = p_c * (lax.dot_general(doh, vc, nt, preferred_element_type=F32) - dl_col)
            ds_p = p_p * (lax.dot_general(doh, vp, nt, preferred_element_type=F32) - dl_col)
            dq.append(jnp.dot(ds_c.astype(BF16), kc, preferred_element_type=F32)
                      + jnp.dot(ds_p.astype(BF16), kp, preferred_element_type=F32))
            dk_c = dk_c + jnp.dot(ds_c.T.astype(BF16), qh, preferred_element_type=F32)
            dv_c = dv_c + jnp.dot(p_c.T.astype(BF16), doh, preferred_element_type=F32)
            dk_p = dk_p + jnp.dot(ds_p.T.astype(BF16), qh, preferred_element_type=F32)
            dv_p = dv_p + jnp.dot(p_p.T.astype(BF16), doh, preferred_element_type=F32)
        dq_ref[cur, :] = jnp.where(first, dq[0], dq[1]) * scale
        dk_ref[cur, :] = dk_c * scale + ck_ref[...]
        dv_ref[cur, :] = dv_c + cv_ref[...]
        ck_ref[...] = dk_p * scale
        cv_ref[...] = dv_p

    slope = pl.BlockSpec((None, 1, LANE), lambda hh, r, s: (hh, 0, 0))
    tok = _slab(seq, 0)
    return pl.pallas_call(
        body,
        out_shape=[_sds((seq, sw), F32)] * 3,
        grid=(pairs, dilation, nb),
        in_specs=[_slab(seq, offs[0] + pattern * sw), _slab(seq, offs[1] + pattern * sw),
                  _slab(seq, offs[2] + pattern * sw), tok, tok, tok, slope, slope],
        out_specs=[tok, tok, tok],
        scratch_shapes=[pltpu.VMEM((ATTN_BLOCK, LANE), F32), pltpu.VMEM((ATTN_BLOCK, LANE), F32)],
        compiler_params=_cp(("parallel", "arbitrary", "arbitrary")),
        name=f"attn_bwd_d{dilation}",
    )(proj, proj, proj, do, lse, delta, *slopes)


def _attn_merge(outs, lses):
    rows, aw = outs[0].shape
    tm = _tile(rows, 256, SUBLANE)

    def body(o0, o1, o2, l0, l1, l2, o_ref, lse_ref):
        lv = [l0[...], l1[...], l2[...]]
        mx = jnp.maximum(jnp.maximum(lv[0], lv[1]), lv[2])
        w = [jnp.exp(t - mx) for t in lv]
        den = w[0] + w[1] + w[2]
        o_ref[...] = (w[0] * o0[...] + w[1] * o1[...] + w[2] * o2[...]) / den
        lse_ref[...] = mx + jnp.log(den)

    spec = pl.BlockSpec((tm, aw), lambda i: (i, 0))
    return pl.pallas_call(
        body,
        out_shape=[_sds((rows, aw), F32)] * 2,
        grid=(rows // tm,),
        in_specs=[spec] * 6,
        out_specs=[spec, spec],
        compiler_params=_cp(("parallel",)),
        name="attn_merge",
    )(*outs, *lses)


def _attn_delta(do, o, head_ones):
    rows, aw = do.shape
    tm = _tile(rows, 256, SUBLANE)

    def body(do_ref, o_ref, e_ref, d_ref):
        d_ref[...] = jnp.dot(do_ref[...] * o_ref[...], e_ref[...], preferred_element_type=F32,
                             precision=lax.Precision.HIGHEST)

    spec = pl.BlockSpec((tm, aw), lambda i: (i, 0))
    return pl.pallas_call(
        body,
        out_shape=_sds((rows, aw), F32),
        grid=(rows // tm,),
        in_specs=[spec, spec, pl.BlockSpec((aw, aw), lambda i: (0, 0))],
        out_specs=spec,
        compiler_params=_cp(("parallel",)),
        name="attn_delta",
    )(do, o, head_ones)


def _alibi_slopes(pattern, hp):
    n_heads = hp * len(DSWA_PATTERNS)
    s = np.array([2.0 ** (-8.0 * (pattern * hp + h + 1) / n_heads) for h in range(hp)], dtype=np.float32)
    return [jnp.asarray(np.broadcast_to(s[par::2, None, None], (hp // 2, 1, LANE)).copy()) for par in (0, 1)]


def _loss_fwd_bwd(y, target):
    rows, d = y.shape
    tm = _tile(rows, 256, SUBLANE)

    def body(y_ref, t_ref, dy_ref, l_ref):
        i = pl.program_id(0)
        err = y_ref[...] - t_ref[...]
        dy_ref[...] = err * (1.0 / d)
        part = 0.5 * jnp.sum(jnp.mean(err * err, axis=-1, keepdims=True), axis=0, keepdims=True)

        @pl.when(i == 0)
        def _():
            l_ref[...] = jnp.zeros_like(l_ref)

        l_ref[...] += jnp.broadcast_to(part, l_ref.shape)

    spec = pl.BlockSpec((tm, d), lambda i: (i, 0))
    dy, loss = pl.pallas_call(
        body,
        out_shape=[_sds((rows, d), F32), _sds((SUBLANE, LANE), F32)],
        grid=(rows // tm,),
        in_specs=[spec, spec],
        out_specs=[spec, pl.BlockSpec((SUBLANE, LANE), lambda i: (0, 0))],
        compiler_params=_cp(("arbitrary",)),
        name="loss",
    )(y, target)
    return dy, loss[0, 0]


def _adam_math(w, g, m, v):
    m = ADAM_B1 * m + (1.0 - ADAM_B1) * g
    v = ADAM_B2 * v + (1.0 - ADAM_B2) * jnp.square(g)
    m_hat = m / (1.0 - ADAM_B1 ** ADAM_STEP)
    v_hat = v / (1.0 - ADAM_B2 ** ADAM_STEP)
    delta = -ADAM_LR * (m_hat / (jnp.sqrt(v_hat) + ADAM_EPS) + ADAM_WD * w)
    return delta, m, v


def _as2d(a):
    if a.ndim == 1:
        return a.reshape(1, -1)
    return a.reshape(-1, a.shape[-1])


def _adam(w, g, m, v, name, after=None):
    shape = w.shape
    w2, g2, m2, v2 = _as2d(w), _as2d(g), _as2d(m), _as2d(v)
    r, c = w2.shape
    tr = _tile(r, 512, SUBLANE)
    tc = _tile(c, 1024)
    extra = [] if after is None else [after]

    def body(w_ref, g_ref, m_ref, v_ref, *rest):
        d_ref, mo_ref, vo_ref = rest[len(extra):]
        delta, mn, vn = _adam_math(w_ref[...], g_ref[...], m_ref[...], v_ref[...])
        d_ref[...] = delta
        mo_ref[...] = mn
        vo_ref[...] = vn

    spec = pl.BlockSpec((tr, tc), lambda i, j: (i, j))
    outs = pl.pallas_call(
        body,
        out_shape=[_sds((r, c), F32)] * 3,
        grid=(r // tr, c // tc),
        in_specs=[spec] * 4 + [ANY] * len(extra),
        out_specs=[spec] * 3,
        compiler_params=_cp(("parallel", "parallel")),
        name=name,
    )(w2, g2, m2, v2, *extra)
    return [o.reshape(shape) for o in outs]


def _wmod_grad_adam(c_t, dmod, w, m, v):
    nl, d, cols = w.shape
    nex = c_t.shape[1]
    tr = _tile(d, 256, SUBLANE)
    tc = _tile(cols, 1024)

    def body(c_ref, dm_ref, w_ref, m_ref, v_ref, g_ref, d_ref, mo_ref, vo_ref):
        cond = jax.nn.silu(c_ref[...]).astype(BF16)
        g = jnp.dot(cond, dm_ref[...].astype(BF16), preferred_element_type=F32)
        delta, mn, vn = _adam_math(w_ref[...], g, m_ref[...], v_ref[...])
        g_ref[...] = g
        d_ref[...] = delta
        mo_ref[...] = mn
        vo_ref[...] = vn

    spec = pl.BlockSpec((None, tr, tc), lambda l, i, j: (l, i, j))
    return pl.pallas_call(
        body,
        out_shape=[_sds((nl, d, cols), F32)] * 4,
        grid=(nl, d // tr, cols // tc),
        in_specs=[pl.BlockSpec((tr, nex), lambda l, i, j: (i, 0)),
                  pl.BlockSpec((None, nex, tc), lambda l, i, j: (l, 0, j)), spec, spec, spec],
        out_specs=[spec] * 4,
        compiler_params=_cp(("parallel", "parallel", "parallel")),
        name="wmod_grad_adam",
    )(c_t, dmod, w, m, v)


def _my_pos():
    return lax.axis_index("x"), lax.axis_index("y"), lax.axis_index("c")


def _ag8(x4, select_half, name):
    a, s, r, c = x4.shape
    assert s == (2 if select_half else 1)

    def body(x_ref, out_ref, send_sems, recv_sems, local_sem):
        x, y, cc = _my_pos()
        me, sibling = (x, y, cc), (x, y, 1 - cc)
        chips = [(1 - x, y), (x, 1 - y), (1 - x, 1 - y)]
        src_mine = x_ref.at[:, pl.ds(cc if select_half else 0, 1)]

        def blk(px, py, pc):
            return out_ref.at[:, pl.ds(4 * px + 2 * py + pc, 1)]

        def copy(k, block, to, src=None):
            return pltpu.make_async_remote_copy(
                src_ref=blk(*block) if src is None else src, dst_ref=blk(*block),
                send_sem=send_sems.at[k], recv_sem=recv_sems.at[k], device_id=to, device_id_type=MESH)

        mine = pltpu.make_async_copy(src_mine, blk(*me), local_sem)
        mine.start()
        first = [copy(0, me, sibling, src=src_mine)]
        first += [copy(1 + j, me, (*chip, cc), src=src_mine) for j, chip in enumerate(chips)]
        for cp in first:
            cp.start()
        passed = [copy(4 + j, (*chip, cc), sibling) for j, chip in enumerate(chips)]
        for j, chip in enumerate(chips):
            copy(1 + j, (*chip, cc), me).wait_recv()
            passed[j].start()
        copy(0, sibling, me).wait_recv()
        for j, chip in enumerate(chips):
            copy(4 + j, (*chip, 1 - cc), me).wait_recv()
        for cp in first + passed:
            cp.wait_send()
        mine.wait()

    return pl.pallas_call(
        body,
        out_shape=_sds((a, N_DEV, r, c), x4.dtype),
        in_specs=[ANY],
        out_specs=ANY,
        scratch_shapes=[pltpu.SemaphoreType.DMA((7,)), pltpu.SemaphoreType.DMA((7,)), pltpu.SemaphoreType.DMA],
        name=name,
    )(x4)


def _chip_of(x, y, k):
    return (1 - x if k & 2 else x), (1 - y if k & 1 else y)


HBM = pl.BlockSpec(memory_space=pltpu.HBM)
SEM = pl.BlockSpec(memory_space=pltpu.SEMAPHORE)
DATAFLOW = pltpu.SideEffectType.DATAFLOW_SIDE_EFFECTING


def _own_block(k, chip, cc):
    del k
    return 2 * chip + cc


def _distance_slot(k, chip, cc):
    del chip, cc
    return k - 1


def _ici_copies(srcs, dsts, slot, send_sems, recv_sems):
    x, y, cc = _my_pos()
    if slot is None:
        return [pltpu.make_async_remote_copy(
            src_ref=s_ref.at[:, pl.ds(1 - cc, 1)], dst_ref=d_ref, send_sem=send_sems.at[n], recv_sem=recv_sems.at[n],
            device_id=(x, y, 1 - cc), device_id_type=MESH) for n, (s_ref, d_ref) in enumerate(zip(srcs, dsts))]
    chip = 2 * x + y
    copies = []
    for n, (s_ref, d_ref) in enumerate(zip(srcs, dsts)):
        for k in (1, 2, 3):
            px, py = _chip_of(x, y, k)
            at = slot(k, chip, cc)
            copies.append(pltpu.make_async_remote_copy(
                src_ref=s_ref.at[pl.ds(at, 1)], dst_ref=d_ref.at[pl.ds(at, 1)],
                send_sem=send_sems.at[3 * n + k - 1], recv_sem=recv_sems.at[3 * n + k - 1],
                device_id=(px, py, cc), device_id_type=MESH))
    return copies


def _ici_start(srcs, lands, slot, after, name):
    n = len(srcs)
    arrays = list(srcs) + ([] if lands is None else list(lands))
    na = len(arrays)
    n_sems = n if slot is None else 3 * n

    def body(*refs):
        s_refs = refs[:n]
        d_refs = s_refs if lands is None else refs[n:na]
        send_sems, recv_sems, token = refs[na + 1], refs[na + 2], refs[-1]
        for cp in _ici_copies(s_refs, d_refs, slot, send_sems, recv_sems):
            cp.start()
        token[...] = jnp.zeros_like(token)

    outs = pl.pallas_call(
        body,
        name=name,
        out_shape=(pltpu.SemaphoreType.DMA((n_sems,)), pltpu.SemaphoreType.DMA((n_sems,)),
                   *[pltpu.HBM(a.shape, a.dtype) for a in arrays], _sds((SUBLANE, LANE), F32)),
        in_specs=[HBM] * na + [ANY],
        out_specs=(SEM, SEM, *([HBM] * na), pl.BlockSpec(memory_space=pltpu.VMEM)),
        input_output_aliases={i: 2 + i for i in range(na)},
        compiler_params=pltpu.CompilerParams(has_side_effects=DATAFLOW),
    )(*[pltpu.with_memory_space_constraint(a, pltpu.HBM) for a in arrays], after)
    return outs[0], outs[1], list(outs[2:2 + na]), outs[-1]


def _ici_wait(send_sems, recv_sems, arrays, n, shared, slot, after, name):
    na = len(arrays)

    def body(*refs):
        s_refs = refs[:n]
        d_refs = s_refs if shared else refs[n:na]
        for cp in _ici_copies(s_refs, d_refs, slot, refs[na], refs[na + 1]):
            cp.wait_send()
            cp.wait_recv()

    outs = pl.pallas_call(
        body,
        name=name,
        out_shape=tuple(pltpu.HBM(a.shape, a.dtype) for a in arrays),
        in_specs=[HBM] * na + [SEM, SEM, ANY],
        out_specs=tuple([HBM] * na),
        input_output_aliases={i: i for i in range(na)},
        compiler_params=pltpu.CompilerParams(has_side_effects=DATAFLOW),
    )(*arrays, send_sems, recv_sems, after)
    return list(outs)


def _my_chip():
    return 2 * lax.axis_index("x") + lax.axis_index("y")


def _cast_own(w, layer, name):
    _, r, cols = w.shape
    tr = _tile(r, 512, 2 * SUBLANE)
    tc = _tile(cols, 1024)

    def body(w_ref, o_ref):
        o_ref[...] = w_ref[...].astype(o_ref.dtype)

    return pl.pallas_call(
        body,
        out_shape=_sds((N_CHIP, r, cols), BF16),
        grid=(r // tr, cols // tc),
        in_specs=[pl.BlockSpec((None, tr, tc), lambda i, j: (layer, i, j))],
        out_specs=pl.BlockSpec((None, tr, tc), lambda i, j: (_my_chip(), i, j)),
        compiler_params=_cp(("parallel", "parallel")),
        name=name,
    )(w)


def _forward_halves(bufs, name):
    n = len(bufs)

    def body(*refs):
        ins, outs = refs[:n], refs[n:2 * n]
        send_sems, recv_sems = refs[2 * n], refs[2 * n + 1]
        x, y, cc = _my_pos()
        chip = 2 * x + y
        copies = []
        for i in range(n):
            for k in (1, 2, 3):
                at = 2 * (chip ^ k) + cc
                copies.append(pltpu.make_async_remote_copy(
                    src_ref=ins[i].at[pl.ds(at, 1)], dst_ref=outs[i].at[pl.ds(at, 1)],
                    send_sem=send_sems.at[3 * i + k - 1], recv_sem=recv_sems.at[3 * i + k - 1],
                    device_id=(x, y, 1 - cc), device_id_type=MESH))
        for cp in copies:
            cp.start()
        for cp in copies:
            cp.wait()

    return pl.pallas_call(
        body,
        out_shape=[_sds(b.shape, b.dtype) for b in bufs],
        in_specs=[ANY] * n,
        out_specs=[ANY] * n,
        scratch_shapes=[pltpu.SemaphoreType.DMA((3 * n,)), pltpu.SemaphoreType.DMA((3 * n,))],
        input_output_aliases={i: i for i in range(n)},
        name=name,
    )(*bufs)


def _rs_sibling(g8s, name):
    n = len(g8s)
    g4s = [g.reshape(N_CHIP, 2, g.shape[1], g.shape[2]) for g in g8s]

    def body(*refs):
        ins, outs = refs[:n], refs[n:2 * n]
        send_sems, recv_sems = refs[2 * n], refs[2 * n + 1]
        x, y, cc = _my_pos()
        copies = [pltpu.make_async_remote_copy(
            src_ref=ins[i].at[:, pl.ds(1 - cc, 1)], dst_ref=outs[i], send_sem=send_sems.at[i],
            recv_sem=recv_sems.at[i], device_id=(x, y, 1 - cc), device_id_type=MESH) for i in range(n)]
        for cp in copies:
            cp.start()
        for cp in copies:
            cp.wait()

    return pl.pallas_call(
        body,
        out_shape=[_sds((N_CHIP, 1, g.shape[2], g.shape[3]), g.dtype) for g in g4s],
        in_specs=[ANY] * n,
        out_specs=[ANY] * n,
        scratch_shapes=[pltpu.SemaphoreType.DMA((n,)), pltpu.SemaphoreType.DMA((n,))],
        name=name,
    )(*g4s)


def _share_halves(halves, name):
    def body(in_ref, out_ref, send_sem, recv_sem):
        x, y, cc = _my_pos()
        cp = pltpu.make_async_remote_copy(
            src_ref=in_ref.at[:, pl.ds(cc, 1)], dst_ref=out_ref.at[:, pl.ds(cc, 1)], send_sem=send_sem,
            recv_sem=recv_sem, device_id=(x, y, 1 - cc), device_id_type=MESH)
        cp.start()
        cp.wait()

    return pl.pallas_call(
        body,
        out_shape=_sds(halves.shape, halves.dtype),
        in_specs=[ANY],
        out_specs=ANY,
        scratch_shapes=[pltpu.SemaphoreType.DMA, pltpu.SemaphoreType.DMA],
        input_output_aliases={0: 0},
        name=name,
    )(halves)


def _rs_add_remote(g8, recv_a, name):
    _, r, c = g8.shape
    ra = recv_a.reshape(N_CHIP, r, c)
    tr = _tile(r, 512, SUBLANE)
    tc = _tile(c, 1024)

    def body(g_ref, r_ref, o_ref):
        o_ref[...] = (g_ref[...] + r_ref[...]).astype(o_ref.dtype)

    return pl.pallas_call(
        body,
        out_shape=_sds((3, r, c), BF16),
        grid=(3, r // tr, c // tc),
        in_specs=[pl.BlockSpec((None, tr, tc),
                               lambda k, i, j: (2 * (_my_chip() ^ (k + 1)) + lax.axis_index("c"), i, j)),
                  pl.BlockSpec((None, tr, tc), lambda k, i, j: (_my_chip() ^ (k + 1), i, j))],
        out_specs=pl.BlockSpec((None, tr, tc), lambda k, i, j: (k, i, j)),
        compiler_params=_cp(("parallel",) * 3),
        name=name,
    )(g8, ra)


def _rs_add_final(g8, recv_a, recv_b, out_buf, layer, name):
    _, r, c = g8.shape
    ra = recv_a.reshape(N_CHIP, r, c)
    tr = _tile(r, 512, SUBLANE)
    tc = _tile(c, 1024)

    def body(g_ref, r_ref, b0_ref, b1_ref, b2_ref, buf_ref, o_ref):
        del buf_ref
        o_ref[...] = (((g_ref[...] + r_ref[...]) + b0_ref[...].astype(F32)) + b1_ref[...].astype(F32)
                      ) + b2_ref[...].astype(F32)

    def bspec(k):
        return pl.BlockSpec((None, tr, tc), functools.partial(lambda i, j, k: (k, i, j), k=k))

    return pl.pallas_call(
        body,
        out_shape=_sds(out_buf.shape, F32),
        grid=(r // tr, c // tc),
        in_specs=[pl.BlockSpec((None, tr, tc), lambda i, j: (2 * _my_chip() + lax.axis_index("c"), i, j)),
                  pl.BlockSpec((None, tr, tc), lambda i, j: (_my_chip(), i, j)),
                  bspec(0), bspec(1), bspec(2), ANY],
        out_specs=pl.BlockSpec((None, None, tr, tc), lambda i, j: (layer, lax.axis_index("c"), i, j)),
        input_output_aliases={5: 0},
        compiler_params=_cp(("parallel",) * 2),
        name=name,
    )(g8, ra, recv_b, recv_b, recv_b, out_buf)


def _sum8(x8, name):
    _, r, c = x8.shape
    tr = _tile(r, 256, SUBLANE)

    def body(x_ref, o_ref):
        acc = x_ref[0]
        for b in range(1, N_DEV):
            acc = acc + x_ref[b]
        o_ref[...] = acc

    return pl.pallas_call(
        body,
        out_shape=_sds((r, c), F32),
        grid=(r // tr,),
        in_specs=[pl.BlockSpec((N_DEV, tr, c), lambda i: (0, i, 0))],
        out_specs=pl.BlockSpec((tr, c), lambda i: (i, 0)),
        compiler_params=_cp(("parallel",)),
        name=name,
    )(x8)


def _block_diag(t):
    g, p, q = t.shape
    eye = jnp.eye(g, dtype=t.dtype)
    return (t[:, :, None, :] * eye[:, None, :, None]).reshape(g * p, g * q)


def _diag_blocks(mat, g):
    p, q = mat.shape[0] // g, mat.shape[1] // g
    eye = jnp.eye(g, dtype=mat.dtype)
    return jnp.sum(mat.reshape(g, p, g, q) * eye[:, None, :, None], axis=2)


def _interleave(re, im, c):
    lead = re.shape[:-1]
    gn = re.shape[-1]
    return jnp.stack([re.reshape(*lead, gn // c, c), im.reshape(*lead, gn // c, c)], axis=-2).reshape(*lead, 2 * gn)


def _deinterleave(cat, c):
    lead = cat.shape[:-1]
    gn = cat.shape[-1] // 2
    t = cat.reshape(*lead, gn // c, 2, c)
    return t[..., 0, :].reshape(*lead, gn), t[..., 1, :].reshape(*lead, gn)


def kernel(x, c, w_mod, b_mod, g_pre_mix, g_post_mix, g_pre_ffn, g_post_ffn, w_in, ssm_log_dt, ssm_a_re, ssm_a_im, ssm_b_re, ssm_b_im, ssm_c_re, ssm_c_im, ssm_d, w_glu, b_glu, conv_mix_w, w_ssm_out, w_attn_out, w_conv_out, b_gate, w_o, w_up, ffn_conv_w, w_down, loss_target, m_w_mod, m_b_mod, m_g_pre_mix, m_g_post_mix, m_g_pre_ffn, m_g_post_ffn, m_w_in, m_ssm_log_dt, m_ssm_a_re, m_ssm_a_im, m_ssm_b_re, m_ssm_b_im, m_ssm_c_re, m_ssm_c_im, m_ssm_d, m_w_glu, m_b_glu, m_conv_mix_w, m_w_ssm_out, m_w_attn_out, m_w_conv_out, m_b_gate, m_w_o, m_w_up, m_ffn_conv_w, m_w_down, v_w_mod, v_b_mod, v_g_pre_mix, v_g_post_mix, v_g_pre_ffn, v_g_post_ffn, v_w_in, v_ssm_log_dt, v_ssm_a_re, v_ssm_a_im, v_ssm_b_re, v_ssm_b_im, v_ssm_c_re, v_ssm_c_im, v_ssm_d, v_w_glu, v_b_glu, v_conv_mix_w, v_w_ssm_out, v_w_attn_out, v_w_conv_out, v_b_gate, v_w_o, v_w_up, v_ffn_conv_w, v_w_down):
    weights = dict(w_mod=w_mod, b_mod=b_mod, g_pre_mix=g_pre_mix, g_post_mix=g_post_mix, g_pre_ffn=g_pre_ffn, g_post_ffn=g_post_ffn, w_in=w_in, ssm_log_dt=ssm_log_dt, ssm_a_re=ssm_a_re, ssm_a_im=ssm_a_im, ssm_b_re=ssm_b_re, ssm_b_im=ssm_b_im, ssm_c_re=ssm_c_re, ssm_c_im=ssm_c_im, ssm_d=ssm_d, w_glu=w_glu, b_glu=b_glu, conv_mix_w=conv_mix_w, w_ssm_out=w_ssm_out, w_attn_out=w_attn_out, w_conv_out=w_conv_out, b_gate=b_gate, w_o=w_o, w_up=w_up, ffn_conv_w=ffn_conv_w, w_down=w_down)
    mom_m = dict(w_mod=m_w_mod, b_mod=m_b_mod, g_pre_mix=m_g_pre_mix, g_post_mix=m_g_post_mix, g_pre_ffn=m_g_pre_ffn, g_post_ffn=m_g_post_ffn, w_in=m_w_in, ssm_log_dt=m_ssm_log_dt, ssm_a_re=m_ssm_a_re, ssm_a_im=m_ssm_a_im, ssm_b_re=m_ssm_b_re, ssm_b_im=m_ssm_b_im, ssm_c_re=m_ssm_c_re, ssm_c_im=m_ssm_c_im, ssm_d=m_ssm_d, w_glu=m_w_glu, b_glu=m_b_glu, conv_mix_w=m_conv_mix_w, w_ssm_out=m_w_ssm_out, w_attn_out=m_w_attn_out, w_conv_out=m_w_conv_out, b_gate=m_b_gate, w_o=m_w_o, w_up=m_w_up, ffn_conv_w=m_ffn_conv_w, w_down=m_w_down)
    mom_v = dict(w_mod=v_w_mod, b_mod=v_b_mod, g_pre_mix=v_g_pre_mix, g_post_mix=v_g_post_mix, g_pre_ffn=v_g_pre_ffn, g_post_ffn=v_g_post_ffn, w_in=v_w_in, ssm_log_dt=v_ssm_log_dt, ssm_a_re=v_ssm_a_re, ssm_a_im=v_ssm_a_im, ssm_b_re=v_ssm_b_re, ssm_b_im=v_ssm_b_im, ssm_c_re=v_ssm_c_re, ssm_c_im=v_ssm_c_im, ssm_d=v_ssm_d, w_glu=v_w_glu, b_glu=v_b_glu, conv_mix_w=v_conv_mix_w, w_ssm_out=v_w_ssm_out, w_attn_out=v_w_attn_out, w_conv_out=v_w_conv_out, b_gate=v_b_gate, w_o=v_w_o, w_up=v_w_up, ffn_conv_w=v_ffn_conv_w, w_down=v_w_down)
    names = list(weights)

    nl = w_in.shape[0]
    seq, d = x.shape[1], x.shape[2]
    sw = d // 4
    groups = sw // SSM_GROUP
    gn = groups * SSM_STATE
    hp = sw // HEAD_DIM
    qw = 3 * sw
    off_q, off_k, off_v = sw, sw + qw, sw + 2 * qw
    off_conv = sw + 3 * qw
    off_gate = off_conv + 3 * sw
    n_in = off_gate + 3 * d
    f = w_down.shape[1] * N_CHIP
    scan_c = _scan_cols(gn)
    assert seq % (ATTN_BLOCK * DSWA_PATTERNS[-1][1]) == 0 and all(w // dl == ATTN_BLOCK for w, dl in DSWA_PATTERNS)

    px, py, pc = _my_pos()
    chip = 2 * px + py
    dev = 2 * chip + pc

    x2 = x.reshape(seq, d)
    target2 = loss_target.reshape(seq, d)

    mix_keys = ("w_in", "w_glu", "w_ssm_out", "w_attn_out", "w_conv_out", "w_o")
    ffn_keys = ("w_up", "w_down")
    col_sharded = ("w_in", "w_ssm_out", "w_attn_out", "w_conv_out", "w_up")
    n_stages = 2 * nl

    def stage_keys(stage):
        return ffn_keys if stage % 2 else mix_keys

    def blocked(k, buf8):
        r, cols = weights[k].shape[1:]
        return buf8.reshape(1, N_CHIP, r, cols) if k in col_sharded else buf8.reshape(1, 1, N_CHIP * r, cols)

    def begin_gather(stage, after):
        keys = stage_keys(stage)
        bufs = []
        for k in keys:
            r, cols = weights[k].shape[1:]
            bufs.append(_cast_own(weights[k], stage // 2, "cast_own").reshape(N_DEV, r // 2, cols))
        return _ici_start(bufs, None, _own_block, after, f"gather_start_{stage}")

    def end_gather(stage, pending, after):
        send_sems, recv_sems, bufs, _ = pending
        bufs = _ici_wait(send_sems, recv_sems, bufs, len(bufs), True, _own_block, after, f"gather_wait_{stage}")
        bufs = _forward_halves(bufs, "gather_forward")
        return {k: blocked(k, b) for k, b in zip(stage_keys(stage), bufs)}

    c_all = _ag8(c.reshape(1, 1, 1, d), False, "ag_cond").reshape(N_DEV, d)
    c_pad = jnp.concatenate([c_all, jnp.zeros((SUBLANE, d), F32)], axis=0)
    mcols = w_mod.shape[2]
    w_mod4 = w_mod.reshape(nl, 1, d, mcols)
    mod_loc = jnp.stack([_mm_nn(c_pad, w_mod4, l, name="mod_fwd", a_fn=jax.nn.silu) for l in range(nl)])
    mod_all = _ag8(mod_loc.reshape(nl, 1, 2 * SUBLANE, mcols), False, "ag_mod")
    mod_rows = lax.dynamic_slice_in_dim(mod_all[:, 0::2], dev, 1, axis=2)
    mod = mod_rows.reshape(nl, N_CHIP * mcols) + b_mod
    mods = mod.reshape(nl, 6, 1, d)

    taps = jnp.concatenate([conv_mix_w.reshape(-1), ffn_conv_w.reshape(-1)])
    tap_w = 8 * LANE
    tap_r = -(-taps.shape[0] // (tap_w * SUBLANE)) * SUBLANE
    taps = jnp.concatenate([taps, jnp.zeros((tap_r * tap_w - taps.shape[0],), F32)])
    taps_all = _ag8(taps.reshape(1, 1, tap_r, tap_w), False, "ag_small_weight")[0, 0::2].reshape(N_CHIP, -1)

    def whole(w, at):
        _, r, cols = w.shape
        got = taps_all[:, at:at + w.size].reshape(N_CHIP, nl, r, cols)
        return got.transpose(1, 2, 0, 3).reshape(nl, r, N_CHIP * cols)

    conv_w_full = whole(conv_mix_w, 0)
    ffn_w_full = whole(ffn_conv_w, conv_mix_w.size)
    wfs = [dict() for _ in range(nl)]
    pendings, order = [], mods[0, 0, :, :1] + taps_all[:1, :1]
    for stage in range(n_stages):
        pendings.append(begin_gather(stage, order))
        order = pendings[-1][3]
    wfs[0].update(end_gather(0, pendings[0], mods))

    head_ones = jnp.asarray(np.kron(np.eye(hp, dtype=np.float32), np.ones((HEAD_DIM, HEAD_DIM), np.float32)))
    slopes = [_alibi_slopes(p, hp) for p in range(len(DSWA_PATTERNS))]
    qkv_offs = (off_q, off_k, off_v)

    def row(v):
        return v.reshape(1, -1)

    saved = []
    xl = x2
    for l in range(nl):
        sh1, sc1, gt1, sh2, sc2, gt2 = [mods[l, q] for q in range(6)]
        s = dict(x_in=xl)
        wf = wfs[l]
        sh1_t = sh1 + order[0, 0] if l == 0 else sh1
        (h1,) = _ew_fwd(_fn_norm_mod, [(xl, 0, d)], [row(g_pre_mix[l]), sc1, sh1_t], [BF16], name="norm_mod_fwd", width=d)
        proj = _mm_nn(h1, wf["w_in"], 0, name="w_in_fwd")
        br_t = jnp.transpose(ssm_b_re[l], (2, 0, 1))
        bi_t = jnp.transpose(ssm_b_im[l], (2, 0, 1))
        disc_in = (ssm_log_dt[l].reshape(groups, 1), ssm_a_re[l], ssm_a_im[l], br_t, bi_t)
        lr, li, bbr_t, bbi_t = _ssm_disc_fwd(*disc_in)
        lam = _interleave(lr.reshape(1, gn), li.reshape(1, gn), scan_c)
        bcat = _interleave(_block_diag(jnp.transpose(bbr_t, (1, 0, 2))), _block_diag(jnp.transpose(bbi_t, (1, 0, 2))),
                           scan_c).astype(BF16).reshape(1, 1, sw, 2 * gn)
        cre = _block_diag(jnp.transpose(ssm_c_re[l], (0, 2, 1)))
        cim = _block_diag(jnp.transpose(ssm_c_im[l], (0, 2, 1)))
        ccat = jnp.transpose(_interleave(cre.T, -cim.T, scan_c)).astype(BF16).reshape(1, 1, 2 * gn, sw)
        xcat = _mm_nn(proj, bcat, 0, name="ssm_b_fwd", k_dim=sw)
        hcat = _ssm_scan_fwd(xcat, lam)
        y_ssm_pre = _mm_nn(hcat, ccat, 0, name="ssm_c_fwd")
        (gact,) = _ew_fwd(_fn_gelu, [(y_ssm_pre, 0, sw), (proj, 0, sw)], [row(ssm_d[l])], [F32], name="gelu_fwd", width=sw,
                          tw=_tile(sw, 512))
        z = _mm_nn(gact, wf["w_glu"], 0, name="w_glu_fwd")
        (s_ssm,) = _ew_fwd(_fn_glu, [(gact, 0, sw), (z, 0, sw)], [row(b_glu[l])], [BF16], name="glu_fwd", width=sw,
                           tw=_tile(sw, 512))
        y_ssm = _mm_nn(s_ssm, wf["w_ssm_out"], 0, name="w_branch_out_fwd")
        outs, lses = [], []
        for p, (_, dl) in enumerate(DSWA_PATTERNS):
            o_p, lse_p = _attn_fwd(proj, qkv_offs, p, dl, sw, slopes[p])
            outs.append(o_p)
            lses.append(lse_p)
        o_attn, lse_attn = _attn_merge(outs, lses)
        y_attn = _mm_nn(o_attn, wf["w_attn_out"], 0, name="w_branch_out_fwd")
        cv = _gconv_fwd(proj, off_conv, sw, conv_w_full[l])
        y_conv = _mm_nn(cv, wf["w_conv_out"], 0, name="w_branch_out_fwd")
        bg = b_gate[l].reshape(3, 1, d)
        gate_xs = [(proj, off_gate + q * d, d) for q in range(3)] + [(y_ssm, 0, d), (y_attn, 0, d), (y_conv, 0, d)]
        (merged,) = _ew_fwd(_fn_gates, gate_xs, [bg[0], bg[1], bg[2]], [BF16], name="gates_fwd", width=d,
                            tw=_tile(sw, 512))
        y_mix = _mm_nn(merged, wf["w_o"], 0, name="w_o_fwd")
        (x_mid,) = _ew_fwd(_fn_residual, [(xl, 0, d), (y_mix, 0, d)], [gt1, row(g_post_mix[l])], [F32], name="residual_fwd",
                           width=d)
        wf.update(end_gather(2 * l + 1, pendings[2 * l + 1], x_mid))
        (h2,) = _ew_fwd(_fn_norm_mod, [(x_mid, 0, d)], [row(g_pre_ffn[l]), sc2, sh2], [BF16], name="norm_mod_fwd", width=d)
        up = _mm_nn(h2, wf["w_up"], 0, name="w_up_fwd")
        act = _ffn_act_fwd(up, ffn_w_full[l])
        y_ffn = _mm_nn(act, wf["w_down"], 0, name="w_down_fwd")
        (x_out,) = _ew_fwd(_fn_residual, [(x_mid, 0, d), (y_ffn, 0, d)], [gt2, row(g_post_ffn[l])], [F32],
                           name="residual_fwd", width=d)
        if l + 1 < nl:
            wfs[l + 1].update(end_gather(2 * l + 2, pendings[2 * l + 2], x_out))
        s.update(h1=h1, proj=proj, disc_in=disc_in, lam=lam, bcat=bcat, ccat=ccat, hcat=hcat, y_ssm_pre=y_ssm_pre,
                 gact=gact, z=z, s_ssm=s_ssm, y_ssm=y_ssm, o_attn=o_attn, lse_attn=lse_attn, y_attn=y_attn,
                 cv=cv, y_conv=y_conv, merged=merged, y_mix=y_mix, x_mid=x_mid, h2=h2, up=up, act=act, y_ffn=y_ffn)
        saved.append(s)
        xl = x_out

    dxl, loss_local = _loss_fwd_bwd(xl, target2)
    loss = lax.psum(loss_local, ("x", "y", "c"))

    gfin = {k: lax.empty((nl, 2, weights[k].shape[1] // 2, weights[k].shape[2]), F32) for k in mix_keys + ffn_keys}

    rs = dict(sib=None, ici=None)

    def rs_stage_end(stage, grads_4d):
        g8s = [grads_4d[k].reshape(N_DEV, weights[k].shape[1] // 2, weights[k].shape[2]) for k in stage_keys(stage)]
        g4s = [g.reshape(N_CHIP, 2, g.shape[1], g.shape[2]) for g in g8s]
        lands = [lax.empty((N_CHIP, 1, g.shape[1], g.shape[2]), F32) for g in g8s]
        rs["sib"] = (stage, _ici_start(g4s, lands, None, mods[0, 0, :, :1], f"rs_sibling_start_{stage}"))

    def rs_midpoint(after):
        if rs["ici"] is not None:
            stage, g8s, recv_a, (send_sems, recv_sems, arrays, _) = rs["ici"]
            n = len(g8s)
            arrays = _ici_wait(send_sems, recv_sems, arrays, n, False, _distance_slot, after, f"rs_wait_{stage}")
            for k, g8, ra, rb in zip(stage_keys(stage), g8s, recv_a, arrays[n:]):
                gfin[k] = _rs_add_final(g8, ra, rb, gfin[k], stage // 2, "rs_add_final")
            rs["ici"] = None
        if rs["sib"] is None:
            return 0.0
        stage, (send_sems, recv_sems, arrays, _) = rs["sib"]
        n = len(arrays) // 2
        arrays = _ici_wait(send_sems, recv_sems, arrays, n, False, None, after, f"rs_sibling_wait_{stage}")
        g8s = [a.reshape(N_DEV, a.shape[2], a.shape[3]) for a in arrays[:n]]
        recv_a = arrays[n:]
        s_rem = [_rs_add_remote(g, ra, "rs_add_remote") for g, ra in zip(g8s, recv_a)]
        lands = [lax.empty(t.shape, BF16) for t in s_rem]
        started = _ici_start(s_rem, lands, _distance_slot, recv_a[0], f"rs_start_{stage}")
        rs["sib"], rs["ici"] = None, (stage, g8s, recv_a, started)
        return started[3][0, 0]

    def after_rs(v):
        for flight in (rs["sib"], rs["ici"]):
            if flight is not None:
                v = v + flight[-1][3][0, 0]
        return v

    def grad_buf(k):
        return lax.empty(wf[k].shape, F32)

    small = {k: [None] * nl for k in ("g_pre_mix", "g_post_mix", "g_pre_ffn", "g_post_ffn", "ssm_log_dt", "ssm_a_re",
                                      "ssm_a_im", "ssm_b_re", "ssm_b_im", "ssm_c_re", "ssm_c_im", "ssm_d", "b_glu",
                                      "conv_mix_w", "b_gate", "ffn_conv_w", "dmod")}
    for l in reversed(range(nl)):
        s = saved[l]
        sh1, sc1, gt1, sh2, sc2, gt2 = [mods[l, q] for q in range(6)]
        proj = s["proj"]
        wf = wfs[l]
        gw = {}
        (dy_ffn,), (dgt2, dg_post_ffn) = _ew_bwd(
            _fn_residual, [(s["x_mid"], 0, d), (s["y_ffn"], 0, d)], [after_rs(gt2), row(g_post_ffn[l])],
            [[dxl]], [None, BF16], name="residual_bwd", width=d)
        dact = _mm_nt(dy_ffn, wf["w_down"], 0, name="w_down_bwd_x", out_dtype=BF16)
        gw["w_down"] = _mm_tn(s["act"], dy_ffn, grad_buf("w_down"), 0, name="w_down_bwd_w")
        behind = rs_midpoint(gw["w_down"])
        dup_a, dup_b, dwa, dwb = _ffn_act_bwd(s["up"], ffn_w_full[l] + behind, dact)
        dup = jnp.concatenate([dup_a, dup_b], axis=1)
        small["ffn_conv_w"][l] = jnp.concatenate([dwa, dwb], axis=1)
        dh2 = _mm_nt(dup, wf["w_up"], 0, name="w_up_bwd_x")
        gw["w_up"] = _mm_tn(s["h2"], dup, grad_buf("w_up"), 0, name="w_up_bwd_w")
        (dx_mid,), (dg_pre_ffn, dsc2, dsh2) = _ew_bwd(
            _fn_norm_mod, [(s["x_mid"], 0, d)], [row(g_pre_ffn[l]), sc2, sh2], [[dh2]], [F32], name="norm_mod_bwd", width=d,
            dx_add={0: dxl})
        rs_stage_end(2 * l + 1, gw)
        (dy_mix,), (dgt1, dg_post_mix) = _ew_bwd(
            _fn_residual, [(s["x_in"], 0, d), (s["y_mix"], 0, d)], [after_rs(gt1), row(g_post_mix[l])],
            [[dx_mid]], [None, BF16], name="residual_bwd", width=d)
        dmerged = _mm_nt(dy_mix, wf["w_o"], 0, name="w_o_bwd_x")
        gw["w_o"] = _mm_tn(s["merged"], dy_mix, grad_buf("w_o"), 0, name="w_o_bwd_w")
        bg = b_gate[l].reshape(3, 1, d)
        gate_xs = [(proj, off_gate + q * d, d) for q in range(3)] + [(s["y_ssm"], 0, d), (s["y_attn"], 0, d),
                                                                     (s["y_conv"], 0, d)]
        (dp0, dp1, dp2, dy_ssm, dy_attn, dy_conv), dbg = _ew_bwd(
            _fn_gates, gate_xs, [bg[0], bg[1], bg[2]], [[dmerged]], [BF16] * 6, name="gates_bwd", width=d,
            tw=_tile(sw, 512))
        small["b_gate"][l] = jnp.concatenate(dbg, axis=1)[0]
        behind = rs_midpoint(dy_conv)
        ds_ssm = _mm_nt(dy_ssm, wf["w_ssm_out"], 0, name="w_branch_out_bwd_x")
        gw["w_ssm_out"] = _mm_tn(s["s_ssm"], dy_ssm, grad_buf("w_ssm_out"), 0, name="w_branch_out_bwd_w")
        (dg1, dz), (db_glu,) = _ew_bwd(_fn_glu, [(s["gact"], 0, sw), (s["z"], 0, sw)], [row(b_glu[l]) + behind], [[ds_ssm]],
                                       [F32, BF16], name="glu_bwd", width=sw, tw=_tile(sw, 512))
        dg2 = _mm_nt(dz, wf["w_glu"], 0, name="w_glu_bwd_x")
        gw["w_glu"] = _mm_tn(s["gact"], dz, grad_buf("w_glu"), 0, name="w_glu_bwd_w")
        (dy_pre, du_skip), (dd_skip,) = _ew_bwd(_fn_gelu, [(s["y_ssm_pre"], 0, sw), (proj, 0, sw)], [row(ssm_d[l])],
                                               [[dg1, dg2]], [BF16, F32], name="gelu_bwd", width=sw, tw=_tile(sw, 512))
        dhcat = _mm_nt(dy_pre, s["ccat"], 0, name="ssm_c_bwd_x")
        dccat = _mm_tn(s["hcat"], dy_pre, lax.empty((1, 1, 2 * gn, sw), F32), 0, name="ssm_c_bwd_w")[0, 0]
        gcat, dlam = _ssm_scan_bwd(dhcat, s["hcat"], s["lam"])
        du_b = _mm_nt(gcat, s["bcat"], 0, name="ssm_b_bwd_x")
        dbcat = _mm_tn(proj, gcat, lax.empty((1, 1, sw, 2 * gn), F32), 0, name="ssm_b_bwd_w")[0, 0]
        dlr, dli = _deinterleave(dlam, scan_c)
        dbre, dbim = _deinterleave(dbcat, scan_c)
        dbbr_t = jnp.transpose(_diag_blocks(dbre, groups), (1, 0, 2))
        dbbi_t = jnp.transpose(_diag_blocks(dbim, groups), (1, 0, 2))
        gld, gar, gai, gbr_t, gbi_t = _ssm_disc_bwd(*s["disc_in"], dlr.reshape(groups, SSM_STATE),
                                                    dli.reshape(groups, SSM_STATE), dbbr_t, dbbi_t)
        dcre_t, dcim_t = _deinterleave(dccat.T, scan_c)
        small["ssm_c_re"][l] = _diag_blocks(dcre_t, groups)
        small["ssm_c_im"][l] = -_diag_blocks(dcim_t, groups)
        small["ssm_log_dt"][l] = gld.reshape(groups)
        small["ssm_a_re"][l], small["ssm_a_im"][l] = gar, gai
        small["ssm_b_re"][l] = jnp.transpose(gbr_t, (1, 2, 0))
        small["ssm_b_im"][l] = jnp.transpose(gbi_t, (1, 2, 0))
        small["ssm_d"][l], small["b_glu"][l] = dd_skip[0], db_glu[0]
        du = (du_skip + du_b).astype(BF16)
        do_attn = _mm_nt(dy_attn, wf["w_attn_out"], 0, name="w_branch_out_bwd_x")
        gw["w_attn_out"] = _mm_tn(s["o_attn"], dy_attn, grad_buf("w_attn_out"), 0, name="w_branch_out_bwd_w")
        delta = _attn_delta(do_attn, s["o_attn"], head_ones)
        dqs, dks, dvs = [], [], []
        for p, (_, dl) in enumerate(DSWA_PATTERNS):
            dq_p, dk_p, dv_p = _attn_bwd(proj, do_attn, s["lse_attn"], delta, qkv_offs, p, dl, sw, slopes[p])
            dqs.append(dq_p)
            dks.append(dk_p)
            dvs.append(dv_p)
        dcv = _mm_nt(dy_conv, wf["w_conv_out"], 0, name="w_branch_out_bwd_x", out_dtype=BF16)
        gw["w_conv_out"] = _mm_tn(s["cv"], dy_conv, grad_buf("w_conv_out"), 0, name="w_branch_out_bwd_w")
        dcb, dcc, dch, dconv_w = _gconv_bwd(proj, off_conv, sw, conv_w_full[l], dcv)
        small["conv_mix_w"][l] = dconv_w
        dproj = jnp.concatenate([du] + [t.astype(BF16) for t in dqs + dks + dvs] + [dcb, dcc, dch, dp0, dp1, dp2],
                                axis=1)
        dh1 = _mm_nt(dproj, wf["w_in"], 0, name="w_in_bwd_x")
        gw["w_in"] = _mm_tn(s["h1"], dproj, grad_buf("w_in"), 0, name="w_in_bwd_w")
        (dx_in,), (dg_pre_mix, dsc1, dsh1) = _ew_bwd(
            _fn_norm_mod, [(s["x_in"], 0, d)], [row(g_pre_mix[l]), sc1, sh1], [[dh1]], [F32], name="norm_mod_bwd", width=d,
            dx_add={0: dx_mid})
        rs_stage_end(2 * l, gw)
        small["g_pre_mix"][l], small["g_post_mix"][l] = dg_pre_mix[0], dg_post_mix[0]
        small["g_pre_ffn"][l], small["g_post_ffn"][l] = dg_pre_ffn[0], dg_post_ffn[0]
        small["dmod"][l] = jnp.concatenate([dsh1, dsc1, dgt1, dsh2, dsc2, dgt2], axis=1)[0]
        dxl = dx_in

    grad_x = dxl.reshape(x.shape)

    small = {k: jnp.stack(v) for k, v in small.items()}
    order = sorted(small)
    flat = jnp.concatenate([small[k].reshape(-1) for k in order])
    n_small = flat.shape[0]
    pack_w = 8 * LANE
    pack_r = -(-n_small // (pack_w * SUBLANE)) * SUBLANE
    flat = jnp.concatenate([flat, jnp.zeros((pack_r * pack_w - n_small,), F32)])
    gathered = _ag8(flat.reshape(1, 1, pack_r, pack_w), False, "ag_small_grads")[0]
    summed = _sum8(gathered, "sum_small_grads").reshape(-1)
    sgrad, at = {}, 0
    for k in order:
        size = small[k].size
        sgrad[k] = summed[at:at + size].reshape(small[k].shape)
        at += size
    dmod_off = sum(small[k].size for k in order[:order.index("dmod")])
    dmod_all = gathered.reshape(N_DEV, -1)[:, dmod_off:dmod_off + nl * 6 * d].reshape(N_DEV, nl, 6 * d)
    dmod_loc = lax.dynamic_slice_in_dim(jnp.transpose(dmod_all, (1, 0, 2)), chip * mcols, mcols, axis=2)

    grads = dict(sgrad)
    grads["b_mod"] = grads.pop("dmod")
    grads["conv_mix_w"] = lax.dynamic_slice_in_dim(sgrad["conv_mix_w"], chip * conv_mix_w.shape[2], conv_mix_w.shape[2], axis=2)
    grads["ffn_conv_w"] = lax.dynamic_slice_in_dim(sgrad["ffn_conv_w"], chip * ffn_conv_w.shape[2], ffn_conv_w.shape[2], axis=2)

    delta_w, new_m, new_v = {}, {}, {}

    def update(k, after=None):
        delta_w[k], new_m[k], new_v[k] = _adam(weights[k], grads[k], mom_m[k], mom_v[k], "adamw", after)

    behind = rs_midpoint(summed)
    in_flight = rs["ici"][3][3]
    c_t = jnp.pad(jnp.transpose(c_all), ((0, 0), (0, LANE - N_DEV))) + behind
    dmod_pad = jnp.pad(dmod_loc, ((0, 0), (0, LANE - N_DEV), (0, 0)))
    grads["w_mod"], delta_w["w_mod"], new_m["w_mod"], new_v["w_mod"] = _wmod_grad_adam(
        c_t, dmod_pad, w_mod, m_w_mod, v_w_mod)
    for k in ffn_keys:
        grads[k] = _share_halves(gfin[k], "rs_share").reshape(weights[k].shape)
        update(k, in_flight)
    for k in names:
        if k != "w_mod" and k not in mix_keys + ffn_keys:
            update(k)
    rs_midpoint(new_v["w_down"][0, :1, :1] + new_v["w_up"][0, :1, :1] + new_v["w_mod"][0, :1, :1])
    for k in mix_keys:
        grads[k] = _share_halves(gfin[k], "rs_share").reshape(weights[k].shape)
        update(k)

    return (loss, grad_x, *[grads[k] for k in names], *[delta_w[k] for k in names], *[new_m[k] for k in names],
            *[new_v[k] for k in names])
```
